```python
import jax, jax.numpy as jnp
from jax import lax
import numpy as np

D_MODEL = 1024
BATCH = 32
SEQ = 256
DEPTH = 1
DEC_BATCH = 8
DEC_SEQ = 1024
PAST_LEN = 256

GRID_W = 64
SSM_EXPAND = 2
D_INNER = SSM_EXPAND * D_MODEL
HEAD_DIM = 64
N_HEADS = D_INNER // HEAD_DIM
D_STATE = 128
N_GROUPS = 8
D_CONV_SSM = 5
CHUNK = 128
CONV_DIM = D_INNER + 2 * N_GROUPS * D_STATE
D_CONF = D_MODEL
CONF_K = 31
N_BRANCHES = 2
SPLITS = [D_INNER, D_INNER + CONV_DIM, D_INNER + CONV_DIM + 2 * N_HEADS,
          D_INNER + CONV_DIM + 2 * N_HEADS + D_CONF, D_INNER + CONV_DIM + 2 * N_HEADS + 2 * D_CONF]
P_TOTAL = SPLITS[-1] + N_BRANCHES * D_MODEL
N_EXPERTS = 256
TOP_K = 8
N_EXPERT_GROUPS = 8
TOPK_GROUPS = 4
D_EXPERT = 256
D_SHARED = 256
ROUTED_SCALE = 2.5
MOE_BLOCK = 128
N_MOD = 6
EPS = 1e-6

kernel_name = "hybrid_ssd_conformer_moe_flow_step"


def rmsnorm(x, w):
    xf = x.astype(jnp.float32)
    y = xf * lax.rsqrt(jnp.mean(xf * xf, axis=-1, keepdims=True) + EPS)
    return (y * w.astype(jnp.float32)).astype(x.dtype)


def layernorm(x, w, b):
    xf = x.astype(jnp.float32)
    mu = jnp.mean(xf, axis=-1, keepdims=True)
    var = jnp.mean(jnp.square(xf - mu), axis=-1, keepdims=True)
    y = (xf - mu) * lax.rsqrt(var + EPS)
    return (y * w.astype(jnp.float32) + b.astype(jnp.float32)).astype(x.dtype)


def dwconv1d(x, w, b):
    k, ch = w.shape
    y = lax.conv_general_dilated(x, w[:, None, :].astype(x.dtype), window_strides=(1,),
                                 padding=[(k // 2, k // 2)],
                                 dimension_numbers=('NWC', 'WIO', 'NWC'),
                                 feature_group_count=ch)
    return y + b.astype(x.dtype)


def axial_dwconv(x, w, b):
    bsz, l, ch = x.shape
    rows = l // GRID_W
    half = ch // 2
    xg = x.reshape(bsz, rows, GRID_W, ch)
    xr = dwconv1d(xg[..., :half].reshape(bsz * rows, GRID_W, half), w[:, :half], b[:half])
    xr = xr.reshape(bsz, rows, GRID_W, half)
    xc = xg[..., half:].transpose(0, 2, 1, 3).reshape(bsz * GRID_W, rows, ch - half)
    xc = dwconv1d(xc, w[:, half:], b[half:]).reshape(bsz, GRID_W, rows, ch - half).transpose(0, 2, 1, 3)
    return jnp.concatenate([xr, xc], axis=-1).reshape(bsz, l, ch)


def segsum(a):
    t = a.shape[-1]
    cs = jnp.cumsum(a, axis=-1)
    diff = cs[..., :, None] - cs[..., None, :]
    mask = jnp.tril(jnp.ones((t, t), dtype=bool))
    return jnp.where(mask, diff, -jnp.inf)


def ssd_chunked(x, dt, a, bm, cm, init):
    b, l, h, p = x.shape
    g, n = bm.shape[-2:]
    r = h // g
    nc = l // CHUNK
    xdt = (x * dt[..., None]).reshape(b, nc, CHUNK, g, r, p)
    la = (dt * a).astype(jnp.float32).reshape(b, nc, CHUNK, g, r).transpose(0, 3, 4, 1, 2)
    bc = bm.reshape(b, nc, CHUNK, g, n)
    cc = cm.reshape(b, nc, CHUNK, g, n)
    cs = jnp.cumsum(la, axis=-1)
    lmat = jnp.exp(segsum(la))
    cb = jnp.einsum('bcqgn,bcsgn->bgcqs', cc, bc)
    y_diag = jnp.einsum('bgrcqs,bcsgrp->bcqgrp', cb[:, :, None] * lmat, xdt)
    decay_states = jnp.exp(cs[..., -1:] - cs).transpose(0, 3, 4, 1, 2)
    states = jnp.einsum('bcqgn,bcqgrp->bcgrpn', bc, xdt * decay_states[..., None])
    states = jnp.concatenate([init.reshape(b, 1, g, r, p, n).astype(states.dtype), states], axis=1)
    chunk_decay = jnp.exp(segsum(jnp.pad(cs[..., -1], ((0, 0), (0, 0), (0, 0), (1, 0)))))
    new_states = jnp.einsum('bgrzc,bcgrpn->bzgrpn', chunk_decay, states)
    prev_states, final = new_states[:, :-1], new_states[:, -1]
    state_decay_out = jnp.exp(cs).transpose(0, 3, 4, 1, 2)
    y_off = jnp.einsum('bcqgn,bcgrpn->bcqgrp', cc, prev_states) * state_decay_out[..., None]
    y = (y_diag + y_off).reshape(b, l, h, p)
    return y.astype(x.dtype), final.reshape(b, h, p, n).astype(x.dtype)


def bi_ssd(xs, dt_raw, bm, cm, a_log, dt_bias, init):
    a = -jnp.exp(a_log.astype(jnp.float32))
    dt = jax.nn.softplus(dt_raw.astype(jnp.float32) + dt_bias.astype(jnp.float32))
    yf, sf = ssd_chunked(xs, dt[:, :, 0], a[0], bm, cm, init[:, 0])
    flip = lambda t: jnp.flip(t, axis=1)
    yb, sb = ssd_chunked(flip(xs), flip(dt[:, :, 1]), a[1], flip(bm), flip(cm), init[:, 1])
    return yf + flip(yb), jnp.stack([sf, sb], axis=1)


def mixer(u, init_state, grid, w_in, ssm_conv_w, ssm_conv_b, ssm_dt_bias, ssm_a_log, ssm_d, ssm_norm_w,
          w_out_ssm, conf_conv_w, conf_conv_b, conf_ln_w, conf_ln_b, w_out_conf, w_o):
    bsz, l, _ = u.shape
    proj = u @ w_in
    z, xbc, dt_raw, cv, cg, gl = jnp.split(proj, SPLITS, axis=-1)
    xbc = jax.nn.silu(dwconv1d(xbc, ssm_conv_w, ssm_conv_b))
    xs, bm, cm = jnp.split(xbc, [D_INNER, D_INNER + N_GROUPS * D_STATE], axis=-1)
    xs = xs.reshape(bsz, l, N_HEADS, HEAD_DIM)
    bm = bm.reshape(bsz, l, N_GROUPS, D_STATE)
    cm = cm.reshape(bsz, l, N_GROUPS, D_STATE)
    y, final = bi_ssd(xs, dt_raw.reshape(bsz, l, 2, N_HEADS), bm, cm, ssm_a_log, ssm_dt_bias, init_state)
    y = (y + ssm_d[:, None] * xs).reshape(bsz, l, D_INNER) * jax.nn.silu(z)
    y = rmsnorm(y.reshape(bsz, l, N_GROUPS, D_INNER // N_GROUPS),
                ssm_norm_w.reshape(N_GROUPS, D_INNER // N_GROUPS)).reshape(bsz, l, D_INNER)
    y_a = y @ w_out_ssm
    v = cv * jax.nn.sigmoid(cg)
    v = axial_dwconv(v, conf_conv_w, conf_conv_b) if grid else dwconv1d(v, conf_conv_w, conf_conv_b)
    v = jax.nn.silu(layernorm(v, conf_ln_w, conf_ln_b))
    y_b = v @ w_out_conf
    gates = jax.nn.sigmoid(gl.reshape(bsz, l, N_BRANCHES, D_MODEL))
    out = (gates[:, :, 0] * y_a + gates[:, :, 1] * y_b) @ w_o
    return out, final


def route(h, router_w, router_bias):
    t = h.shape[0]
    s = jax.nn.sigmoid(h.astype(jnp.float32) @ router_w.astype(jnp.float32))
    sb = s + router_bias.astype(jnp.float32)
    grp_score = lax.top_k(sb.reshape(t, N_EXPERT_GROUPS, N_EXPERTS // N_EXPERT_GROUPS), 2)[0].sum(-1)
    _, gidx = lax.top_k(grp_score, TOPK_GROUPS)
    gmask = jnp.any(gidx[:, :, None] == jnp.arange(N_EXPERT_GROUPS)[None, None, :], axis=1)
    emask = jnp.repeat(gmask, N_EXPERTS // N_EXPERT_GROUPS, axis=1)
    _, idx = lax.top_k(jnp.where(emask, sb, -jnp.inf), TOP_K)
    w = jnp.take_along_axis(s, idx, axis=1)
    w = w / jnp.sum(w, axis=-1, keepdims=True) * ROUTED_SCALE
    return idx, w


def moe(h, router_w, router_bias, exp_w_gate, exp_w_up, exp_w_down, sh_w_gate, sh_w_up, sh_w_down):
    t, d = h.shape
    idx, wts = route(h, router_w, router_bias)
    n_assign = t * TOP_K
    e_flat = idx.reshape(n_assign)
    tok_flat = jnp.repeat(jnp.arange(t, dtype=jnp.int32), TOP_K)
    order = jnp.argsort(e_flat)
    e_sorted = e_flat[order]
    tok_sorted = tok_flat[order]
    w_sorted = wts.reshape(n_assign)[order]
    counts = jnp.bincount(e_flat, length=N_EXPERTS)
    padded = (counts + MOE_BLOCK - 1) // MOE_BLOCK * MOE_BLOCK
    pad_end = jnp.cumsum(padded)
    pad_start = pad_end - padded
    start = jnp.cumsum(counts) - counts
    dest = pad_start[e_sorted] + jnp.arange(n_assign, dtype=jnp.int32) - start[e_sorted]
    n_blocks = -(-n_assign // MOE_BLOCK) + N_EXPERTS
    n_slots = n_blocks * MOE_BLOCK
    slot_tok = jnp.full((n_slots,), t, dtype=jnp.int32).at[dest].set(tok_sorted)
    slot_w = jnp.zeros((n_slots,), jnp.float32).at[dest].set(w_sorted)
    block_expert = jnp.minimum(
        jnp.searchsorted(pad_end, jnp.arange(n_blocks, dtype=jnp.int32) * MOE_BLOCK, side='right'),
        N_EXPERTS - 1)
    h_pad = jnp.concatenate([h, jnp.zeros((1, d), h.dtype)], axis=0)

    def expert_block(args):
        tok, e = args
        xb = h_pad[tok]
        return (jax.nn.silu(xb @ exp_w_gate[e]) * (xb @ exp_w_up[e])) @ exp_w_down[e]

    yb = lax.map(expert_block, (slot_tok.reshape(n_blocks, MOE_BLOCK), block_expert))
    y = yb.reshape(n_slots, d) * slot_w[:, None].astype(h.dtype)
    routed = jnp.zeros((t + 1, d), h.dtype).at[slot_tok].add(y)[:t]
    shared = (jax.nn.silu(h @ sh_w_gate) * (h @ sh_w_up)) @ sh_w_down
    return routed + shared


def modulation(cvec, w_mod, b_mod):
    m = jax.nn.silu(cvec) @ w_mod + b_mod
    return jnp.split(m[..., None, :], N_MOD, axis=-1)


def trunk_layer(x, mod, init_state, grid, norm1_w, norm2_w, w_in, ssm_conv_w, ssm_conv_b, ssm_dt_bias,
                ssm_a_log, ssm_d, ssm_norm_w, w_out_ssm, conf_conv_w, conf_conv_b, conf_ln_w, conf_ln_b,
                w_out_conf, w_o, router_w, router_bias, exp_w_gate, exp_w_up, exp_w_down,
                sh_w_gate, sh_w_up, sh_w_down):
    sh1, sc1, g1, sh2, sc2, g2 = mod
    u = rmsnorm(x, norm1_w) * (1 + sc1) + sh1
    mix, final = mixer(u, init_state, grid, w_in, ssm_conv_w, ssm_conv_b, ssm_dt_bias, ssm_a_log, ssm_d,
                       ssm_norm_w, w_out_ssm, conf_conv_w, conf_conv_b, conf_ln_w, conf_ln_b, w_out_conf, w_o)
    x = x + g1 * mix
    v = rmsnorm(x, norm2_w) * (1 + sc2) + sh2
    b, l, d = v.shape
    ff = moe(v.reshape(b * l, d), router_w, router_bias, exp_w_gate, exp_w_up, exp_w_down,
             sh_w_gate, sh_w_up, sh_w_down).reshape(b, l, d)
    return x + g2 * ff, final


def setup_inputs(seed: int = 0) -> dict:
    key = jax.random.key(seed)
    ks = jax.random.split(key, 32)
    nrm = lambda k, shape, scale: jax.random.normal(k, shape, jnp.float32) * scale
    gain = lambda k, shape: 1.0 + 0.05 * jax.random.normal(k, shape, jnp.float32)
    dt0 = jnp.exp(jax.random.uniform(ks[10], (DEPTH, 2, N_HEADS), jnp.float32,
                                     np.log(1e-3), np.log(1e-1)))
    return {
        'x_prompt': nrm(ks[0], (BATCH, SEQ, D_MODEL), 1.0),
        'x_sample': nrm(ks[1], (DEC_BATCH, DEC_SEQ, D_MODEL), 1.0),
        'state_ssm': nrm(ks[2], (DEC_BATCH, DEPTH, 2, N_HEADS, HEAD_DIM, D_STATE), 0.5),
        'c': nrm(ks[3], (DEC_BATCH, D_MODEL), 1.0),
        'c_ctx': nrm(ks[4], (D_MODEL,), 1.0),
        'norm1_w': gain(ks[5], (DEPTH, D_MODEL)),
        'norm2_w': gain(ks[6], (DEPTH, D_MODEL)),
        'w_mod': nrm(ks[7], (DEPTH, D_MODEL, N_MOD * D_MODEL), 0.5 * D_MODEL ** -0.5),
        'b_mod': nrm(ks[8], (DEPTH, N_MOD * D_MODEL), 0.02),
        'w_in': nrm(ks[9], (DEPTH, D_MODEL, P_TOTAL), D_MODEL ** -0.5),
        'ssm_conv_w': nrm(ks[11], (DEPTH, D_CONV_SSM, CONV_DIM), D_CONV_SSM ** -0.5),
        'ssm_conv_b': nrm(ks[12], (DEPTH, CONV_DIM), 0.02),
        'ssm_dt_bias': dt0 + jnp.log(-jnp.expm1(-dt0)),
        'ssm_a_log': jnp.log(jax.random.uniform(ks[13], (DEPTH, 2, N_HEADS), jnp.float32, 1.0, 16.0)),
        'ssm_d': gain(ks[14], (DEPTH, N_HEADS)),
        'ssm_norm_w': gain(ks[15], (DEPTH, D_INNER)),
        'w_out_ssm': nrm(ks[16], (DEPTH, D_INNER, D_MODEL), D_INNER ** -0.5),
        'conf_conv_w': nrm(ks[17], (DEPTH, CONF_K, D_CONF), CONF_K ** -0.5),
        'conf_conv_b': nrm(ks[18], (DEPTH, D_CONF), 0.02),
        'conf_ln_w': gain(ks[19], (DEPTH, D_CONF)),
        'conf_ln_b': nrm(ks[20], (DEPTH, D_CONF), 0.02),
        'w_out_conf': nrm(ks[21], (DEPTH, D_CONF, D_MODEL), D_CONF ** -0.5),
        'w_o': nrm(ks[22], (DEPTH, D_MODEL, D_MODEL), D_MODEL ** -0.5),
        'router_w': nrm(ks[23], (DEPTH, D_MODEL, N_EXPERTS), D_MODEL ** -0.5),
        'router_bias': nrm(ks[24], (DEPTH, N_EXPERTS), 0.01),
        'exp_w_gate': nrm(ks[25], (DEPTH, N_EXPERTS, D_MODEL, D_EXPERT), D_MODEL ** -0.5),
        'exp_w_up': nrm(ks[26], (DEPTH, N_EXPERTS, D_MODEL, D_EXPERT), D_MODEL ** -0.5),
        'exp_w_down': nrm(ks[27], (DEPTH, N_EXPERTS, D_EXPERT, D_MODEL), D_EXPERT ** -0.5),
        'sh_w_gate': nrm(ks[28], (DEPTH, D_MODEL, D_SHARED), D_MODEL ** -0.5),
        'sh_w_up': nrm(ks[29], (DEPTH, D_MODEL, D_SHARED), D_MODEL ** -0.5),
        'sh_w_down': nrm(ks[30], (DEPTH, D_SHARED, D_MODEL), D_SHARED ** -0.5),
        'final_norm_w': gain(ks[31], (D_MODEL,)),
    }


def reference(x_prompt, x_sample, state_ssm, c, c_ctx, norm1_w, norm2_w, w_mod, b_mod, w_in,
              ssm_conv_w, ssm_conv_b, ssm_dt_bias, ssm_a_log, ssm_d, ssm_norm_w, w_out_ssm,
              conf_conv_w, conf_conv_b, conf_ln_w, conf_ln_b, w_out_conf, w_o, router_w, router_bias,
              exp_w_gate, exp_w_up, exp_w_down, sh_w_gate, sh_w_up, sh_w_down, final_norm_w):
    x = x_prompt
    xl = x_sample
    zero_init = jnp.zeros((x.shape[0], 2, N_HEADS, HEAD_DIM, D_STATE), x.dtype)
    ctx_states = []
    for i in range(DEPTH):
        lp = (norm1_w[i], norm2_w[i], w_in[i], ssm_conv_w[i], ssm_conv_b[i], ssm_dt_bias[i], ssm_a_log[i],
              ssm_d[i], ssm_norm_w[i], w_out_ssm[i], conf_conv_w[i], conf_conv_b[i], conf_ln_w[i],
              conf_ln_b[i], w_out_conf[i], w_o[i], router_w[i], router_bias[i], exp_w_gate[i],
              exp_w_up[i], exp_w_down[i], sh_w_gate[i], sh_w_up[i], sh_w_down[i])
        x, fin = trunk_layer(x, modulation(c_ctx, w_mod[i], b_mod[i]), zero_init, False, *lp)
        ctx_states.append(fin)
        xl, _ = trunk_layer(xl, modulation(c, w_mod[i], b_mod[i]), state_ssm[:, i], True, *lp)
    y_prompt = rmsnorm(x, final_norm_w)
    y_sample = rmsnorm(xl, final_norm_w)
    new_state_ssm = jnp.stack(ctx_states, axis=1)
    return (y_prompt, y_sample, new_state_ssm)
```

```python
import functools

import jax
import jax.numpy as jnp
from jax import lax
from jax.experimental import pallas as pl
from jax.experimental.pallas import tpu as pltpu

F32 = jnp.float32
BF16 = jnp.bfloat16
I32 = jnp.int32

D_MODEL = 1024
BATCH = 32
SEQ = 256
DEC_BATCH = 8
DEC_SEQ = 1024
GRID_W = 64
GRID_H = DEC_SEQ // GRID_W
D_INNER = 2048
HEAD_DIM = 64
N_HEADS = 32
D_STATE = 128
N_GROUPS = 8
HEADS_PER_GROUP = N_HEADS // N_GROUPS
GROUP_W = HEADS_PER_GROUP * HEAD_DIM
D_CONV_SSM = 5
CHUNK = 128
CONV_DIM = D_INNER + 2 * N_GROUPS * D_STATE
CONF_K = 31
CONF_PAD = CONF_K // 2
N_EXPERTS = 256
TOP_K = 8
N_EXPERT_GROUPS = 8
EXPERTS_PER_GROUP = N_EXPERTS // N_EXPERT_GROUPS
TOPK_GROUPS = 4
D_EXPERT = 256
D_SHARED = 256
ROUTED_SCALE = 2.5
N_MOD = 6
EPS = 1e-6

T_PROMPT = BATCH * SEQ
T_SAMPLE = DEC_BATCH * DEC_SEQ
T_ALL = T_PROMPT + T_SAMPLE
N_ASSIGN = T_ALL * TOP_K
MOD_ROWS = 16
CTX_ROW = DEC_BATCH

SUBLANES = 8
LANES = 128
VMEM_LIMIT = 56 * 1024 * 1024

PROJ_W = CONV_DIM + D_INNER + 2 * D_MODEL + 2 * D_MODEL
COL_Z = CONV_DIM // D_INNER
COL_GL = COL_Z + 1
COL_CV = (CONV_DIM + 2 * D_INNER) // D_MODEL
COL_CG = COL_CV + 1

ROW_PAD = SUBLANES
GMM_TM = 256
N_SLOTS = N_ASSIGN + N_EXPERTS * ROW_PAD
N_ROWS = N_SLOTS + GMM_TM
N_TILES = N_ASSIGN // GMM_TM + N_EXPERTS
TAIL_SIZES = (128, 64, 32, 16, 8)


def _cparams(sem, vmem=VMEM_LIMIT):
    return pltpu.CompilerParams(dimension_semantics=sem, vmem_limit_bytes=vmem)


def _silu(x):
    return x * jax.nn.sigmoid(x)


def _split2(x):
    hi = x.astype(BF16)
    lo = (x - hi.astype(F32)).astype(BF16)
    return hi, lo


def _split3(x):
    b1 = x.astype(BF16)
    r = x - b1.astype(F32)
    b2 = r.astype(BF16)
    b3 = (r - b2.astype(F32)).astype(BF16)
    return b1, b2, b3


def _dot(a, b):
    return jnp.dot(a, b, preferred_element_type=F32)


def _dot_exact_lhs(a_exact, b):
    b1, b2, b3 = _split3(b)
    return _dot(a_exact, b1) + _dot(a_exact, b2) + _dot(a_exact, b3)


def _dot_exact_rhs(a, b_exact):
    a1, a2, a3 = _split3(a)
    return _dot(a1, b_exact) + _dot(a2, b_exact) + _dot(a3, b_exact)


def _dot_f32(a, b):
    a1, a2 = _split2(a)
    b1, b2 = _split2(b)
    return _dot(a1, b1) + _dot(a1, b2) + _dot(a2, b1)


def _mod_row_of_block(i, blocks_prompt, blocks_per_sample_seq):
    return jnp.where(i < blocks_prompt, CTX_ROW, (i - blocks_prompt) // blocks_per_sample_seq)


def _mod_kernel(c_ref, w_ref, b_ref, o_ref):
    c = c_ref[...]
    o_ref[...] = _dot_f32(_silu(c), w_ref[...]) + b_ref[...]


def _modulation(cc, w_mod, b_mod):
    tn = 512
    n = N_MOD * D_MODEL
    return pl.pallas_call(
        _mod_kernel,
        out_shape=jax.ShapeDtypeStruct((MOD_ROWS, n), F32),
        grid=(n // tn,),
        in_specs=[pl.BlockSpec((MOD_ROWS, D_MODEL), lambda j: (0, 0)),
                  pl.BlockSpec((D_MODEL, tn), lambda j: (0, j)),
                  pl.BlockSpec((1, tn), lambda j: (0, j))],
        out_specs=pl.BlockSpec((MOD_ROWS, tn), lambda j: (0, j)),
        compiler_params=_cparams(("arbitrary",)),
        name="modulation",
    )(cc, w_mod, b_mod.reshape(1, n))


INPROJ_TM = 1024
INPROJ_TN = 1024


def _inproj_kernel(x_ref, mod_ref, n1_ref, w_ref, wdt_ref, o_ref, dt_ref, u_scr):
    @pl.when(pl.program_id(1) == 0)
    def _():
        x = x_ref[...]
        ms = jnp.mean(x * x, axis=-1, keepdims=True)
        y = x * lax.rsqrt(ms + EPS) * n1_ref[...]
        m = mod_ref[...]
        u = (y * (1.0 + m[1:2]) + m[0:1]).astype(BF16)
        u_scr[...] = u
        dt_ref[...] = _dot(u, wdt_ref[...])

    o_ref[...] = _dot(u_scr[...], w_ref[...]).astype(BF16)


def _in_projection(x_all, mod3, norm1_w, w_main, w_dt):
    tm, tn = INPROJ_TM, INPROJ_TN
    row = functools.partial(_mod_row_of_block, blocks_prompt=T_PROMPT // tm,
                            blocks_per_sample_seq=DEC_SEQ // tm)
    return pl.pallas_call(
        _inproj_kernel,
        out_shape=(jax.ShapeDtypeStruct((T_ALL, PROJ_W), BF16),
                   jax.ShapeDtypeStruct((T_ALL, LANES), F32)),
        grid=(T_ALL // tm, PROJ_W // tn),
        in_specs=[pl.BlockSpec((tm, D_MODEL), lambda i, j: (i, 0)),
                  pl.BlockSpec((None, N_MOD, D_MODEL), lambda i, j: (row(i), 0, 0)),
                  pl.BlockSpec((1, D_MODEL), lambda i, j: (0, 0)),
                  pl.BlockSpec((D_MODEL, tn), lambda i, j: (0, j)),
                  pl.BlockSpec((D_MODEL, LANES), lambda i, j: (0, 0))],
        out_specs=(pl.BlockSpec((tm, tn), lambda i, j: (i, j)),
                   pl.BlockSpec((tm, LANES), lambda i, j: (i, 0))),
        scratch_shapes=[pltpu.VMEM((tm, D_MODEL), BF16)],
        compiler_params=_cparams(("arbitrary", "arbitrary")),
        name="in_projection",
    )(x_all, mod3, norm1_w.reshape(1, D_MODEL), w_main, w_dt)


SSMCONV_TN = 512


def _ssmconv_kernel(x_ref, w_ref, b_ref, o_ref, *, seq_len):
    x = x_ref[...].astype(F32)
    row = lax.broadcasted_iota(I32, x.shape, 0)
    half = D_CONV_SSM // 2
    acc = b_ref[...] + w_ref[half:half + 1, :] * x
    for k in range(D_CONV_SSM):
        s = k - half
        if s == 0:
            continue
        shifted = pltpu.roll(x, shift=(-s) % seq_len, axis=0)
        valid = jnp.logical_and(row + s >= 0, row + s < seq_len)
        acc = acc + w_ref[k:k + 1, :] * jnp.where(valid, shifted, 0.0)
    o_ref[...] = _silu(acc).astype(BF16)


def _ssm_conv(proj, conv_w, conv_b, *, seq_len, n_seq, row_block0):
    tn = SSMCONV_TN
    return pl.pallas_call(
        functools.partial(_ssmconv_kernel, seq_len=seq_len),
        out_shape=jax.ShapeDtypeStruct((n_seq * seq_len, CONV_DIM), BF16),
        grid=(n_seq, CONV_DIM // tn),
        in_specs=[pl.BlockSpec((seq_len, tn), lambda b, j: (row_block0 + b, j)),
                  pl.BlockSpec((D_CONV_SSM, tn), lambda b, j: (0, j)),
                  pl.BlockSpec((1, tn), lambda b, j: (0, j))],
        out_specs=pl.BlockSpec((seq_len, tn), lambda b, j: (b, j)),
        compiler_params=_cparams(("arbitrary", "arbitrary")),
        name="ssm_conv",
    )(proj, conv_w, conv_b.reshape(1, CONV_DIM))


def _tri_masks():
    ii = lax.broadcasted_iota(I32, (CHUNK, CHUNK), 0)
    jj = lax.broadcasted_iota(I32, (CHUNK, CHUNK), 1)
    return ii, jj


def _chunk_decays(dt_ref, dtb_ref, a_ref):
    ii, jj = _tri_masks()
    pre = dt_ref[...] + dtb_ref[...]
    dt = jnp.maximum(pre, 0.0) + jnp.log(1.0 + jnp.exp(-jnp.abs(pre)))
    la = dt * a_ref[...]
    tri_lo = jnp.where(jj <= ii, 1.0, 0.0).astype(BF16)
    tri_up = jnp.where(jj >= ii, 1.0, 0.0).astype(BF16)
    cs_prefix = _dot_exact_lhs(tri_lo, la)
    cs_suffix = _dot_exact_lhs(tri_up, la)
    fwd_lane = lax.broadcasted_iota(I32, (CHUNK, LANES), 1) < N_HEADS
    cs = jnp.where(fwd_lane, cs_prefix, cs_suffix)
    tot = jnp.where(fwd_lane[0:1], cs_prefix[CHUNK - 1:CHUNK, :], cs_suffix[0:1, :])
    return dt, cs, tot


def _transpose_blocks(src, rows, cols):
    out_rows = []
    for cb in range(cols // LANES):
        pieces = [src[rb * LANES:(rb + 1) * LANES, cb * LANES:(cb + 1) * LANES].T
                  for rb in range(rows // LANES)]
        out_rows.append(jnp.concatenate(pieces, axis=1) if len(pieces) > 1 else pieces[0])
    return jnp.concatenate(out_rows, axis=0) if len(out_rows) > 1 else out_rows[0]


def _ssd_kernel(*refs, n_chunks, has_init, want_final):
    it = iter(refs)
    xbc_ref, dt_ref, z_ref = next(it), next(it), next(it)
    init_ref = next(it) if has_init else None
    dtb_ref, a_ref, dvec_ref, nw_ref, e2_ref = next(it), next(it), next(it), next(it), next(it)
    out_ref = next(it)
    fin_ref = next(it) if want_final else None
    sf_scr, sb_scr, df_scr, db_scr = next(it), next(it), next(it), next(it)

    phase = pl.program_id(1)
    c = pl.program_id(2)

    @pl.when(phase == 0)
    def _chunk_states():
        dt, cs, tot = _chunk_decays(dt_ref, dtb_ref, a_ref)
        w_in = dt * jnp.exp(tot - cs)
        w_exp = _dot_exact_rhs(w_in, e2_ref[...])
        dec = jnp.exp(jnp.broadcast_to(tot, (SUBLANES, LANES)))
        dec_exp = _dot_exact_rhs(dec, e2_ref[...])
        df_scr[c] = dec_exp[:, :D_INNER]
        db_scr[c] = dec_exp[:, D_INNER:]
        for g in range(N_GROUPS):
            lo = g * GROUP_W
            xg = xbc_ref[:, lo:lo + GROUP_W].astype(F32)
            xd_f = (xg * w_exp[:, lo:lo + GROUP_W]).astype(BF16)
            xd_b = (xg * w_exp[:, D_INNER + lo:D_INNER + lo + GROUP_W]).astype(BF16)
            bg = xbc_ref[:, D_INNER + g * D_STATE:D_INNER + (g + 1) * D_STATE]
            bg_t = bg.astype(F32).T.astype(BF16)
            sf_scr[c, :, lo:lo + GROUP_W] = _dot(bg_t, xd_f)
            sb_scr[c, :, lo:lo + GROUP_W] = _dot(bg_t, xd_b)

    @pl.when(jnp.logical_and(phase == 1, c == 0))
    def _recurrence():
        for g in range(N_GROUPS):
            lo = g * GROUP_W
            if has_init:
                prev_f = _transpose_blocks(init_ref[0, 0, lo:lo + GROUP_W, :], GROUP_W, D_STATE)
                prev_b = _transpose_blocks(init_ref[0, 1, lo:lo + GROUP_W, :], GROUP_W, D_STATE)
            else:
                prev_f = jnp.zeros((D_STATE, GROUP_W), F32)
                prev_b = jnp.zeros((D_STATE, GROUP_W), F32)
            for cc in range(n_chunks):
                s = sf_scr[cc, :, lo:lo + GROUP_W]
                sf_scr[cc, :, lo:lo + GROUP_W] = prev_f
                prev_f = df_scr[cc, 0:1, lo:lo + GROUP_W] * prev_f + s
            for cc in reversed(range(n_chunks)):
                s = sb_scr[cc, :, lo:lo + GROUP_W]
                sb_scr[cc, :, lo:lo + GROUP_W] = prev_b
                prev_b = db_scr[cc, 0:1, lo:lo + GROUP_W] * prev_b + s
            if want_final:
                fin_ref[0, 0, lo:lo + GROUP_W, :] = _transpose_blocks(prev_f, D_STATE, GROUP_W)
                fin_ref[0, 1, lo:lo + GROUP_W, :] = _transpose_blocks(prev_b, D_STATE, GROUP_W)

    @pl.when(phase == 1)
    def _outputs():
        ii, jj = _tri_masks()
        dt, cs, _ = _chunk_decays(dt_ref, dtb_ref, a_ref)
        out_dec = _dot_exact_rhs(jnp.exp(cs), e2_ref[...])
        cs_t = cs.T
        dt_t = dt.T
        causal = ii >= jj
        anti = jj >= ii
        neg = jnp.float32(-1e30)
        left = lax.broadcasted_iota(I32, (CHUNK, LANES), 1) < HEAD_DIM
        for g in range(N_GROUPS):
            lo = g * GROUP_W
            bg = xbc_ref[:, D_INNER + g * D_STATE:D_INNER + (g + 1) * D_STATE]
            cg = xbc_ref[:, D_INNER + N_GROUPS * D_STATE + g * D_STATE:
                         D_INNER + N_GROUPS * D_STATE + (g + 1) * D_STATE]
            cb = lax.dot_general(cg, bg, (((1,), (1,)), ((), ())), preferred_element_type=F32)
            pf = sf_scr[c, :, lo:lo + GROUP_W].astype(BF16)
            pb = sb_scr[c, :, lo:lo + GROUP_W].astype(BF16)
            y_off = (_dot(cg, pf) * out_dec[:, lo:lo + GROUP_W]
                     + _dot(cg, pb) * out_dec[:, D_INNER + lo:D_INNER + lo + GROUP_W])
            pairs = []
            for m in range(HEADS_PER_GROUP // 2):
                x_pair = xbc_ref[:, lo + m * LANES:lo + (m + 1) * LANES]
                ys = []
                for hh in range(2):
                    h = g * HEADS_PER_GROUP + 2 * m + hh
                    hb = N_HEADS + h
                    seg_f = jnp.where(causal, cs[:, h:h + 1] - cs_t[h:h + 1, :], neg)
                    seg_b = jnp.where(anti, cs[:, hb:hb + 1] - cs_t[hb:hb + 1, :], neg)
                    mix = (jnp.exp(seg_f) * dt_t[h:h + 1, :] + jnp.exp(seg_b) * dt_t[hb:hb + 1, :])
                    ys.append(_dot((cb * mix).astype(BF16), x_pair))
                pairs.append(jnp.where(left, ys[0], ys[1]))
            y_diag = jnp.concatenate(pairs, axis=1)
            xg = xbc_ref[:, lo:lo + GROUP_W].astype(F32)
            y = y_diag + y_off + dvec_ref[:, lo:lo + GROUP_W] * xg
            zg = z_ref[:, lo:lo + GROUP_W].astype(F32)
            y = y * _silu(zg)
            ms = jnp.mean(y * y, axis=-1, keepdims=True)
            out_ref[:, lo:lo + GROUP_W] = (y * lax.rsqrt(ms + EPS) * nw_ref[:, lo:lo + GROUP_W]).astype(BF16)


def _ssd(xbc, dt_raw, proj, init, dtb, a_neg, dvec, norm_w, e2, *, seq_len, n_seq, tok0, want_final):
    nc = seq_len // CHUNK
    blk0 = tok0 // CHUNK
    has_init = init is not None
    chunk_map = lambda b, p, c: (b * nc + c, 0)
    in_specs = [pl.BlockSpec((CHUNK, CONV_DIM), chunk_map),
                pl.BlockSpec((CHUNK, LANES), lambda b, p, c: (blk0 + b * nc + c, 0)),
                pl.BlockSpec((CHUNK, D_INNER), lambda b, p, c: (blk0 + b * nc + c * p, COL_Z))]
    args = [xbc, dt_raw, proj]
    if has_init:
        in_specs.append(pl.BlockSpec((1, 2, D_INNER, D_STATE), lambda b, p, c: (b, 0, 0, 0)))
        args.append(init)
    const = lambda b, p, c: (0, 0)
    in_specs += [pl.BlockSpec((1, LANES), const), pl.BlockSpec((1, LANES), const),
                 pl.BlockSpec((1, D_INNER), const), pl.BlockSpec((1, D_INNER), const),
                 pl.BlockSpec((LANES, 2 * D_INNER), const)]
    args += [dtb, a_neg, dvec, norm_w, e2]
    out_shape = [jax.ShapeDtypeStruct((n_seq * seq_len, D_INNER), BF16)]
    out_specs = [pl.BlockSpec((CHUNK, D_INNER), lambda b, p, c: (b * nc + c * p, 0))]
    if want_final:
        out_shape.append(jax.ShapeDtypeStruct((n_seq, 2, D_INNER, D_STATE), F32))
        out_specs.append(pl.BlockSpec((1, 2, D_INNER, D_STATE), lambda b, p, c: (b, 0, 0, 0)))
    res = pl.pallas_call(
        functools.partial(_ssd_kernel, n_chunks=nc, has_init=has_init, want_final=want_final),
        out_shape=tuple(out_shape),
        grid=(n_seq, 2, nc),
        in_specs=in_specs,
        out_specs=tuple(out_specs),
        scratch_shapes=[pltpu.VMEM((nc, D_STATE, D_INNER), F32),
                        pltpu.VMEM((nc, D_STATE, D_INNER), F32),
                        pltpu.VMEM((nc, SUBLANES, D_INNER), F32),
                        pltpu.VMEM((nc, SUBLANES, D_INNER), F32)],
        compiler_params=_cparams(("arbitrary", "arbitrary", "arbitrary")),
        name="ssd_final" if want_final else "ssd_init",
    )(*args)
    return res


def _layernorm_silu(v, lw_ref, lb_ref):
    mu = jnp.mean(v, axis=-1, keepdims=True)
    d = v - mu
    var = jnp.mean(d * d, axis=-1, keepdims=True)
    return _silu(d * lax.rsqrt(var + EPS) * lw_ref[...] + lb_ref[...])


def _conf_seq_kernel(cv_ref, cg_ref, w_ref, b_ref, lw_ref, lb_ref, o_ref, pad_scr, acc_scr):
    seq_len = cv_ref.shape[0]
    base = 2 * SUBLANES
    v = cv_ref[...].astype(F32) * jax.nn.sigmoid(cg_ref[...].astype(F32))
    pad_scr[0:base, :] = jnp.zeros((base, D_MODEL), F32)
    pad_scr[base + seq_len:base + seq_len + base, :] = jnp.zeros((base, D_MODEL), F32)
    pad_scr[base:base + seq_len, :] = v
    rt, ct = 128, 256
    for r0 in range(0, seq_len, rt):
        for c0 in range(0, D_MODEL, ct):
            acc = jnp.broadcast_to(b_ref[:, c0:c0 + ct], (rt, ct))
            for k in range(CONF_K):
                off = base + r0 + k - CONF_PAD
                acc = acc + w_ref[k:k + 1, c0:c0 + ct] * pad_scr[off:off + rt, c0:c0 + ct]
            acc_scr[r0:r0 + rt, c0:c0 + ct] = acc
    o_ref[...] = _layernorm_silu(acc_scr[...], lw_ref, lb_ref).astype(BF16)


def _conf_grid_kernel(cv_ref, cg_ref, w_ref, b_ref, lw_ref, lb_ref, o_ref, v_scr, pad_scr, acc_scr):
    half = D_MODEL // 2
    base = 2 * SUBLANES
    stride = GRID_W + 2 * base
    v_scr[...] = cv_ref[...].astype(F32) * jax.nn.sigmoid(cg_ref[...].astype(F32))
    pad_scr[...] = jnp.zeros(pad_scr.shape, F32)
    for r in range(GRID_H):
        pad_scr[r * stride + base:r * stride + base + GRID_W, :] = v_scr[r * GRID_W:(r + 1) * GRID_W, 0:half]
    ct = 256
    for r in range(GRID_H):
        for c0 in range(0, half, ct):
            acc = jnp.broadcast_to(b_ref[:, c0:c0 + ct], (GRID_W, ct))
            for k in range(CONF_K):
                off = r * stride + base + k - CONF_PAD
                acc = acc + w_ref[k:k + 1, c0:c0 + ct] * pad_scr[off:off + GRID_W, c0:c0 + ct]
            acc_scr[r * GRID_W:(r + 1) * GRID_W, c0:c0 + ct] = acc
        for c0 in range(half, D_MODEL, ct):
            acc = jnp.broadcast_to(b_ref[:, c0:c0 + ct], (GRID_W, ct))
            for r2 in range(GRID_H):
                k = r2 - r + CONF_PAD
                acc = acc + w_ref[k:k + 1, c0:c0 + ct] * v_scr[r2 * GRID_W:(r2 + 1) * GRID_W, c0:c0 + ct]
            acc_scr[r * GRID_W:(r + 1) * GRID_W, c0:c0 + ct] = acc
    o_ref[...] = _layernorm_silu(acc_scr[...], lw_ref, lb_ref).astype(BF16)


def _conformer(proj, conv_w, conv_b, ln_w, ln_b, *, seq_len, n_seq, row_block0, grid_mode):
    base = 2 * SUBLANES
    if grid_mode:
        body = _conf_grid_kernel
        scratch = [pltpu.VMEM((seq_len, D_MODEL), F32),
                   pltpu.VMEM((GRID_H * (GRID_W + 2 * base), D_MODEL // 2), F32),
                   pltpu.VMEM((seq_len, D_MODEL), F32)]
    else:
        body = _conf_seq_kernel
        scratch = [pltpu.VMEM((seq_len + 2 * base, D_MODEL), F32),
                   pltpu.VMEM((seq_len, D_MODEL), F32)]
    const = lambda b: (0, 0)
    return pl.pallas_call(
        body,
        out_shape=jax.ShapeDtypeStruct((n_seq * seq_len, D_MODEL), BF16),
        grid=(n_seq,),
        in_specs=[pl.BlockSpec((seq_len, D_MODEL), lambda b: (row_block0 + b, COL_CV)),
                  pl.BlockSpec((seq_len, D_MODEL), lambda b: (row_block0 + b, COL_CG)),
                  pl.BlockSpec((CONF_K, D_MODEL), const),
                  pl.BlockSpec((1, D_MODEL), const),
                  pl.BlockSpec((1, D_MODEL), const),
                  pl.BlockSpec((1, D_MODEL), const)],
        out_specs=pl.BlockSpec((seq_len, D_MODEL), lambda b: (b, 0)),
        scratch_shapes=scratch,
        compiler_params=_cparams(("arbitrary",)),
        name="conformer_grid" if grid_mode else "conformer_seq",
    )(proj, proj, conv_w, conv_b.reshape(1, D_MODEL), ln_w.reshape(1, D_MODEL), ln_b.reshape(1, D_MODEL))


MERGE_TM = 256


def _merge_kernel(ya_ref, yb_ref, gl_ref, x_ref, mod_ref, wa_ref, wb_ref, wo_ref, n2_ref, rw_ref,
                  sgu_ref, sd_ref, pre_ref, v_ref, s_ref):
    y_a = _dot(ya_ref[...], wa_ref[...])
    y_b = _dot(yb_ref[...], wb_ref[...])
    gates = jax.nn.sigmoid(gl_ref[...].astype(F32))
    mix = gates[:, :D_MODEL] * y_a + gates[:, D_MODEL:] * y_b
    out = _dot(mix.astype(BF16), wo_ref[...])
    m = mod_ref[...]
    x1 = x_ref[...] + m[2:3] * out
    ms = jnp.mean(x1 * x1, axis=-1, keepdims=True)
    v = x1 * lax.rsqrt(ms + EPS) * n2_ref[...] * (1.0 + m[4:5]) + m[3:4]
    v_ref[...] = v
    s_ref[...] = jax.nn.sigmoid(_dot_f32(v, rw_ref[...]))
    hgu = _dot(v.astype(BF16), sgu_ref[...])
    act = _silu(hgu[:, :D_SHARED]) * hgu[:, D_SHARED:]
    shared = _dot(act.astype(BF16), sd_ref[...])
    pre_ref[...] = x1 + m[5:6] * shared


def _merge(y_ssm, v_conf, proj, x_all, mod3, w_out_ssm, w_out_conf, w_o, norm2_w, router_w, sh_gu, sh_d):
    tm = MERGE_TM
    row = functools.partial(_mod_row_of_block, blocks_prompt=T_PROMPT // tm,
                            blocks_per_sample_seq=DEC_SEQ // tm)
    const = lambda i: (0, 0)
    return pl.pallas_call(
        _merge_kernel,
        out_shape=(jax.ShapeDtypeStruct((T_ALL, D_MODEL), F32),
                   jax.ShapeDtypeStruct((T_ALL, D_MODEL), F32),
                   jax.ShapeDtypeStruct((T_ALL, N_EXPERTS), F32)),
        grid=(T_ALL // tm,),
        in_specs=[pl.BlockSpec((tm, D_INNER), lambda i: (i, 0)),
                  pl.BlockSpec((tm, D_MODEL), lambda i: (i, 0)),
                  pl.BlockSpec((tm, 2 * D_MODEL), lambda i: (i, COL_GL)),
                  pl.BlockSpec((tm, D_MODEL), lambda i: (i, 0)),
                  pl.BlockSpec((None, N_MOD, D_MODEL), lambda i: (row(i), 0, 0)),
                  pl.BlockSpec((D_INNER, D_MODEL), const),
                  pl.BlockSpec((D_MODEL, D_MODEL), const),
                  pl.BlockSpec((D_MODEL, D_MODEL), const),
                  pl.BlockSpec((1, D_MODEL), const),
                  pl.BlockSpec((D_MODEL, N_EXPERTS), const),
                  pl.BlockSpec((D_MODEL, 2 * D_SHARED), const),
                  pl.BlockSpec((D_SHARED, D_MODEL), const)],
        out_specs=(pl.BlockSpec((tm, D_MODEL), lambda i: (i, 0)),
                   pl.BlockSpec((tm, D_MODEL), lambda i: (i, 0)),
                   pl.BlockSpec((tm, N_EXPERTS), lambda i: (i, 0))),
        compiler_params=_cparams(("arbitrary",)),
        name="merge",
    )(y_ssm, v_conf, proj, x_all, mod3, w_out_ssm, w_out_conf, w_o, norm2_w.reshape(1, D_MODEL),
      router_w, sh_gu, sh_d)


ROUTE_TB = 256


def _first_index_of_max(vals, iota, sentinel):
    m = jnp.max(vals, axis=0, keepdims=True)
    idx = jnp.min(jnp.where(vals == m, iota, jnp.float32(sentinel)), axis=0, keepdims=True)
    return m, idx


def _route_kernel(s_ref, bias_ref, idx_ref, w_ref, rank_ref, cnt_ref, run_scr):
    i = pl.program_id(0)
    tb = ROUTE_TB
    neg = jnp.float32(-jnp.inf)

    @pl.when(i == 0)
    def _():
        run_scr[...] = jnp.zeros(run_scr.shape, F32)

    s_t = s_ref[...].T
    sb_t = s_t + bias_ref[...]
    eiota = lax.broadcasted_iota(I32, (N_EXPERTS, tb), 0).astype(F32)

    liota = lax.broadcasted_iota(I32, (EXPERTS_PER_GROUP, tb), 0).astype(F32)
    gscores = []
    for g in range(N_EXPERT_GROUPS):
        blk = sb_t[g * EXPERTS_PER_GROUP:(g + 1) * EXPERTS_PER_GROUP, :]
        m1, i1 = _first_index_of_max(blk, liota, EXPERTS_PER_GROUP)
        m2 = jnp.max(jnp.where(liota == i1, neg, blk), axis=0, keepdims=True)
        gscores.append(m1 + m2)
    gs = jnp.concatenate(gscores, axis=0)
    giota = lax.broadcasted_iota(I32, (N_EXPERT_GROUPS, tb), 0).astype(F32)
    gsel = jnp.zeros((N_EXPERT_GROUPS, tb), F32)
    for _ in range(TOPK_GROUPS):
        _, gi = _first_index_of_max(gs, giota, N_EXPERT_GROUPS)
        hit = giota == gi
        gsel = jnp.where(hit, 1.0, gsel)
        gs = jnp.where(hit, neg, gs)
    emask = jnp.concatenate(
        [jnp.broadcast_to(gsel[g:g + 1, :], (EXPERTS_PER_GROUP, tb)) for g in range(N_EXPERT_GROUPS)], axis=0)
    masked = jnp.where(emask > 0.5, sb_t, neg)

    onehots, idxs, wts = [], [], []
    for _ in range(TOP_K):
        _, ei = _first_index_of_max(masked, eiota, N_EXPERTS)
        hit = eiota == ei
        onehots.append(hit)
        idxs.append(ei)
        wts.append(jnp.sum(jnp.where(hit, s_t, 0.0), axis=0, keepdims=True))
        masked = jnp.where(hit, neg, masked)
    w = jnp.concatenate(wts, axis=0)
    w_ref[...] = w / jnp.sum(w, axis=0, keepdims=True) * ROUTED_SCALE
    idx_ref[...] = jnp.concatenate(idxs, axis=0).astype(I32)

    assign = jnp.zeros((N_EXPERTS, tb), F32)
    for hit in onehots:
        assign = jnp.where(hit, 1.0, assign)
    assign_b = assign.astype(BF16)
    ti = lax.broadcasted_iota(I32, (tb, tb), 0)
    tj = lax.broadcasted_iota(I32, (tb, tb), 1)
    before = jnp.where(ti < tj, 1.0, 0.0).astype(BF16)
    within = _dot(assign_b, before)
    run = run_scr[...]
    total = within + jnp.concatenate([run] * (tb // LANES), axis=1)
    rank_ref[...] = jnp.concatenate(
        [jnp.sum(jnp.where(hit, total, 0.0), axis=0, keepdims=True) for hit in onehots], axis=0).astype(I32)
    new_run = run + _dot(assign_b, jnp.ones((tb, LANES), BF16))
    run_scr[...] = new_run
    cnt_ref[...] = new_run


def _route(scores, router_bias):
    tb = ROUTE_TB
    return pl.pallas_call(
        _route_kernel,
        out_shape=(jax.ShapeDtypeStruct((TOP_K, T_ALL), I32),
                   jax.ShapeDtypeStruct((TOP_K, T_ALL), F32),
                   jax.ShapeDtypeStruct((TOP_K, T_ALL), I32),
                   jax.ShapeDtypeStruct((N_EXPERTS, LANES), F32)),
        grid=(T_ALL // tb,),
        in_specs=[pl.BlockSpec((tb, N_EXPERTS), lambda i: (i, 0)),
                  pl.BlockSpec((N_EXPERTS, 1), lambda i: (0, 0))],
        out_specs=(pl.BlockSpec((TOP_K, tb), lambda i: (0, i)),
                   pl.BlockSpec((TOP_K, tb), lambda i: (0, i)),
                   pl.BlockSpec((TOP_K, tb), lambda i: (0, i)),
                   pl.BlockSpec((N_EXPERTS, LANES), lambda i: (0, 0))),
        scratch_shapes=[pltpu.VMEM((N_EXPERTS, LANES), F32)],
        compiler_params=_cparams(("arbitrary",)),
        name="route",
    )(scores, router_bias.reshape(N_EXPERTS, 1))


def _dest_kernel(idx_ref, rank_ref, start_ref, dest_ref):
    tb = idx_ref.shape[1]
    eiota = lax.broadcasted_iota(I32, (N_EXPERTS, tb), 0)
    start = jnp.broadcast_to(start_ref[...], (N_EXPERTS, tb))
    idx = idx_ref[...]
    rows = [jnp.sum(jnp.where(eiota == idx[k:k + 1, :], start, 0.0), axis=0, keepdims=True)
            for k in range(TOP_K)]
    dest_ref[...] = jnp.concatenate(rows, axis=0).astype(I32) + rank_ref[...]


def _dest_slots(idx_t, rank_t, group_start):
    tb = 512
    return pl.pallas_call(
        _dest_kernel,
        out_shape=jax.ShapeDtypeStruct((TOP_K, T_ALL), I32),
        grid=(T_ALL // tb,),
        in_specs=[pl.BlockSpec((TOP_K, tb), lambda i: (0, i)),
                  pl.BlockSpec((TOP_K, tb), lambda i: (0, i)),
                  pl.BlockSpec((N_EXPERTS, 1), lambda i: (0, 0))],
        out_specs=pl.BlockSpec((TOP_K, tb), lambda i: (0, i)),
        compiler_params=_cparams(("arbitrary",)),
        name="dest_slots",
    )(idx_t, rank_t, group_start.astype(F32).reshape(N_EXPERTS, 1))


DISPATCH_TB = 256


def _dispatch_kernel(cnt_ref, start_ref, dest_ref, v_ref, xs_hbm, zero_scr, sem, zsem):
    i = pl.program_id(0)
    tb = DISPATCH_TB

    def row_copy(t, k):
        return pltpu.make_async_copy(v_ref.at[pl.ds(t, 1), :],
                                     xs_hbm.at[pl.ds(dest_ref[k, t], 1), :], sem)

    def issue(t, carry):
        for k in range(TOP_K):
            row_copy(t, k).start()
        return carry

    lax.fori_loop(0, tb, issue, 0)

    @pl.when(i == 0)
    def _():
        zero_scr[...] = jnp.zeros(zero_scr.shape, F32)

        def zero_row(r):
            return pltpu.make_async_copy(zero_scr.at[pl.ds(0, 1), :], xs_hbm.at[pl.ds(r, 1), :], zsem)

        def per_expert(e, carry):
            lo = start_ref[e] + cnt_ref[e]
            hi = start_ref[e + 1]

            def one(r, c2):
                cp = zero_row(r)
                cp.start()
                cp.wait()
                return c2

            return lax.fori_loop(lo, hi, one, carry)

        lax.fori_loop(0, N_EXPERTS, per_expert, 0)

        def slack(j, carry):
            r0 = pl.multiple_of(start_ref[N_EXPERTS] + j * SUBLANES, SUBLANES)
            cp = pltpu.make_async_copy(zero_scr, xs_hbm.at[pl.ds(r0, SUBLANES), :], zsem)
            cp.start()
            cp.wait()
            return carry

        lax.fori_loop(0, (N_ROWS - start_ref[N_EXPERTS]) // SUBLANES, slack, 0)

    def drain(t, carry):
        for k in range(TOP_K):
            row_copy(t, k).wait()
        return carry

    lax.fori_loop(0, tb, drain, 0)


def _dispatch(counts, group_start, dest_t, v_all):
    tb = DISPATCH_TB
    grid_spec = pltpu.PrefetchScalarGridSpec(
        num_scalar_prefetch=2,
        grid=(T_ALL // tb,),
        in_specs=[pl.BlockSpec((TOP_K, tb), lambda i, c, s: (0, i), memory_space=pltpu.SMEM),
                  pl.BlockSpec((tb, D_MODEL), lambda i, c, s: (i, 0))],
        out_specs=pl.BlockSpec(memory_space=pl.ANY),
        scratch_shapes=[pltpu.VMEM((SUBLANES, D_MODEL), F32),
                        pltpu.SemaphoreType.DMA(()),
                        pltpu.SemaphoreType.DMA(())],
    )
    return pl.pallas_call(
        _dispatch_kernel,
        out_shape=jax.ShapeDtypeStruct((N_ROWS, D_MODEL), F32),
        grid_spec=grid_spec,
        compiler_params=_cparams(("arbitrary",)),
        name="dispatch",
    )(counts, group_start, dest_t, v_all)


def _tile_writes(y_scr, y_hbm, slot, row0, valid, sem):
    full = pltpu.make_async_copy(y_scr.at[slot], y_hbm.at[pl.ds(row0, GMM_TM), :], sem)
    parts = [(valid == GMM_TM, full)]
    off = jnp.int32(0)
    rem = valid
    for p in TAIL_SIZES:
        take = jnp.logical_and(valid < GMM_TM, (rem & p) != 0)
        src = y_scr.at[slot, pl.ds(pl.multiple_of(off, SUBLANES), p), :]
        dst = y_hbm.at[pl.ds(pl.multiple_of(row0 + off, SUBLANES), p), :]
        parts.append((take, pltpu.make_async_copy(src, dst, sem)))
        off = off + jnp.where((rem & p) != 0, p, 0)
    return parts


def _gmm_kernel(te_ref, tr_ref, tv_ref, nt_ref, xs_hbm, wg_ref, wu_ref, wd_ref, y_hbm,
                x_scr, y_scr, wgu_scr, wdn_scr, zero_scr, xsem, ysem, zsem):
    t = pl.program_id(0)
    n_tiles = nt_ref[0]
    slot = t % 2

    def x_copy(tile, sl):
        r0 = pl.multiple_of(tr_ref[tile], SUBLANES)
        return pltpu.make_async_copy(xs_hbm.at[pl.ds(r0, GMM_TM), :], x_scr.at[sl], xsem.at[sl])

    @pl.when(jnp.logical_and(t == 0, n_tiles > 0))
    def _():
        x_copy(0, 0).start()

    @pl.when(t < n_tiles)
    def _():
        x_copy(t, slot).wait()

        @pl.when(t + 1 < n_tiles)
        def _():
            x_copy(t + 1, 1 - slot).start()

        new_expert = jnp.logical_or(t == 0, te_ref[t] != te_ref[jnp.maximum(t - 1, 0)])

        @pl.when(new_expert)
        def _():
            wgu_scr[:, :D_EXPERT] = wg_ref[0].astype(BF16)
            wgu_scr[:, D_EXPERT:] = wu_ref[0].astype(BF16)
            wdn_scr[...] = wd_ref[0].astype(BF16)

        x = x_scr[slot].astype(BF16)
        h = _dot(x, wgu_scr[...])
        act = (_silu(h[:, :D_EXPERT]) * h[:, D_EXPERT:]).astype(BF16)
        y = _dot(act, wdn_scr[...])

        @pl.when(t >= 2)
        def _():
            r0p = pl.multiple_of(tr_ref[t - 2], SUBLANES)
            for pred, cp in _tile_writes(y_scr, y_hbm, slot, r0p, tv_ref[t - 2], ysem.at[slot]):
                @pl.when(pred)
                def _():
                    cp.wait()

        y_scr[slot] = y
        r0 = pl.multiple_of(tr_ref[t], SUBLANES)
        for pred, cp in _tile_writes(y_scr, y_hbm, slot, r0, tv_ref[t], ysem.at[slot]):
            @pl.when(pred)
            def _():
                cp.start()

    @pl.when(t == pl.num_programs(0) - 1)
    def _():
        for back in (2, 1):
            tile = n_tiles - back

            @pl.when(tile >= 0)
            def _():
                sl = tile % 2
                r0p = pl.multiple_of(tr_ref[jnp.maximum(tile, 0)], SUBLANES)
                for pred, cp in _tile_writes(y_scr, y_hbm, sl, r0p, tv_ref[jnp.maximum(tile, 0)], ysem.at[sl]):
                    @pl.when(pred)
                    def _():
                        cp.wait()

        zero_scr[...] = jnp.zeros(zero_scr.shape, F32)
        used = nt_ref[1]

        def slack(j, carry):
            r0 = pl.multiple_of(used + j * SUBLANES, SUBLANES)
            cp = pltpu.make_async_copy(zero_scr, y_hbm.at[pl.ds(r0, SUBLANES), :], zsem)
            cp.start()
            cp.wait()
            return carry

        lax.fori_loop(0, (N_ROWS - used) // SUBLANES, slack, 0)


def _grouped_mlp(tile_expert, tile_row, tile_valid, n_info, xs, w_gate, w_up, w_down):
    grid_spec = pltpu.PrefetchScalarGridSpec(
        num_scalar_prefetch=4,
        grid=(N_TILES,),
        in_specs=[pl.BlockSpec(memory_space=pl.ANY),
                  pl.BlockSpec((1, D_MODEL, D_EXPERT), lambda t, te, tr, tv, nt: (te[t], 0, 0)),
                  pl.BlockSpec((1, D_MODEL, D_EXPERT), lambda t, te, tr, tv, nt: (te[t], 0, 0)),
                  pl.BlockSpec((1, D_EXPERT, D_MODEL), lambda t, te, tr, tv, nt: (te[t], 0, 0))],
        out_specs=pl.BlockSpec(memory_space=pl.ANY),
        scratch_shapes=[pltpu.VMEM((2, GMM_TM, D_MODEL), F32),
                        pltpu.VMEM((2, GMM_TM, D_MODEL), F32),
                        pltpu.VMEM((D_MODEL, 2 * D_EXPERT), BF16),
                        pltpu.VMEM((D_EXPERT, D_MODEL), BF16),
                        pltpu.VMEM((SUBLANES, D_MODEL), F32),
                        pltpu.SemaphoreType.DMA((2,)),
                        pltpu.SemaphoreType.DMA((2,)),
                        pltpu.SemaphoreType.DMA(())],
    )
    return pl.pallas_call(
        _gmm_kernel,
        out_shape=jax.ShapeDtypeStruct((N_ROWS, D_MODEL), F32),
        grid_spec=grid_spec,
        compiler_params=_cparams(("arbitrary",)),
        name="grouped_mlp",
    )(tile_expert, tile_row, tile_valid, n_info, xs, w_gate, w_up, w_down)


COMBINE_TB = 128


def _combine_kernel(dest_ref, y_hbm, pre_ref, w_ref, mod_ref, fw_ref, o_ref, buf, sem):
    tb = COMBINE_TB

    def row_copy(t, k):
        return pltpu.make_async_copy(y_hbm.at[pl.ds(dest_ref[k, t], 1), :],
                                     buf.at[k, pl.ds(t, 1), :], sem)

    def issue(t, carry):
        for k in range(TOP_K):
            row_copy(t, k).start()
        return carry

    lax.fori_loop(0, tb, issue, 0)

    def drain(t, carry):
        for k in range(TOP_K):
            row_copy(t, k).wait()
        return carry

    lax.fori_loop(0, tb, drain, 0)

    w = w_ref[...]
    routed = w[:, 0:1] * buf[0]
    for k in range(1, TOP_K):
        routed = routed + w[:, k:k + 1] * buf[k]
    m = mod_ref[...]
    x2 = pre_ref[...] + m[5:6] * routed
    ms = jnp.mean(x2 * x2, axis=-1, keepdims=True)
    o_ref[...] = x2 * lax.rsqrt(ms + EPS) * fw_ref[...]


def _combine(dest_t, y_rows, pre, w_tok, mod3, final_norm_w):
    tb = COMBINE_TB
    row = functools.partial(_mod_row_of_block, blocks_prompt=T_PROMPT // tb,
                            blocks_per_sample_seq=DEC_SEQ // tb)
    return pl.pallas_call(
        _combine_kernel,
        out_shape=jax.ShapeDtypeStruct((T_ALL, D_MODEL), F32),
        grid=(T_ALL // tb,),
        in_specs=[pl.BlockSpec((TOP_K, tb), lambda i: (0, i), memory_space=pltpu.SMEM),
                  pl.BlockSpec(memory_space=pl.ANY),
                  pl.BlockSpec((tb, D_MODEL), lambda i: (i, 0)),
                  pl.BlockSpec((tb, TOP_K), lambda i: (i, 0)),
                  pl.BlockSpec((None, N_MOD, D_MODEL), lambda i: (row(i), 0, 0)),
                  pl.BlockSpec((1, D_MODEL), lambda i: (0, 0))],
        out_specs=pl.BlockSpec((tb, D_MODEL), lambda i: (i, 0)),
        scratch_shapes=[pltpu.VMEM((TOP_K, tb, D_MODEL), F32),
                        pltpu.SemaphoreType.DMA(())],
        compiler_params=_cparams(("arbitrary",)),
        name="combine",
    )(dest_t, y_rows, pre, w_tok, mod3, final_norm_w.reshape(1, D_MODEL))


def _tile_tables(counts):
    padded = (counts + ROW_PAD - 1) // ROW_PAD * ROW_PAD
    start = jnp.concatenate([jnp.zeros((1,), I32), jnp.cumsum(padded).astype(I32)])
    tiles = (padded + GMM_TM - 1) // GMM_TM
    tile_end = jnp.cumsum(tiles).astype(I32)
    n_tiles = tile_end[-1]
    t = jnp.arange(N_TILES, dtype=I32)
    expert = jnp.minimum(jnp.searchsorted(tile_end, t, side='right').astype(I32), N_EXPERTS - 1)
    local = t - (tile_end - tiles)[expert]
    row = start[expert] + local * GMM_TM
    valid = jnp.clip(padded[expert] - local * GMM_TM, 0, GMM_TM)
    live = t < n_tiles
    last = jnp.maximum(n_tiles - 1, 0)
    expert = jnp.where(live, expert, expert[last])
    row = jnp.where(live, row, 0)
    valid = jnp.where(live, valid, 0)
    info = jnp.stack([n_tiles, start[N_EXPERTS]]).astype(I32)
    return start, expert.astype(I32), row.astype(I32), valid.astype(I32), info


def _head_expand_matrix():
    r = jnp.arange(LANES)[:, None]
    cidx = jnp.arange(2 * D_INNER)[None, :]
    direction = cidx // D_INNER
    head = (cidx % D_INNER) // HEAD_DIM
    return (r == direction * N_HEADS + head).astype(BF16)


def kernel(x_prompt, x_sample, state_ssm, c, c_ctx, norm1_w, norm2_w, w_mod, b_mod, w_in, ssm_conv_w, ssm_conv_b, ssm_dt_bias, ssm_a_log, ssm_d, ssm_norm_w, w_out_ssm, conf_conv_w, conf_conv_b, conf_ln_w, conf_ln_b, w_out_conf, w_o, router_w, router_bias, exp_w_gate, exp_w_up, exp_w_down, sh_w_gate, sh_w_up, sh_w_down, final_norm_w):
    x_all = jnp.concatenate([x_prompt.reshape(T_PROMPT, D_MODEL), x_sample.reshape(T_SAMPLE, D_MODEL)], axis=0)

    cc = jnp.zeros((MOD_ROWS, D_MODEL), F32).at[:DEC_BATCH].set(c).at[CTX_ROW].set(c_ctx)
    mod3 = _modulation(cc, w_mod[0], b_mod[0]).reshape(MOD_ROWS, N_MOD, D_MODEL)

    w = w_in[0]
    o_xbc, o_dt, o_cv, o_cg, o_gl = D_INNER, D_INNER + CONV_DIM, D_INNER + CONV_DIM + 2 * N_HEADS, \
        D_INNER + CONV_DIM + 2 * N_HEADS + D_MODEL, D_INNER + CONV_DIM + 2 * N_HEADS + 2 * D_MODEL
    w_main = jnp.concatenate([w[:, o_xbc:o_dt], w[:, :o_xbc], w[:, o_gl:], w[:, o_cv:o_cg], w[:, o_cg:o_gl]],
                             axis=1).astype(BF16)
    w_dt = jnp.pad(w[:, o_dt:o_cv], ((0, 0), (0, LANES - 2 * N_HEADS))).astype(BF16)
    proj, dt_raw = _in_projection(x_all, mod3, norm1_w[0], w_main, w_dt)

    pad_heads = lambda v: jnp.pad(v.reshape(1, 2 * N_HEADS), ((0, 0), (0, LANES - 2 * N_HEADS)))
    dtb = pad_heads(ssm_dt_bias[0])
    a_neg = pad_heads(-jnp.exp(ssm_a_log[0]))
    dvec = jnp.repeat(ssm_d[0], HEAD_DIM).reshape(1, D_INNER)
    nw = ssm_norm_w[0].reshape(1, D_INNER)
    e2 = _head_expand_matrix()
    xbc_p = _ssm_conv(proj, ssm_conv_w[0], ssm_conv_b[0], seq_len=SEQ, n_seq=BATCH, row_block0=0)
    xbc_s = _ssm_conv(proj, ssm_conv_w[0], ssm_conv_b[0], seq_len=DEC_SEQ, n_seq=DEC_BATCH,
                      row_block0=T_PROMPT // DEC_SEQ)
    y_p, fin = _ssd(xbc_p, dt_raw, proj, None, dtb, a_neg, dvec, nw, e2,
                    seq_len=SEQ, n_seq=BATCH, tok0=0, want_final=True)
    init = state_ssm.reshape(DEC_BATCH, 2, D_INNER, D_STATE)
    (y_s,) = _ssd(xbc_s, dt_raw, proj, init, dtb, a_neg, dvec, nw, e2,
                  seq_len=DEC_SEQ, n_seq=DEC_BATCH, tok0=T_PROMPT, want_final=False)
    y_ssm = jnp.concatenate([y_p, y_s], axis=0)

    v_p = _conformer(proj, conf_conv_w[0], conf_conv_b[0], conf_ln_w[0], conf_ln_b[0],
                     seq_len=SEQ, n_seq=BATCH, row_block0=0, grid_mode=False)
    v_s = _conformer(proj, conf_conv_w[0], conf_conv_b[0], conf_ln_w[0], conf_ln_b[0],
                     seq_len=DEC_SEQ, n_seq=DEC_BATCH, row_block0=T_PROMPT // DEC_SEQ, grid_mode=True)
    v_conf = jnp.concatenate([v_p, v_s], axis=0)

    sh_gu = jnp.concatenate([sh_w_gate[0], sh_w_up[0]], axis=1).astype(BF16)
    pre, v_all, scores = _merge(y_ssm, v_conf, proj, x_all, mod3, w_out_ssm[0].astype(BF16),
                                w_out_conf[0].astype(BF16), w_o[0].astype(BF16), norm2_w[0], router_w[0],
                                sh_gu, sh_w_down[0].astype(BF16))

    idx_t, w_t, rank_t, cnt = _route(scores, router_bias[0])
    counts = cnt[:, 0].astype(I32)
    start, tile_expert, tile_row, tile_valid, info = _tile_tables(counts)
    dest_t = _dest_slots(idx_t, rank_t, start[:N_EXPERTS])

    xs = _dispatch(counts, start, dest_t, v_all)
    y_rows = _grouped_mlp(tile_expert, tile_row, tile_valid, info, xs, exp_w_gate[0], exp_w_up[0], exp_w_down[0])
    out = _combine(dest_t, y_rows, pre, w_t.T, mod3, final_norm_w)

    y_prompt = out[:T_PROMPT].reshape(BATCH, SEQ, D_MODEL)
    y_sample = out[T_PROMPT:].reshape(DEC_BATCH, DEC_SEQ, D_MODEL)
    new_state = fin.reshape(BATCH, 1, 2, N_HEADS, HEAD_DIM, D_STATE)
    return (y_prompt, y_sample, new_state)
```

```python
import functools

import jax
import jax.numpy as jnp
from jax import lax
from jax.experimental import pallas as pl
from jax.experimental.pallas import tpu as pltpu

F32 = jnp.float32
BF16 = jnp.bfloat16
I32 = jnp.int32

D_MODEL = 1024
BATCH = 32
SEQ = 256
DEC_BATCH = 8
DEC_SEQ = 1024
GRID_W = 64
GRID_H = DEC_SEQ // GRID_W
D_INNER = 2048
HEAD_DIM = 64
N_HEADS = 32
D_STATE = 128
N_GROUPS = 8
HEADS_PER_GROUP = N_HEADS // N_GROUPS
GROUP_W = HEADS_PER_GROUP * HEAD_DIM
D_CONV_SSM = 5
CHUNK = 128
CONV_DIM = D_INNER + 2 * N_GROUPS * D_STATE
CONF_K = 31
CONF_PAD = CONF_K // 2
N_EXPERTS = 256
TOP_K = 8
N_EXPERT_GROUPS = 8
EXPERTS_PER_GROUP = N_EXPERTS // N_EXPERT_GROUPS
TOPK_GROUPS = 4
D_EXPERT = 256
D_SHARED = 256
ROUTED_SCALE = 2.5
N_MOD = 6
EPS = 1e-6

T_PROMPT = BATCH * SEQ
T_SAMPLE = DEC_BATCH * DEC_SEQ
T_ALL = T_PROMPT + T_SAMPLE
N_ASSIGN = T_ALL * TOP_K
MOD_ROWS = 16
CTX_ROW = DEC_BATCH

SUBLANES = 8
LANES = 128
VMEM_LIMIT = 56 * 1024 * 1024

PROJ_W = CONV_DIM + D_INNER + 2 * D_MODEL + 2 * D_MODEL
COL_Z = CONV_DIM // D_INNER
COL_GL = COL_Z + 1
COL_CV = (CONV_DIM + 2 * D_INNER) // D_MODEL
COL_CG = COL_CV + 1

ROW_TILE = D_MODEL // LANES
GMM_TM = 256
N_ROWS = N_ASSIGN + GMM_TM
N_TILES = N_ASSIGN // GMM_TM + N_EXPERTS
TAIL_SIZES = (128, 64, 32, 16, 8, 4, 2, 1)


def _cparams(sem, vmem=VMEM_LIMIT):
    return pltpu.CompilerParams(dimension_semantics=sem, vmem_limit_bytes=vmem)


def _silu(x):
    return x * jax.nn.sigmoid(x)


def _split2(x):
    hi = x.astype(BF16)
    lo = (x - hi.astype(F32)).astype(BF16)
    return hi, lo


def _split3(x):
    b1 = x.astype(BF16)
    r = x - b1.astype(F32)
    b2 = r.astype(BF16)
    b3 = (r - b2.astype(F32)).astype(BF16)
    return b1, b2, b3


def _dot(a, b):
    return jnp.dot(a, b, preferred_element_type=F32)


def _dot_exact_lhs(a_exact, b):
    b1, b2, b3 = _split3(b)
    return _dot(a_exact, b1) + _dot(a_exact, b2) + _dot(a_exact, b3)


def _dot_exact_rhs(a, b_exact):
    a1, a2, a3 = _split3(a)
    return _dot(a1, b_exact) + _dot(a2, b_exact) + _dot(a3, b_exact)


def _dot_f32(a, b):
    a1, a2 = _split2(a)
    b1, b2 = _split2(b)
    return _dot(a1, b1) + _dot(a1, b2) + _dot(a2, b1)


def _mod_row_of_block(i, blocks_prompt, blocks_per_sample_seq):
    return jnp.where(i < blocks_prompt, CTX_ROW, (i - blocks_prompt) // blocks_per_sample_seq)


def _mod_kernel(c_ref, w_ref, b_ref, o_ref):
    c = c_ref[...]
    o_ref[...] = _dot_f32(_silu(c), w_ref[...]) + b_ref[...]


def _modulation(cc, w_mod, b_mod):
    tn = 512
    n = N_MOD * D_MODEL
    return pl.pallas_call(
        _mod_kernel,
        out_shape=jax.ShapeDtypeStruct((MOD_ROWS, n), F32),
        grid=(n // tn,),
        in_specs=[pl.BlockSpec((MOD_ROWS, D_MODEL), lambda j: (0, 0)),
                  pl.BlockSpec((D_MODEL, tn), lambda j: (0, j)),
                  pl.BlockSpec((1, tn), lambda j: (0, j))],
        out_specs=pl.BlockSpec((MOD_ROWS, tn), lambda j: (0, j)),
        compiler_params=_cparams(("arbitrary",)),
        name="modulation",
    )(cc, w_mod, b_mod.reshape(1, n))


INPROJ_TM = 1024
INPROJ_TN = 1024


def _inproj_kernel(x_ref, mod_ref, n1_ref, w_ref, wdt_ref, o_ref, dt_ref, u_scr):
    @pl.when(pl.program_id(1) == 0)
    def _():
        x = x_ref[...]
        ms = jnp.mean(x * x, axis=-1, keepdims=True)
        y = x * lax.rsqrt(ms + EPS) * n1_ref[...]
        m = mod_ref[...]
        u = (y * (1.0 + m[1:2]) + m[0:1]).astype(BF16)
        u_scr[...] = u
        dt_ref[...] = _dot(u, wdt_ref[...])

    o_ref[...] = _dot(u_scr[...], w_ref[...]).astype(BF16)


def _in_projection(x_all, mod3, norm1_w, w_main, w_dt):
    tm, tn = INPROJ_TM, INPROJ_TN
    row = functools.partial(_mod_row_of_block, blocks_prompt=T_PROMPT // tm,
                            blocks_per_sample_seq=DEC_SEQ // tm)
    return pl.pallas_call(
        _inproj_kernel,
        out_shape=(jax.ShapeDtypeStruct((T_ALL, PROJ_W), BF16),
                   jax.ShapeDtypeStruct((T_ALL, LANES), F32)),
        grid=(T_ALL // tm, PROJ_W // tn),
        in_specs=[pl.BlockSpec((tm, D_MODEL), lambda i, j: (i, 0)),
                  pl.BlockSpec((None, N_MOD, D_MODEL), lambda i, j: (row(i), 0, 0)),
                  pl.BlockSpec((1, D_MODEL), lambda i, j: (0, 0)),
                  pl.BlockSpec((D_MODEL, tn), lambda i, j: (0, j)),
                  pl.BlockSpec((D_MODEL, LANES), lambda i, j: (0, 0))],
        out_specs=(pl.BlockSpec((tm, tn), lambda i, j: (i, j)),
                   pl.BlockSpec((tm, LANES), lambda i, j: (i, 0))),
        scratch_shapes=[pltpu.VMEM((tm, D_MODEL), BF16)],
        compiler_params=_cparams(("arbitrary", "arbitrary")),
        name="in_projection",
    )(x_all, mod3, norm1_w.reshape(1, D_MODEL), w_main, w_dt)


SSMCONV_TN = 512


def _ssmconv_kernel(x_ref, w_ref, b_ref, o_ref, *, seq_len):
    x = x_ref[...].astype(F32)
    row = lax.broadcasted_iota(I32, x.shape, 0)
    half = D_CONV_SSM // 2
    acc = b_ref[...] + w_ref[half:half + 1, :] * x
    for k in range(D_CONV_SSM):
        s = k - half
        if s == 0:
            continue
        shifted = pltpu.roll(x, shift=(-s) % seq_len, axis=0)
        valid = jnp.logical_and(row + s >= 0, row + s < seq_len)
        acc = acc + w_ref[k:k + 1, :] * jnp.where(valid, shifted, 0.0)
    o_ref[...] = _silu(acc).astype(BF16)


def _ssm_conv(proj, conv_w, conv_b, *, seq_len, n_seq, row_block0):
    tn = SSMCONV_TN
    return pl.pallas_call(
        functools.partial(_ssmconv_kernel, seq_len=seq_len),
        out_shape=jax.ShapeDtypeStruct((n_seq * seq_len, CONV_DIM), BF16),
        grid=(n_seq, CONV_DIM // tn),
        in_specs=[pl.BlockSpec((seq_len, tn), lambda b, j: (row_block0 + b, j)),
                  pl.BlockSpec((D_CONV_SSM, tn), lambda b, j: (0, j)),
                  pl.BlockSpec((1, tn), lambda b, j: (0, j))],
        out_specs=pl.BlockSpec((seq_len, tn), lambda b, j: (b, j)),
        compiler_params=_cparams(("arbitrary", "arbitrary")),
        name="ssm_conv",
    )(proj, conv_w, conv_b.reshape(1, CONV_DIM))


def _tri_masks():
    ii = lax.broadcasted_iota(I32, (CHUNK, CHUNK), 0)
    jj = lax.broadcasted_iota(I32, (CHUNK, CHUNK), 1)
    return ii, jj


def _chunk_decays(dt_ref, dtb_ref, a_ref):
    ii, jj = _tri_masks()
    pre = dt_ref[...] + dtb_ref[...]
    dt = jnp.maximum(pre, 0.0) + jnp.log(1.0 + jnp.exp(-jnp.abs(pre)))
    la = dt * a_ref[...]
    tri_lo = jnp.where(jj <= ii, 1.0, 0.0).astype(BF16)
    tri_up = jnp.where(jj >= ii, 1.0, 0.0).astype(BF16)
    cs_prefix = _dot_exact_lhs(tri_lo, la)
    cs_suffix = _dot_exact_lhs(tri_up, la)
    fwd_lane = lax.broadcasted_iota(I32, (CHUNK, LANES), 1) < N_HEADS
    cs = jnp.where(fwd_lane, cs_prefix, cs_suffix)
    tot = jnp.where(fwd_lane[0:1], cs_prefix[CHUNK - 1:CHUNK, :], cs_suffix[0:1, :])
    return dt, cs, tot


def _transpose_blocks(src, rows, cols):
    out_rows = []
    for cb in range(cols // LANES):
        pieces = [src[rb * LANES:(rb + 1) * LANES, cb * LANES:(cb + 1) * LANES].T
                  for rb in range(rows // LANES)]
        out_rows.append(jnp.concatenate(pieces, axis=1) if len(pieces) > 1 else pieces[0])
    return jnp.concatenate(out_rows, axis=0) if len(out_rows) > 1 else out_rows[0]


def _ssd_kernel(*refs, n_chunks, has_init, want_final):
    it = iter(refs)
    xbc_ref, dt_ref, z_ref = next(it), next(it), next(it)
    init_ref = next(it) if has_init else None
    dtb_ref, a_ref, dvec_ref, nw_ref, e2_ref = next(it), next(it), next(it), next(it), next(it)
    out_ref = next(it)
    fin_ref = next(it) if want_final else None
    sf_scr, sb_scr, df_scr, db_scr = next(it), next(it), next(it), next(it)

    phase = pl.program_id(1)
    c = pl.program_id(2)

    @pl.when(phase == 0)
    def _chunk_states():
        dt, cs, tot = _chunk_decays(dt_ref, dtb_ref, a_ref)
        w_in = dt * jnp.exp(tot - cs)
        w_exp = _dot_exact_rhs(w_in, e2_ref[...])
        dec = jnp.exp(jnp.broadcast_to(tot, (SUBLANES, LANES)))
        dec_exp = _dot_exact_rhs(dec, e2_ref[...])
        df_scr[c] = dec_exp[:, :D_INNER]
        db_scr[c] = dec_exp[:, D_INNER:]
        for g in range(N_GROUPS):
            lo = g * GROUP_W
            xg = xbc_ref[:, lo:lo + GROUP_W].astype(F32)
            xd_f = (xg * w_exp[:, lo:lo + GROUP_W]).astype(BF16)
            xd_b = (xg * w_exp[:, D_INNER + lo:D_INNER + lo + GROUP_W]).astype(BF16)
            bg = xbc_ref[:, D_INNER + g * D_STATE:D_INNER + (g + 1) * D_STATE]
            bg_t = bg.astype(F32).T.astype(BF16)
            sf_scr[c, :, lo:lo + GROUP_W] = _dot(bg_t, xd_f)
            sb_scr[c, :, lo:lo + GROUP_W] = _dot(bg_t, xd_b)

    @pl.when(jnp.logical_and(phase == 1, c == 0))
    def _recurrence():
        for g in range(N_GROUPS):
            lo = g * GROUP_W
            if has_init:
                prev_f = _transpose_blocks(init_ref[0, 0, lo:lo + GROUP_W, :], GROUP_W, D_STATE)
                prev_b = _transpose_blocks(init_ref[0, 1, lo:lo + GROUP_W, :], GROUP_W, D_STATE)
            else:
                prev_f = jnp.zeros((D_STATE, GROUP_W), F32)
                prev_b = jnp.zeros((D_STATE, GROUP_W), F32)
            for cc in range(n_chunks):
                s = sf_scr[cc, :, lo:lo + GROUP_W]
                sf_scr[cc, :, lo:lo + GROUP_W] = prev_f
                prev_f = df_scr[cc, 0:1, lo:lo + GROUP_W] * prev_f + s
            for cc in reversed(range(n_chunks)):
                s = sb_scr[cc, :, lo:lo + GROUP_W]
                sb_scr[cc, :, lo:lo + GROUP_W] = prev_b
                prev_b = db_scr[cc, 0:1, lo:lo + GROUP_W] * prev_b + s
            if want_final:
                fin_ref[0, 0, lo:lo + GROUP_W, :] = _transpose_blocks(prev_f, D_STATE, GROUP_W)
                fin_ref[0, 1, lo:lo + GROUP_W, :] = _transpose_blocks(prev_b, D_STATE, GROUP_W)

    @pl.when(phase == 1)
    def _outputs():
        ii, jj = _tri_masks()
        dt, cs, _ = _chunk_decays(dt_ref, dtb_ref, a_ref)
        out_dec = _dot_exact_rhs(jnp.exp(cs), e2_ref[...])
        cs_t = cs.T
        dt_t = dt.T
        causal = ii >= jj
        anti = jj >= ii
        neg = jnp.float32(-1e30)
        left = lax.broadcasted_iota(I32, (CHUNK, LANES), 1) < HEAD_DIM
        for g in range(N_GROUPS):
            lo = g * GROUP_W
            bg = xbc_ref[:, D_INNER + g * D_STATE:D_INNER + (g + 1) * D_STATE]
            cg = xbc_ref[:, D_INNER + N_GROUPS * D_STATE + g * D_STATE:
                         D_INNER + N_GROUPS * D_STATE + (g + 1) * D_STATE]
            cb = lax.dot_general(cg, bg, (((1,), (1,)), ((), ())), preferred_element_type=F32)
            pf = sf_scr[c, :, lo:lo + GROUP_W].astype(BF16)
            pb = sb_scr[c, :, lo:lo + GROUP_W].astype(BF16)
            y_off = (_dot(cg, pf) * out_dec[:, lo:lo + GROUP_W]
                     + _dot(cg, pb) * out_dec[:, D_INNER + lo:D_INNER + lo + GROUP_W])
            pairs = []
            for m in range(HEADS_PER_GROUP // 2):
                x_pair = xbc_ref[:, lo + m * LANES:lo + (m + 1) * LANES]
                ys = []
                for hh in range(2):
                    h = g * HEADS_PER_GROUP + 2 * m + hh
                    hb = N_HEADS + h
                    seg_f = jnp.where(causal, cs[:, h:h + 1] - cs_t[h:h + 1, :], neg)
                    seg_b = jnp.where(anti, cs[:, hb:hb + 1] - cs_t[hb:hb + 1, :], neg)
                    mix = (jnp.exp(seg_f) * dt_t[h:h + 1, :] + jnp.exp(seg_b) * dt_t[hb:hb + 1, :])
                    ys.append(_dot((cb * mix).astype(BF16), x_pair))
                pairs.append(jnp.where(left, ys[0], ys[1]))
            y_diag = jnp.concatenate(pairs, axis=1)
            xg = xbc_ref[:, lo:lo + GROUP_W].astype(F32)
            y = y_diag + y_off + dvec_ref[:, lo:lo + GROUP_W] * xg
            zg = z_ref[:, lo:lo + GROUP_W].astype(F32)
            y = y * _silu(zg)
            ms = jnp.mean(y * y, axis=-1, keepdims=True)
            out_ref[:, lo:lo + GROUP_W] = (y * lax.rsqrt(ms + EPS) * nw_ref[:, lo:lo + GROUP_W]).astype(BF16)


def _ssd(xbc, dt_raw, proj, init, dtb, a_neg, dvec, norm_w, e2, *, seq_len, n_seq, tok0, want_final):
    nc = seq_len // CHUNK
    blk0 = tok0 // CHUNK
    has_init = init is not None
    chunk_map = lambda b, p, c: (b * nc + c, 0)
    in_specs = [pl.BlockSpec((CHUNK, CONV_DIM), chunk_map),
                pl.BlockSpec((CHUNK, LANES), lambda b, p, c: (blk0 + b * nc + c, 0)),
                pl.BlockSpec((CHUNK, D_INNER), lambda b, p, c: (blk0 + b * nc + c * p, COL_Z))]
    args = [xbc, dt_raw, proj]
    if has_init:
        in_specs.append(pl.BlockSpec((1, 2, D_INNER, D_STATE), lambda b, p, c: (b, 0, 0, 0)))
        args.append(init)
    const = lambda b, p, c: (0, 0)
    in_specs += [pl.BlockSpec((1, LANES), const), pl.BlockSpec((1, LANES), const),
                 pl.BlockSpec((1, D_INNER), const), pl.BlockSpec((1, D_INNER), const),
                 pl.BlockSpec((LANES, 2 * D_INNER), const)]
    args += [dtb, a_neg, dvec, norm_w, e2]
    out_shape = [jax.ShapeDtypeStruct((n_seq * seq_len, D_INNER), BF16)]
    out_specs = [pl.BlockSpec((CHUNK, D_INNER), lambda b, p, c: (b * nc + c * p, 0))]
    if want_final:
        out_shape.append(jax.ShapeDtypeStruct((n_seq, 2, D_INNER, D_STATE), F32))
        out_specs.append(pl.BlockSpec((1, 2, D_INNER, D_STATE), lambda b, p, c: (b, 0, 0, 0)))
    res = pl.pallas_call(
        functools.partial(_ssd_kernel, n_chunks=nc, has_init=has_init, want_final=want_final),
        out_shape=tuple(out_shape),
        grid=(n_seq, 2, nc),
        in_specs=in_specs,
        out_specs=tuple(out_specs),
        scratch_shapes=[pltpu.VMEM((nc, D_STATE, D_INNER), F32),
                        pltpu.VMEM((nc, D_STATE, D_INNER), F32),
                        pltpu.VMEM((nc, SUBLANES, D_INNER), F32),
                        pltpu.VMEM((nc, SUBLANES, D_INNER), F32)],
        compiler_params=_cparams(("arbitrary", "arbitrary", "arbitrary")),
        name="ssd_final" if want_final else "ssd_init",
    )(*args)
    return res


def _layernorm_silu(v, lw_ref, lb_ref):
    mu = jnp.mean(v, axis=-1, keepdims=True)
    d = v - mu
    var = jnp.mean(d * d, axis=-1, keepdims=True)
    return _silu(d * lax.rsqrt(var + EPS) * lw_ref[...] + lb_ref[...])


def _conf_seq_kernel(cv_ref, cg_ref, w_ref, b_ref, lw_ref, lb_ref, o_ref, pad_scr, acc_scr):
    seq_len = cv_ref.shape[0]
    base = 2 * SUBLANES
    v = cv_ref[...].astype(F32) * jax.nn.sigmoid(cg_ref[...].astype(F32))
    pad_scr[0:base, :] = jnp.zeros((base, D_MODEL), F32)
    pad_scr[base + seq_len:base + seq_len + base, :] = jnp.zeros((base, D_MODEL), F32)
    pad_scr[base:base + seq_len, :] = v
    rt, ct = 128, 256
    for r0 in range(0, seq_len, rt):
        for c0 in range(0, D_MODEL, ct):
            acc = jnp.broadcast_to(b_ref[:, c0:c0 + ct], (rt, ct))
            for k in range(CONF_K):
                off = base + r0 + k - CONF_PAD
                acc = acc + w_ref[k:k + 1, c0:c0 + ct] * pad_scr[off:off + rt, c0:c0 + ct]
            acc_scr[r0:r0 + rt, c0:c0 + ct] = acc
    o_ref[...] = _layernorm_silu(acc_scr[...], lw_ref, lb_ref).astype(BF16)


def _conf_grid_kernel(cv_ref, cg_ref, w_ref, b_ref, lw_ref, lb_ref, o_ref, v_scr, pad_scr, acc_scr):
    half = D_MODEL // 2
    base = 2 * SUBLANES
    stride = GRID_W + 2 * base
    v_scr[...] = cv_ref[...].astype(F32) * jax.nn.sigmoid(cg_ref[...].astype(F32))
    pad_scr[...] = jnp.zeros(pad_scr.shape, F32)
    for r in range(GRID_H):
        pad_scr[r * stride + base:r * stride + base + GRID_W, :] = v_scr[r * GRID_W:(r + 1) * GRID_W, 0:half]
    ct = 256
    for r in range(GRID_H):
        for c0 in range(0, half, ct):
            acc = jnp.broadcast_to(b_ref[:, c0:c0 + ct], (GRID_W, ct))
            for k in range(CONF_K):
                off = r * stride + base + k - CONF_PAD
                acc = acc + w_ref[k:k + 1, c0:c0 + ct] * pad_scr[off:off + GRID_W, c0:c0 + ct]
            acc_scr[r * GRID_W:(r + 1) * GRID_W, c0:c0 + ct] = acc
        for c0 in range(half, D_MODEL, ct):
            acc = jnp.broadcast_to(b_ref[:, c0:c0 + ct], (GRID_W, ct))
            for r2 in range(GRID_H):
                k = r2 - r + CONF_PAD
                acc = acc + w_ref[k:k + 1, c0:c0 + ct] * v_scr[r2 * GRID_W:(r2 + 1) * GRID_W, c0:c0 + ct]
            acc_scr[r * GRID_W:(r + 1) * GRID_W, c0:c0 + ct] = acc
    o_ref[...] = _layernorm_silu(acc_scr[...], lw_ref, lb_ref).astype(BF16)


def _conformer(proj, conv_w, conv_b, ln_w, ln_b, *, seq_len, n_seq, row_block0, grid_mode):
    base = 2 * SUBLANES
    if grid_mode:
        body = _conf_grid_kernel
        scratch = [pltpu.VMEM((seq_len, D_MODEL), F32),
                   pltpu.VMEM((GRID_H * (GRID_W + 2 * base), D_MODEL // 2), F32),
                   pltpu.VMEM((seq_len, D_MODEL), F32)]
    else:
        body = _conf_seq_kernel
        scratch = [pltpu.VMEM((seq_len + 2 * base, D_MODEL), F32),
                   pltpu.VMEM((seq_len, D_MODEL), F32)]
    const = lambda b: (0, 0)
    return pl.pallas_call(
        body,
        out_shape=jax.ShapeDtypeStruct((n_seq * seq_len, D_MODEL), BF16),
        grid=(n_seq,),
        in_specs=[pl.BlockSpec((seq_len, D_MODEL), lambda b: (row_block0 + b, COL_CV)),
                  pl.BlockSpec((seq_len, D_MODEL), lambda b: (row_block0 + b, COL_CG)),
                  pl.BlockSpec((CONF_K, D_MODEL), const),
                  pl.BlockSpec((1, D_MODEL), const),
                  pl.BlockSpec((1, D_MODEL), const),
                  pl.BlockSpec((1, D_MODEL), const)],
        out_specs=pl.BlockSpec((seq_len, D_MODEL), lambda b: (b, 0)),
        scratch_shapes=scratch,
        compiler_params=_cparams(("arbitrary",)),
        name="conformer_grid" if grid_mode else "conformer_seq",
    )(proj, proj, conv_w, conv_b.reshape(1, D_MODEL), ln_w.reshape(1, D_MODEL), ln_b.reshape(1, D_MODEL))


MERGE_TM = 256


def _merge_kernel(ya_ref, yb_ref, gl_ref, x_ref, mod_ref, wa_ref, wb_ref, wo_ref, n2_ref, rw_ref,
                  sgu_ref, sd_ref, pre_ref, v_ref, s_ref):
    y_a = _dot(ya_ref[...], wa_ref[...])
    y_b = _dot(yb_ref[...], wb_ref[...])
    gates = jax.nn.sigmoid(gl_ref[...].astype(F32))
    mix = gates[:, :D_MODEL] * y_a + gates[:, D_MODEL:] * y_b
    out = _dot(mix.astype(BF16), wo_ref[...])
    m = mod_ref[...]
    x1 = x_ref[...] + m[2:3] * out
    ms = jnp.mean(x1 * x1, axis=-1, keepdims=True)
    v = x1 * lax.rsqrt(ms + EPS) * n2_ref[...] * (1.0 + m[4:5]) + m[3:4]
    v_ref[...] = v
    s_ref[...] = jax.nn.sigmoid(_dot_f32(v, rw_ref[...]))
    hgu = _dot(v.astype(BF16), sgu_ref[...])
    act = _silu(hgu[:, :D_SHARED]) * hgu[:, D_SHARED:]
    shared = _dot(act.astype(BF16), sd_ref[...])
    pre_ref[...] = x1 + m[5:6] * shared


def _merge(y_ssm, v_conf, proj, x_all, mod3, w_out_ssm, w_out_conf, w_o, norm2_w, router_w, sh_gu, sh_d):
    tm = MERGE_TM
    row = functools.partial(_mod_row_of_block, blocks_prompt=T_PROMPT // tm,
                            blocks_per_sample_seq=DEC_SEQ // tm)
    const = lambda i: (0, 0)
    return pl.pallas_call(
        _merge_kernel,
        out_shape=(jax.ShapeDtypeStruct((T_ALL, D_MODEL), F32),
                   jax.ShapeDtypeStruct((T_ALL, D_MODEL), F32),
                   jax.ShapeDtypeStruct((T_ALL, N_EXPERTS), F32)),
        grid=(T_ALL // tm,),
        in_specs=[pl.BlockSpec((tm, D_INNER), lambda i: (i, 0)),
                  pl.BlockSpec((tm, D_MODEL), lambda i: (i, 0)),
                  pl.BlockSpec((tm, 2 * D_MODEL), lambda i: (i, COL_GL)),
                  pl.BlockSpec((tm, D_MODEL), lambda i: (i, 0)),
                  pl.BlockSpec((None, N_MOD, D_MODEL), lambda i: (row(i), 0, 0)),
                  pl.BlockSpec((D_INNER, D_MODEL), const),
                  pl.BlockSpec((D_MODEL, D_MODEL), const),
                  pl.BlockSpec((D_MODEL, D_MODEL), const),
                  pl.BlockSpec((1, D_MODEL), const),
                  pl.BlockSpec((D_MODEL, N_EXPERTS), const),
                  pl.BlockSpec((D_MODEL, 2 * D_SHARED), const),
                  pl.BlockSpec((D_SHARED, D_MODEL), const)],
        out_specs=(pl.BlockSpec((tm, D_MODEL), lambda i: (i, 0)),
                   pl.BlockSpec((tm, D_MODEL), lambda i: (i, 0)),
                   pl.BlockSpec((tm, N_EXPERTS), lambda i: (i, 0))),
        compiler_params=_cparams(("arbitrary",)),
        name="merge",
    )(y_ssm, v_conf, proj, x_all, mod3, w_out_ssm, w_out_conf, w_o, norm2_w.reshape(1, D_MODEL),
      router_w, sh_gu, sh_d)


ROUTE_TB = 256


def _first_index_of_max(vals, iota, sentinel):
    m = jnp.max(vals, axis=0, keepdims=True)
    idx = jnp.min(jnp.where(vals == m, iota, jnp.float32(sentinel)), axis=0, keepdims=True)
    return m, idx


def _route_kernel(s_ref, bias_ref, idx_ref, w_ref, rank_ref, cnt_ref, run_scr):
    i = pl.program_id(0)
    tb = ROUTE_TB
    neg = jnp.float32(-jnp.inf)

    @pl.when(i == 0)
    def _():
        run_scr[...] = jnp.zeros(run_scr.shape, F32)

    s_t = s_ref[...].T
    sb_t = s_t + bias_ref[...]
    eiota = lax.broadcasted_iota(I32, (N_EXPERTS, tb), 0).astype(F32)

    liota = lax.broadcasted_iota(I32, (EXPERTS_PER_GROUP, tb), 0).astype(F32)
    gscores = []
    for g in range(N_EXPERT_GROUPS):
        blk = sb_t[g * EXPERTS_PER_GROUP:(g + 1) * EXPERTS_PER_GROUP, :]
        m1, i1 = _first_index_of_max(blk, liota, EXPERTS_PER_GROUP)
        m2 = jnp.max(jnp.where(liota == i1, neg, blk), axis=0, keepdims=True)
        gscores.append(m1 + m2)
    gs = jnp.concatenate(gscores, axis=0)
    giota = lax.broadcasted_iota(I32, (N_EXPERT_GROUPS, tb), 0).astype(F32)
    gsel = jnp.zeros((N_EXPERT_GROUPS, tb), F32)
    for _ in range(TOPK_GROUPS):
        _, gi = _first_index_of_max(gs, giota, N_EXPERT_GROUPS)
        hit = giota == gi
        gsel = jnp.where(hit, 1.0, gsel)
        gs = jnp.where(hit, neg, gs)
    emask = jnp.concatenate(
        [jnp.broadcast_to(gsel[g:g + 1, :], (EXPERTS_PER_GROUP, tb)) for g in range(N_EXPERT_GROUPS)], axis=0)
    masked = jnp.where(emask > 0.5, sb_t, neg)

    onehots, idxs, wts = [], [], []
    for _ in range(TOP_K):
        _, ei = _first_index_of_max(masked, eiota, N_EXPERTS)
        hit = eiota == ei
        onehots.append(hit)
        idxs.append(ei)
        wts.append(jnp.sum(jnp.where(hit, s_t, 0.0), axis=0, keepdims=True))
        masked = jnp.where(hit, neg, masked)
    w = jnp.concatenate(wts, axis=0)
    w_ref[...] = w / jnp.sum(w, axis=0, keepdims=True) * ROUTED_SCALE
    idx_ref[...] = jnp.concatenate(idxs, axis=0).astype(I32)

    assign = jnp.zeros((N_EXPERTS, tb), F32)
    for hit in onehots:
        assign = jnp.where(hit, 1.0, assign)
    assign_b = assign.astype(BF16)
    ti = lax.broadcasted_iota(I32, (tb, tb), 0)
    tj = lax.broadcasted_iota(I32, (tb, tb), 1)
    before = jnp.where(ti < tj, 1.0, 0.0).astype(BF16)
    within = _dot(assign_b, before)
    run = run_scr[...]
    total = within + jnp.concatenate([run] * (tb // LANES), axis=1)
    rank_ref[...] = jnp.concatenate(
        [jnp.sum(jnp.where(hit, total, 0.0), axis=0, keepdims=True) for hit in onehots], axis=0).astype(I32)
    new_run = run + _dot(assign_b, jnp.ones((tb, LANES), BF16))
    run_scr[...] = new_run
    cnt_ref[...] = new_run


def _route(scores, router_bias):
    tb = ROUTE_TB
    return pl.pallas_call(
        _route_kernel,
        out_shape=(jax.ShapeDtypeStruct((TOP_K, T_ALL), I32),
                   jax.ShapeDtypeStruct((TOP_K, T_ALL), F32),
                   jax.ShapeDtypeStruct((TOP_K, T_ALL), I32),
                   jax.ShapeDtypeStruct((N_EXPERTS, LANES), F32)),
        grid=(T_ALL // tb,),
        in_specs=[pl.BlockSpec((tb, N_EXPERTS), lambda i: (i, 0)),
                  pl.BlockSpec((N_EXPERTS, 1), lambda i: (0, 0))],
        out_specs=(pl.BlockSpec((TOP_K, tb), lambda i: (0, i)),
                   pl.BlockSpec((TOP_K, tb), lambda i: (0, i)),
                   pl.BlockSpec((TOP_K, tb), lambda i: (0, i)),
                   pl.BlockSpec((N_EXPERTS, LANES), lambda i: (0, 0))),
        scratch_shapes=[pltpu.VMEM((N_EXPERTS, LANES), F32)],
        compiler_params=_cparams(("arbitrary",)),
        name="route",
    )(scores, router_bias.reshape(N_EXPERTS, 1))


def _dest_kernel(idx_ref, rank_ref, start_ref, dest_ref):
    tb = idx_ref.shape[1]
    eiota = lax.broadcasted_iota(I32, (N_EXPERTS, tb), 0)
    start = jnp.broadcast_to(start_ref[...], (N_EXPERTS, tb))
    idx = idx_ref[...]
    rows = [jnp.sum(jnp.where(eiota == idx[k:k + 1, :], start, 0.0), axis=0, keepdims=True)
            for k in range(TOP_K)]
    dest_ref[...] = jnp.concatenate(rows, axis=0).astype(I32) + rank_ref[...]


def _dest_slots(idx_t, rank_t, group_start):
    tb = 512
    return pl.pallas_call(
        _dest_kernel,
        out_shape=jax.ShapeDtypeStruct((TOP_K, T_ALL), I32),
        grid=(T_ALL // tb,),
        in_specs=[pl.BlockSpec((TOP_K, tb), lambda i: (0, i)),
                  pl.BlockSpec((TOP_K, tb), lambda i: (0, i)),
                  pl.BlockSpec((N_EXPERTS, 1), lambda i: (0, 0))],
        out_specs=pl.BlockSpec((TOP_K, tb), lambda i: (0, i)),
        compiler_params=_cparams(("arbitrary",)),
        name="dest_slots",
    )(idx_t, rank_t, group_start.astype(F32).reshape(N_EXPERTS, 1))


DISPATCH_TB = 256


def _rows_to_tiles(dst_scr, base, rows):
    n = rows.shape[0]
    for s in range(ROW_TILE):
        dst_scr[pl.ds(base + s, n, stride=ROW_TILE), :] = rows[:, s * LANES:(s + 1) * LANES]


def _tile_column(src_scr, base, n, s):
    return src_scr[pl.ds(base + s, n, stride=ROW_TILE), :]


def _row_tile(ref, row):
    return ref.at[pl.ds(pl.multiple_of(row * ROW_TILE, ROW_TILE), ROW_TILE), :]


def _dispatch_kernel(dest_ref, v_ref, xs_hbm, tile_scr, zero_scr, sem, zsem):
    i = pl.program_id(0)
    n = pl.num_programs(0)
    tb = DISPATCH_TB
    slot = i % 2
    base = pl.multiple_of(slot * (tb * ROW_TILE), ROW_TILE)
    _rows_to_tiles(tile_scr, base, v_ref[...])

    def issue(t, carry):
        src = _row_tile(tile_scr, slot * tb + t)
        for k in range(TOP_K):
            pltpu.make_async_copy(src, _row_tile(xs_hbm, dest_ref[k, t]), sem.at[slot]).start(priority=k % 2)
        return carry

    lax.fori_loop(0, tb, issue, 0)

    def wait_block(sl):
        blk = tile_scr.at[pl.ds(pl.multiple_of(sl * (tb * ROW_TILE), ROW_TILE), tb * ROW_TILE), :]
        for _ in range(TOP_K):
            pltpu.make_async_copy(blk, blk, sem.at[sl]).wait()

    @pl.when(i == 0)
    def _():
        zero_scr[...] = jnp.zeros(zero_scr.shape, F32)
        for j in range(GMM_TM // SUBLANES):
            cp = pltpu.make_async_copy(
                zero_scr, xs_hbm.at[pl.ds((N_ASSIGN + j * SUBLANES) * ROW_TILE, SUBLANES * ROW_TILE), :], zsem)
            cp.start()
            cp.wait()

    @pl.when(i > 0)
    def _():
        wait_block(1 - slot)

    @pl.when(i == n - 1)
    def _():
        wait_block(slot)


def _dispatch(dest_t, v_all):
    tb = DISPATCH_TB
    return pl.pallas_call(
        _dispatch_kernel,
        out_shape=jax.ShapeDtypeStruct((N_ROWS * ROW_TILE, LANES), F32),
        grid=(T_ALL // tb,),
        in_specs=[pl.BlockSpec((TOP_K, tb), lambda i: (0, i), memory_space=pltpu.SMEM),
                  pl.BlockSpec((tb, D_MODEL), lambda i: (i, 0))],
        out_specs=pl.BlockSpec(memory_space=pl.ANY),
        scratch_shapes=[pltpu.VMEM((2 * tb * ROW_TILE, LANES), F32),
                        pltpu.VMEM((SUBLANES * ROW_TILE, LANES), F32),
                        pltpu.SemaphoreType.DMA((2,)),
                        pltpu.SemaphoreType.DMA(())],
        compiler_params=_cparams(("arbitrary",)),
        name="dispatch",
    )(dest_t, v_all)


def _flat_rows(ref, row, n_rows):
    return ref.at[pl.ds(pl.multiple_of(row * ROW_TILE, ROW_TILE), n_rows * ROW_TILE), :]


def _tile_writes(y_scr, y_hbm, slot, row0, valid, sem):
    base = slot * GMM_TM
    parts = [(valid == GMM_TM, pltpu.make_async_copy(_flat_rows(y_scr, base, GMM_TM),
                                                     _flat_rows(y_hbm, row0, GMM_TM), sem))]
    off = jnp.int32(0)
    for p in TAIL_SIZES:
        bit = (valid & p) != 0
        take = jnp.logical_and(valid < GMM_TM, bit)
        parts.append((take, pltpu.make_async_copy(_flat_rows(y_scr, base + off, p),
                                                  _flat_rows(y_hbm, row0 + off, p), sem)))
        off = off + jnp.where(bit, p, 0)
    return parts


def _gmm_kernel(te_ref, tr_ref, tv_ref, nt_ref, xs_hbm, wg_ref, wu_ref, wd_ref, y_hbm,
                x_scr, y_scr, wgu_scr, wdn_scr, zero_scr, xsem, ysem, zsem):
    t = pl.program_id(0)
    n_tiles = nt_ref[0]
    slot = t % 2

    def x_copy(tile, sl):
        return pltpu.make_async_copy(_flat_rows(xs_hbm, tr_ref[tile], GMM_TM),
                                     _flat_rows(x_scr, sl * GMM_TM, GMM_TM), xsem.at[sl])

    @pl.when(jnp.logical_and(t == 0, n_tiles > 0))
    def _():
        x_copy(0, 0).start()

    @pl.when(t < n_tiles)
    def _():
        x_copy(t, slot).wait()

        @pl.when(t + 1 < n_tiles)
        def _():
            x_copy(t + 1, 1 - slot).start()

        new_expert = jnp.logical_or(t == 0, te_ref[t] != te_ref[jnp.maximum(t - 1, 0)])

        @pl.when(new_expert)
        def _():
            wgu_scr[:, :D_EXPERT] = wg_ref[0].astype(BF16)
            wgu_scr[:, D_EXPERT:] = wu_ref[0].astype(BF16)
            wdn_scr[...] = wd_ref[0].astype(BF16)

        base = pl.multiple_of(slot * (GMM_TM * ROW_TILE), ROW_TILE)
        x = jnp.concatenate([_tile_column(x_scr, base, GMM_TM, s).astype(BF16) for s in range(ROW_TILE)], axis=1)
        h = _dot(x, wgu_scr[...])
        act = (_silu(h[:, :D_EXPERT]) * h[:, D_EXPERT:]).astype(BF16)
        y = _dot(act, wdn_scr[...])

        @pl.when(t >= 2)
        def _():
            for pred, cp in _tile_writes(y_scr, y_hbm, slot, tr_ref[t - 2], tv_ref[t - 2], ysem.at[slot]):
                @pl.when(pred)
                def _():
                    cp.wait()

        _rows_to_tiles(y_scr, base, y)
        for pred, cp in _tile_writes(y_scr, y_hbm, slot, tr_ref[t], tv_ref[t], ysem.at[slot]):
            @pl.when(pred)
            def _():
                cp.start()

    @pl.when(t == pl.num_programs(0) - 1)
    def _():
        for back in (2, 1):
            tile = n_tiles - back

            @pl.when(tile >= 0)
            def _():
                sl = tile % 2
                safe = jnp.maximum(tile, 0)
                for pred, cp in _tile_writes(y_scr, y_hbm, sl, tr_ref[safe], tv_ref[safe], ysem.at[sl]):
                    @pl.when(pred)
                    def _():
                        cp.wait()

        zero_scr[...] = jnp.zeros(zero_scr.shape, F32)
        for j in range(GMM_TM // SUBLANES):
            cp = pltpu.make_async_copy(zero_scr, _flat_rows(y_hbm, N_ASSIGN + j * SUBLANES, SUBLANES), zsem)
            cp.start()
            cp.wait()


def _grouped_mlp(tile_expert, tile_row, tile_valid, n_info, xs, w_gate, w_up, w_down):
    grid_spec = pltpu.PrefetchScalarGridSpec(
        num_scalar_prefetch=4,
        grid=(N_TILES,),
        in_specs=[pl.BlockSpec(memory_space=pl.ANY),
                  pl.BlockSpec((1, D_MODEL, D_EXPERT), lambda t, te, tr, tv, nt: (te[t], 0, 0)),
                  pl.BlockSpec((1, D_MODEL, D_EXPERT), lambda t, te, tr, tv, nt: (te[t], 0, 0)),
                  pl.BlockSpec((1, D_EXPERT, D_MODEL), lambda t, te, tr, tv, nt: (te[t], 0, 0))],
        out_specs=pl.BlockSpec(memory_space=pl.ANY),
        scratch_shapes=[pltpu.VMEM((2 * GMM_TM * ROW_TILE, LANES), F32),
                        pltpu.VMEM((2 * GMM_TM * ROW_TILE, LANES), F32),
                        pltpu.VMEM((D_MODEL, 2 * D_EXPERT), BF16),
                        pltpu.VMEM((D_EXPERT, D_MODEL), BF16),
                        pltpu.VMEM((SUBLANES * ROW_TILE, LANES), F32),
                        pltpu.SemaphoreType.DMA((2,)),
                        pltpu.SemaphoreType.DMA((2,)),
                        pltpu.SemaphoreType.DMA(())],
    )
    return pl.pallas_call(
        _gmm_kernel,
        out_shape=jax.ShapeDtypeStruct((N_ROWS * ROW_TILE, LANES), F32),
        grid_spec=grid_spec,
        compiler_params=_cparams(("arbitrary",)),
        name="grouped_mlp",
    )(tile_expert, tile_row, tile_valid, n_info, xs, w_gate, w_up, w_down)


COMBINE_TB = 128


def _combine_kernel(dest_ref, dest_next_ref, y_hbm, pre_ref, w_ref, mod_ref, fw_ref, o_ref, buf, sem):
    i = pl.program_id(0)
    n = pl.num_programs(0)
    tb = COMBINE_TB
    slot = i % 2
    blk_rows = TOP_K * tb

    def issue_block(d_ref, sl):
        def issue(t, carry):
            for k in range(TOP_K):
                pltpu.make_async_copy(_row_tile(y_hbm, d_ref[k, t]),
                                      _row_tile(buf, sl * blk_rows + k * tb + t),
                                      sem.at[sl]).start(priority=k % 2)
            return carry

        lax.fori_loop(0, tb, issue, 0)

    @pl.when(i == 0)
    def _():
        issue_block(dest_ref, 0)

    @pl.when(i + 1 < n)
    def _():
        issue_block(dest_next_ref, 1 - slot)

    whole = _flat_rows(buf, slot * blk_rows, blk_rows)
    pltpu.make_async_copy(whole, whole, sem.at[slot]).wait()

    base = pl.multiple_of(slot * (blk_rows * ROW_TILE), ROW_TILE)
    w = w_ref[...]
    m = mod_ref[...]
    ssq = jnp.zeros((tb, 1), F32)
    for s in range(ROW_TILE):
        cols = slice(s * LANES, (s + 1) * LANES)
        routed = w[:, 0:1] * _tile_column(buf, base, tb, s)
        for k in range(1, TOP_K):
            routed = routed + w[:, k:k + 1] * _tile_column(buf, base + k * tb * ROW_TILE, tb, s)
        x2 = pre_ref[:, cols] + m[5:6, cols] * routed
        o_ref[:, cols] = x2
        ssq = ssq + jnp.sum(x2 * x2, axis=-1, keepdims=True)
    o_ref[...] = o_ref[...] * lax.rsqrt(ssq * (1.0 / D_MODEL) + EPS) * fw_ref[...]


def _combine(dest_t, y_rows, pre, w_tok, mod3, final_norm_w):
    tb = COMBINE_TB
    n_blocks = T_ALL // tb
    row = functools.partial(_mod_row_of_block, blocks_prompt=T_PROMPT // tb,
                            blocks_per_sample_seq=DEC_SEQ // tb)
    return pl.pallas_call(
        _combine_kernel,
        out_shape=jax.ShapeDtypeStruct((T_ALL, D_MODEL), F32),
        grid=(n_blocks,),
        in_specs=[pl.BlockSpec((TOP_K, tb), lambda i: (0, i), memory_space=pltpu.SMEM),
                  pl.BlockSpec((TOP_K, tb), lambda i: (0, jnp.minimum(i + 1, n_blocks - 1)),
                               memory_space=pltpu.SMEM),
                  pl.BlockSpec(memory_space=pl.ANY),
                  pl.BlockSpec((tb, D_MODEL), lambda i: (i, 0)),
                  pl.BlockSpec((tb, TOP_K), lambda i: (i, 0)),
                  pl.BlockSpec((None, N_MOD, D_MODEL), lambda i: (row(i), 0, 0)),
                  pl.BlockSpec((1, D_MODEL), lambda i: (0, 0))],
        out_specs=pl.BlockSpec((tb, D_MODEL), lambda i: (i, 0)),
        scratch_shapes=[pltpu.VMEM((2 * TOP_K * tb * ROW_TILE, LANES), F32),
                        pltpu.SemaphoreType.DMA((2,))],
        compiler_params=_cparams(("arbitrary",)),
        name="combine",
    )(dest_t, dest_t, y_rows, pre, w_tok, mod3, final_norm_w.reshape(1, D_MODEL))


def _tile_tables(counts):
    start = jnp.concatenate([jnp.zeros((1,), I32), jnp.cumsum(counts).astype(I32)])
    tiles = (counts + GMM_TM - 1) // GMM_TM
    tile_end = jnp.cumsum(tiles).astype(I32)
    n_tiles = tile_end[-1]
    t = jnp.arange(N_TILES, dtype=I32)
    expert = jnp.minimum(jnp.searchsorted(tile_end, t, side='right').astype(I32), N_EXPERTS - 1)
    local = t - (tile_end - tiles)[expert]
    row = start[expert] + local * GMM_TM
    valid = jnp.clip(counts[expert] - local * GMM_TM, 0, GMM_TM)
    live = t < n_tiles
    last = jnp.maximum(n_tiles - 1, 0)
    expert = jnp.where(live, expert, expert[last])
    row = jnp.where(live, row, 0)
    valid = jnp.where(live, valid, 0)
    info = jnp.reshape(n_tiles, (1,)).astype(I32)
    return start, expert.astype(I32), row.astype(I32), valid.astype(I32), info


def _head_expand_matrix():
    r = jnp.arange(LANES)[:, None]
    cidx = jnp.arange(2 * D_INNER)[None, :]
    direction = cidx // D_INNER
    head = (cidx % D_INNER) // HEAD_DIM
    return (r == direction * N_HEADS + head).astype(BF16)


def kernel(x_prompt, x_sample, state_ssm, c, c_ctx, norm1_w, norm2_w, w_mod, b_mod, w_in, ssm_conv_w, ssm_conv_b, ssm_dt_bias, ssm_a_log, ssm_d, ssm_norm_w, w_out_ssm, conf_conv_w, conf_conv_b, conf_ln_w, conf_ln_b, w_out_conf, w_o, router_w, router_bias, exp_w_gate, exp_w_up, exp_w_down, sh_w_gate, sh_w_up, sh_w_down, final_norm_w):
    x_all = jnp.concatenate([x_prompt.reshape(T_PROMPT, D_MODEL), x_sample.reshape(T_SAMPLE, D_MODEL)], axis=0)

    cc = jnp.zeros((MOD_ROWS, D_MODEL), F32).at[:DEC_BATCH].set(c).at[CTX_ROW].set(c_ctx)
    mod3 = _modulation(cc, w_mod[0], b_mod[0]).reshape(MOD_ROWS, N_MOD, D_MODEL)

    w = w_in[0]
    o_xbc, o_dt, o_cv, o_cg, o_gl = D_INNER, D_INNER + CONV_DIM, D_INNER + CONV_DIM + 2 * N_HEADS, \
        D_INNER + CONV_DIM + 2 * N_HEADS + D_MODEL, D_INNER + CONV_DIM + 2 * N_HEADS + 2 * D_MODEL
    w_main = jnp.concatenate([w[:, o_xbc:o_dt], w[:, :o_xbc], w[:, o_gl:], w[:, o_cv:o_cg], w[:, o_cg:o_gl]],
                             axis=1).astype(BF16)
    w_dt = jnp.pad(w[:, o_dt:o_cv], ((0, 0), (0, LANES - 2 * N_HEADS))).astype(BF16)
    proj, dt_raw = _in_projection(x_all, mod3, norm1_w[0], w_main, w_dt)

    pad_heads = lambda v: jnp.pad(v.reshape(1, 2 * N_HEADS), ((0, 0), (0, LANES - 2 * N_HEADS)))
    dtb = pad_heads(ssm_dt_bias[0])
    a_neg = pad_heads(-jnp.exp(ssm_a_log[0]))
    dvec = jnp.repeat(ssm_d[0], HEAD_DIM).reshape(1, D_INNER)
    nw = ssm_norm_w[0].reshape(1, D_INNER)
    e2 = _head_expand_matrix()
    xbc_p = _ssm_conv(proj, ssm_conv_w[0], ssm_conv_b[0], seq_len=SEQ, n_seq=BATCH, row_block0=0)
    xbc_s = _ssm_conv(proj, ssm_conv_w[0], ssm_conv_b[0], seq_len=DEC_SEQ, n_seq=DEC_BATCH,
                      row_block0=T_PROMPT // DEC_SEQ)
    y_p, fin = _ssd(xbc_p, dt_raw, proj, None, dtb, a_neg, dvec, nw, e2,
                    seq_len=SEQ, n_seq=BATCH, tok0=0, want_final=True)
    init = state_ssm.reshape(DEC_BATCH, 2, D_INNER, D_STATE)
    (y_s,) = _ssd(xbc_s, dt_raw, proj, init, dtb, a_neg, dvec, nw, e2,
                  seq_len=DEC_SEQ, n_seq=DEC_BATCH, tok0=T_PROMPT, want_final=False)
    y_ssm = jnp.concatenate([y_p, y_s], axis=0)

    v_p = _conformer(proj, conf_conv_w[0], conf_conv_b[0], conf_ln_w[0], conf_ln_b[0],
                     seq_len=SEQ, n_seq=BATCH, row_block0=0, grid_mode=False)
    v_s = _conformer(proj, conf_conv_w[0], conf_conv_b[0], conf_ln_w[0], conf_ln_b[0],
                     seq_len=DEC_SEQ, n_seq=DEC_BATCH, row_block0=T_PROMPT // DEC_SEQ, grid_mode=True)
    v_conf = jnp.concatenate([v_p, v_s], axis=0)

    sh_gu = jnp.concatenate([sh_w_gate[0], sh_w_up[0]], axis=1).astype(BF16)
    pre, v_all, scores = _merge(y_ssm, v_conf, proj, x_all, mod3, w_out_ssm[0].astype(BF16),
                                w_out_conf[0].astype(BF16), w_o[0].astype(BF16), norm2_w[0], router_w[0],
                                sh_gu, sh_w_down[0].astype(BF16))

    idx_t, w_t, rank_t, cnt = _route(scores, router_bias[0])
    counts = cnt[:, 0].astype(I32)
    start, tile_expert, tile_row, tile_valid, info = _tile_tables(counts)
    dest_t = _dest_slots(idx_t, rank_t, start[:N_EXPERTS])

    xs = _dispatch(dest_t, v_all)
    y_rows = _grouped_mlp(tile_expert, tile_row, tile_valid, info, xs, exp_w_gate[0], exp_w_up[0], exp_w_down[0])
    out = _combine(dest_t, y_rows, pre, w_t.T, mod3, final_norm_w)

    y_prompt = out[:T_PROMPT].reshape(BATCH, SEQ, D_MODEL)
    y_sample = out[T_PROMPT:].reshape(DEC_BATCH, DEC_SEQ, D_MODEL)
    new_state = fin.reshape(BATCH, 1, 2, N_HEADS, HEAD_DIM, D_STATE)
    return (y_prompt, y_sample, new_state)
```

```python
import functools

import jax
import jax.numpy as jnp
from jax import lax
from jax.experimental import pallas as pl
from jax.experimental.pallas import tpu as pltpu

F32 = jnp.float32
BF16 = jnp.bfloat16
I32 = jnp.int32

D_MODEL = 1024
BATCH = 32
SEQ = 256
DEC_BATCH = 8
DEC_SEQ = 1024
GRID_W = 64
GRID_H = DEC_SEQ // GRID_W
D_INNER = 2048
HEAD_DIM = 64
N_HEADS = 32
D_STATE = 128
N_GROUPS = 8
HEADS_PER_GROUP = N_HEADS // N_GROUPS
GROUP_W = HEADS_PER_GROUP * HEAD_DIM
D_CONV_SSM = 5
CHUNK = 128
CONV_DIM = D_INNER + 2 * N_GROUPS * D_STATE
CONF_K = 31
CONF_PAD = CONF_K // 2
N_EXPERTS = 256
TOP_K = 8
N_EXPERT_GROUPS = 8
EXPERTS_PER_GROUP = N_EXPERTS // N_EXPERT_GROUPS
TOPK_GROUPS = 4
D_EXPERT = 256
D_SHARED = 256
ROUTED_SCALE = 2.5
N_MOD = 6
EPS = 1e-6

T_PROMPT = BATCH * SEQ
T_SAMPLE = DEC_BATCH * DEC_SEQ
T_ALL = T_PROMPT + T_SAMPLE
N_ASSIGN = T_ALL * TOP_K
MOD_ROWS = 16
CTX_ROW = DEC_BATCH

SUBLANES = 8
LANES = 128
VMEM_LIMIT = 56 * 1024 * 1024

PROJ_W = CONV_DIM + D_INNER + 2 * D_MODEL + 2 * D_MODEL
COL_Z = CONV_DIM // D_INNER
COL_GL = COL_Z + 1
COL_CV = (CONV_DIM + 2 * D_INNER) // D_MODEL
COL_CG = COL_CV + 1

ROW_TILE = D_MODEL // LANES
GMM_TM = 256
N_ROWS = N_ASSIGN + GMM_TM
TAIL_SIZES = (128, 64, 32, 16, 8, 4, 2, 1)


def _cparams(sem, vmem=VMEM_LIMIT):
    return pltpu.CompilerParams(dimension_semantics=sem, vmem_limit_bytes=vmem)


def _silu(x):
    return x * jax.nn.sigmoid(x)


def _split2(x):
    hi = x.astype(BF16)
    lo = (x - hi.astype(F32)).astype(BF16)
    return hi, lo


def _split3(x):
    b1 = x.astype(BF16)
    r = x - b1.astype(F32)
    b2 = r.astype(BF16)
    b3 = (r - b2.astype(F32)).astype(BF16)
    return b1, b2, b3


def _dot(a, b):
    return jnp.dot(a, b, preferred_element_type=F32)


def _dot_exact_lhs(a_exact, b):
    b1, b2, b3 = _split3(b)
    return _dot(a_exact, b1) + _dot(a_exact, b2) + _dot(a_exact, b3)


def _dot_exact_rhs(a, b_exact):
    a1, a2, a3 = _split3(a)
    return _dot(a1, b_exact) + _dot(a2, b_exact) + _dot(a3, b_exact)


def _dot_f32(a, b):
    a1, a2 = _split2(a)
    b1, b2 = _split2(b)
    return _dot(a1, b1) + _dot(a1, b2) + _dot(a2, b1)


def _mod_row_of_block(i, blocks_prompt, blocks_per_sample_seq):
    return jnp.where(i < blocks_prompt, CTX_ROW, (i - blocks_prompt) // blocks_per_sample_seq)


def _mod_kernel(c_ref, w_ref, b_ref, o_ref):
    c = c_ref[...]
    o_ref[...] = _dot_f32(_silu(c), w_ref[...]) + b_ref[...]


def _modulation(cc, w_mod, b_mod):
    tn = 512
    n = N_MOD * D_MODEL
    return pl.pallas_call(
        _mod_kernel,
        out_shape=jax.ShapeDtypeStruct((MOD_ROWS, n), F32),
        grid=(n // tn,),
        in_specs=[pl.BlockSpec((MOD_ROWS, D_MODEL), lambda j: (0, 0)),
                  pl.BlockSpec((D_MODEL, tn), lambda j: (0, j)),
                  pl.BlockSpec((1, tn), lambda j: (0, j))],
        out_specs=pl.BlockSpec((MOD_ROWS, tn), lambda j: (0, j)),
        compiler_params=_cparams(("arbitrary",)),
        name="modulation",
    )(cc, w_mod, b_mod.reshape(1, n))


INPROJ_TM = 1024
INPROJ_TN = 1024


def _two_path_specs(block, n_prompt_blocks):
    last = n_prompt_blocks - 1
    return (pl.BlockSpec(block, lambda i, *_: (jnp.minimum(i, last), 0)),
            pl.BlockSpec(block, lambda i, *_: (jnp.maximum(i - n_prompt_blocks, 0), 0)))


def _pick_path(n_prompt_blocks, prompt_ref, sample_ref):
    return jnp.where(pl.program_id(0) < n_prompt_blocks, prompt_ref[...], sample_ref[...])


def _inproj_kernel(xp_ref, xs_ref, mod_ref, n1_ref, w_ref, wdt_ref, o_ref, dt_ref, u_scr):
    @pl.when(pl.program_id(1) == 0)
    def _():
        x = _pick_path(T_PROMPT // INPROJ_TM, xp_ref, xs_ref)
        ms = jnp.mean(x * x, axis=-1, keepdims=True)
        y = x * lax.rsqrt(ms + EPS) * n1_ref[...]
        m = mod_ref[...]
        u = (y * (1.0 + m[1:2]) + m[0:1]).astype(BF16)
        u_scr[...] = u
        dt_ref[...] = _dot(u, wdt_ref[...])

    o_ref[...] = _dot(u_scr[...], w_ref[...]).astype(BF16)


def _in_projection(x_p, x_s, mod3, norm1_w, w_main, w_dt):
    tm, tn = INPROJ_TM, INPROJ_TN
    row = functools.partial(_mod_row_of_block, blocks_prompt=T_PROMPT // tm,
                            blocks_per_sample_seq=DEC_SEQ // tm)
    return pl.pallas_call(
        _inproj_kernel,
        out_shape=(jax.ShapeDtypeStruct((T_ALL, PROJ_W), BF16),
                   jax.ShapeDtypeStruct((T_ALL, LANES), F32)),
        grid=(T_ALL // tm, PROJ_W // tn),
        in_specs=[*_two_path_specs((tm, D_MODEL), T_PROMPT // tm),
                  pl.BlockSpec((None, N_MOD, D_MODEL), lambda i, j: (row(i), 0, 0)),
                  pl.BlockSpec((1, D_MODEL), lambda i, j: (0, 0)),
                  pl.BlockSpec((D_MODEL, tn), lambda i, j: (0, j)),
                  pl.BlockSpec((D_MODEL, LANES), lambda i, j: (0, 0))],
        out_specs=(pl.BlockSpec((tm, tn), lambda i, j: (i, j)),
                   pl.BlockSpec((tm, LANES), lambda i, j: (i, 0))),
        scratch_shapes=[pltpu.VMEM((tm, D_MODEL), BF16)],
        compiler_params=_cparams(("arbitrary", "arbitrary")),
        name="in_projection",
    )(x_p, x_s, mod3, norm1_w.reshape(1, D_MODEL), w_main, w_dt)


SSMCONV_TN = 512


def _ssmconv_kernel(x_ref, w_ref, b_ref, o_ref, *, seq_len):
    x = x_ref[...].astype(F32)
    row = lax.broadcasted_iota(I32, x.shape, 0)
    half = D_CONV_SSM // 2
    acc = b_ref[...] + w_ref[half:half + 1, :] * x
    for k in range(D_CONV_SSM):
        s = k - half
        if s == 0:
            continue
        shifted = pltpu.roll(x, shift=(-s) % seq_len, axis=0)
        valid = jnp.logical_and(row + s >= 0, row + s < seq_len)
        acc = acc + w_ref[k:k + 1, :] * jnp.where(valid, shifted, 0.0)
    o_ref[...] = _silu(acc).astype(BF16)


def _ssm_conv(proj, conv_w, conv_b, *, seq_len, n_seq, row_block0):
    tn = SSMCONV_TN
    return pl.pallas_call(
        functools.partial(_ssmconv_kernel, seq_len=seq_len),
        out_shape=jax.ShapeDtypeStruct((n_seq * seq_len, CONV_DIM), BF16),
        grid=(n_seq, CONV_DIM // tn),
        in_specs=[pl.BlockSpec((seq_len, tn), lambda b, j: (row_block0 + b, j)),
                  pl.BlockSpec((D_CONV_SSM, tn), lambda b, j: (0, j)),
                  pl.BlockSpec((1, tn), lambda b, j: (0, j))],
        out_specs=pl.BlockSpec((seq_len, tn), lambda b, j: (b, j)),
        compiler_params=_cparams(("arbitrary", "arbitrary")),
        name="ssm_conv",
    )(proj, conv_w, conv_b.reshape(1, CONV_DIM))


def _tri_masks():
    ii = lax.broadcasted_iota(I32, (CHUNK, CHUNK), 0)
    jj = lax.broadcasted_iota(I32, (CHUNK, CHUNK), 1)
    return ii, jj


def _chunk_decays(dt_ref, dtb_ref, a_ref):
    ii, jj = _tri_masks()
    pre = dt_ref[...] + dtb_ref[...]
    dt = jnp.maximum(pre, 0.0) + jnp.log(1.0 + jnp.exp(-jnp.abs(pre)))
    la = dt * a_ref[...]
    tri_lo = jnp.where(jj <= ii, 1.0, 0.0).astype(BF16)
    tri_up = jnp.where(jj >= ii, 1.0, 0.0).astype(BF16)
    cs_prefix = _dot_exact_lhs(tri_lo, la)
    cs_suffix = _dot_exact_lhs(tri_up, la)
    fwd_lane = lax.broadcasted_iota(I32, (CHUNK, LANES), 1) < N_HEADS
    cs = jnp.where(fwd_lane, cs_prefix, cs_suffix)
    tot = jnp.where(fwd_lane[0:1], cs_prefix[CHUNK - 1:CHUNK, :], cs_suffix[0:1, :])
    return dt, cs, tot


def _transpose_blocks(src, rows, cols):
    out_rows = []
    for cb in range(cols // LANES):
        pieces = [src[rb * LANES:(rb + 1) * LANES, cb * LANES:(cb + 1) * LANES].T
                  for rb in range(rows // LANES)]
        out_rows.append(jnp.concatenate(pieces, axis=1) if len(pieces) > 1 else pieces[0])
    return jnp.concatenate(out_rows, axis=0) if len(out_rows) > 1 else out_rows[0]


def _ssd_kernel(*refs, n_chunks, has_init, want_final):
    it = iter(refs)
    xbc_ref, dt_ref, z_ref = next(it), next(it), next(it)
    init_ref = next(it) if has_init else None
    dtb_ref, a_ref, dvec_ref, nw_ref, e2_ref = next(it), next(it), next(it), next(it), next(it)
    out_ref = next(it)
    fin_ref = next(it) if want_final else None
    sf_scr, sb_scr, df_scr, db_scr = next(it), next(it), next(it), next(it)

    phase = pl.program_id(1)
    c = pl.program_id(2)

    @pl.when(phase == 0)
    def _chunk_states():
        dt, cs, tot = _chunk_decays(dt_ref, dtb_ref, a_ref)
        w_in = dt * jnp.exp(tot - cs)
        w_exp = _dot_exact_rhs(w_in, e2_ref[...])
        dec = jnp.exp(jnp.broadcast_to(tot, (SUBLANES, LANES)))
        dec_exp = _dot_exact_rhs(dec, e2_ref[...])
        df_scr[c] = dec_exp[:, :D_INNER]
        db_scr[c] = dec_exp[:, D_INNER:]
        for g in range(N_GROUPS):
            lo = g * GROUP_W
            xg = xbc_ref[:, lo:lo + GROUP_W].astype(F32)
            xd_f = (xg * w_exp[:, lo:lo + GROUP_W]).astype(BF16)
            xd_b = (xg * w_exp[:, D_INNER + lo:D_INNER + lo + GROUP_W]).astype(BF16)
            bg = xbc_ref[:, D_INNER + g * D_STATE:D_INNER + (g + 1) * D_STATE]
            bg_t = bg.astype(F32).T.astype(BF16)
            sf_scr[c, :, lo:lo + GROUP_W] = _dot(bg_t, xd_f)
            sb_scr[c, :, lo:lo + GROUP_W] = _dot(bg_t, xd_b)

    @pl.when(jnp.logical_and(phase == 1, c == 0))
    def _recurrence():
        for g in range(N_GROUPS):
            lo = g * GROUP_W
            if has_init:
                prev_f = _transpose_blocks(init_ref[0, 0, lo:lo + GROUP_W, :], GROUP_W, D_STATE)
                prev_b = _transpose_blocks(init_ref[0, 1, lo:lo + GROUP_W, :], GROUP_W, D_STATE)
            else:
                prev_f = jnp.zeros((D_STATE, GROUP_W), F32)
                prev_b = jnp.zeros((D_STATE, GROUP_W), F32)
            for cc in range(n_chunks):
                s = sf_scr[cc, :, lo:lo + GROUP_W]
                sf_scr[cc, :, lo:lo + GROUP_W] = prev_f
                prev_f = df_scr[cc, 0:1, lo:lo + GROUP_W] * prev_f + s
            for cc in reversed(range(n_chunks)):
                s = sb_scr[cc, :, lo:lo + GROUP_W]
                sb_scr[cc, :, lo:lo + GROUP_W] = prev_b
                prev_b = db_scr[cc, 0:1, lo:lo + GROUP_W] * prev_b + s
            if want_final:
                fin_ref[0, 0, lo:lo + GROUP_W, :] = _transpose_blocks(prev_f, D_STATE, GROUP_W)
                fin_ref[0, 1, lo:lo + GROUP_W, :] = _transpose_blocks(prev_b, D_STATE, GROUP_W)

    @pl.when(phase == 1)
    def _outputs():
        ii, jj = _tri_masks()
        dt, cs, _ = _chunk_decays(dt_ref, dtb_ref, a_ref)
        out_dec = _dot_exact_rhs(jnp.exp(cs), e2_ref[...])
        cs_t = cs.T
        dt_t = dt.T
        causal = ii >= jj
        anti = jj >= ii
        neg = jnp.float32(-1e30)
        left = lax.broadcasted_iota(I32, (CHUNK, LANES), 1) < HEAD_DIM
        for g in range(N_GROUPS):
            lo = g * GROUP_W
            bg = xbc_ref[:, D_INNER + g * D_STATE:D_INNER + (g + 1) * D_STATE]
            cg = xbc_ref[:, D_INNER + N_GROUPS * D_STATE + g * D_STATE:
                         D_INNER + N_GROUPS * D_STATE + (g + 1) * D_STATE]
            cb = lax.dot_general(cg, bg, (((1,), (1,)), ((), ())), preferred_element_type=F32)
            pf = sf_scr[c, :, lo:lo + GROUP_W].astype(BF16)
            pb = sb_scr[c, :, lo:lo + GROUP_W].astype(BF16)
            y_off = (_dot(cg, pf) * out_dec[:, lo:lo + GROUP_W]
                     + _dot(cg, pb) * out_dec[:, D_INNER + lo:D_INNER + lo + GROUP_W])
            pairs = []
            for m in range(HEADS_PER_GROUP // 2):
                x_pair = xbc_ref[:, lo + m * LANES:lo + (m + 1) * LANES]
                ys = []
                for hh in range(2):
                    h = g * HEADS_PER_GROUP + 2 * m + hh
                    hb = N_HEADS + h
                    seg_f = jnp.where(causal, cs[:, h:h + 1] - cs_t[h:h + 1, :], neg)
                    seg_b = jnp.where(anti, cs[:, hb:hb + 1] - cs_t[hb:hb + 1, :], neg)
                    mix = (jnp.exp(seg_f) * dt_t[h:h + 1, :] + jnp.exp(seg_b) * dt_t[hb:hb + 1, :])
                    ys.append(_dot((cb * mix).astype(BF16), x_pair))
                pairs.append(jnp.where(left, ys[0], ys[1]))
            y_diag = jnp.concatenate(pairs, axis=1)
            xg = xbc_ref[:, lo:lo + GROUP_W].astype(F32)
            y = y_diag + y_off + dvec_ref[:, lo:lo + GROUP_W] * xg
            zg = z_ref[:, lo:lo + GROUP_W].astype(F32)
            y = y * _silu(zg)
            ms = jnp.mean(y * y, axis=-1, keepdims=True)
            out_ref[:, lo:lo + GROUP_W] = (y * lax.rsqrt(ms + EPS) * nw_ref[:, lo:lo + GROUP_W]).astype(BF16)


def _ssd(xbc, dt_raw, proj, init, dtb, a_neg, dvec, norm_w, e2, *, seq_len, n_seq, tok0, want_final):
    nc = seq_len // CHUNK
    blk0 = tok0 // CHUNK
    has_init = init is not None
    chunk_map = lambda b, p, c: (b * nc + c, 0)
    in_specs = [pl.BlockSpec((CHUNK, CONV_DIM), chunk_map),
                pl.BlockSpec((CHUNK, LANES), lambda b, p, c: (blk0 + b * nc + c, 0)),
                pl.BlockSpec((CHUNK, D_INNER), lambda b, p, c: (blk0 + b * nc + c * p, COL_Z))]
    args = [xbc, dt_raw, proj]
    if has_init:
        in_specs.append(pl.BlockSpec((1, 2, D_INNER, D_STATE), lambda b, p, c: (b, 0, 0, 0)))
        args.append(init)
    const = lambda b, p, c: (0, 0)
    in_specs += [pl.BlockSpec((1, LANES), const), pl.BlockSpec((1, LANES), const),
                 pl.BlockSpec((1, D_INNER), const), pl.BlockSpec((1, D_INNER), const),
                 pl.BlockSpec((LANES, 2 * D_INNER), const)]
    args += [dtb, a_neg, dvec, norm_w, e2]
    out_shape = [jax.ShapeDtypeStruct((n_seq * seq_len, D_INNER), BF16)]
    out_specs = [pl.BlockSpec((CHUNK, D_INNER), lambda b, p, c: (b * nc + c * p, 0))]
    if want_final:
        out_shape.append(jax.ShapeDtypeStruct((n_seq, 2, D_INNER, D_STATE), F32))
        out_specs.append(pl.BlockSpec((1, 2, D_INNER, D_STATE), lambda b, p, c: (b, 0, 0, 0)))
    res = pl.pallas_call(
        functools.partial(_ssd_kernel, n_chunks=nc, has_init=has_init, want_final=want_final),
        out_shape=tuple(out_shape),
        grid=(n_seq, 2, nc),
        in_specs=in_specs,
        out_specs=tuple(out_specs),
        scratch_shapes=[pltpu.VMEM((nc, D_STATE, D_INNER), F32),
                        pltpu.VMEM((nc, D_STATE, D_INNER), F32),
                        pltpu.VMEM((nc, SUBLANES, D_INNER), F32),
                        pltpu.VMEM((nc, SUBLANES, D_INNER), F32)],
        compiler_params=_cparams(("arbitrary", "arbitrary", "arbitrary")),
        name="ssd_final" if want_final else "ssd_init",
    )(*args)
    return res


def _layernorm_silu(v, lw_ref, lb_ref):
    mu = jnp.mean(v, axis=-1, keepdims=True)
    d = v - mu
    var = jnp.mean(d * d, axis=-1, keepdims=True)
    return _silu(d * lax.rsqrt(var + EPS) * lw_ref[...] + lb_ref[...])


def _conf_seq_kernel(cv_ref, cg_ref, w_ref, b_ref, lw_ref, lb_ref, o_ref, pad_scr, acc_scr):
    seq_len = cv_ref.shape[0]
    base = 2 * SUBLANES
    v = cv_ref[...].astype(F32) * jax.nn.sigmoid(cg_ref[...].astype(F32))
    pad_scr[0:base, :] = jnp.zeros((base, D_MODEL), F32)
    pad_scr[base + seq_len:base + seq_len + base, :] = jnp.zeros((base, D_MODEL), F32)
    pad_scr[base:base + seq_len, :] = v
    rt, ct = 128, 256
    for r0 in range(0, seq_len, rt):
        for c0 in range(0, D_MODEL, ct):
            acc = jnp.broadcast_to(b_ref[:, c0:c0 + ct], (rt, ct))
            for k in range(CONF_K):
                off = base + r0 + k - CONF_PAD
                acc = acc + w_ref[k:k + 1, c0:c0 + ct] * pad_scr[off:off + rt, c0:c0 + ct]
            acc_scr[r0:r0 + rt, c0:c0 + ct] = acc
    o_ref[...] = _layernorm_silu(acc_scr[...], lw_ref, lb_ref).astype(BF16)


def _conf_grid_kernel(cv_ref, cg_ref, w_ref, b_ref, lw_ref, lb_ref, o_ref, v_scr, pad_scr, acc_scr):
    half = D_MODEL // 2
    base = 2 * SUBLANES
    stride = GRID_W + 2 * base
    v_scr[...] = cv_ref[...].astype(F32) * jax.nn.sigmoid(cg_ref[...].astype(F32))
    pad_scr[...] = jnp.zeros(pad_scr.shape, F32)
    for r in range(GRID_H):
        pad_scr[r * stride + base:r * stride + base + GRID_W, :] = v_scr[r * GRID_W:(r + 1) * GRID_W, 0:half]
    ct = 256
    for r in range(GRID_H):
        for c0 in range(0, half, ct):
            acc = jnp.broadcast_to(b_ref[:, c0:c0 + ct], (GRID_W, ct))
            for k in range(CONF_K):
                off = r * stride + base + k - CONF_PAD
                acc = acc + w_ref[k:k + 1, c0:c0 + ct] * pad_scr[off:off + GRID_W, c0:c0 + ct]
            acc_scr[r * GRID_W:(r + 1) * GRID_W, c0:c0 + ct] = acc
        for c0 in range(half, D_MODEL, ct):
            acc = jnp.broadcast_to(b_ref[:, c0:c0 + ct], (GRID_W, ct))
            for r2 in range(GRID_H):
                k = r2 - r + CONF_PAD
                acc = acc + w_ref[k:k + 1, c0:c0 + ct] * v_scr[r2 * GRID_W:(r2 + 1) * GRID_W, c0:c0 + ct]
            acc_scr[r * GRID_W:(r + 1) * GRID_W, c0:c0 + ct] = acc
    o_ref[...] = _layernorm_silu(acc_scr[...], lw_ref, lb_ref).astype(BF16)


def _conformer(proj, conv_w, conv_b, ln_w, ln_b, *, seq_len, n_seq, row_block0, grid_mode):
    base = 2 * SUBLANES
    if grid_mode:
        body = _conf_grid_kernel
        scratch = [pltpu.VMEM((seq_len, D_MODEL), F32),
                   pltpu.VMEM((GRID_H * (GRID_W + 2 * base), D_MODEL // 2), F32),
                   pltpu.VMEM((seq_len, D_MODEL), F32)]
    else:
        body = _conf_seq_kernel
        scratch = [pltpu.VMEM((seq_len + 2 * base, D_MODEL), F32),
                   pltpu.VMEM((seq_len, D_MODEL), F32)]
    const = lambda b: (0, 0)
    return pl.pallas_call(
        body,
        out_shape=jax.ShapeDtypeStruct((n_seq * seq_len, D_MODEL), BF16),
        grid=(n_seq,),
        in_specs=[pl.BlockSpec((seq_len, D_MODEL), lambda b: (row_block0 + b, COL_CV)),
                  pl.BlockSpec((seq_len, D_MODEL), lambda b: (row_block0 + b, COL_CG)),
                  pl.BlockSpec((CONF_K, D_MODEL), const),
                  pl.BlockSpec((1, D_MODEL), const),
                  pl.BlockSpec((1, D_MODEL), const),
                  pl.BlockSpec((1, D_MODEL), const)],
        out_specs=pl.BlockSpec((seq_len, D_MODEL), lambda b: (b, 0)),
        scratch_shapes=scratch,
        compiler_params=_cparams(("arbitrary",)),
        name="conformer_grid" if grid_mode else "conformer_seq",
    )(proj, proj, conv_w, conv_b.reshape(1, D_MODEL), ln_w.reshape(1, D_MODEL), ln_b.reshape(1, D_MODEL))


MERGE_TM = 256


def _merge_kernel(yap_ref, yas_ref, ybp_ref, ybs_ref, gl_ref, xp_ref, xs_ref, mod_ref, wa_ref, wb_ref, wo_ref,
                  n2_ref, rw_ref, sgu_ref, sd_ref, pre_ref, v_ref, s_ref):
    npb = T_PROMPT // MERGE_TM
    y_a = _dot(_pick_path(npb, yap_ref, yas_ref), wa_ref[...])
    y_b = _dot(_pick_path(npb, ybp_ref, ybs_ref), wb_ref[...])
    gates = jax.nn.sigmoid(gl_ref[...].astype(F32))
    mix = gates[:, :D_MODEL] * y_a + gates[:, D_MODEL:] * y_b
    out = _dot(mix.astype(BF16), wo_ref[...])
    m = mod_ref[...]
    x1 = _pick_path(npb, xp_ref, xs_ref) + m[2:3] * out
    ms = jnp.mean(x1 * x1, axis=-1, keepdims=True)
    v = x1 * lax.rsqrt(ms + EPS) * n2_ref[...] * (1.0 + m[4:5]) + m[3:4]
    v_ref[...] = v
    s_ref[...] = jax.nn.sigmoid(_dot_f32(v, rw_ref[...]))
    hgu = _dot(v.astype(BF16), sgu_ref[...])
    act = _silu(hgu[:, :D_SHARED]) * hgu[:, D_SHARED:]
    shared = _dot(act.astype(BF16), sd_ref[...])
    pre_ref[...] = x1 + m[5:6] * shared


def _merge(y_ssm_p, y_ssm_s, v_conf_p, v_conf_s, proj, x_p, x_s, mod3, w_out_ssm, w_out_conf, w_o, norm2_w,
           router_w, sh_gu, sh_d):
    tm = MERGE_TM
    npb = T_PROMPT // tm
    row = functools.partial(_mod_row_of_block, blocks_prompt=npb, blocks_per_sample_seq=DEC_SEQ // tm)
    const = lambda i: (0, 0)
    return pl.pallas_call(
        _merge_kernel,
        out_shape=(jax.ShapeDtypeStruct((T_ALL, D_MODEL), F32),
                   jax.ShapeDtypeStruct((T_ALL, D_MODEL), F32),
                   jax.ShapeDtypeStruct((T_ALL, N_EXPERTS), F32)),
        grid=(T_ALL // tm,),
        in_specs=[*_two_path_specs((tm, D_INNER), npb),
                  *_two_path_specs((tm, D_MODEL), npb),
                  pl.BlockSpec((tm, 2 * D_MODEL), lambda i: (i, COL_GL)),
                  *_two_path_specs((tm, D_MODEL), npb),
                  pl.BlockSpec((None, N_MOD, D_MODEL), lambda i: (row(i), 0, 0)),
                  pl.BlockSpec((D_INNER, D_MODEL), const),
                  pl.BlockSpec((D_MODEL, D_MODEL), const),
                  pl.BlockSpec((D_MODEL, D_MODEL), const),
                  pl.BlockSpec((1, D_MODEL), const),
                  pl.BlockSpec((D_MODEL, N_EXPERTS), const),
                  pl.BlockSpec((D_MODEL, 2 * D_SHARED), const),
                  pl.BlockSpec((D_SHARED, D_MODEL), const)],
        out_specs=(pl.BlockSpec((tm, D_MODEL), lambda i: (i, 0)),
                   pl.BlockSpec((tm, D_MODEL), lambda i: (i, 0)),
                   pl.BlockSpec((tm, N_EXPERTS), lambda i: (i, 0))),
        compiler_params=_cparams(("arbitrary",)),
        name="merge",
    )(y_ssm_p, y_ssm_s, v_conf_p, v_conf_s, proj, x_p, x_s, mod3, w_out_ssm, w_out_conf, w_o,
      norm2_w.reshape(1, D_MODEL), router_w, sh_gu, sh_d)


ROUTE_TB = 256


def _first_index_of_max(vals, iota, sentinel):
    m = jnp.max(vals, axis=0, keepdims=True)
    idx = jnp.min(jnp.where(vals == m, iota, jnp.float32(sentinel)), axis=0, keepdims=True)
    return m, idx


def _route_kernel(s_ref, bias_ref, idx_ref, w_ref, rank_ref, cnt_ref, run_scr):
    i = pl.program_id(0)
    tb = ROUTE_TB
    neg = jnp.float32(-jnp.inf)

    @pl.when(i == 0)
    def _():
        run_scr[...] = jnp.zeros(run_scr.shape, F32)

    s_t = s_ref[...].T
    sb_t = s_t + bias_ref[...]
    eiota = lax.broadcasted_iota(I32, (N_EXPERTS, tb), 0).astype(F32)

    liota = lax.broadcasted_iota(I32, (EXPERTS_PER_GROUP, tb), 0).astype(F32)
    gscores = []
    for g in range(N_EXPERT_GROUPS):
        blk = sb_t[g * EXPERTS_PER_GROUP:(g + 1) * EXPERTS_PER_GROUP, :]
        m1, i1 = _first_index_of_max(blk, liota, EXPERTS_PER_GROUP)
        m2 = jnp.max(jnp.where(liota == i1, neg, blk), axis=0, keepdims=True)
        gscores.append(m1 + m2)
    gs = jnp.concatenate(gscores, axis=0)
    giota = lax.broadcasted_iota(I32, (N_EXPERT_GROUPS, tb), 0).astype(F32)
    gsel = jnp.zeros((N_EXPERT_GROUPS, tb), F32)
    for _ in range(TOPK_GROUPS):
        _, gi = _first_index_of_max(gs, giota, N_EXPERT_GROUPS)
        hit = giota == gi
        gsel = jnp.where(hit, 1.0, gsel)
        gs = jnp.where(hit, neg, gs)
    emask = jnp.concatenate(
        [jnp.broadcast_to(gsel[g:g + 1, :], (EXPERTS_PER_GROUP, tb)) for g in range(N_EXPERT_GROUPS)], axis=0)
    masked = jnp.where(emask > 0.5, sb_t, neg)

    onehots, idxs, wts = [], [], []
    for _ in range(TOP_K):
        _, ei = _first_index_of_max(masked, eiota, N_EXPERTS)
        hit = eiota == ei
        onehots.append(hit)
        idxs.append(ei)
        wts.append(jnp.sum(jnp.where(hit, s_t, 0.0), axis=0, keepdims=True))
        masked = jnp.where(hit, neg, masked)
    w = jnp.concatenate(wts, axis=0)
    w_ref[...] = w / jnp.sum(w, axis=0, keepdims=True) * ROUTED_SCALE
    idx_ref[...] = jnp.concatenate(idxs, axis=0).astype(I32)

    assign = jnp.zeros((N_EXPERTS, tb), F32)
    for hit in onehots:
        assign = jnp.where(hit, 1.0, assign)
    assign_b = assign.astype(BF16)
    ti = lax.broadcasted_iota(I32, (tb, tb), 0)
    tj = lax.broadcasted_iota(I32, (tb, tb), 1)
    before = jnp.where(ti < tj, 1.0, 0.0).astype(BF16)
    within = _dot(assign_b, before)
    run = run_scr[...]
    total = within + jnp.concatenate([run] * (tb // LANES), axis=1)
    rank_ref[...] = jnp.concatenate(
        [jnp.sum(jnp.where(hit, total, 0.0), axis=0, keepdims=True) for hit in onehots], axis=0).astype(I32)
    new_run = run + _dot(assign_b, jnp.ones((tb, LANES), BF16))
    run_scr[...] = new_run
    cnt_ref[...] = new_run


def _route(scores, router_bias):
    tb = ROUTE_TB
    return pl.pallas_call(
        _route_kernel,
        out_shape=(jax.ShapeDtypeStruct((TOP_K, T_ALL), I32),
                   jax.ShapeDtypeStruct((TOP_K, T_ALL), F32),
                   jax.ShapeDtypeStruct((TOP_K, T_ALL), I32),
                   jax.ShapeDtypeStruct((N_EXPERTS, LANES), F32)),
        grid=(T_ALL // tb,),
        in_specs=[pl.BlockSpec((tb, N_EXPERTS), lambda i: (i, 0)),
                  pl.BlockSpec((N_EXPERTS, 1), lambda i: (0, 0))],
        out_specs=(pl.BlockSpec((TOP_K, tb), lambda i: (0, i)),
                   pl.BlockSpec((TOP_K, tb), lambda i: (0, i)),
                   pl.BlockSpec((TOP_K, tb), lambda i: (0, i)),
                   pl.BlockSpec((N_EXPERTS, LANES), lambda i: (0, 0))),
        scratch_shapes=[pltpu.VMEM((N_EXPERTS, LANES), F32)],
        compiler_params=_cparams(("arbitrary",)),
        name="route",
    )(scores, router_bias.reshape(N_EXPERTS, 1))


def _dest_kernel(idx_ref, rank_ref, start_ref, dest_ref):
    tb = idx_ref.shape[1]
    eiota = lax.broadcasted_iota(I32, (N_EXPERTS, tb), 0)
    start = jnp.broadcast_to(start_ref[...], (N_EXPERTS, tb))
    idx = idx_ref[...]
    rows = [jnp.sum(jnp.where(eiota == idx[k:k + 1, :], start, 0.0), axis=0, keepdims=True)
            for k in range(TOP_K)]
    dest_ref[...] = jnp.concatenate(rows, axis=0).astype(I32) + rank_ref[...]


def _dest_slots(idx_t, rank_t, group_start):
    tb = 512
    return pl.pallas_call(
        _dest_kernel,
        out_shape=jax.ShapeDtypeStruct((TOP_K, T_ALL), I32),
        grid=(T_ALL // tb,),
        in_specs=[pl.BlockSpec((TOP_K, tb), lambda i: (0, i)),
                  pl.BlockSpec((TOP_K, tb), lambda i: (0, i)),
                  pl.BlockSpec((N_EXPERTS, 1), lambda i: (0, 0))],
        out_specs=pl.BlockSpec((TOP_K, tb), lambda i: (0, i)),
        compiler_params=_cparams(("arbitrary",)),
        name="dest_slots",
    )(idx_t, rank_t, group_start.astype(F32).reshape(N_EXPERTS, 1))


DISPATCH_TB = 256


def _rows_to_tiles(dst_scr, base, rows):
    n = rows.shape[0]
    for s in range(ROW_TILE):
        dst_scr[pl.ds(base + s, n, stride=ROW_TILE), :] = rows[:, s * LANES:(s + 1) * LANES]


def _tile_column(src_scr, base, n, s):
    return src_scr[pl.ds(base + s, n, stride=ROW_TILE), :]


def _row_tile(ref, row):
    return ref.at[pl.ds(pl.multiple_of(row * ROW_TILE, ROW_TILE), ROW_TILE), :]


def _dispatch_kernel(dest_ref, v_ref, xs_hbm, tile_scr, zero_scr, sem, zsem):
    i = pl.program_id(0)
    n = pl.num_programs(0)
    tb = DISPATCH_TB
    slot = i % 2
    base = pl.multiple_of(slot * (tb * ROW_TILE), ROW_TILE)
    _rows_to_tiles(tile_scr, base, v_ref[...])

    def issue(t, carry):
        src = _row_tile(tile_scr, slot * tb + t)
        for k in range(TOP_K):
            pltpu.make_async_copy(src, _row_tile(xs_hbm, dest_ref[k, t]), sem.at[slot]).start(priority=k % 2)
        return carry

    lax.fori_loop(0, tb, issue, 0)

    def wait_block(sl):
        blk = tile_scr.at[pl.ds(pl.multiple_of(sl * (tb * ROW_TILE), ROW_TILE), tb * ROW_TILE), :]
        for _ in range(TOP_K):
            pltpu.make_async_copy(blk, blk, sem.at[sl]).wait()

    @pl.when(i == 0)
    def _():
        zero_scr[...] = jnp.zeros(zero_scr.shape, F32)
        for j in range(GMM_TM // SUBLANES):
            cp = pltpu.make_async_copy(
                zero_scr, xs_hbm.at[pl.ds((N_ASSIGN + j * SUBLANES) * ROW_TILE, SUBLANES * ROW_TILE), :], zsem)
            cp.start()
            cp.wait()

    @pl.when(i > 0)
    def _():
        wait_block(1 - slot)

    @pl.when(i == n - 1)
    def _():
        wait_block(slot)


def _dispatch(dest_t, v_all):
    tb = DISPATCH_TB
    return pl.pallas_call(
        _dispatch_kernel,
        out_shape=jax.ShapeDtypeStruct((N_ROWS * ROW_TILE, LANES), F32),
        grid=(T_ALL // tb,),
        in_specs=[pl.BlockSpec((TOP_K, tb), lambda i: (0, i), memory_space=pltpu.SMEM),
                  pl.BlockSpec((tb, D_MODEL), lambda i: (i, 0))],
        out_specs=pl.BlockSpec(memory_space=pl.ANY),
        scratch_shapes=[pltpu.VMEM((2 * tb * ROW_TILE, LANES), F32),
                        pltpu.VMEM((SUBLANES * ROW_TILE, LANES), F32),
                        pltpu.SemaphoreType.DMA((2,)),
                        pltpu.SemaphoreType.DMA(())],
        compiler_params=_cparams(("arbitrary",)),
        name="dispatch",
    )(dest_t, v_all)


def _flat_rows(ref, row, n_rows):
    return ref.at[pl.ds(pl.multiple_of(row * ROW_TILE, ROW_TILE), n_rows * ROW_TILE), :]


def _tile_writes(y_scr, y_hbm, slot, row0, valid, sem):
    base = slot * GMM_TM
    parts = [(valid == GMM_TM, pltpu.make_async_copy(_flat_rows(y_scr, base, GMM_TM),
                                                     _flat_rows(y_hbm, row0, GMM_TM), sem))]
    off = jnp.int32(0)
    for p in TAIL_SIZES:
        bit = (valid & p) != 0
        take = jnp.logical_and(valid < GMM_TM, bit)
        parts.append((take, pltpu.make_async_copy(_flat_rows(y_scr, base + off, p),
                                                  _flat_rows(y_hbm, row0 + off, p), sem)))
        off = off + jnp.where(bit, p, 0)
    return parts


ST_TILES, ST_ROW, ST_VALID = 0, 1, 3


def _gmm_kernel(start_ref, xs_hbm, wg_ref, wu_ref, wd_ref, y_hbm,
                x_scr, y_scr, wgu_scr, wdn_scr, zero_scr, st_ref, xsem, ysem, zsem):
    e = pl.program_id(0)
    row_lo = start_ref[e]
    row_hi = start_ref[e + 1]
    n_rows = row_hi - row_lo
    n_tiles = lax.shift_right_logical(n_rows + (GMM_TM - 1), GMM_TM.bit_length() - 1)

    def x_copy(row, sl):
        return pltpu.make_async_copy(_flat_rows(xs_hbm, row, GMM_TM),
                                     _flat_rows(x_scr, sl * GMM_TM, GMM_TM), xsem.at[sl])

    def wait_writes(sl):
        @pl.when(st_ref[ST_VALID + sl] > 0)
        def _():
            for pred, cp in _tile_writes(y_scr, y_hbm, sl, st_ref[ST_ROW + sl], st_ref[ST_VALID + sl],
                                         ysem.at[sl]):
                @pl.when(pred)
                def _():
                    cp.wait()
            st_ref[ST_VALID + sl] = 0

    @pl.when(e == 0)
    def _():
        for j in range(ST_VALID + 2):
            st_ref[j] = 0

    done = st_ref[ST_TILES]

    @pl.when(n_tiles > 0)
    def _():
        @pl.when(done == 0)
        def _():
            x_copy(row_lo, 0).start()

        wgu_scr[:, :D_EXPERT] = wg_ref[0].astype(BF16)
        wgu_scr[:, D_EXPERT:] = wu_ref[0].astype(BF16)
        wdn_scr[...] = wd_ref[0].astype(BF16)

        def tile(i, carry):
            slot = (done + i) % 2
            row0 = row_lo + i * GMM_TM
            valid = jnp.minimum(n_rows - i * GMM_TM, GMM_TM)
            x_copy(row0, slot).wait()

            nxt = jnp.where(i + 1 < n_tiles, row0 + GMM_TM, row_hi)

            @pl.when(nxt < N_ASSIGN)
            def _():
                x_copy(nxt, 1 - slot).start()

            base = pl.multiple_of(slot * (GMM_TM * ROW_TILE), ROW_TILE)
            x = jnp.concatenate([_tile_column(x_scr, base, GMM_TM, s).astype(BF16) for s in range(ROW_TILE)],
                                axis=1)
            h = _dot(x, wgu_scr[...])
            act = (_silu(h[:, :D_EXPERT]) * h[:, D_EXPERT:]).astype(BF16)
            y = _dot(act, wdn_scr[...])

            wait_writes(slot)
            _rows_to_tiles(y_scr, base, y)
            for pred, cp in _tile_writes(y_scr, y_hbm, slot, row0, valid, ysem.at[slot]):
                @pl.when(pred)
                def _():
                    cp.start()
            st_ref[ST_ROW + slot] = row0
            st_ref[ST_VALID + slot] = valid
            return carry

        lax.fori_loop(0, n_tiles, tile, 0)
        st_ref[ST_TILES] = done + n_tiles

    @pl.when(e == pl.num_programs(0) - 1)
    def _():
        wait_writes(0)
        wait_writes(1)
        zero_scr[...] = jnp.zeros(zero_scr.shape, F32)
        for j in range(GMM_TM // SUBLANES):
            cp = pltpu.make_async_copy(zero_scr, _flat_rows(y_hbm, N_ASSIGN + j * SUBLANES, SUBLANES), zsem)
            cp.start()
            cp.wait()


def _grouped_mlp(group_start, xs, w_gate, w_up, w_down):
    grid_spec = pltpu.PrefetchScalarGridSpec(
        num_scalar_prefetch=1,
        grid=(N_EXPERTS,),
        in_specs=[pl.BlockSpec(memory_space=pl.ANY),
                  pl.BlockSpec((1, D_MODEL, D_EXPERT), lambda e, st: (e, 0, 0)),
                  pl.BlockSpec((1, D_MODEL, D_EXPERT), lambda e, st: (e, 0, 0)),
                  pl.BlockSpec((1, D_EXPERT, D_MODEL), lambda e, st: (e, 0, 0))],
        out_specs=pl.BlockSpec(memory_space=pl.ANY),
        scratch_shapes=[pltpu.VMEM((2 * GMM_TM * ROW_TILE, LANES), F32),
                        pltpu.VMEM((2 * GMM_TM * ROW_TILE, LANES), F32),
                        pltpu.VMEM((D_MODEL, 2 * D_EXPERT), BF16),
                        pltpu.VMEM((D_EXPERT, D_MODEL), BF16),
                        pltpu.VMEM((SUBLANES * ROW_TILE, LANES), F32),
                        pltpu.SMEM((ST_VALID + 2,), I32),
                        pltpu.SemaphoreType.DMA((2,)),
                        pltpu.SemaphoreType.DMA((2,)),
                        pltpu.SemaphoreType.DMA(())],
    )
    return pl.pallas_call(
        _gmm_kernel,
        out_shape=jax.ShapeDtypeStruct((N_ROWS * ROW_TILE, LANES), F32),
        grid_spec=grid_spec,
        compiler_params=_cparams(("arbitrary",)),
        name="grouped_mlp",
    )(group_start, xs, w_gate, w_up, w_down)


COMBINE_TB = 128


def _combine_kernel(dest_ref, dest_next_ref, y_hbm, pre_ref, w_ref, mod_ref, fw_ref, op_ref, os_ref, buf, x2_scr,
                    sem):
    i = pl.program_id(0)
    n = pl.num_programs(0)
    tb = COMBINE_TB
    slot = i % 2
    blk_rows = TOP_K * tb

    def issue_block(d_ref, sl):
        def issue(t, carry):
            for k in range(TOP_K):
                pltpu.make_async_copy(_row_tile(y_hbm, d_ref[k, t]),
                                      _row_tile(buf, sl * blk_rows + k * tb + t),
                                      sem.at[sl]).start(priority=k % 2)
            return carry

        lax.fori_loop(0, tb, issue, 0)

    @pl.when(i == 0)
    def _():
        issue_block(dest_ref, 0)

    @pl.when(i + 1 < n)
    def _():
        issue_block(dest_next_ref, 1 - slot)

    whole = _flat_rows(buf, slot * blk_rows, blk_rows)
    pltpu.make_async_copy(whole, whole, sem.at[slot]).wait()

    base = pl.multiple_of(slot * (blk_rows * ROW_TILE), ROW_TILE)
    w = w_ref[...]
    m = mod_ref[...]
    ssq = jnp.zeros((tb, 1), F32)
    for s in range(ROW_TILE):
        cols = slice(s * LANES, (s + 1) * LANES)
        routed = w[:, 0:1] * _tile_column(buf, base, tb, s)
        for k in range(1, TOP_K):
            routed = routed + w[:, k:k + 1] * _tile_column(buf, base + k * tb * ROW_TILE, tb, s)
        x2 = pre_ref[:, cols] + m[5:6, cols] * routed
        x2_scr[:, cols] = x2
        ssq = ssq + jnp.sum(x2 * x2, axis=-1, keepdims=True)
    scale = lax.rsqrt(ssq * (1.0 / D_MODEL) + EPS)

    @pl.when(i < T_PROMPT // tb)
    def _():
        op_ref[...] = x2_scr[...] * scale * fw_ref[...]

    @pl.when(i >= T_PROMPT // tb)
    def _():
        os_ref[...] = x2_scr[...] * scale * fw_ref[...]


def _combine(dest_t, y_rows, pre, w_tok, mod3, final_norm_w):
    tb = COMBINE_TB
    n_blocks = T_ALL // tb
    npb = T_PROMPT // tb
    row = functools.partial(_mod_row_of_block, blocks_prompt=npb, blocks_per_sample_seq=DEC_SEQ // tb)
    return pl.pallas_call(
        _combine_kernel,
        out_shape=(jax.ShapeDtypeStruct((T_PROMPT, D_MODEL), F32),
                   jax.ShapeDtypeStruct((T_SAMPLE, D_MODEL), F32)),
        grid=(n_blocks,),
        in_specs=[pl.BlockSpec((TOP_K, tb), lambda i: (0, i), memory_space=pltpu.SMEM),
                  pl.BlockSpec((TOP_K, tb), lambda i: (0, jnp.minimum(i + 1, n_blocks - 1)),
                               memory_space=pltpu.SMEM),
                  pl.BlockSpec(memory_space=pl.ANY),
                  pl.BlockSpec((tb, D_MODEL), lambda i: (i, 0)),
                  pl.BlockSpec((tb, TOP_K), lambda i: (i, 0)),
                  pl.BlockSpec((None, N_MOD, D_MODEL), lambda i: (row(i), 0, 0)),
                  pl.BlockSpec((1, D_MODEL), lambda i: (0, 0))],
        out_specs=_two_path_specs((tb, D_MODEL), npb),
        scratch_shapes=[pltpu.VMEM((2 * TOP_K * tb * ROW_TILE, LANES), F32),
                        pltpu.VMEM((tb, D_MODEL), F32),
                        pltpu.SemaphoreType.DMA((2,))],
        compiler_params=_cparams(("arbitrary",)),
        name="combine",
    )(dest_t, dest_t, y_rows, pre, w_tok, mod3, final_norm_w.reshape(1, D_MODEL))


def _group_starts(counts):
    return jnp.concatenate([jnp.zeros((1,), I32), jnp.cumsum(counts).astype(I32)])


def _head_expand_matrix():
    r = jnp.arange(LANES)[:, None]
    cidx = jnp.arange(2 * D_INNER)[None, :]
    direction = cidx // D_INNER
    head = (cidx % D_INNER) // HEAD_DIM
    return (r == direction * N_HEADS + head).astype(BF16)


def kernel(x_prompt, x_sample, state_ssm, c, c_ctx, norm1_w, norm2_w, w_mod, b_mod, w_in, ssm_conv_w, ssm_conv_b, ssm_dt_bias, ssm_a_log, ssm_d, ssm_norm_w, w_out_ssm, conf_conv_w, conf_conv_b, conf_ln_w, conf_ln_b, w_out_conf, w_o, router_w, router_bias, exp_w_gate, exp_w_up, exp_w_down, sh_w_gate, sh_w_up, sh_w_down, final_norm_w):
    x_p = x_prompt.reshape(T_PROMPT, D_MODEL)
    x_s = x_sample.reshape(T_SAMPLE, D_MODEL)

    cc = jnp.zeros((MOD_ROWS, D_MODEL), F32).at[:DEC_BATCH].set(c).at[CTX_ROW].set(c_ctx)
    mod3 = _modulation(cc, w_mod[0], b_mod[0]).reshape(MOD_ROWS, N_MOD, D_MODEL)

    w = w_in[0]
    o_xbc, o_dt, o_cv, o_cg, o_gl = D_INNER, D_INNER + CONV_DIM, D_INNER + CONV_DIM + 2 * N_HEADS, \
        D_INNER + CONV_DIM + 2 * N_HEADS + D_MODEL, D_INNER + CONV_DIM + 2 * N_HEADS + 2 * D_MODEL
    w_main = jnp.concatenate([w[:, o_xbc:o_dt], w[:, :o_xbc], w[:, o_gl:], w[:, o_cv:o_cg], w[:, o_cg:o_gl]],
                             axis=1).astype(BF16)
    w_dt = jnp.pad(w[:, o_dt:o_cv], ((0, 0), (0, LANES - 2 * N_HEADS))).astype(BF16)
    proj, dt_raw = _in_projection(x_p, x_s, mod3, norm1_w[0], w_main, w_dt)

    pad_heads = lambda v: jnp.pad(v.reshape(1, 2 * N_HEADS), ((0, 0), (0, LANES - 2 * N_HEADS)))
    dtb = pad_heads(ssm_dt_bias[0])
    a_neg = pad_heads(-jnp.exp(ssm_a_log[0]))
    dvec = jnp.repeat(ssm_d[0], HEAD_DIM).reshape(1, D_INNER)
    nw = ssm_norm_w[0].reshape(1, D_INNER)
    e2 = _head_expand_matrix()
    xbc_p = _ssm_conv(proj, ssm_conv_w[0], ssm_conv_b[0], seq_len=SEQ, n_seq=BATCH, row_block0=0)
    xbc_s = _ssm_conv(proj, ssm_conv_w[0], ssm_conv_b[0], seq_len=DEC_SEQ, n_seq=DEC_BATCH,
                      row_block0=T_PROMPT // DEC_SEQ)
    y_p, fin = _ssd(xbc_p, dt_raw, proj, None, dtb, a_neg, dvec, nw, e2,
                    seq_len=SEQ, n_seq=BATCH, tok0=0, want_final=True)
    init = state_ssm.reshape(DEC_BATCH, 2, D_INNER, D_STATE)
    (y_s,) = _ssd(xbc_s, dt_raw, proj, init, dtb, a_neg, dvec, nw, e2,
                  seq_len=DEC_SEQ, n_seq=DEC_BATCH, tok0=T_PROMPT, want_final=False)

    v_p = _conformer(proj, conf_conv_w[0], conf_conv_b[0], conf_ln_w[0], conf_ln_b[0],
                     seq_len=SEQ, n_seq=BATCH, row_block0=0, grid_mode=False)
    v_s = _conformer(proj, conf_conv_w[0], conf_conv_b[0], conf_ln_w[0], conf_ln_b[0],
                     seq_len=DEC_SEQ, n_seq=DEC_BATCH, row_block0=T_PROMPT // DEC_SEQ, grid_mode=True)

    sh_gu = jnp.concatenate([sh_w_gate[0], sh_w_up[0]], axis=1).astype(BF16)
    pre, v_all, scores = _merge(y_p, y_s, v_p, v_s, proj, x_p, x_s, mod3, w_out_ssm[0].astype(BF16),
                                w_out_conf[0].astype(BF16), w_o[0].astype(BF16), norm2_w[0], router_w[0],
                                sh_gu, sh_w_down[0].astype(BF16))

    idx_t, w_t, rank_t, cnt = _route(scores, router_bias[0])
    start = _group_starts(cnt[:, 0].astype(I32))
    dest_t = _dest_slots(idx_t, rank_t, start[:N_EXPERTS])

    xs = _dispatch(dest_t, v_all)
    y_rows = _grouped_mlp(start, xs, exp_w_gate[0], exp_w_up[0], exp_w_down[0])
    out_p, out_s = _combine(dest_t, y_rows, pre, w_t.T, mod3, final_norm_w)

    y_prompt = out_p.reshape(BATCH, SEQ, D_MODEL)
    y_sample = out_s.reshape(DEC_BATCH, DEC_SEQ, D_MODEL)
    new_state = fin.reshape(BATCH, 1, 2, N_HEADS, HEAD_DIM, D_STATE)
    return (y_prompt, y_sample, new_state)
```

```python
import functools

import jax
import jax.numpy as jnp
from jax import lax
from jax.experimental import pallas as pl
from jax.experimental.pallas import tpu as pltpu

F32 = jnp.float32
BF16 = jnp.bfloat16
I32 = jnp.int32

D_MODEL = 1024
BATCH = 32
SEQ = 256
DEC_BATCH = 8
DEC_SEQ = 1024
GRID_W = 64
GRID_H = DEC_SEQ // GRID_W
D_INNER = 2048
HEAD_DIM = 64
N_HEADS = 32
D_STATE = 128
N_GROUPS = 8
HEADS_PER_GROUP = N_HEADS // N_GROUPS
GROUP_W = HEADS_PER_GROUP * HEAD_DIM
D_CONV_SSM = 5
CHUNK = 128
CONV_DIM = D_INNER + 2 * N_GROUPS * D_STATE
CONF_K = 31
CONF_PAD = CONF_K // 2
N_EXPERTS = 256
TOP_K = 8
N_EXPERT_GROUPS = 8
EXPERTS_PER_GROUP = N_EXPERTS // N_EXPERT_GROUPS
TOPK_GROUPS = 4
D_EXPERT = 256
D_SHARED = 256
ROUTED_SCALE = 2.5
N_MOD = 6
EPS = 1e-6

T_PROMPT = BATCH * SEQ
T_SAMPLE = DEC_BATCH * DEC_SEQ
T_ALL = T_PROMPT + T_SAMPLE
N_ASSIGN = T_ALL * TOP_K
MOD_ROWS = 16
CTX_ROW = DEC_BATCH

SUBLANES = 8
LANES = 128
VMEM_LIMIT = 56 * 1024 * 1024

PROJ_W = CONV_DIM + D_INNER + 2 * D_MODEL + 2 * D_MODEL
COL_Z = CONV_DIM // D_INNER
COL_GL = COL_Z + 1
COL_CV = (CONV_DIM + 2 * D_INNER) // D_MODEL
COL_CG = COL_CV + 1

ROW_TILE = D_MODEL // LANES
GMM_TM = 256
N_ROWS = N_ASSIGN + GMM_TM
TAIL_SIZES = (128, 64, 32, 16, 8, 4, 2, 1)


def _cparams(sem, vmem=VMEM_LIMIT):
    return pltpu.CompilerParams(dimension_semantics=sem, vmem_limit_bytes=vmem)


def _silu(x):
    return x * jax.nn.sigmoid(x)


def _split2(x):
    hi = x.astype(BF16)
    lo = (x - hi.astype(F32)).astype(BF16)
    return hi, lo


def _split3(x):
    b1 = x.astype(BF16)
    r = x - b1.astype(F32)
    b2 = r.astype(BF16)
    b3 = (r - b2.astype(F32)).astype(BF16)
    return b1, b2, b3


def _dot(a, b):
    return jnp.dot(a, b, preferred_element_type=F32)


def _dot_exact_lhs(a_exact, b):
    b1, b2, b3 = _split3(b)
    return _dot(a_exact, b1) + _dot(a_exact, b2) + _dot(a_exact, b3)


def _dot_f32(a, b):
    a1, a2 = _split2(a)
    b1, b2 = _split2(b)
    return _dot(a1, b1) + _dot(a1, b2) + _dot(a2, b1)


def _mod_row_of_block(i, blocks_prompt, blocks_per_sample_seq):
    return jnp.where(i < blocks_prompt, CTX_ROW, (i - blocks_prompt) // blocks_per_sample_seq)


def _mod_kernel(c_ref, w_ref, b_ref, o_ref):
    c = c_ref[...]
    o_ref[...] = _dot_f32(_silu(c), w_ref[...]) + b_ref[...]


def _modulation(cc, w_mod, b_mod):
    tn = 512
    n = N_MOD * D_MODEL
    return pl.pallas_call(
        _mod_kernel,
        out_shape=jax.ShapeDtypeStruct((MOD_ROWS, n), F32),
        grid=(n // tn,),
        in_specs=[pl.BlockSpec((MOD_ROWS, D_MODEL), lambda j: (0, 0)),
                  pl.BlockSpec((D_MODEL, tn), lambda j: (0, j)),
                  pl.BlockSpec((1, tn), lambda j: (0, j))],
        out_specs=pl.BlockSpec((MOD_ROWS, tn), lambda j: (0, j)),
        compiler_params=_cparams(("arbitrary",)),
        name="modulation",
    )(cc, w_mod, b_mod.reshape(1, n))


INPROJ_TM = 1024
INPROJ_TN = 1024


def _two_path_specs(block, n_prompt_blocks):
    last = n_prompt_blocks - 1
    return (pl.BlockSpec(block, lambda i, *_: (jnp.minimum(i, last), 0)),
            pl.BlockSpec(block, lambda i, *_: (jnp.maximum(i - n_prompt_blocks, 0), 0)))


def _pick_path(n_prompt_blocks, prompt_ref, sample_ref):
    return jnp.where(pl.program_id(0) < n_prompt_blocks, prompt_ref[...], sample_ref[...])


def _inproj_kernel(xp_ref, xs_ref, mod_ref, n1_ref, w_ref, wdt_ref, o_ref, dt_ref, u_scr):
    @pl.when(pl.program_id(1) == 0)
    def _():
        x = _pick_path(T_PROMPT // INPROJ_TM, xp_ref, xs_ref)
        ms = jnp.mean(x * x, axis=-1, keepdims=True)
        y = x * lax.rsqrt(ms + EPS) * n1_ref[...]
        m = mod_ref[...]
        u = (y * (1.0 + m[1:2]) + m[0:1]).astype(BF16)
        u_scr[...] = u
        dt_ref[...] = _dot(u, wdt_ref[...])

    o_ref[...] = _dot(u_scr[...], w_ref[...]).astype(BF16)


def _in_projection(x_p, x_s, mod3, norm1_w, w_main, w_dt):
    tm, tn = INPROJ_TM, INPROJ_TN
    row = functools.partial(_mod_row_of_block, blocks_prompt=T_PROMPT // tm,
                            blocks_per_sample_seq=DEC_SEQ // tm)
    return pl.pallas_call(
        _inproj_kernel,
        out_shape=(jax.ShapeDtypeStruct((T_ALL, PROJ_W), BF16),
                   jax.ShapeDtypeStruct((T_ALL, LANES), F32)),
        grid=(T_ALL // tm, PROJ_W // tn),
        in_specs=[*_two_path_specs((tm, D_MODEL), T_PROMPT // tm),
                  pl.BlockSpec((None, N_MOD, D_MODEL), lambda i, j: (row(i), 0, 0)),
                  pl.BlockSpec((1, D_MODEL), lambda i, j: (0, 0)),
                  pl.BlockSpec((D_MODEL, tn), lambda i, j: (0, j)),
                  pl.BlockSpec((D_MODEL, LANES), lambda i, j: (0, 0))],
        out_specs=(pl.BlockSpec((tm, tn), lambda i, j: (i, j)),
                   pl.BlockSpec((tm, LANES), lambda i, j: (i, 0))),
        scratch_shapes=[pltpu.VMEM((tm, D_MODEL), BF16)],
        compiler_params=_cparams(("arbitrary", "arbitrary")),
        name="in_projection",
    )(x_p, x_s, mod3, norm1_w.reshape(1, D_MODEL), w_main, w_dt)


SSMCONV_TN = 512


def _ssmconv_kernel(x_ref, w_ref, b_ref, o_ref, *, seq_len):
    x = x_ref[...].astype(F32)
    row = lax.broadcasted_iota(I32, x.shape, 0)
    half = D_CONV_SSM // 2
    acc = b_ref[...] + w_ref[half:half + 1, :] * x
    for k in range(D_CONV_SSM):
        s = k - half
        if s == 0:
            continue
        shifted = pltpu.roll(x, shift=(-s) % seq_len, axis=0)
        valid = jnp.logical_and(row + s >= 0, row + s < seq_len)
        acc = acc + w_ref[k:k + 1, :] * jnp.where(valid, shifted, 0.0)
    o_ref[...] = _silu(acc).astype(BF16)


def _ssm_conv(proj, conv_w, conv_b, *, seq_len, n_seq, row_block0):
    tn = SSMCONV_TN
    return pl.pallas_call(
        functools.partial(_ssmconv_kernel, seq_len=seq_len),
        out_shape=jax.ShapeDtypeStruct((n_seq * seq_len, CONV_DIM), BF16),
        grid=(n_seq, CONV_DIM // tn),
        in_specs=[pl.BlockSpec((seq_len, tn), lambda b, j: (row_block0 + b, j)),
                  pl.BlockSpec((D_CONV_SSM, tn), lambda b, j: (0, j)),
                  pl.BlockSpec((1, tn), lambda b, j: (0, j))],
        out_specs=pl.BlockSpec((seq_len, tn), lambda b, j: (b, j)),
        compiler_params=_cparams(("arbitrary", "arbitrary")),
        name="ssm_conv",
    )(proj, conv_w, conv_b.reshape(1, CONV_DIM))


def _tri_masks():
    ii = lax.broadcasted_iota(I32, (CHUNK, CHUNK), 0)
    jj = lax.broadcasted_iota(I32, (CHUNK, CHUNK), 1)
    return ii, jj


def _chunk_decays(dt_ref, dtb_ref, a_ref):
    ii, jj = _tri_masks()
    pre = dt_ref[...] + dtb_ref[...]
    dt = jnp.maximum(pre, 0.0) + jnp.log(1.0 + jnp.exp(-jnp.abs(pre)))
    la = dt * a_ref[...]
    tri_lo = jnp.where(jj <= ii, 1.0, 0.0).astype(BF16)
    tri_up = jnp.where(jj >= ii, 1.0, 0.0).astype(BF16)
    cs_prefix = _dot_exact_lhs(tri_lo, la)
    cs_suffix = _dot_exact_lhs(tri_up, la)
    fwd_lane = lax.broadcasted_iota(I32, (CHUNK, LANES), 1) < N_HEADS
    cs = jnp.where(fwd_lane, cs_prefix, cs_suffix)
    tot = jnp.where(fwd_lane[0:1], cs_prefix[CHUNK - 1:CHUNK, :], cs_suffix[0:1, :])
    return dt, cs, tot


def _transpose_blocks(src, rows, cols):
    out_rows = []
    for cb in range(cols // LANES):
        pieces = [src[rb * LANES:(rb + 1) * LANES, cb * LANES:(cb + 1) * LANES].T
                  for rb in range(rows // LANES)]
        out_rows.append(jnp.concatenate(pieces, axis=1) if len(pieces) > 1 else pieces[0])
    return jnp.concatenate(out_rows, axis=0) if len(out_rows) > 1 else out_rows[0]


def _ssd_kernel(*refs, n_chunks, has_init, want_final):
    it = iter(refs)
    xbc_ref, dt_ref, z_ref = next(it), next(it), next(it)
    init_ref = next(it) if has_init else None
    dtb_ref, a_ref, dvec_ref, nw_ref, e2_ref = next(it), next(it), next(it), next(it), next(it)
    out_ref = next(it)
    fin_ref = next(it) if want_final else None
    sf_scr, sb_scr, df_scr, db_scr = next(it), next(it), next(it), next(it)

    phase = pl.program_id(1)
    c = pl.program_id(2)

    @pl.when(phase == 0)
    def _chunk_states():
        dt, cs, tot = _chunk_decays(dt_ref, dtb_ref, a_ref)
        w_in = dt * jnp.exp(tot - cs)
        pack = 2 * SUBLANES
        dec_hi, dec_lo = _split2(jnp.exp(jnp.broadcast_to(tot, (pack, LANES))))
        expanded = _dot(jnp.concatenate([w_in.astype(BF16), dec_hi, dec_lo], axis=0), e2_ref[...])
        w_exp = expanded[:CHUNK]
        dec_exp = expanded[CHUNK:CHUNK + SUBLANES] + expanded[CHUNK + pack:CHUNK + pack + SUBLANES]
        df_scr[c] = dec_exp[:, :D_INNER]
        db_scr[c] = dec_exp[:, D_INNER:]
        for g in range(N_GROUPS):
            lo = g * GROUP_W
            xg = xbc_ref[:, lo:lo + GROUP_W].astype(F32)
            xd_f = (xg * w_exp[:, lo:lo + GROUP_W]).astype(BF16)
            xd_b = (xg * w_exp[:, D_INNER + lo:D_INNER + lo + GROUP_W]).astype(BF16)
            bg = xbc_ref[:, D_INNER + g * D_STATE:D_INNER + (g + 1) * D_STATE]
            bg_t = bg.astype(F32).T.astype(BF16)
            sf_scr[c, :, lo:lo + GROUP_W] = _dot(bg_t, xd_f)
            sb_scr[c, :, lo:lo + GROUP_W] = _dot(bg_t, xd_b)

    @pl.when(jnp.logical_and(phase == 1, c == 0))
    def _recurrence():
        for g in range(N_GROUPS):
            lo = g * GROUP_W
            if has_init:
                prev_f = _transpose_blocks(init_ref[0, 0, lo:lo + GROUP_W, :], GROUP_W, D_STATE)
                prev_b = _transpose_blocks(init_ref[0, 1, lo:lo + GROUP_W, :], GROUP_W, D_STATE)
            else:
                prev_f = jnp.zeros((D_STATE, GROUP_W), F32)
                prev_b = jnp.zeros((D_STATE, GROUP_W), F32)
            for cc in range(n_chunks):
                s = sf_scr[cc, :, lo:lo + GROUP_W]
                sf_scr[cc, :, lo:lo + GROUP_W] = prev_f
                prev_f = df_scr[cc, 0:1, lo:lo + GROUP_W] * prev_f + s
            for cc in reversed(range(n_chunks)):
                s = sb_scr[cc, :, lo:lo + GROUP_W]
                sb_scr[cc, :, lo:lo + GROUP_W] = prev_b
                prev_b = db_scr[cc, 0:1, lo:lo + GROUP_W] * prev_b + s
            if want_final:
                fin_ref[0, 0, lo:lo + GROUP_W, :] = _transpose_blocks(prev_f, D_STATE, GROUP_W)
                fin_ref[0, 1, lo:lo + GROUP_W, :] = _transpose_blocks(prev_b, D_STATE, GROUP_W)

    @pl.when(phase == 1)
    def _outputs():
        ii, jj = _tri_masks()
        dt, cs, _ = _chunk_decays(dt_ref, dtb_ref, a_ref)
        out_dec = _dot(jnp.exp(cs).astype(BF16), e2_ref[...])
        cs_t = cs.T
        dt_t = dt.T
        causal = ii >= jj
        anti = jj >= ii
        neg = jnp.float32(-1e30)
        left = lax.broadcasted_iota(I32, (CHUNK, LANES), 1) < HEAD_DIM
        for g in range(N_GROUPS):
            lo = g * GROUP_W
            bg = xbc_ref[:, D_INNER + g * D_STATE:D_INNER + (g + 1) * D_STATE]
            cg = xbc_ref[:, D_INNER + N_GROUPS * D_STATE + g * D_STATE:
                         D_INNER + N_GROUPS * D_STATE + (g + 1) * D_STATE]
            cb = lax.dot_general(cg, bg, (((1,), (1,)), ((), ())), preferred_element_type=F32)
            pf = sf_scr[c, :, lo:lo + GROUP_W].astype(BF16)
            pb = sb_scr[c, :, lo:lo + GROUP_W].astype(BF16)
            y_off = (_dot(cg, pf) * out_dec[:, lo:lo + GROUP_W]
                     + _dot(cg, pb) * out_dec[:, D_INNER + lo:D_INNER + lo + GROUP_W])
            pairs = []
            for m in range(HEADS_PER_GROUP // 2):
                x_pair = xbc_ref[:, lo + m * LANES:lo + (m + 1) * LANES]
                ys = []
                for hh in range(2):
                    h = g * HEADS_PER_GROUP + 2 * m + hh
                    hb = N_HEADS + h
                    seg_f = jnp.where(causal, cs[:, h:h + 1] - cs_t[h:h + 1, :], neg)
                    seg_b = jnp.where(anti, cs[:, hb:hb + 1] - cs_t[hb:hb + 1, :], neg)
                    mix = (jnp.exp(seg_f) * dt_t[h:h + 1, :] + jnp.exp(seg_b) * dt_t[hb:hb + 1, :])
                    ys.append(_dot((cb * mix).astype(BF16), x_pair))
                pairs.append(jnp.where(left, ys[0], ys[1]))
            y_diag = jnp.concatenate(pairs, axis=1)
            xg = xbc_ref[:, lo:lo + GROUP_W].astype(F32)
            y = y_diag + y_off + dvec_ref[:, lo:lo + GROUP_W] * xg
            zg = z_ref[:, lo:lo + GROUP_W].astype(F32)
            y = y * _silu(zg)
            ms = jnp.mean(y * y, axis=-1, keepdims=True)
            out_ref[:, lo:lo + GROUP_W] = (y * lax.rsqrt(ms + EPS) * nw_ref[:, lo:lo + GROUP_W]).astype(BF16)


def _ssd(xbc, dt_raw, proj, init, dtb, a_neg, dvec, norm_w, e2, *, seq_len, n_seq, tok0, want_final):
    nc = seq_len // CHUNK
    blk0 = tok0 // CHUNK
    has_init = init is not None
    chunk_map = lambda b, p, c: (b * nc + c, 0)
    in_specs = [pl.BlockSpec((CHUNK, CONV_DIM), chunk_map),
                pl.BlockSpec((CHUNK, LANES), lambda b, p, c: (blk0 + b * nc + c, 0)),
                pl.BlockSpec((CHUNK, D_INNER), lambda b, p, c: (blk0 + b * nc + c * p, COL_Z))]
    args = [xbc, dt_raw, proj]
    if has_init:
        in_specs.append(pl.BlockSpec((1, 2, D_INNER, D_STATE), lambda b, p, c: (b, 0, 0, 0)))
        args.append(init)
    const = lambda b, p, c: (0, 0)
    in_specs += [pl.BlockSpec((1, LANES), const), pl.BlockSpec((1, LANES), const),
                 pl.BlockSpec((1, D_INNER), const), pl.BlockSpec((1, D_INNER), const),
                 pl.BlockSpec((LANES, 2 * D_INNER), const)]
    args += [dtb, a_neg, dvec, norm_w, e2]
    out_shape = [jax.ShapeDtypeStruct((n_seq * seq_len, D_INNER), BF16)]
    out_specs = [pl.BlockSpec((CHUNK, D_INNER), lambda b, p, c: (b * nc + c * p, 0))]
    if want_final:
        out_shape.append(jax.ShapeDtypeStruct((n_seq, 2, D_INNER, D_STATE), F32))
        out_specs.append(pl.BlockSpec((1, 2, D_INNER, D_STATE), lambda b, p, c: (b, 0, 0, 0)))
    res = pl.pallas_call(
        functools.partial(_ssd_kernel, n_chunks=nc, has_init=has_init, want_final=want_final),
        out_shape=tuple(out_shape),
        grid=(n_seq, 2, nc),
        in_specs=in_specs,
        out_specs=tuple(out_specs),
        scratch_shapes=[pltpu.VMEM((nc, D_STATE, D_INNER), F32),
                        pltpu.VMEM((nc, D_STATE, D_INNER), F32),
                        pltpu.VMEM((nc, SUBLANES, D_INNER), F32),
                        pltpu.VMEM((nc, SUBLANES, D_INNER), F32)],
        compiler_params=_cparams(("arbitrary", "arbitrary", "arbitrary")),
        name="ssd_final" if want_final else "ssd_init",
    )(*args)
    return res


def _layernorm_silu(v, lw_ref, lb_ref):
    mu = jnp.mean(v, axis=-1, keepdims=True)
    d = v - mu
    var = jnp.mean(d * d, axis=-1, keepdims=True)
    return _silu(d * lax.rsqrt(var + EPS) * lw_ref[...] + lb_ref[...])


def _conf_seq_kernel(cv_ref, cg_ref, w_ref, b_ref, lw_ref, lb_ref, o_ref, pad_scr, acc_scr):
    seq_len = cv_ref.shape[0]
    base = 2 * SUBLANES
    v = cv_ref[...].astype(F32) * jax.nn.sigmoid(cg_ref[...].astype(F32))
    pad_scr[0:base, :] = jnp.zeros((base, D_MODEL), F32)
    pad_scr[base + seq_len:base + seq_len + base, :] = jnp.zeros((base, D_MODEL), F32)
    pad_scr[base:base + seq_len, :] = v
    rt, ct = 128, 256
    for r0 in range(0, seq_len, rt):
        for c0 in range(0, D_MODEL, ct):
            acc = jnp.broadcast_to(b_ref[:, c0:c0 + ct], (rt, ct))
            for k in range(CONF_K):
                off = base + r0 + k - CONF_PAD
                acc = acc + w_ref[k:k + 1, c0:c0 + ct] * pad_scr[off:off + rt, c0:c0 + ct]
            acc_scr[r0:r0 + rt, c0:c0 + ct] = acc
    o_ref[...] = _layernorm_silu(acc_scr[...], lw_ref, lb_ref).astype(BF16)


def _conf_grid_kernel(cv_ref, cg_ref, w_ref, b_ref, lw_ref, lb_ref, o_ref, v_scr, pad_scr, acc_scr):
    half = D_MODEL // 2
    base = 2 * SUBLANES
    stride = GRID_W + 2 * base
    v_scr[...] = cv_ref[...].astype(F32) * jax.nn.sigmoid(cg_ref[...].astype(F32))
    pad_scr[...] = jnp.zeros(pad_scr.shape, F32)
    for r in range(GRID_H):
        pad_scr[r * stride + base:r * stride + base + GRID_W, :] = v_scr[r * GRID_W:(r + 1) * GRID_W, 0:half]
    ct = 256
    for r in range(GRID_H):
        for c0 in range(0, half, ct):
            acc = jnp.broadcast_to(b_ref[:, c0:c0 + ct], (GRID_W, ct))
            for k in range(CONF_K):
                off = r * stride + base + k - CONF_PAD
                acc = acc + w_ref[k:k + 1, c0:c0 + ct] * pad_scr[off:off + GRID_W, c0:c0 + ct]
            acc_scr[r * GRID_W:(r + 1) * GRID_W, c0:c0 + ct] = acc
        for c0 in range(half, D_MODEL, ct):
            acc = jnp.broadcast_to(b_ref[:, c0:c0 + ct], (GRID_W, ct))
            for r2 in range(GRID_H):
                k = r2 - r + CONF_PAD
                acc = acc + w_ref[k:k + 1, c0:c0 + ct] * v_scr[r2 * GRID_W:(r2 + 1) * GRID_W, c0:c0 + ct]
            acc_scr[r * GRID_W:(r + 1) * GRID_W, c0:c0 + ct] = acc
    o_ref[...] = _layernorm_silu(acc_scr[...], lw_ref, lb_ref).astype(BF16)


def _conformer(proj, conv_w, conv_b, ln_w, ln_b, *, seq_len, n_seq, row_block0, grid_mode):
    base = 2 * SUBLANES
    if grid_mode:
        body = _conf_grid_kernel
        scratch = [pltpu.VMEM((seq_len, D_MODEL), F32),
                   pltpu.VMEM((GRID_H * (GRID_W + 2 * base), D_MODEL // 2), F32),
                   pltpu.VMEM((seq_len, D_MODEL), F32)]
    else:
        body = _conf_seq_kernel
        scratch = [pltpu.VMEM((seq_len + 2 * base, D_MODEL), F32),
                   pltpu.VMEM((seq_len, D_MODEL), F32)]
    const = lambda b: (0, 0)
    return pl.pallas_call(
        body,
        out_shape=jax.ShapeDtypeStruct((n_seq * seq_len, D_MODEL), BF16),
        grid=(n_seq,),
        in_specs=[pl.BlockSpec((seq_len, D_MODEL), lambda b: (row_block0 + b, COL_CV)),
                  pl.BlockSpec((seq_len, D_MODEL), lambda b: (row_block0 + b, COL_CG)),
                  pl.BlockSpec((CONF_K, D_MODEL), const),
                  pl.BlockSpec((1, D_MODEL), const),
                  pl.BlockSpec((1, D_MODEL), const),
                  pl.BlockSpec((1, D_MODEL), const)],
        out_specs=pl.BlockSpec((seq_len, D_MODEL), lambda b: (b, 0)),
        scratch_shapes=scratch,
        compiler_params=_cparams(("arbitrary",)),
        name="conformer_grid" if grid_mode else "conformer_seq",
    )(proj, proj, conv_w, conv_b.reshape(1, D_MODEL), ln_w.reshape(1, D_MODEL), ln_b.reshape(1, D_MODEL))


MERGE_TM = 256


def _merge_kernel(yap_ref, yas_ref, ybp_ref, ybs_ref, gl_ref, xp_ref, xs_ref, mod_ref, wa_ref, wb_ref, wo_ref,
                  n2_ref, rw_ref, sgu_ref, sd_ref, pre_ref, v_ref, s_ref):
    npb = T_PROMPT // MERGE_TM
    y_a = _dot(_pick_path(npb, yap_ref, yas_ref), wa_ref[...])
    y_b = _dot(_pick_path(npb, ybp_ref, ybs_ref), wb_ref[...])
    gates = jax.nn.sigmoid(gl_ref[...].astype(F32))
    mix = gates[:, :D_MODEL] * y_a + gates[:, D_MODEL:] * y_b
    out = _dot(mix.astype(BF16), wo_ref[...])
    m = mod_ref[...]
    x1 = _pick_path(npb, xp_ref, xs_ref) + m[2:3] * out
    ms = jnp.mean(x1 * x1, axis=-1, keepdims=True)
    v = x1 * lax.rsqrt(ms + EPS) * n2_ref[...] * (1.0 + m[4:5]) + m[3:4]
    v_ref[...] = v
    s_ref[...] = jax.nn.sigmoid(_dot_f32(v, rw_ref[...]))
    hgu = _dot(v.astype(BF16), sgu_ref[...])
    act = _silu(hgu[:, :D_SHARED]) * hgu[:, D_SHARED:]
    shared = _dot(act.astype(BF16), sd_ref[...])
    pre_ref[...] = x1 + m[5:6] * shared


def _merge(y_ssm_p, y_ssm_s, v_conf_p, v_conf_s, proj, x_p, x_s, mod3, w_out_ssm, w_out_conf, w_o, norm2_w,
           router_w, sh_gu, sh_d):
    tm = MERGE_TM
    npb = T_PROMPT // tm
    row = functools.partial(_mod_row_of_block, blocks_prompt=npb, blocks_per_sample_seq=DEC_SEQ // tm)
    const = lambda i: (0, 0)
    return pl.pallas_call(
        _merge_kernel,
        out_shape=(jax.ShapeDtypeStruct((T_ALL, D_MODEL), F32),
                   jax.ShapeDtypeStruct((T_ALL, D_MODEL), F32),
                   jax.ShapeDtypeStruct((T_ALL, N_EXPERTS), F32)),
        grid=(T_ALL // tm,),
        in_specs=[*_two_path_specs((tm, D_INNER), npb),
                  *_two_path_specs((tm, D_MODEL), npb),
                  pl.BlockSpec((tm, 2 * D_MODEL), lambda i: (i, COL_GL)),
                  *_two_path_specs((tm, D_MODEL), npb),
                  pl.BlockSpec((None, N_MOD, D_MODEL), lambda i: (row(i), 0, 0)),
                  pl.BlockSpec((D_INNER, D_MODEL), const),
                  pl.BlockSpec((D_MODEL, D_MODEL), const),
                  pl.BlockSpec((D_MODEL, D_MODEL), const),
                  pl.BlockSpec((1, D_MODEL), const),
                  pl.BlockSpec((D_MODEL, N_EXPERTS), const),
                  pl.BlockSpec((D_MODEL, 2 * D_SHARED), const),
                  pl.BlockSpec((D_SHARED, D_MODEL), const)],
        out_specs=(pl.BlockSpec((tm, D_MODEL), lambda i: (i, 0)),
                   pl.BlockSpec((tm, D_MODEL), lambda i: (i, 0)),
                   pl.BlockSpec((tm, N_EXPERTS), lambda i: (i, 0))),
        compiler_params=_cparams(("arbitrary",)),
        name="merge",
    )(y_ssm_p, y_ssm_s, v_conf_p, v_conf_s, proj, x_p, x_s, mod3, w_out_ssm, w_out_conf, w_o,
      norm2_w.reshape(1, D_MODEL), router_w, sh_gu, sh_d)


ROUTE_TB = 256


def _first_index_of_max(vals, iota, sentinel):
    m = jnp.max(vals, axis=0, keepdims=True)
    idx = jnp.min(jnp.where(vals == m, iota, jnp.float32(sentinel)), axis=0, keepdims=True)
    return m, idx


def _route_kernel(s_ref, bias_ref, idx_ref, w_ref, rank_ref, cnt_ref, run_scr):
    i = pl.program_id(0)
    tb = ROUTE_TB
    neg = jnp.float32(-jnp.inf)

    @pl.when(i == 0)
    def _():
        run_scr[...] = jnp.zeros(run_scr.shape, F32)

    s_t = s_ref[...].T
    sb_t = s_t + bias_ref[...]
    eiota = lax.broadcasted_iota(I32, (N_EXPERTS, tb), 0).astype(F32)

    liota = lax.broadcasted_iota(I32, (EXPERTS_PER_GROUP, tb), 0).astype(F32)
    gscores = []
    for g in range(N_EXPERT_GROUPS):
        blk = sb_t[g * EXPERTS_PER_GROUP:(g + 1) * EXPERTS_PER_GROUP, :]
        m1, i1 = _first_index_of_max(blk, liota, EXPERTS_PER_GROUP)
        m2 = jnp.max(jnp.where(liota == i1, neg, blk), axis=0, keepdims=True)
        gscores.append(m1 + m2)
    gs = jnp.concatenate(gscores, axis=0)
    giota = lax.broadcasted_iota(I32, (N_EXPERT_GROUPS, tb), 0).astype(F32)
    gsel = jnp.zeros((N_EXPERT_GROUPS, tb), F32)
    for _ in range(TOPK_GROUPS):
        _, gi = _first_index_of_max(gs, giota, N_EXPERT_GROUPS)
        hit = giota == gi
        gsel = jnp.where(hit, 1.0, gsel)
        gs = jnp.where(hit, neg, gs)
    emask = jnp.concatenate(
        [jnp.broadcast_to(gsel[g:g + 1, :], (EXPERTS_PER_GROUP, tb)) for g in range(N_EXPERT_GROUPS)], axis=0)
    masked = jnp.where(emask > 0.5, sb_t, neg)

    onehots, idxs, wts = [], [], []
    for _ in range(TOP_K):
        _, ei = _first_index_of_max(masked, eiota, N_EXPERTS)
        hit = eiota == ei
        onehots.append(hit)
        idxs.append(ei)
        wts.append(jnp.sum(jnp.where(hit, s_t, 0.0), axis=0, keepdims=True))
        masked = jnp.where(hit, neg, masked)
    w = jnp.concatenate(wts, axis=0)
    w_ref[...] = w / jnp.sum(w, axis=0, keepdims=True) * ROUTED_SCALE
    idx_ref[...] = jnp.concatenate(idxs, axis=0).astype(I32)

    assign = jnp.zeros((N_EXPERTS, tb), F32)
    for hit in onehots:
        assign = jnp.where(hit, 1.0, assign)
    assign_b = assign.astype(BF16)
    ti = lax.broadcasted_iota(I32, (tb, tb), 0)
    tj = lax.broadcasted_iota(I32, (tb, tb), 1)
    before = jnp.where(ti < tj, 1.0, 0.0).astype(BF16)
    within = _dot(assign_b, before)
    run = run_scr[...]
    total = within + jnp.concatenate([run] * (tb // LANES), axis=1)
    rank_ref[...] = jnp.concatenate(
        [jnp.sum(jnp.where(hit, total, 0.0), axis=0, keepdims=True) for hit in onehots], axis=0).astype(I32)
    new_run = run + _dot(assign_b, jnp.ones((tb, LANES), BF16))
    run_scr[...] = new_run
    cnt_ref[...] = new_run


def _route(scores, router_bias):
    tb = ROUTE_TB
    return pl.pallas_call(
        _route_kernel,
        out_shape=(jax.ShapeDtypeStruct((TOP_K, T_ALL), I32),
                   jax.ShapeDtypeStruct((TOP_K, T_ALL), F32),
                   jax.ShapeDtypeStruct((TOP_K, T_ALL), I32),
                   jax.ShapeDtypeStruct((N_EXPERTS, LANES), F32)),
        grid=(T_ALL // tb,),
        in_specs=[pl.BlockSpec((tb, N_EXPERTS), lambda i: (i, 0)),
                  pl.BlockSpec((N_EXPERTS, 1), lambda i: (0, 0))],
        out_specs=(pl.BlockSpec((TOP_K, tb), lambda i: (0, i)),
                   pl.BlockSpec((TOP_K, tb), lambda i: (0, i)),
                   pl.BlockSpec((TOP_K, tb), lambda i: (0, i)),
                   pl.BlockSpec((N_EXPERTS, LANES), lambda i: (0, 0))),
        scratch_shapes=[pltpu.VMEM((N_EXPERTS, LANES), F32)],
        compiler_params=_cparams(("arbitrary",)),
        name="route",
    )(scores, router_bias.reshape(N_EXPERTS, 1))


def _dest_kernel(idx_ref, rank_ref, start_ref, dest_ref):
    tb = idx_ref.shape[1]
    eiota = lax.broadcasted_iota(I32, (N_EXPERTS, tb), 0)
    start = jnp.broadcast_to(start_ref[...], (N_EXPERTS, tb))
    idx = idx_ref[...]
    rows = [jnp.sum(jnp.where(eiota == idx[k:k + 1, :], start, 0.0), axis=0, keepdims=True)
            for k in range(TOP_K)]
    dest_ref[...] = jnp.concatenate(rows, axis=0).astype(I32) + rank_ref[...]


def _dest_slots(idx_t, rank_t, group_start):
    tb = 512
    return pl.pallas_call(
        _dest_kernel,
        out_shape=jax.ShapeDtypeStruct((TOP_K, T_ALL), I32),
        grid=(T_ALL // tb,),
        in_specs=[pl.BlockSpec((TOP_K, tb), lambda i: (0, i)),
                  pl.BlockSpec((TOP_K, tb), lambda i: (0, i)),
                  pl.BlockSpec((N_EXPERTS, 1), lambda i: (0, 0))],
        out_specs=pl.BlockSpec((TOP_K, tb), lambda i: (0, i)),
        compiler_params=_cparams(("arbitrary",)),
        name="dest_slots",
    )(idx_t, rank_t, group_start.astype(F32).reshape(N_EXPERTS, 1))


DISPATCH_TB = 256


def _rows_to_tiles(dst_scr, base, rows):
    n = rows.shape[0]
    for s in range(ROW_TILE):
        dst_scr[pl.ds(base + s, n, stride=ROW_TILE), :] = rows[:, s * LANES:(s + 1) * LANES]


def _tile_column(src_scr, base, n, s):
    return src_scr[pl.ds(base + s, n, stride=ROW_TILE), :]


def _row_tile(ref, row):
    return ref.at[pl.ds(pl.multiple_of(row * ROW_TILE, ROW_TILE), ROW_TILE), :]


def _dispatch_kernel(dest_ref, v_ref, xs_hbm, tile_scr, zero_scr, sem, zsem):
    i = pl.program_id(0)
    n = pl.num_programs(0)
    tb = DISPATCH_TB
    slot = i % 2
    base = pl.multiple_of(slot * (tb * ROW_TILE), ROW_TILE)
    _rows_to_tiles(tile_scr, base, v_ref[...])

    def issue(t, carry):
        src = _row_tile(tile_scr, slot * tb + t)
        for k in range(TOP_K):
            pltpu.make_async_copy(src, _row_tile(xs_hbm, dest_ref[k, t]), sem.at[slot]).start(priority=k % 2)
        return carry

    lax.fori_loop(0, tb, issue, 0)

    def wait_block(sl):
        blk = tile_scr.at[pl.ds(pl.multiple_of(sl * (tb * ROW_TILE), ROW_TILE), tb * ROW_TILE), :]
        for _ in range(TOP_K):
            pltpu.make_async_copy(blk, blk, sem.at[sl]).wait()

    @pl.when(i == 0)
    def _():
        zero_scr[...] = jnp.zeros(zero_scr.shape, F32)
        for j in range(GMM_TM // SUBLANES):
            cp = pltpu.make_async_copy(
                zero_scr, xs_hbm.at[pl.ds((N_ASSIGN + j * SUBLANES) * ROW_TILE, SUBLANES * ROW_TILE), :], zsem)
            cp.start()
            cp.wait()

    @pl.when(i > 0)
    def _():
        wait_block(1 - slot)

    @pl.when(i == n - 1)
    def _():
        wait_block(slot)


def _dispatch(dest_t, v_all):
    tb = DISPATCH_TB
    return pl.pallas_call(
        _dispatch_kernel,
        out_shape=jax.ShapeDtypeStruct((N_ROWS * ROW_TILE, LANES), F32),
        grid=(T_ALL // tb,),
        in_specs=[pl.BlockSpec((TOP_K, tb), lambda i: (0, i), memory_space=pltpu.SMEM),
                  pl.BlockSpec((tb, D_MODEL), lambda i: (i, 0))],
        out_specs=pl.BlockSpec(memory_space=pl.ANY),
        scratch_shapes=[pltpu.VMEM((2 * tb * ROW_TILE, LANES), F32),
                        pltpu.VMEM((SUBLANES * ROW_TILE, LANES), F32),
                        pltpu.SemaphoreType.DMA((2,)),
                        pltpu.SemaphoreType.DMA(())],
        compiler_params=_cparams(("arbitrary",)),
        name="dispatch",
    )(dest_t, v_all)


def _flat_rows(ref, row, n_rows):
    return ref.at[pl.ds(pl.multiple_of(row * ROW_TILE, ROW_TILE), n_rows * ROW_TILE), :]


def _tile_writes(y_scr, y_hbm, slot, row0, valid, sem):
    base = slot * GMM_TM
    parts = [(valid == GMM_TM, pltpu.make_async_copy(_flat_rows(y_scr, base, GMM_TM),
                                                     _flat_rows(y_hbm, row0, GMM_TM), sem))]
    off = jnp.int32(0)
    for p in TAIL_SIZES:
        bit = (valid & p) != 0
        take = jnp.logical_and(valid < GMM_TM, bit)
        parts.append((take, pltpu.make_async_copy(_flat_rows(y_scr, base + off, p),
                                                  _flat_rows(y_hbm, row0 + off, p), sem)))
        off = off + jnp.where(bit, p, 0)
    return parts


ST_TILES, ST_ROW, ST_VALID, ST_FETCH_E, ST_FETCH_I, ST_FETCH_G, ST_SIZE = 0, 1, 3, 5, 6, 7, 8
X_SLOTS = 4
X_AHEAD = X_SLOTS - 1


def _gmm_kernel(start_ref, xs_hbm, wg_ref, wu_ref, wd_ref, y_hbm,
                x_scr, y_scr, wgu_scr, wdn_scr, zero_scr, st_ref, xsem, ysem, zsem):
    e = pl.program_id(0)
    row_lo = start_ref[e]
    row_hi = start_ref[e + 1]
    n_rows = row_hi - row_lo
    n_tiles = lax.shift_right_logical(n_rows + (GMM_TM - 1), GMM_TM.bit_length() - 1)

    def x_copy(row, sl):
        return pltpu.make_async_copy(_flat_rows(xs_hbm, row, GMM_TM),
                                     _flat_rows(x_scr, sl * GMM_TM, GMM_TM), xsem.at[sl])

    def tiles_of(ex):
        rows = start_ref[ex + 1] - start_ref[ex]
        return lax.shift_right_logical(rows + (GMM_TM - 1), GMM_TM.bit_length() - 1)

    def skip_empty(ex):
        def empty(q):
            qc = jnp.minimum(q, N_EXPERTS - 1)
            return jnp.logical_and(q < N_EXPERTS, start_ref[qc + 1] == start_ref[qc])
        return lax.while_loop(empty, lambda q: q + 1, ex)

    def fetch_next_tile():
        pe = st_ref[ST_FETCH_E]

        @pl.when(pe < N_EXPERTS)
        def _():
            pi = st_ref[ST_FETCH_I]
            pg = st_ref[ST_FETCH_G]
            pec = jnp.minimum(pe, N_EXPERTS - 1)
            x_copy(start_ref[pec] + pi * GMM_TM, pg % X_SLOTS).start()
            last = pi + 1 >= tiles_of(pec)
            st_ref[ST_FETCH_E] = jnp.where(last, skip_empty(pe + 1), pe)
            st_ref[ST_FETCH_I] = jnp.where(last, 0, pi + 1)
            st_ref[ST_FETCH_G] = pg + 1

    def wait_writes(sl):
        @pl.when(st_ref[ST_VALID + sl] > 0)
        def _():
            for pred, cp in _tile_writes(y_scr, y_hbm, sl, st_ref[ST_ROW + sl], st_ref[ST_VALID + sl],
                                         ysem.at[sl]):
                @pl.when(pred)
                def _():
                    cp.wait()
            st_ref[ST_VALID + sl] = 0

    @pl.when(e == 0)
    def _():
        for j in range(ST_SIZE):
            st_ref[j] = 0
        st_ref[ST_FETCH_E] = skip_empty(jnp.int32(0))
        for _ in range(X_AHEAD):
            fetch_next_tile()

    done = st_ref[ST_TILES]

    @pl.when(n_tiles > 0)
    def _():
        wgu_scr[:, :D_EXPERT] = wg_ref[0].astype(BF16)
        wgu_scr[:, D_EXPERT:] = wu_ref[0].astype(BF16)
        wdn_scr[...] = wd_ref[0].astype(BF16)

        def tile(i, carry):
            g = done + i
            slot = g % 2
            xslot = g % X_SLOTS
            row0 = row_lo + i * GMM_TM
            valid = jnp.minimum(n_rows - i * GMM_TM, GMM_TM)
            x_copy(row0, xslot).wait()
            fetch_next_tile()

            base = pl.multiple_of(slot * (GMM_TM * ROW_TILE), ROW_TILE)
            xbase = pl.multiple_of(xslot * (GMM_TM * ROW_TILE), ROW_TILE)
            x = jnp.concatenate([_tile_column(x_scr, xbase, GMM_TM, s).astype(BF16) for s in range(ROW_TILE)],
                                axis=1)
            h = _dot(x, wgu_scr[...])
            act = (_silu(h[:, :D_EXPERT]) * h[:, D_EXPERT:]).astype(BF16)
            y = _dot(act, wdn_scr[...])

            wait_writes(slot)
            _rows_to_tiles(y_scr, base, y)
            for pred, cp in _tile_writes(y_scr, y_hbm, slot, row0, valid, ysem.at[slot]):
                @pl.when(pred)
                def _():
                    cp.start()
            st_ref[ST_ROW + slot] = row0
            st_ref[ST_VALID + slot] = valid
            return carry

        lax.fori_loop(0, n_tiles, tile, 0)
        st_ref[ST_TILES] = done + n_tiles

    @pl.when(e == pl.num_programs(0) - 1)
    def _():
        wait_writes(0)
        wait_writes(1)
        zero_scr[...] = jnp.zeros(zero_scr.shape, F32)
        for j in range(GMM_TM // SUBLANES):
            cp = pltpu.make_async_copy(zero_scr, _flat_rows(y_hbm, N_ASSIGN + j * SUBLANES, SUBLANES), zsem)
            cp.start()
            cp.wait()


def _grouped_mlp(group_start, xs, w_gate, w_up, w_down):
    grid_spec = pltpu.PrefetchScalarGridSpec(
        num_scalar_prefetch=1,
        grid=(N_EXPERTS,),
        in_specs=[pl.BlockSpec(memory_space=pl.ANY),
                  pl.BlockSpec((1, D_MODEL, D_EXPERT), lambda e, st: (e, 0, 0)),
                  pl.BlockSpec((1, D_MODEL, D_EXPERT), lambda e, st: (e, 0, 0)),
                  pl.BlockSpec((1, D_EXPERT, D_MODEL), lambda e, st: (e, 0, 0))],
        out_specs=pl.BlockSpec(memory_space=pl.ANY),
        scratch_shapes=[pltpu.VMEM((X_SLOTS * GMM_TM * ROW_TILE, LANES), F32),
                        pltpu.VMEM((2 * GMM_TM * ROW_TILE, LANES), F32),
                        pltpu.VMEM((D_MODEL, 2 * D_EXPERT), BF16),
                        pltpu.VMEM((D_EXPERT, D_MODEL), BF16),
                        pltpu.VMEM((SUBLANES * ROW_TILE, LANES), F32),
                        pltpu.SMEM((ST_SIZE,), I32),
                        pltpu.SemaphoreType.DMA((X_SLOTS,)),
                        pltpu.SemaphoreType.DMA((2,)),
                        pltpu.SemaphoreType.DMA(())],
    )
    return pl.pallas_call(
        _gmm_kernel,
        out_shape=jax.ShapeDtypeStruct((N_ROWS * ROW_TILE, LANES), F32),
        grid_spec=grid_spec,
        compiler_params=_cparams(("arbitrary",)),
        name="grouped_mlp",
    )(group_start, xs, w_gate, w_up, w_down)


COMBINE_TB = 128


def _combine_kernel(dest_ref, dest_next_ref, y_hbm, pre_ref, w_ref, mod_ref, fw_ref, op_ref, os_ref, buf, x2_scr,
                    sem):
    i = pl.program_id(0)
    n = pl.num_programs(0)
    tb = COMBINE_TB
    slot = i % 2
    blk_rows = TOP_K * tb

    def issue_block(d_ref, sl):
        def issue(t, carry):
            for k in range(TOP_K):
                pltpu.make_async_copy(_row_tile(y_hbm, d_ref[k, t]),
                                      _row_tile(buf, sl * blk_rows + k * tb + t),
                                      sem.at[sl]).start(priority=k % 2)
            return carry

        lax.fori_loop(0, tb, issue, 0)

    @pl.when(i == 0)
    def _():
        issue_block(dest_ref, 0)

    @pl.when(i + 1 < n)
    def _():
        issue_block(dest_next_ref, 1 - slot)

    whole = _flat_rows(buf, slot * blk_rows, blk_rows)
    pltpu.make_async_copy(whole, whole, sem.at[slot]).wait()

    base = pl.multiple_of(slot * (blk_rows * ROW_TILE), ROW_TILE)
    w = w_ref[...]
    m = mod_ref[...]
    ssq = jnp.zeros((tb, 1), F32)
    for s in range(ROW_TILE):
        cols = slice(s * LANES, (s + 1) * LANES)
        routed = w[:, 0:1] * _tile_column(buf, base, tb, s)
        for k in range(1, TOP_K):
            routed = routed + w[:, k:k + 1] * _tile_column(buf, base + k * tb * ROW_TILE, tb, s)
        x2 = pre_ref[:, cols] + m[5:6, cols] * routed
        x2_scr[:, cols] = x2
        ssq = ssq + jnp.sum(x2 * x2, axis=-1, keepdims=True)
    scale = lax.rsqrt(ssq * (1.0 / D_MODEL) + EPS)

    @pl.when(i < T_PROMPT // tb)
    def _():
        op_ref[...] = x2_scr[...] * scale * fw_ref[...]

    @pl.when(i >= T_PROMPT // tb)
    def _():
        os_ref[...] = x2_scr[...] * scale * fw_ref[...]


def _combine(dest_t, y_rows, pre, w_tok, mod3, final_norm_w):
    tb = COMBINE_TB
    n_blocks = T_ALL // tb
    npb = T_PROMPT // tb
    row = functools.partial(_mod_row_of_block, blocks_prompt=npb, blocks_per_sample_seq=DEC_SEQ // tb)
    return pl.pallas_call(
        _combine_kernel,
        out_shape=(jax.ShapeDtypeStruct((T_PROMPT, D_MODEL), F32),
                   jax.ShapeDtypeStruct((T_SAMPLE, D_MODEL), F32)),
        grid=(n_blocks,),
        in_specs=[pl.BlockSpec((TOP_K, tb), lambda i: (0, i), memory_space=pltpu.SMEM),
                  pl.BlockSpec((TOP_K, tb), lambda i: (0, jnp.minimum(i + 1, n_blocks - 1)),
                               memory_space=pltpu.SMEM),
                  pl.BlockSpec(memory_space=pl.ANY),
                  pl.BlockSpec((tb, D_MODEL), lambda i: (i, 0)),
                  pl.BlockSpec((tb, TOP_K), lambda i: (i, 0)),
                  pl.BlockSpec((None, N_MOD, D_MODEL), lambda i: (row(i), 0, 0)),
                  pl.BlockSpec((1, D_MODEL), lambda i: (0, 0))],
        out_specs=_two_path_specs((tb, D_MODEL), npb),
        scratch_shapes=[pltpu.VMEM((2 * TOP_K * tb * ROW_TILE, LANES), F32),
                        pltpu.VMEM((tb, D_MODEL), F32),
                        pltpu.SemaphoreType.DMA((2,))],
        compiler_params=_cparams(("arbitrary",)),
        name="combine",
    )(dest_t, dest_t, y_rows, pre, w_tok, mod3, final_norm_w.reshape(1, D_MODEL))


def _group_starts(counts):
    return jnp.concatenate([jnp.zeros((1,), I32), jnp.cumsum(counts).astype(I32)])


def _head_expand_matrix():
    r = jnp.arange(LANES)[:, None]
    cidx = jnp.arange(2 * D_INNER)[None, :]
    direction = cidx // D_INNER
    head = (cidx % D_INNER) // HEAD_DIM
    return (r == direction * N_HEADS + head).astype(BF16)


def kernel(x_prompt, x_sample, state_ssm, c, c_ctx, norm1_w, norm2_w, w_mod, b_mod, w_in, ssm_conv_w, ssm_conv_b, ssm_dt_bias, ssm_a_log, ssm_d, ssm_norm_w, w_out_ssm, conf_conv_w, conf_conv_b, conf_ln_w, conf_ln_b, w_out_conf, w_o, router_w, router_bias, exp_w_gate, exp_w_up, exp_w_down, sh_w_gate, sh_w_up, sh_w_down, final_norm_w):
    x_p = x_prompt.reshape(T_PROMPT, D_MODEL)
    x_s = x_sample.reshape(T_SAMPLE, D_MODEL)

    cc = jnp.zeros((MOD_ROWS, D_MODEL), F32).at[:DEC_BATCH].set(c).at[CTX_ROW].set(c_ctx)
    mod3 = _modulation(cc, w_mod[0], b_mod[0]).reshape(MOD_ROWS, N_MOD, D_MODEL)

    w = w_in[0]
    o_xbc, o_dt, o_cv, o_cg, o_gl = D_INNER, D_INNER + CONV_DIM, D_INNER + CONV_DIM + 2 * N_HEADS, \
        D_INNER + CONV_DIM + 2 * N_HEADS + D_MODEL, D_INNER + CONV_DIM + 2 * N_HEADS + 2 * D_MODEL
    w_main = jnp.concatenate([w[:, o_xbc:o_dt], w[:, :o_xbc], w[:, o_gl:], w[:, o_cv:o_cg], w[:, o_cg:o_gl]],
                             axis=1).astype(BF16)
    w_dt = jnp.pad(w[:, o_dt:o_cv], ((0, 0), (0, LANES - 2 * N_HEADS))).astype(BF16)
    proj, dt_raw = _in_projection(x_p, x_s, mod3, norm1_w[0], w_main, w_dt)

    pad_heads = lambda v: jnp.pad(v.reshape(1, 2 * N_HEADS), ((0, 0), (0, LANES - 2 * N_HEADS)))
    dtb = pad_heads(ssm_dt_bias[0])
    a_neg = pad_heads(-jnp.exp(ssm_a_log[0]))
    dvec = jnp.repeat(ssm_d[0], HEAD_DIM).reshape(1, D_INNER)
    nw = ssm_norm_w[0].reshape(1, D_INNER)
    e2 = _head_expand_matrix()
    xbc_p = _ssm_conv(proj, ssm_conv_w[0], ssm_conv_b[0], seq_len=SEQ, n_seq=BATCH, row_block0=0)
    xbc_s = _ssm_conv(proj, ssm_conv_w[0], ssm_conv_b[0], seq_len=DEC_SEQ, n_seq=DEC_BATCH,
                      row_block0=T_PROMPT // DEC_SEQ)
    y_p, fin = _ssd(xbc_p, dt_raw, proj, None, dtb, a_neg, dvec, nw, e2,
                    seq_len=SEQ, n_seq=BATCH, tok0=0, want_final=True)
    init = state_ssm.reshape(DEC_BATCH, 2, D_INNER, D_STATE)
    (y_s,) = _ssd(xbc_s, dt_raw, proj, init, dtb, a_neg, dvec, nw, e2,
                  seq_len=DEC_SEQ, n_seq=DEC_BATCH, tok0=T_PROMPT, want_final=False)

    v_p = _conformer(proj, conf_conv_w[0], conf_conv_b[0], conf_ln_w[0], conf_ln_b[0],
                     seq_len=SEQ, n_seq=BATCH, row_block0=0, grid_mode=False)
    v_s = _conformer(proj, conf_conv_w[0], conf_conv_b[0], conf_ln_w[0], conf_ln_b[0],
                     seq_len=DEC_SEQ, n_seq=DEC_BATCH, row_block0=T_PROMPT // DEC_SEQ, grid_mode=True)

    sh_gu = jnp.concatenate([sh_w_gate[0], sh_w_up[0]], axis=1).astype(BF16)
    pre, v_all, scores = _merge(y_p, y_s, v_p, v_s, proj, x_p, x_s, mod3, w_out_ssm[0].astype(BF16),
                                w_out_conf[0].astype(BF16), w_o[0].astype(BF16), norm2_w[0], router_w[0],
                                sh_gu, sh_w_down[0].astype(BF16))

    idx_t, w_t, rank_t, cnt = _route(scores, router_bias[0])
    start = _group_starts(cnt[:, 0].astype(I32))
    dest_t = _dest_slots(idx_t, rank_t, start[:N_EXPERTS])

    xs = _dispatch(dest_t, v_all)
    y_rows = _grouped_mlp(start, xs, exp_w_gate[0], exp_w_up[0], exp_w_down[0])
    out_p, out_s = _combine(dest_t, y_rows, pre, w_t.T, mod3, final_norm_w)

    y_prompt = out_p.reshape(BATCH, SEQ, D_MODEL)
    y_sample = out_s.reshape(DEC_BATCH, DEC_SEQ, D_MODEL)
    new_state = fin.reshape(BATCH, 1, 2, N_HEADS, HEAD_DIM, D_STATE)
    return (y_prompt, y_sample, new_state)
```

```python
import functools

import jax
import jax.numpy as jnp
from jax import lax
from jax.experimental import pallas as pl
from jax.experimental.pallas import tpu as pltpu

F32 = jnp.float32
BF16 = jnp.bfloat16
I32 = jnp.int32

D_MODEL = 1024
BATCH = 32
SEQ = 256
DEC_BATCH = 8
DEC_SEQ = 1024
GRID_W = 64
GRID_H = DEC_SEQ // GRID_W
D_INNER = 2048
HEAD_DIM = 64
N_HEADS = 32
D_STATE = 128
N_GROUPS = 8
HEADS_PER_GROUP = N_HEADS // N_GROUPS
GROUP_W = HEADS_PER_GROUP * HEAD_DIM
D_CONV_SSM = 5
CHUNK = 128
CONV_DIM = D_INNER + 2 * N_GROUPS * D_STATE
CONF_K = 31
CONF_PAD = CONF_K // 2
N_EXPERTS = 256
TOP_K = 8
N_EXPERT_GROUPS = 8
EXPERTS_PER_GROUP = N_EXPERTS // N_EXPERT_GROUPS
TOPK_GROUPS = 4
D_EXPERT = 256
D_SHARED = 256
ROUTED_SCALE = 2.5
N_MOD = 6
EPS = 1e-6

T_PROMPT = BATCH * SEQ
T_SAMPLE = DEC_BATCH * DEC_SEQ
T_ALL = T_PROMPT + T_SAMPLE
N_ASSIGN = T_ALL * TOP_K
MOD_ROWS = 16
CTX_ROW = DEC_BATCH

SUBLANES = 8
LANES = 128
VMEM_LIMIT = 56 * 1024 * 1024

PROJ_W = CONV_DIM + D_INNER + 2 * D_MODEL + 2 * D_MODEL
COL_Z = CONV_DIM // D_INNER
COL_GL = COL_Z + 1
COL_CV = (CONV_DIM + 2 * D_INNER) // D_MODEL
COL_CG = COL_CV + 1

HALF_D = D_MODEL // 2
ROW_TILE = HALF_D // LANES
GMM_TM = 256
N_ROWS = N_ASSIGN + GMM_TM
TAIL_SIZES = (128, 64, 32, 16, 8, 4, 2, 1)


def _cparams(sem, vmem=VMEM_LIMIT):
    return pltpu.CompilerParams(dimension_semantics=sem, vmem_limit_bytes=vmem)


def _silu(x):
    return x * jax.nn.sigmoid(x)


def _split2(x):
    hi = x.astype(BF16)
    lo = (x - hi.astype(F32)).astype(BF16)
    return hi, lo


def _split3(x):
    b1 = x.astype(BF16)
    r = x - b1.astype(F32)
    b2 = r.astype(BF16)
    b3 = (r - b2.astype(F32)).astype(BF16)
    return b1, b2, b3


def _dot(a, b):
    return jnp.dot(a, b, preferred_element_type=F32)


def _dot_exact_lhs(a_exact, b):
    b1, b2, b3 = _split3(b)
    return _dot(a_exact, b1) + _dot(a_exact, b2) + _dot(a_exact, b3)


def _dot_f32(a, b):
    a1, a2 = _split2(a)
    b1, b2 = _split2(b)
    return _dot(a1, b1) + _dot(a1, b2) + _dot(a2, b1)


def _mod_row_of_block(i, blocks_prompt, blocks_per_sample_seq):
    return jnp.where(i < blocks_prompt, CTX_ROW, (i - blocks_prompt) // blocks_per_sample_seq)


def _mod_kernel(c_ref, w_ref, b_ref, o_ref):
    c = c_ref[...]
    o_ref[...] = _dot_f32(_silu(c), w_ref[...]) + b_ref[...]


def _modulation(cc, w_mod, b_mod):
    tn = 512
    n = N_MOD * D_MODEL
    return pl.pallas_call(
        _mod_kernel,
        out_shape=jax.ShapeDtypeStruct((MOD_ROWS, n), F32),
        grid=(n // tn,),
        in_specs=[pl.BlockSpec((MOD_ROWS, D_MODEL), lambda j: (0, 0)),
                  pl.BlockSpec((D_MODEL, tn), lambda j: (0, j)),
                  pl.BlockSpec((1, tn), lambda j: (0, j))],
        out_specs=pl.BlockSpec((MOD_ROWS, tn), lambda j: (0, j)),
        compiler_params=_cparams(("arbitrary",)),
        name="modulation",
    )(cc, w_mod, b_mod.reshape(1, n))


INPROJ_TM = 1024
INPROJ_TN = 2048


def _two_path_specs(block, n_prompt_blocks):
    last = n_prompt_blocks - 1
    return (pl.BlockSpec(block, lambda i, *_: (jnp.minimum(i, last), 0)),
            pl.BlockSpec(block, lambda i, *_: (jnp.maximum(i - n_prompt_blocks, 0), 0)))


def _pick_path(n_prompt_blocks, prompt_ref, sample_ref):
    return jnp.where(pl.program_id(0) < n_prompt_blocks, prompt_ref[...], sample_ref[...])


def _inproj_kernel(xp_ref, xs_ref, mod_ref, n1_ref, w_ref, wdt_ref, o_ref, dt_ref, u_scr):
    @pl.when(pl.program_id(1) == 0)
    def _():
        x = _pick_path(T_PROMPT // INPROJ_TM, xp_ref, xs_ref)
        ms = jnp.mean(x * x, axis=-1, keepdims=True)
        y = x * lax.rsqrt(ms + EPS) * n1_ref[...]
        m = mod_ref[...]
        u = (y * (1.0 + m[1:2]) + m[0:1]).astype(BF16)
        u_scr[...] = u
        dt_ref[...] = _dot(u, wdt_ref[...])

    o_ref[...] = _dot(u_scr[...], w_ref[...]).astype(BF16)


def _in_projection(x_p, x_s, mod3, norm1_w, w_main, w_dt):
    tm, tn = INPROJ_TM, INPROJ_TN
    row = functools.partial(_mod_row_of_block, blocks_prompt=T_PROMPT // tm,
                            blocks_per_sample_seq=DEC_SEQ // tm)
    return pl.pallas_call(
        _inproj_kernel,
        out_shape=(jax.ShapeDtypeStruct((T_ALL, PROJ_W), BF16),
                   jax.ShapeDtypeStruct((T_ALL, LANES), F32)),
        grid=(T_ALL // tm, PROJ_W // tn),
        in_specs=[*_two_path_specs((tm, D_MODEL), T_PROMPT // tm),
                  pl.BlockSpec((None, N_MOD, D_MODEL), lambda i, j: (row(i), 0, 0)),
                  pl.BlockSpec((1, D_MODEL), lambda i, j: (0, 0)),
                  pl.BlockSpec((D_MODEL, tn), lambda i, j: (0, j)),
                  pl.BlockSpec((D_MODEL, LANES), lambda i, j: (0, 0))],
        out_specs=(pl.BlockSpec((tm, tn), lambda i, j: (i, j)),
                   pl.BlockSpec((tm, LANES), lambda i, j: (i, 0))),
        scratch_shapes=[pltpu.VMEM((tm, D_MODEL), BF16)],
        compiler_params=_cparams(("arbitrary", "arbitrary")),
        name="in_projection",
    )(x_p, x_s, mod3, norm1_w.reshape(1, D_MODEL), w_main, w_dt)


SSMCONV_TN = 512


def _ssmconv_kernel(x_ref, w_ref, b_ref, o_ref, *, seq_len):
    x = x_ref[...].astype(F32)
    row = lax.broadcasted_iota(I32, x.shape, 0)
    half = D_CONV_SSM // 2
    acc = b_ref[...] + w_ref[half:half + 1, :] * x
    for k in range(D_CONV_SSM):
        s = k - half
        if s == 0:
            continue
        shifted = pltpu.roll(x, shift=(-s) % seq_len, axis=0)
        valid = jnp.logical_and(row + s >= 0, row + s < seq_len)
        acc = acc + w_ref[k:k + 1, :] * jnp.where(valid, shifted, 0.0)
    o_ref[...] = _silu(acc).astype(BF16)


def _ssm_conv(proj, conv_w, conv_b, *, seq_len, n_seq, row_block0):
    tn = SSMCONV_TN
    return pl.pallas_call(
        functools.partial(_ssmconv_kernel, seq_len=seq_len),
        out_shape=jax.ShapeDtypeStruct((n_seq * seq_len, CONV_DIM), BF16),
        grid=(n_seq, CONV_DIM // tn),
        in_specs=[pl.BlockSpec((seq_len, tn), lambda b, j: (row_block0 + b, j)),
                  pl.BlockSpec((D_CONV_SSM, tn), lambda b, j: (0, j)),
                  pl.BlockSpec((1, tn), lambda b, j: (0, j))],
        out_specs=pl.BlockSpec((seq_len, tn), lambda b, j: (b, j)),
        compiler_params=_cparams(("arbitrary", "arbitrary")),
        name="ssm_conv",
    )(proj, conv_w, conv_b.reshape(1, CONV_DIM))


def _tri_masks():
    ii = lax.broadcasted_iota(I32, (CHUNK, CHUNK), 0)
    jj = lax.broadcasted_iota(I32, (CHUNK, CHUNK), 1)
    return ii, jj


def _chunk_decays(dt_ref, dtb_ref, a_ref):
    ii, jj = _tri_masks()
    pre = dt_ref[...] + dtb_ref[...]
    dt = jnp.maximum(pre, 0.0) + jnp.log(1.0 + jnp.exp(-jnp.abs(pre)))
    la = dt * a_ref[...]
    tri_lo = jnp.where(jj <= ii, 1.0, 0.0).astype(BF16)
    tri_up = jnp.where(jj >= ii, 1.0, 0.0).astype(BF16)
    cs_prefix = _dot_exact_lhs(tri_lo, la)
    cs_suffix = _dot_exact_lhs(tri_up, la)
    fwd_lane = lax.broadcasted_iota(I32, (CHUNK, LANES), 1) < N_HEADS
    cs = jnp.where(fwd_lane, cs_prefix, cs_suffix)
    tot = jnp.where(fwd_lane[0:1], cs_prefix[CHUNK - 1:CHUNK, :], cs_suffix[0:1, :])
    return dt, cs, tot


def _transpose_blocks(src, rows, cols):
    out_rows = []
    for cb in range(cols // LANES):
        pieces = [src[rb * LANES:(rb + 1) * LANES, cb * LANES:(cb + 1) * LANES].T
                  for rb in range(rows // LANES)]
        out_rows.append(jnp.concatenate(pieces, axis=1) if len(pieces) > 1 else pieces[0])
    return jnp.concatenate(out_rows, axis=0) if len(out_rows) > 1 else out_rows[0]


def _ssd_kernel(*refs, n_chunks, has_init, want_final):
    it = iter(refs)
    xbc_ref, dt_ref, z_ref = next(it), next(it), next(it)
    init_ref = next(it) if has_init else None
    dtb_ref, a_ref, dvec_ref, nw_ref, e2_ref = next(it), next(it), next(it), next(it), next(it)
    out_ref = next(it)
    fin_ref = next(it) if want_final else None
    sf_scr, sb_scr, df_scr, db_scr = next(it), next(it), next(it), next(it)

    phase = pl.program_id(1)
    c = pl.program_id(2)

    @pl.when(phase == 0)
    def _chunk_states():
        dt, cs, tot = _chunk_decays(dt_ref, dtb_ref, a_ref)
        w_in = dt * jnp.exp(tot - cs)
        pack = 2 * SUBLANES
        dec_hi, dec_lo = _split2(jnp.exp(jnp.broadcast_to(tot, (pack, LANES))))
        expanded = _dot(jnp.concatenate([w_in.astype(BF16), dec_hi, dec_lo], axis=0), e2_ref[...])
        w_exp = expanded[:CHUNK]
        dec_exp = expanded[CHUNK:CHUNK + SUBLANES] + expanded[CHUNK + pack:CHUNK + pack + SUBLANES]
        df_scr[c] = dec_exp[:, :D_INNER]
        db_scr[c] = dec_exp[:, D_INNER:]
        for g in range(N_GROUPS):
            lo = g * GROUP_W
            xg = xbc_ref[:, lo:lo + GROUP_W].astype(F32)
            xd_f = (xg * w_exp[:, lo:lo + GROUP_W]).astype(BF16)
            xd_b = (xg * w_exp[:, D_INNER + lo:D_INNER + lo + GROUP_W]).astype(BF16)
            bg = xbc_ref[:, D_INNER + g * D_STATE:D_INNER + (g + 1) * D_STATE]
            bg_t = bg.astype(F32).T.astype(BF16)
            sf_scr[c, :, lo:lo + GROUP_W] = _dot(bg_t, xd_f)
            sb_scr[c, :, lo:lo + GROUP_W] = _dot(bg_t, xd_b)

    @pl.when(jnp.logical_and(phase == 1, c == 0))
    def _recurrence():
        for g in range(N_GROUPS):
            lo = g * GROUP_W
            if has_init:
                prev_f = _transpose_blocks(init_ref[0, 0, lo:lo + GROUP_W, :], GROUP_W, D_STATE)
                prev_b = _transpose_blocks(init_ref[0, 1, lo:lo + GROUP_W, :], GROUP_W, D_STATE)
            else:
                prev_f = jnp.zeros((D_STATE, GROUP_W), F32)
                prev_b = jnp.zeros((D_STATE, GROUP_W), F32)
            for cc in range(n_chunks):
                s = sf_scr[cc, :, lo:lo + GROUP_W]
                sf_scr[cc, :, lo:lo + GROUP_W] = prev_f
                prev_f = df_scr[cc, 0:1, lo:lo + GROUP_W] * prev_f + s
            for cc in reversed(range(n_chunks)):
                s = sb_scr[cc, :, lo:lo + GROUP_W]
                sb_scr[cc, :, lo:lo + GROUP_W] = prev_b
                prev_b = db_scr[cc, 0:1, lo:lo + GROUP_W] * prev_b + s
            if want_final:
                fin_ref[0, 0, lo:lo + GROUP_W, :] = _transpose_blocks(prev_f, D_STATE, GROUP_W)
                fin_ref[0, 1, lo:lo + GROUP_W, :] = _transpose_blocks(prev_b, D_STATE, GROUP_W)

    @pl.when(phase == 1)
    def _outputs():
        ii, jj = _tri_masks()
        dt, cs, _ = _chunk_decays(dt_ref, dtb_ref, a_ref)
        out_dec = _dot(jnp.exp(cs).astype(BF16), e2_ref[...])
        cs_t = cs.T
        dt_t = dt.T
        causal = ii >= jj
        anti = jj >= ii
        neg = jnp.float32(-1e30)
        left = lax.broadcasted_iota(I32, (CHUNK, LANES), 1) < HEAD_DIM
        for g in range(N_GROUPS):
            lo = g * GROUP_W
            bg = xbc_ref[:, D_INNER + g * D_STATE:D_INNER + (g + 1) * D_STATE]
            cg = xbc_ref[:, D_INNER + N_GROUPS * D_STATE + g * D_STATE:
                         D_INNER + N_GROUPS * D_STATE + (g + 1) * D_STATE]
            cb = lax.dot_general(cg, bg, (((1,), (1,)), ((), ())), preferred_element_type=F32)
            pf = sf_scr[c, :, lo:lo + GROUP_W].astype(BF16)
            pb = sb_scr[c, :, lo:lo + GROUP_W].astype(BF16)
            y_off = (_dot(cg, pf) * out_dec[:, lo:lo + GROUP_W]
                     + _dot(cg, pb) * out_dec[:, D_INNER + lo:D_INNER + lo + GROUP_W])
            pairs = []
            for m in range(HEADS_PER_GROUP // 2):
                x_pair = xbc_ref[:, lo + m * LANES:lo + (m + 1) * LANES]
                ys = []
                for hh in range(2):
                    h = g * HEADS_PER_GROUP + 2 * m + hh
                    hb = N_HEADS + h
                    seg_f = jnp.where(causal, cs[:, h:h + 1] - cs_t[h:h + 1, :], neg)
                    seg_b = jnp.where(anti, cs[:, hb:hb + 1] - cs_t[hb:hb + 1, :], neg)
                    mix = (jnp.exp(seg_f) * dt_t[h:h + 1, :] + jnp.exp(seg_b) * dt_t[hb:hb + 1, :])
                    ys.append(_dot((cb * mix).astype(BF16), x_pair))
                pairs.append(jnp.where(left, ys[0], ys[1]))
            y_diag = jnp.concatenate(pairs, axis=1)
            xg = xbc_ref[:, lo:lo + GROUP_W].astype(F32)
            y = y_diag + y_off + dvec_ref[:, lo:lo + GROUP_W] * xg
            zg = z_ref[:, lo:lo + GROUP_W].astype(F32)
            y = y * _silu(zg)
            ms = jnp.mean(y * y, axis=-1, keepdims=True)
            out_ref[:, lo:lo + GROUP_W] = (y * lax.rsqrt(ms + EPS) * nw_ref[:, lo:lo + GROUP_W]).astype(BF16)


def _ssd(xbc, dt_raw, proj, init, dtb, a_neg, dvec, norm_w, e2, *, seq_len, n_seq, tok0, want_final):
    nc = seq_len // CHUNK
    blk0 = tok0 // CHUNK
    has_init = init is not None
    chunk_map = lambda b, p, c: (b * nc + c, 0)
    in_specs = [pl.BlockSpec((CHUNK, CONV_DIM), chunk_map),
                pl.BlockSpec((CHUNK, LANES), lambda b, p, c: (blk0 + b * nc + c, 0)),
                pl.BlockSpec((CHUNK, D_INNER), lambda b, p, c: (blk0 + b * nc + c * p, COL_Z))]
    args = [xbc, dt_raw, proj]
    if has_init:
        in_specs.append(pl.BlockSpec((1, 2, D_INNER, D_STATE), lambda b, p, c: (b, 0, 0, 0)))
        args.append(init)
    const = lambda b, p, c: (0, 0)
    in_specs += [pl.BlockSpec((1, LANES), const), pl.BlockSpec((1, LANES), const),
                 pl.BlockSpec((1, D_INNER), const), pl.BlockSpec((1, D_INNER), const),
                 pl.BlockSpec((LANES, 2 * D_INNER), const)]
    args += [dtb, a_neg, dvec, norm_w, e2]
    out_shape = [jax.ShapeDtypeStruct((n_seq * seq_len, D_INNER), BF16)]
    out_specs = [pl.BlockSpec((CHUNK, D_INNER), lambda b, p, c: (b * nc + c * p, 0))]
    if want_final:
        out_shape.append(jax.ShapeDtypeStruct((n_seq, 2, D_INNER, D_STATE), F32))
        out_specs.append(pl.BlockSpec((1, 2, D_INNER, D_STATE), lambda b, p, c: (b, 0, 0, 0)))
    res = pl.pallas_call(
        functools.partial(_ssd_kernel, n_chunks=nc, has_init=has_init, want_final=want_final),
        out_shape=tuple(out_shape),
        grid=(n_seq, 2, nc),
        in_specs=in_specs,
        out_specs=tuple(out_specs),
        scratch_shapes=[pltpu.VMEM((nc, D_STATE, D_INNER), F32),
                        pltpu.VMEM((nc, D_STATE, D_INNER), F32),
                        pltpu.VMEM((nc, SUBLANES, D_INNER), F32),
                        pltpu.VMEM((nc, SUBLANES, D_INNER), F32)],
        compiler_params=_cparams(("arbitrary", "arbitrary", "arbitrary")),
        name="ssd_final" if want_final else "ssd_init",
    )(*args)
    return res


def _layernorm_silu(v, lw_ref, lb_ref):
    mu = jnp.mean(v, axis=-1, keepdims=True)
    d = v - mu
    var = jnp.mean(d * d, axis=-1, keepdims=True)
    return _silu(d * lax.rsqrt(var + EPS) * lw_ref[...] + lb_ref[...])


def _conf_seq_kernel(cv_ref, cg_ref, w_ref, b_ref, lw_ref, lb_ref, o_ref, pad_scr, acc_scr):
    seq_len = cv_ref.shape[0]
    base = 2 * SUBLANES
    v = cv_ref[...].astype(F32) * jax.nn.sigmoid(cg_ref[...].astype(F32))
    pad_scr[0:base, :] = jnp.zeros((base, D_MODEL), F32)
    pad_scr[base + seq_len:base + seq_len + base, :] = jnp.zeros((base, D_MODEL), F32)
    pad_scr[base:base + seq_len, :] = v
    rt, ct = 128, 256
    for r0 in range(0, seq_len, rt):
        for c0 in range(0, D_MODEL, ct):
            acc = jnp.broadcast_to(b_ref[:, c0:c0 + ct], (rt, ct))
            for k in range(CONF_K):
                off = base + r0 + k - CONF_PAD
                acc = acc + w_ref[k:k + 1, c0:c0 + ct] * pad_scr[off:off + rt, c0:c0 + ct]
            acc_scr[r0:r0 + rt, c0:c0 + ct] = acc
    o_ref[...] = _layernorm_silu(acc_scr[...], lw_ref, lb_ref).astype(BF16)


def _conf_grid_kernel(cv_ref, cg_ref, w_ref, b_ref, lw_ref, lb_ref, o_ref, v_scr, pad_scr, acc_scr):
    half = D_MODEL // 2
    base = 2 * SUBLANES
    stride = GRID_W + 2 * base
    v_scr[...] = cv_ref[...].astype(F32) * jax.nn.sigmoid(cg_ref[...].astype(F32))
    pad_scr[...] = jnp.zeros(pad_scr.shape, F32)
    for r in range(GRID_H):
        pad_scr[r * stride + base:r * stride + base + GRID_W, :] = v_scr[r * GRID_W:(r + 1) * GRID_W, 0:half]
    ct = 256
    for r in range(GRID_H):
        for c0 in range(0, half, ct):
            acc = jnp.broadcast_to(b_ref[:, c0:c0 + ct], (GRID_W, ct))
            for k in range(CONF_K):
                off = r * stride + base + k - CONF_PAD
                acc = acc + w_ref[k:k + 1, c0:c0 + ct] * pad_scr[off:off + GRID_W, c0:c0 + ct]
            acc_scr[r * GRID_W:(r + 1) * GRID_W, c0:c0 + ct] = acc
        for c0 in range(half, D_MODEL, ct):
            acc = jnp.broadcast_to(b_ref[:, c0:c0 + ct], (GRID_W, ct))
            for r2 in range(GRID_H):
                k = r2 - r + CONF_PAD
                acc = acc + w_ref[k:k + 1, c0:c0 + ct] * v_scr[r2 * GRID_W:(r2 + 1) * GRID_W, c0:c0 + ct]
            acc_scr[r * GRID_W:(r + 1) * GRID_W, c0:c0 + ct] = acc
    o_ref[...] = _layernorm_silu(acc_scr[...], lw_ref, lb_ref).astype(BF16)


def _conformer(proj, conv_w, conv_b, ln_w, ln_b, *, seq_len, n_seq, row_block0, grid_mode):
    base = 2 * SUBLANES
    if grid_mode:
        body = _conf_grid_kernel
        scratch = [pltpu.VMEM((seq_len, D_MODEL), F32),
                   pltpu.VMEM((GRID_H * (GRID_W + 2 * base), D_MODEL // 2), F32),
                   pltpu.VMEM((seq_len, D_MODEL), F32)]
    else:
        body = _conf_seq_kernel
        scratch = [pltpu.VMEM((seq_len + 2 * base, D_MODEL), F32),
                   pltpu.VMEM((seq_len, D_MODEL), F32)]
    const = lambda b: (0, 0)
    return pl.pallas_call(
        body,
        out_shape=jax.ShapeDtypeStruct((n_seq * seq_len, D_MODEL), BF16),
        grid=(n_seq,),
        in_specs=[pl.BlockSpec((seq_len, D_MODEL), lambda b: (row_block0 + b, COL_CV)),
                  pl.BlockSpec((seq_len, D_MODEL), lambda b: (row_block0 + b, COL_CG)),
                  pl.BlockSpec((CONF_K, D_MODEL), const),
                  pl.BlockSpec((1, D_MODEL), const),
                  pl.BlockSpec((1, D_MODEL), const),
                  pl.BlockSpec((1, D_MODEL), const)],
        out_specs=pl.BlockSpec((seq_len, D_MODEL), lambda b: (b, 0)),
        scratch_shapes=scratch,
        compiler_params=_cparams(("arbitrary",)),
        name="conformer_grid" if grid_mode else "conformer_seq",
    )(proj, proj, conv_w, conv_b.reshape(1, D_MODEL), ln_w.reshape(1, D_MODEL), ln_b.reshape(1, D_MODEL))


MERGE_TM = 256


def _merge_kernel(yap_ref, yas_ref, ybp_ref, ybs_ref, gl_ref, xp_ref, xs_ref, mod_ref, wa_ref, wb_ref, wo_ref,
                  n2_ref, rw_ref, sgu_ref, sd_ref, pre_ref, v_ref, s_ref):
    npb = T_PROMPT // MERGE_TM
    y_a = _dot(_pick_path(npb, yap_ref, yas_ref), wa_ref[...])
    y_b = _dot(_pick_path(npb, ybp_ref, ybs_ref), wb_ref[...])
    gates = jax.nn.sigmoid(gl_ref[...].astype(F32))
    mix = gates[:, :D_MODEL] * y_a + gates[:, D_MODEL:] * y_b
    out = _dot(mix.astype(BF16), wo_ref[...])
    m = mod_ref[...]
    x1 = _pick_path(npb, xp_ref, xs_ref) + m[2:3] * out
    ms = jnp.mean(x1 * x1, axis=-1, keepdims=True)
    v = x1 * lax.rsqrt(ms + EPS) * n2_ref[...] * (1.0 + m[4:5]) + m[3:4]
    v_ref[...] = v
    s_ref[...] = jax.nn.sigmoid(_dot_f32(v, rw_ref[...]))
    hgu = _dot(v.astype(BF16), sgu_ref[...])
    act = _silu(hgu[:, :D_SHARED]) * hgu[:, D_SHARED:]
    shared = _dot(act.astype(BF16), sd_ref[...])
    pre_ref[...] = x1 + m[5:6] * shared


def _merge(y_ssm_p, y_ssm_s, v_conf_p, v_conf_s, proj, x_p, x_s, mod3, w_out_ssm, w_out_conf, w_o, norm2_w,
           router_w, sh_gu, sh_d):
    tm = MERGE_TM
    npb = T_PROMPT // tm
    row = functools.partial(_mod_row_of_block, blocks_prompt=npb, blocks_per_sample_seq=DEC_SEQ // tm)
    const = lambda i: (0, 0)
    return pl.pallas_call(
        _merge_kernel,
        out_shape=(jax.ShapeDtypeStruct((T_ALL, D_MODEL), F32),
                   jax.ShapeDtypeStruct((T_ALL, D_MODEL), F32),
                   jax.ShapeDtypeStruct((T_ALL, N_EXPERTS), F32)),
        grid=(T_ALL // tm,),
        in_specs=[*_two_path_specs((tm, D_INNER), npb),
                  *_two_path_specs((tm, D_MODEL), npb),
                  pl.BlockSpec((tm, 2 * D_MODEL), lambda i: (i, COL_GL)),
                  *_two_path_specs((tm, D_MODEL), npb),
                  pl.BlockSpec((None, N_MOD, D_MODEL), lambda i: (row(i), 0, 0)),
                  pl.BlockSpec((D_INNER, D_MODEL), const),
                  pl.BlockSpec((D_MODEL, D_MODEL), const),
                  pl.BlockSpec((D_MODEL, D_MODEL), const),
                  pl.BlockSpec((1, D_MODEL), const),
                  pl.BlockSpec((D_MODEL, N_EXPERTS), const),
                  pl.BlockSpec((D_MODEL, 2 * D_SHARED), const),
                  pl.BlockSpec((D_SHARED, D_MODEL), const)],
        out_specs=(pl.BlockSpec((tm, D_MODEL), lambda i: (i, 0)),
                   pl.BlockSpec((tm, D_MODEL), lambda i: (i, 0)),
                   pl.BlockSpec((tm, N_EXPERTS), lambda i: (i, 0))),
        compiler_params=_cparams(("arbitrary",)),
        name="merge",
    )(y_ssm_p, y_ssm_s, v_conf_p, v_conf_s, proj, x_p, x_s, mod3, w_out_ssm, w_out_conf, w_o,
      norm2_w.reshape(1, D_MODEL), router_w, sh_gu, sh_d)


ROUTE_TB = 256


def _first_index_of_max(vals, iota, sentinel):
    m = jnp.max(vals, axis=0, keepdims=True)
    idx = jnp.min(jnp.where(vals == m, iota, jnp.float32(sentinel)), axis=0, keepdims=True)
    return m, idx


def _route_kernel(s_ref, bias_ref, idx_ref, w_ref, rank_ref, cnt_ref, run_scr):
    i = pl.program_id(0)
    tb = ROUTE_TB
    neg = jnp.float32(-jnp.inf)

    @pl.when(i == 0)
    def _():
        run_scr[...] = jnp.zeros(run_scr.shape, F32)

    s_t = s_ref[...].T
    sb_t = s_t + bias_ref[...]
    eiota = lax.broadcasted_iota(I32, (N_EXPERTS, tb), 0).astype(F32)

    liota = lax.broadcasted_iota(I32, (EXPERTS_PER_GROUP, tb), 0).astype(F32)
    gscores = []
    for g in range(N_EXPERT_GROUPS):
        blk = sb_t[g * EXPERTS_PER_GROUP:(g + 1) * EXPERTS_PER_GROUP, :]
        m1, i1 = _first_index_of_max(blk, liota, EXPERTS_PER_GROUP)
        m2 = jnp.max(jnp.where(liota == i1, neg, blk), axis=0, keepdims=True)
        gscores.append(m1 + m2)
    gs = jnp.concatenate(gscores, axis=0)
    giota = lax.broadcasted_iota(I32, (N_EXPERT_GROUPS, tb), 0).astype(F32)
    gsel = jnp.zeros((N_EXPERT_GROUPS, tb), F32)
    for _ in range(TOPK_GROUPS):
        _, gi = _first_index_of_max(gs, giota, N_EXPERT_GROUPS)
        hit = giota == gi
        gsel = jnp.where(hit, 1.0, gsel)
        gs = jnp.where(hit, neg, gs)
    emask = jnp.concatenate(
        [jnp.broadcast_to(gsel[g:g + 1, :], (EXPERTS_PER_GROUP, tb)) for g in range(N_EXPERT_GROUPS)], axis=0)
    masked = jnp.where(emask > 0.5, sb_t, neg)

    onehots, idxs, wts = [], [], []
    for _ in range(TOP_K):
        _, ei = _first_index_of_max(masked, eiota, N_EXPERTS)
        hit = eiota == ei
        onehots.append(hit)
        idxs.append(ei)
        wts.append(jnp.sum(jnp.where(hit, s_t, 0.0), axis=0, keepdims=True))
        masked = jnp.where(hit, neg, masked)
    w = jnp.concatenate(wts, axis=0)
    w_ref[...] = w / jnp.sum(w, axis=0, keepdims=True) * ROUTED_SCALE
    idx_ref[...] = jnp.concatenate(idxs, axis=0).astype(I32)

    assign = jnp.zeros((N_EXPERTS, tb), F32)
    for hit in onehots:
        assign = jnp.where(hit, 1.0, assign)
    assign_b = assign.astype(BF16)
    ti = lax.broadcasted_iota(I32, (tb, tb), 0)
    tj = lax.broadcasted_iota(I32, (tb, tb), 1)
    before = jnp.where(ti < tj, 1.0, 0.0).astype(BF16)
    within = _dot(assign_b, before)
    run = run_scr[...]
    total = within + jnp.concatenate([run] * (tb // LANES), axis=1)
    rank_ref[...] = jnp.concatenate(
        [jnp.sum(jnp.where(hit, total, 0.0), axis=0, keepdims=True) for hit in onehots], axis=0).astype(I32)
    new_run = run + _dot(assign_b, jnp.ones((tb, LANES), BF16))
    run_scr[...] = new_run
    cnt_ref[...] = new_run


def _route(scores, router_bias):
    tb = ROUTE_TB
    return pl.pallas_call(
        _route_kernel,
        out_shape=(jax.ShapeDtypeStruct((TOP_K, T_ALL), I32),
                   jax.ShapeDtypeStruct((TOP_K, T_ALL), F32),
                   jax.ShapeDtypeStruct((TOP_K, T_ALL), I32),
                   jax.ShapeDtypeStruct((N_EXPERTS, LANES), F32)),
        grid=(T_ALL // tb,),
        in_specs=[pl.BlockSpec((tb, N_EXPERTS), lambda i: (i, 0)),
                  pl.BlockSpec((N_EXPERTS, 1), lambda i: (0, 0))],
        out_specs=(pl.BlockSpec((TOP_K, tb), lambda i: (0, i)),
                   pl.BlockSpec((TOP_K, tb), lambda i: (0, i)),
                   pl.BlockSpec((TOP_K, tb), lambda i: (0, i)),
                   pl.BlockSpec((N_EXPERTS, LANES), lambda i: (0, 0))),
        scratch_shapes=[pltpu.VMEM((N_EXPERTS, LANES), F32)],
        compiler_params=_cparams(("arbitrary",)),
        name="route",
    )(scores, router_bias.reshape(N_EXPERTS, 1))


def _dest_kernel(idx_ref, rank_ref, start_ref, dest_ref):
    tb = idx_ref.shape[1]
    eiota = lax.broadcasted_iota(I32, (N_EXPERTS, tb), 0)
    start = jnp.broadcast_to(start_ref[...], (N_EXPERTS, tb))
    idx = idx_ref[...]
    rows = [jnp.sum(jnp.where(eiota == idx[k:k + 1, :], start, 0.0), axis=0, keepdims=True)
            for k in range(TOP_K)]
    dest_ref[...] = jnp.concatenate(rows, axis=0).astype(I32) + rank_ref[...]


def _dest_slots(idx_t, rank_t, group_start):
    tb = 512
    return pl.pallas_call(
        _dest_kernel,
        out_shape=jax.ShapeDtypeStruct((TOP_K, T_ALL), I32),
        grid=(T_ALL // tb,),
        in_specs=[pl.BlockSpec((TOP_K, tb), lambda i: (0, i)),
                  pl.BlockSpec((TOP_K, tb), lambda i: (0, i)),
                  pl.BlockSpec((N_EXPERTS, 1), lambda i: (0, 0))],
        out_specs=pl.BlockSpec((TOP_K, tb), lambda i: (0, i)),
        compiler_params=_cparams(("arbitrary",)),
        name="dest_slots",
    )(idx_t, rank_t, group_start.astype(F32).reshape(N_EXPERTS, 1))


DISPATCH_TB = 256


HI_MASK = -65536


def _pack_pair(hi, lo):
    hb = pltpu.bitcast(hi.astype(BF16).astype(F32), I32)
    lb = pltpu.bitcast(lo.astype(BF16).astype(F32), I32)
    return jnp.bitwise_or(hb, lax.shift_right_logical(lb, 16))


def _unpack_pair(word):
    hi = pltpu.bitcast(jnp.bitwise_and(word, HI_MASK), F32)
    lo = pltpu.bitcast(lax.shift_left(word, 16), F32)
    return hi, lo


def _rows_to_tiles(dst_scr, base, rows):
    n = rows.shape[0]
    for s in range(ROW_TILE):
        dst_scr[pl.ds(base + s, n, stride=ROW_TILE), :] = _pack_pair(
            rows[:, s * LANES:(s + 1) * LANES], rows[:, HALF_D + s * LANES:HALF_D + (s + 1) * LANES])


def _tile_column(src_scr, base, n, s):
    return src_scr[pl.ds(base + s, n, stride=ROW_TILE), :]


def _tiles_to_rows(src_scr, base, n, dtype):
    pairs = [_unpack_pair(_tile_column(src_scr, base, n, s)) for s in range(ROW_TILE)]
    return jnp.concatenate([p[0].astype(dtype) for p in pairs] + [p[1].astype(dtype) for p in pairs], axis=1)


def _row_tile(ref, row):
    return ref.at[pl.ds(pl.multiple_of(row * ROW_TILE, ROW_TILE), ROW_TILE), :]


def _dispatch_kernel(dest_ref, v_ref, xs_hbm, tile_scr, zero_scr, sem, zsem):
    i = pl.program_id(0)
    n = pl.num_programs(0)
    tb = DISPATCH_TB
    slot = i % 2
    base = pl.multiple_of(slot * (tb * ROW_TILE), ROW_TILE)
    _rows_to_tiles(tile_scr, base, v_ref[...])

    def issue(t, carry):
        src = _row_tile(tile_scr, slot * tb + t)
        for k in range(TOP_K):
            pltpu.make_async_copy(src, _row_tile(xs_hbm, dest_ref[k, t]), sem.at[slot]).start(priority=k % 2)
        return carry

    lax.fori_loop(0, tb, issue, 0)

    def wait_block(sl):
        blk = tile_scr.at[pl.ds(pl.multiple_of(sl * (tb * ROW_TILE), ROW_TILE), tb * ROW_TILE), :]
        for _ in range(TOP_K):
            pltpu.make_async_copy(blk, blk, sem.at[sl]).wait()

    @pl.when(i == 0)
    def _():
        zero_scr[...] = jnp.zeros(zero_scr.shape, I32)
        for j in range(GMM_TM // SUBLANES):
            cp = pltpu.make_async_copy(
                zero_scr, xs_hbm.at[pl.ds((N_ASSIGN + j * SUBLANES) * ROW_TILE, SUBLANES * ROW_TILE), :], zsem)
            cp.start()
            cp.wait()

    @pl.when(i > 0)
    def _():
        wait_block(1 - slot)

    @pl.when(i == n - 1)
    def _():
        wait_block(slot)


def _dispatch(dest_t, v_all):
    tb = DISPATCH_TB
    return pl.pallas_call(
        _dispatch_kernel,
        out_shape=jax.ShapeDtypeStruct((N_ROWS * ROW_TILE, LANES), I32),
        grid=(T_ALL // tb,),
        in_specs=[pl.BlockSpec((TOP_K, tb), lambda i: (0, i), memory_space=pltpu.SMEM),
                  pl.BlockSpec((tb, D_MODEL), lambda i: (i, 0))],
        out_specs=pl.BlockSpec(memory_space=pl.ANY),
        scratch_shapes=[pltpu.VMEM((2 * tb * ROW_TILE, LANES), I32),
                        pltpu.VMEM((SUBLANES * ROW_TILE, LANES), I32),
                        pltpu.SemaphoreType.DMA((2,)),
                        pltpu.SemaphoreType.DMA(())],
        compiler_params=_cparams(("arbitrary",)),
        name="dispatch",
    )(dest_t, v_all)


def _flat_rows(ref, row, n_rows):
    return ref.at[pl.ds(pl.multiple_of(row * ROW_TILE, ROW_TILE), n_rows * ROW_TILE), :]


def _tile_writes(y_scr, y_hbm, slot, row0, valid, sem):
    base = slot * GMM_TM
    parts = [(valid == GMM_TM, pltpu.make_async_copy(_flat_rows(y_scr, base, GMM_TM),
                                                     _flat_rows(y_hbm, row0, GMM_TM), sem))]
    off = jnp.int32(0)
    for p in TAIL_SIZES:
        bit = (valid & p) != 0
        take = jnp.logical_and(valid < GMM_TM, bit)
        parts.append((take, pltpu.make_async_copy(_flat_rows(y_scr, base + off, p),
                                                  _flat_rows(y_hbm, row0 + off, p), sem)))
        off = off + jnp.where(bit, p, 0)
    return parts


ST_TILES, ST_ROW, ST_VALID, ST_FETCH_E, ST_FETCH_I, ST_FETCH_G, ST_SIZE = 0, 1, 3, 5, 6, 7, 8
X_SLOTS = 4
X_AHEAD = X_SLOTS - 1


def _gmm_kernel(start_ref, xs_hbm, wg_ref, wu_ref, wd_ref, y_hbm,
                x_scr, y_scr, wgu_scr, wdn_scr, zero_scr, st_ref, xsem, ysem, zsem):
    e = pl.program_id(0)
    row_lo = start_ref[e]
    row_hi = start_ref[e + 1]
    n_rows = row_hi - row_lo
    n_tiles = lax.shift_right_logical(n_rows + (GMM_TM - 1), GMM_TM.bit_length() - 1)

    def x_copy(row, sl):
        return pltpu.make_async_copy(_flat_rows(xs_hbm, row, GMM_TM),
                                     _flat_rows(x_scr, sl * GMM_TM, GMM_TM), xsem.at[sl])

    def tiles_of(ex):
        rows = start_ref[ex + 1] - start_ref[ex]
        return lax.shift_right_logical(rows + (GMM_TM - 1), GMM_TM.bit_length() - 1)

    def skip_empty(ex):
        def empty(q):
            qc = jnp.minimum(q, N_EXPERTS - 1)
            return jnp.logical_and(q < N_EXPERTS, start_ref[qc + 1] == start_ref[qc])
        return lax.while_loop(empty, lambda q: q + 1, ex)

    def fetch_next_tile():
        pe = st_ref[ST_FETCH_E]

        @pl.when(pe < N_EXPERTS)
        def _():
            pi = st_ref[ST_FETCH_I]
            pg = st_ref[ST_FETCH_G]
            pec = jnp.minimum(pe, N_EXPERTS - 1)
            x_copy(start_ref[pec] + pi * GMM_TM, pg % X_SLOTS).start()
            last = pi + 1 >= tiles_of(pec)
            st_ref[ST_FETCH_E] = jnp.where(last, skip_empty(pe + 1), pe)
            st_ref[ST_FETCH_I] = jnp.where(last, 0, pi + 1)
            st_ref[ST_FETCH_G] = pg + 1

    def wait_writes(sl):
        @pl.when(st_ref[ST_VALID + sl] > 0)
        def _():
            for pred, cp in _tile_writes(y_scr, y_hbm, sl, st_ref[ST_ROW + sl], st_ref[ST_VALID + sl],
                                         ysem.at[sl]):
                @pl.when(pred)
                def _():
                    cp.wait()
            st_ref[ST_VALID + sl] = 0

    @pl.when(e == 0)
    def _():
        for j in range(ST_SIZE):
            st_ref[j] = 0
        st_ref[ST_FETCH_E] = skip_empty(jnp.int32(0))
        for _ in range(X_AHEAD):
            fetch_next_tile()

    done = st_ref[ST_TILES]

    @pl.when(n_tiles > 0)
    def _():
        wgu_scr[:, :D_EXPERT] = wg_ref[0].astype(BF16)
        wgu_scr[:, D_EXPERT:] = wu_ref[0].astype(BF16)
        wdn_scr[...] = wd_ref[0].astype(BF16)

        def tile(i, carry):
            g = done + i
            slot = g % 2
            xslot = g % X_SLOTS
            row0 = row_lo + i * GMM_TM
            valid = jnp.minimum(n_rows - i * GMM_TM, GMM_TM)
            x_copy(row0, xslot).wait()
            fetch_next_tile()

            base = pl.multiple_of(slot * (GMM_TM * ROW_TILE), ROW_TILE)
            xbase = pl.multiple_of(xslot * (GMM_TM * ROW_TILE), ROW_TILE)
            x = _tiles_to_rows(x_scr, xbase, GMM_TM, BF16)
            h = _dot(x, wgu_scr[...])
            act = (_silu(h[:, :D_EXPERT]) * h[:, D_EXPERT:]).astype(BF16)
            y = _dot(act, wdn_scr[...])

            wait_writes(slot)
            _rows_to_tiles(y_scr, base, y)
            for pred, cp in _tile_writes(y_scr, y_hbm, slot, row0, valid, ysem.at[slot]):
                @pl.when(pred)
                def _():
                    cp.start()
            st_ref[ST_ROW + slot] = row0
            st_ref[ST_VALID + slot] = valid
            return carry

        lax.fori_loop(0, n_tiles, tile, 0)
        st_ref[ST_TILES] = done + n_tiles

    @pl.when(e == pl.num_programs(0) - 1)
    def _():
        wait_writes(0)
        wait_writes(1)
        zero_scr[...] = jnp.zeros(zero_scr.shape, I32)
        for j in range(GMM_TM // SUBLANES):
            cp = pltpu.make_async_copy(zero_scr, _flat_rows(y_hbm, N_ASSIGN + j * SUBLANES, SUBLANES), zsem)
            cp.start()
            cp.wait()


def _grouped_mlp(group_start, xs, w_gate, w_up, w_down):
    grid_spec = pltpu.PrefetchScalarGridSpec(
        num_scalar_prefetch=1,
        grid=(N_EXPERTS,),
        in_specs=[pl.BlockSpec(memory_space=pl.ANY),
                  pl.BlockSpec((1, D_MODEL, D_EXPERT), lambda e, st: (e, 0, 0)),
                  pl.BlockSpec((1, D_MODEL, D_EXPERT), lambda e, st: (e, 0, 0)),
                  pl.BlockSpec((1, D_EXPERT, D_MODEL), lambda e, st: (e, 0, 0))],
        out_specs=pl.BlockSpec(memory_space=pl.ANY),
        scratch_shapes=[pltpu.VMEM((X_SLOTS * GMM_TM * ROW_TILE, LANES), I32),
                        pltpu.VMEM((2 * GMM_TM * ROW_TILE, LANES), I32),
                        pltpu.VMEM((D_MODEL, 2 * D_EXPERT), BF16),
                        pltpu.VMEM((D_EXPERT, D_MODEL), BF16),
                        pltpu.VMEM((SUBLANES * ROW_TILE, LANES), I32),
                        pltpu.SMEM((ST_SIZE,), I32),
                        pltpu.SemaphoreType.DMA((X_SLOTS,)),
                        pltpu.SemaphoreType.DMA((2,)),
                        pltpu.SemaphoreType.DMA(())],
    )
    return pl.pallas_call(
        _gmm_kernel,
        out_shape=jax.ShapeDtypeStruct((N_ROWS * ROW_TILE, LANES), I32),
        grid_spec=grid_spec,
        compiler_params=_cparams(("arbitrary",)),
        name="grouped_mlp",
    )(group_start, xs, w_gate, w_up, w_down)


COMBINE_TB = 256


def _combine_kernel(dest_ref, dest_next_ref, y_hbm, pre_ref, w_ref, mod_ref, fw_ref, op_ref, os_ref, buf, x2_scr,
                    sem):
    i = pl.program_id(0)
    n = pl.num_programs(0)
    tb = COMBINE_TB
    slot = i % 2
    blk_rows = TOP_K * tb

    def issue_block(d_ref, sl):
        def issue(t, carry):
            for k in range(TOP_K):
                pltpu.make_async_copy(_row_tile(y_hbm, d_ref[k, t]),
                                      _row_tile(buf, sl * blk_rows + k * tb + t),
                                      sem.at[sl]).start(priority=k % 2)
            return carry

        lax.fori_loop(0, tb, issue, 0)

    @pl.when(i == 0)
    def _():
        issue_block(dest_ref, 0)

    @pl.when(i + 1 < n)
    def _():
        issue_block(dest_next_ref, 1 - slot)

    whole = _flat_rows(buf, slot * blk_rows, blk_rows)
    pltpu.make_async_copy(whole, whole, sem.at[slot]).wait()

    base = pl.multiple_of(slot * (blk_rows * ROW_TILE), ROW_TILE)
    w = w_ref[...]
    m = mod_ref[...]
    ssq = jnp.zeros((tb, 1), F32)
    for s in range(ROW_TILE):
        routed = [None, None]
        for k in range(TOP_K):
            halves = _unpack_pair(_tile_column(buf, base + k * tb * ROW_TILE, tb, s))
            for j in range(2):
                term = w[:, k:k + 1] * halves[j]
                routed[j] = term if routed[j] is None else routed[j] + term
        for j in range(2):
            cols = slice(j * HALF_D + s * LANES, j * HALF_D + (s + 1) * LANES)
            x2 = pre_ref[:, cols] + m[5:6, cols] * routed[j]
            x2_scr[:, cols] = x2
            ssq = ssq + jnp.sum(x2 * x2, axis=-1, keepdims=True)
    scale = lax.rsqrt(ssq * (1.0 / D_MODEL) + EPS)

    @pl.when(i < T_PROMPT // tb)
    def _():
        op_ref[...] = x2_scr[...] * scale * fw_ref[...]

    @pl.when(i >= T_PROMPT // tb)
    def _():
        os_ref[...] = x2_scr[...] * scale * fw_ref[...]


def _combine(dest_t, y_rows, pre, w_tok, mod3, final_norm_w):
    tb = COMBINE_TB
    n_blocks = T_ALL // tb
    npb = T_PROMPT // tb
    row = functools.partial(_mod_row_of_block, blocks_prompt=npb, blocks_per_sample_seq=DEC_SEQ // tb)
    return pl.pallas_call(
        _combine_kernel,
        out_shape=(jax.ShapeDtypeStruct((T_PROMPT, D_MODEL), F32),
                   jax.ShapeDtypeStruct((T_SAMPLE, D_MODEL), F32)),
        grid=(n_blocks,),
        in_specs=[pl.BlockSpec((TOP_K, tb), lambda i: (0, i), memory_space=pltpu.SMEM),
                  pl.BlockSpec((TOP_K, tb), lambda i: (0, jnp.minimum(i + 1, n_blocks - 1)),
                               memory_space=pltpu.SMEM),
                  pl.BlockSpec(memory_space=pl.ANY),
                  pl.BlockSpec((tb, D_MODEL), lambda i: (i, 0)),
                  pl.BlockSpec((tb, TOP_K), lambda i: (i, 0)),
                  pl.BlockSpec((None, N_MOD, D_MODEL), lambda i: (row(i), 0, 0)),
                  pl.BlockSpec((1, D_MODEL), lambda i: (0, 0))],
        out_specs=_two_path_specs((tb, D_MODEL), npb),
        scratch_shapes=[pltpu.VMEM((2 * TOP_K * tb * ROW_TILE, LANES), I32),
                        pltpu.VMEM((tb, D_MODEL), F32),
                        pltpu.SemaphoreType.DMA((2,))],
        compiler_params=_cparams(("arbitrary",)),
        name="combine",
    )(dest_t, dest_t, y_rows, pre, w_tok, mod3, final_norm_w.reshape(1, D_MODEL))


def _group_starts(counts):
    return jnp.concatenate([jnp.zeros((1,), I32), jnp.cumsum(counts).astype(I32)])


def _head_expand_matrix():
    r = jnp.arange(LANES)[:, None]
    cidx = jnp.arange(2 * D_INNER)[None, :]
    direction = cidx // D_INNER
    head = (cidx % D_INNER) // HEAD_DIM
    return (r == direction * N_HEADS + head).astype(BF16)


def kernel(x_prompt, x_sample, state_ssm, c, c_ctx, norm1_w, norm2_w, w_mod, b_mod, w_in, ssm_conv_w, ssm_conv_b, ssm_dt_bias, ssm_a_log, ssm_d, ssm_norm_w, w_out_ssm, conf_conv_w, conf_conv_b, conf_ln_w, conf_ln_b, w_out_conf, w_o, router_w, router_bias, exp_w_gate, exp_w_up, exp_w_down, sh_w_gate, sh_w_up, sh_w_down, final_norm_w):
    x_p = x_prompt.reshape(T_PROMPT, D_MODEL)
    x_s = x_sample.reshape(T_SAMPLE, D_MODEL)

    cc = jnp.zeros((MOD_ROWS, D_MODEL), F32).at[:DEC_BATCH].set(c).at[CTX_ROW].set(c_ctx)
    mod3 = _modulation(cc, w_mod[0], b_mod[0]).reshape(MOD_ROWS, N_MOD, D_MODEL)

    w = w_in[0]
    o_xbc, o_dt, o_cv, o_cg, o_gl = D_INNER, D_INNER + CONV_DIM, D_INNER + CONV_DIM + 2 * N_HEADS, \
        D_INNER + CONV_DIM + 2 * N_HEADS + D_MODEL, D_INNER + CONV_DIM + 2 * N_HEADS + 2 * D_MODEL
    w_main = jnp.concatenate([w[:, o_xbc:o_dt], w[:, :o_xbc], w[:, o_gl:], w[:, o_cv:o_cg], w[:, o_cg:o_gl]],
                             axis=1).astype(BF16)
    w_dt = jnp.pad(w[:, o_dt:o_cv], ((0, 0), (0, LANES - 2 * N_HEADS))).astype(BF16)
    proj, dt_raw = _in_projection(x_p, x_s, mod3, norm1_w[0], w_main, w_dt)

    pad_heads = lambda v: jnp.pad(v.reshape(1, 2 * N_HEADS), ((0, 0), (0, LANES - 2 * N_HEADS)))
    dtb = pad_heads(ssm_dt_bias[0])
    a_neg = pad_heads(-jnp.exp(ssm_a_log[0]))
    dvec = jnp.repeat(ssm_d[0], HEAD_DIM).reshape(1, D_INNER)
    nw = ssm_norm_w[0].reshape(1, D_INNER)
    e2 = _head_expand_matrix()
    xbc_p = _ssm_conv(proj, ssm_conv_w[0], ssm_conv_b[0], seq_len=SEQ, n_seq=BATCH, row_block0=0)
    xbc_s = _ssm_conv(proj, ssm_conv_w[0], ssm_conv_b[0], seq_len=DEC_SEQ, n_seq=DEC_BATCH,
                      row_block0=T_PROMPT // DEC_SEQ)
    y_p, fin = _ssd(xbc_p, dt_raw, proj, None, dtb, a_neg, dvec, nw, e2,
                    seq_len=SEQ, n_seq=BATCH, tok0=0, want_final=True)
    init = state_ssm.reshape(DEC_BATCH, 2, D_INNER, D_STATE)
    (y_s,) = _ssd(xbc_s, dt_raw, proj, init, dtb, a_neg, dvec, nw, e2,
                  seq_len=DEC_SEQ, n_seq=DEC_BATCH, tok0=T_PROMPT, want_final=False)

    v_p = _conformer(proj, conf_conv_w[0], conf_conv_b[0], conf_ln_w[0], conf_ln_b[0],
                     seq_len=SEQ, n_seq=BATCH, row_block0=0, grid_mode=False)
    v_s = _conformer(proj, conf_conv_w[0], conf_conv_b[0], conf_ln_w[0], conf_ln_b[0],
                     seq_len=DEC_SEQ, n_seq=DEC_BATCH, row_block0=T_PROMPT // DEC_SEQ, grid_mode=True)

    sh_gu = jnp.concatenate([sh_w_gate[0], sh_w_up[0]], axis=1).astype(BF16)
    pre, v_all, scores = _merge(y_p, y_s, v_p, v_s, proj, x_p, x_s, mod3, w_out_ssm[0].astype(BF16),
                                w_out_conf[0].astype(BF16), w_o[0].astype(BF16), norm2_w[0], router_w[0],
                                sh_gu, sh_w_down[0].astype(BF16))

    idx_t, w_t, rank_t, cnt = _route(scores, router_bias[0])
    start = _group_starts(cnt[:, 0].astype(I32))
    dest_t = _dest_slots(idx_t, rank_t, start[:N_EXPERTS])

    xs = _dispatch(dest_t, v_all)
    y_rows = _grouped_mlp(start, xs, exp_w_gate[0], exp_w_up[0], exp_w_down[0])
    out_p, out_s = _combine(dest_t, y_rows, pre, w_t.T, mod3, final_norm_w)

    y_prompt = out_p.reshape(BATCH, SEQ, D_MODEL)
    y_sample = out_s.reshape(DEC_BATCH, DEC_SEQ, D_MODEL)
    new_state = fin.reshape(BATCH, 1, 2, N_HEADS, HEAD_DIM, D_STATE)
    return (y_prompt, y_sample, new_state)
```

```python
import functools

import jax
import jax.numpy as jnp
from jax import lax
from jax.experimental import pallas as pl
from jax.experimental.pallas import tpu as pltpu

F32 = jnp.float32
BF16 = jnp.bfloat16
I32 = jnp.int32

D_MODEL = 1024
BATCH = 32
SEQ = 256
DEC_BATCH = 8
DEC_SEQ = 1024
GRID_W = 64
GRID_H = DEC_SEQ // GRID_W
D_INNER = 2048
HEAD_DIM = 64
N_HEADS = 32
D_STATE = 128
N_GROUPS = 8
HEADS_PER_GROUP = N_HEADS // N_GROUPS
GROUP_W = HEADS_PER_GROUP * HEAD_DIM
D_CONV_SSM = 5
CHUNK = 128
CONV_DIM = D_INNER + 2 * N_GROUPS * D_STATE
CONF_K = 31
CONF_PAD = CONF_K // 2
N_EXPERTS = 256
TOP_K = 8
N_EXPERT_GROUPS = 8
EXPERTS_PER_GROUP = N_EXPERTS // N_EXPERT_GROUPS
TOPK_GROUPS = 4
D_EXPERT = 256
D_SHARED = 256
ROUTED_SCALE = 2.5
N_MOD = 6
EPS = 1e-6

T_PROMPT = BATCH * SEQ
T_SAMPLE = DEC_BATCH * DEC_SEQ
T_ALL = T_PROMPT + T_SAMPLE
N_ASSIGN = T_ALL * TOP_K
MOD_ROWS = 16
CTX_ROW = DEC_BATCH

SUBLANES = 8
LANES = 128
VMEM_LIMIT = 56 * 1024 * 1024

PROJ_W = CONV_DIM + D_INNER + 2 * D_MODEL + 2 * D_MODEL
COL_Z = CONV_DIM // D_INNER
COL_GL = COL_Z + 1
COL_CV = (CONV_DIM + 2 * D_INNER) // D_MODEL
COL_CG = COL_CV + 1

HALF_D = D_MODEL // 2
ROW_TILE = HALF_D // LANES
GMM_TM = 256
N_ROWS = N_ASSIGN + GMM_TM
TAIL_SIZES = (128, 64, 32, 16, 8, 4, 2, 1)


def _cparams(sem, vmem=VMEM_LIMIT):
    return pltpu.CompilerParams(dimension_semantics=sem, vmem_limit_bytes=vmem)


def _silu(x):
    return x * jax.nn.sigmoid(x)


def _split2(x):
    hi = x.astype(BF16)
    lo = (x - hi.astype(F32)).astype(BF16)
    return hi, lo


def _split3(x):
    b1 = x.astype(BF16)
    r = x - b1.astype(F32)
    b2 = r.astype(BF16)
    b3 = (r - b2.astype(F32)).astype(BF16)
    return b1, b2, b3


def _dot(a, b):
    return jnp.dot(a, b, preferred_element_type=F32)


def _dot_exact_lhs(a_exact, b):
    b1, b2, b3 = _split3(b)
    return _dot(a_exact, b1) + _dot(a_exact, b2) + _dot(a_exact, b3)


def _dot_f32(a, b):
    a1, a2 = _split2(a)
    b1, b2 = _split2(b)
    return _dot(a1, b1) + _dot(a1, b2) + _dot(a2, b1)


def _mod_row_of_block(i, blocks_prompt, blocks_per_sample_seq):
    return jnp.where(i < blocks_prompt, CTX_ROW, (i - blocks_prompt) // blocks_per_sample_seq)


def _mod_kernel(c_ref, w_ref, b_ref, o_ref):
    c = c_ref[...]
    o_ref[...] = _dot_f32(_silu(c), w_ref[...]) + b_ref[...]


def _modulation(cc, w_mod, b_mod):
    tn = 512
    n = N_MOD * D_MODEL
    return pl.pallas_call(
        _mod_kernel,
        out_shape=jax.ShapeDtypeStruct((MOD_ROWS, n), F32),
        grid=(n // tn,),
        in_specs=[pl.BlockSpec((MOD_ROWS, D_MODEL), lambda j: (0, 0)),
                  pl.BlockSpec((D_MODEL, tn), lambda j: (0, j)),
                  pl.BlockSpec((1, tn), lambda j: (0, j))],
        out_specs=pl.BlockSpec((MOD_ROWS, tn), lambda j: (0, j)),
        compiler_params=_cparams(("arbitrary",)),
        name="modulation",
    )(cc, w_mod, b_mod.reshape(1, n))


INPROJ_TM = 1024
INPROJ_TN = 1024


def _two_path_specs(block, n_prompt_blocks):
    last = n_prompt_blocks - 1
    return (pl.BlockSpec(block, lambda i, *_: (jnp.minimum(i, last), 0)),
            pl.BlockSpec(block, lambda i, *_: (jnp.maximum(i - n_prompt_blocks, 0), 0)))


def _pick_path(n_prompt_blocks, prompt_ref, sample_ref):
    return jnp.where(pl.program_id(0) < n_prompt_blocks, prompt_ref[...], sample_ref[...])


INPROJ_NJ = PROJ_W // INPROJ_TN
INPROJ_CONV_TILES = CONV_DIM // INPROJ_TN
INPROJ_CT = 256
INPROJ_RT = 64
INPROJ_HALO = SUBLANES


def _ssm_conv_silu(acc_ref, w_ref, b_ref, o_ref, seq_len):
    rows, cols = o_ref.shape
    half = D_CONV_SSM // 2
    chunks_per_seq = SEQ // INPROJ_RT
    for c0 in range(0, cols, INPROJ_CT):
        cs = slice(c0, c0 + INPROJ_CT)
        wk = [w_ref[k:k + 1, cs] for k in range(D_CONV_SSM)]
        bias = b_ref[:, cs]
        for ci, r0 in enumerate(range(0, rows, INPROJ_RT)):
            edge = ci % chunks_per_seq in (0, chunks_per_seq - 1)
            if edge:
                pos = jnp.bitwise_and(r0 + lax.broadcasted_iota(I32, (INPROJ_RT, INPROJ_CT), 0), seq_len - 1)
            acc = bias + wk[half] * acc_ref[INPROJ_HALO + r0:INPROJ_HALO + r0 + INPROJ_RT, cs]
            for k in range(D_CONV_SSM):
                s = k - half
                if s == 0:
                    continue
                x = acc_ref[INPROJ_HALO + r0 + s:INPROJ_HALO + r0 + s + INPROJ_RT, cs]
                if edge:
                    x = jnp.where(jnp.logical_and(pos + s >= 0, pos + s < seq_len), x, 0.0)
                acc = acc + wk[k] * x
            o_ref[r0:r0 + INPROJ_RT, cs] = _silu(acc).astype(BF16)


def _inproj_kernel(xp_ref, xs_ref, mod_ref, n1_ref, w_ref, wdt_ref, cw_ref, cb_ref, o_ref, dt_ref, u_scr, acc_scr):
    i = pl.program_id(0)
    j = pl.program_id(1)
    seq_len = jnp.where(i < T_PROMPT // INPROJ_TM, SEQ, DEC_SEQ)

    @pl.when(j == 0)
    def _():
        x = _pick_path(T_PROMPT // INPROJ_TM, xp_ref, xs_ref)
        ms = jnp.mean(x * x, axis=-1, keepdims=True)
        y = x * lax.rsqrt(ms + EPS) * n1_ref[...]
        m = mod_ref[...]
        u = (y * (1.0 + m[1:2]) + m[0:1]).astype(BF16)
        u_scr[...] = u
        dt_ref[...] = _dot(u, wdt_ref[...])
        for slot in range(2):
            acc_scr[slot, 0:INPROJ_HALO, :] = jnp.zeros((INPROJ_HALO, INPROJ_TN), F32)
            acc_scr[slot, INPROJ_HALO + INPROJ_TM:, :] = jnp.zeros((INPROJ_HALO, INPROJ_TN), F32)
        acc_scr[0, INPROJ_HALO:INPROJ_HALO + INPROJ_TM, :] = _dot(u, w_ref[...])

    body = slice(INPROJ_HALO, INPROJ_HALO + INPROJ_TM)
    for parity in range(2):
        cur, prev = acc_scr.at[parity], acc_scr.at[1 - parity]
        mine = jnp.logical_and(j >= 1, j % 2 == parity)

        @pl.when(jnp.logical_and(mine, j <= INPROJ_CONV_TILES))
        def _():
            cur[body, :] = _dot(u_scr[...], w_ref[...])
            _ssm_conv_silu(prev, cw_ref, cb_ref, o_ref, seq_len)

        @pl.when(jnp.logical_and(mine, jnp.logical_and(j > INPROJ_CONV_TILES, j < INPROJ_NJ)))
        def _():
            cur[body, :] = _dot(u_scr[...], w_ref[...])
            o_ref[...] = prev[body, :].astype(BF16)

        @pl.when(jnp.logical_and(mine, j == INPROJ_NJ))
        def _():
            o_ref[...] = prev[body, :].astype(BF16)


def _in_projection(x_p, x_s, mod3, norm1_w, w_main, w_dt, conv_w, conv_b):
    tm, tn = INPROJ_TM, INPROJ_TN
    row = functools.partial(_mod_row_of_block, blocks_prompt=T_PROMPT // tm,
                            blocks_per_sample_seq=DEC_SEQ // tm)
    w_tile = lambda i, j: (0, jnp.minimum(j, INPROJ_NJ - 1))
    conv_tile = lambda i, j: (0, jnp.clip(j - 1, 0, INPROJ_CONV_TILES - 1))
    return pl.pallas_call(
        _inproj_kernel,
        out_shape=(jax.ShapeDtypeStruct((T_ALL, PROJ_W), BF16),
                   jax.ShapeDtypeStruct((T_ALL, LANES), F32)),
        grid=(T_ALL // tm, INPROJ_NJ + 1),
        in_specs=[*_two_path_specs((tm, D_MODEL), T_PROMPT // tm),
                  pl.BlockSpec((None, N_MOD, D_MODEL), lambda i, j: (row(i), 0, 0)),
                  pl.BlockSpec((1, D_MODEL), lambda i, j: (0, 0)),
                  pl.BlockSpec((D_MODEL, tn), w_tile),
                  pl.BlockSpec((D_MODEL, LANES), lambda i, j: (0, 0)),
                  pl.BlockSpec((D_CONV_SSM, tn), conv_tile),
                  pl.BlockSpec((1, tn), conv_tile)],
        out_specs=(pl.BlockSpec((tm, tn), lambda i, j: (i, jnp.maximum(j - 1, 0))),
                   pl.BlockSpec((tm, LANES), lambda i, j: (i, 0))),
        scratch_shapes=[pltpu.VMEM((tm, D_MODEL), BF16),
                        pltpu.VMEM((2, tm + 2 * INPROJ_HALO, tn), F32)],
        compiler_params=_cparams(("arbitrary", "arbitrary")),
        name="in_projection",
    )(x_p, x_s, mod3, norm1_w.reshape(1, D_MODEL), w_main, w_dt, conv_w, conv_b.reshape(1, CONV_DIM))


def _tri_masks():
    ii = lax.broadcasted_iota(I32, (CHUNK, CHUNK), 0)
    jj = lax.broadcasted_iota(I32, (CHUNK, CHUNK), 1)
    return ii, jj


def _chunk_decays(dt_ref, dtb_ref, a_ref):
    ii, jj = _tri_masks()
    pre = dt_ref[...] + dtb_ref[...]
    dt = jnp.maximum(pre, 0.0) + jnp.log(1.0 + jnp.exp(-jnp.abs(pre)))
    la = dt * a_ref[...]
    tri_lo = jnp.where(jj <= ii, 1.0, 0.0).astype(BF16)
    tri_up = jnp.where(jj >= ii, 1.0, 0.0).astype(BF16)
    cs_prefix = _dot_exact_lhs(tri_lo, la)
    cs_suffix = _dot_exact_lhs(tri_up, la)
    fwd_lane = lax.broadcasted_iota(I32, (CHUNK, LANES), 1) < N_HEADS
    cs = jnp.where(fwd_lane, cs_prefix, cs_suffix)
    tot = jnp.where(fwd_lane[0:1], cs_prefix[CHUNK - 1:CHUNK, :], cs_suffix[0:1, :])
    return dt, cs, tot


def _transpose_blocks(src, rows, cols):
    out_rows = []
    for cb in range(cols // LANES):
        pieces = [src[rb * LANES:(rb + 1) * LANES, cb * LANES:(cb + 1) * LANES].T
                  for rb in range(rows // LANES)]
        out_rows.append(jnp.concatenate(pieces, axis=1) if len(pieces) > 1 else pieces[0])
    return jnp.concatenate(out_rows, axis=0) if len(out_rows) > 1 else out_rows[0]


def _ssd_kernel(*refs, n_chunks, has_init, want_final):
    it = iter(refs)
    xbc_ref, dt_ref, z_ref = next(it), next(it), next(it)
    init_ref = next(it) if has_init else None
    dtb_ref, a_ref, dvec_ref, nw_ref, e2_ref = next(it), next(it), next(it), next(it), next(it)
    out_ref = next(it)
    fin_ref = next(it) if want_final else None
    sf_scr, sb_scr, df_scr, db_scr = next(it), next(it), next(it), next(it)

    phase = pl.program_id(1)
    c = pl.program_id(2)

    @pl.when(phase == 0)
    def _chunk_states():
        dt, cs, tot = _chunk_decays(dt_ref, dtb_ref, a_ref)
        w_in = dt * jnp.exp(tot - cs)
        pack = 2 * SUBLANES
        dec_hi, dec_lo = _split2(jnp.exp(jnp.broadcast_to(tot, (pack, LANES))))
        expanded = _dot(jnp.concatenate([w_in.astype(BF16), dec_hi, dec_lo], axis=0), e2_ref[...])
        w_exp = expanded[:CHUNK]
        dec_exp = expanded[CHUNK:CHUNK + SUBLANES] + expanded[CHUNK + pack:CHUNK + pack + SUBLANES]
        df_scr[c] = dec_exp[:, :D_INNER]
        db_scr[c] = dec_exp[:, D_INNER:]
        for g in range(N_GROUPS):
            lo = g * GROUP_W
            xg = xbc_ref[:, lo:lo + GROUP_W].astype(F32)
            xd_f = (xg * w_exp[:, lo:lo + GROUP_W]).astype(BF16)
            xd_b = (xg * w_exp[:, D_INNER + lo:D_INNER + lo + GROUP_W]).astype(BF16)
            bg = xbc_ref[:, D_INNER + g * D_STATE:D_INNER + (g + 1) * D_STATE]
            bg_t = bg.astype(F32).T.astype(BF16)
            sf_scr[c, :, lo:lo + GROUP_W] = _dot(bg_t, xd_f)
            sb_scr[c, :, lo:lo + GROUP_W] = _dot(bg_t, xd_b)

    @pl.when(jnp.logical_and(phase == 1, c == 0))
    def _recurrence():
        for g in range(N_GROUPS):
            lo = g * GROUP_W
            if has_init:
                prev_f = _transpose_blocks(init_ref[0, 0, lo:lo + GROUP_W, :], GROUP_W, D_STATE)
                prev_b = _transpose_blocks(init_ref[0, 1, lo:lo + GROUP_W, :], GROUP_W, D_STATE)
            else:
                prev_f = jnp.zeros((D_STATE, GROUP_W), F32)
                prev_b = jnp.zeros((D_STATE, GROUP_W), F32)
            for cc in range(n_chunks):
                s = sf_scr[cc, :, lo:lo + GROUP_W]
                sf_scr[cc, :, lo:lo + GROUP_W] = prev_f
                prev_f = df_scr[cc, 0:1, lo:lo + GROUP_W] * prev_f + s
            for cc in reversed(range(n_chunks)):
                s = sb_scr[cc, :, lo:lo + GROUP_W]
                sb_scr[cc, :, lo:lo + GROUP_W] = prev_b
                prev_b = db_scr[cc, 0:1, lo:lo + GROUP_W] * prev_b + s
            if want_final:
                fin_ref[0, 0, lo:lo + GROUP_W, :] = _transpose_blocks(prev_f, D_STATE, GROUP_W)
                fin_ref[0, 1, lo:lo + GROUP_W, :] = _transpose_blocks(prev_b, D_STATE, GROUP_W)

    @pl.when(phase == 1)
    def _outputs():
        ii, jj = _tri_masks()
        dt, cs, _ = _chunk_decays(dt_ref, dtb_ref, a_ref)
        out_dec = _dot(jnp.exp(cs).astype(BF16), e2_ref[...])
        cs_t = cs.T
        dt_t = dt.T
        causal = ii >= jj
        anti = jj >= ii
        neg = jnp.float32(-1e30)
        left = lax.broadcasted_iota(I32, (CHUNK, LANES), 1) < HEAD_DIM
        for g in range(N_GROUPS):
            lo = g * GROUP_W
            bg = xbc_ref[:, D_INNER + g * D_STATE:D_INNER + (g + 1) * D_STATE]
            cg = xbc_ref[:, D_INNER + N_GROUPS * D_STATE + g * D_STATE:
                         D_INNER + N_GROUPS * D_STATE + (g + 1) * D_STATE]
            cb = lax.dot_general(cg, bg, (((1,), (1,)), ((), ())), preferred_element_type=F32)
            pf = sf_scr[c, :, lo:lo + GROUP_W].astype(BF16)
            pb = sb_scr[c, :, lo:lo + GROUP_W].astype(BF16)
            y_off = (_dot(cg, pf) * out_dec[:, lo:lo + GROUP_W]
                     + _dot(cg, pb) * out_dec[:, D_INNER + lo:D_INNER + lo + GROUP_W])
            pairs = []
            for m in range(HEADS_PER_GROUP // 2):
                x_pair = xbc_ref[:, lo + m * LANES:lo + (m + 1) * LANES]
                ys = []
                for hh in range(2):
                    h = g * HEADS_PER_GROUP + 2 * m + hh
                    hb = N_HEADS + h
                    seg_f = jnp.where(causal, cs[:, h:h + 1] - cs_t[h:h + 1, :], neg)
                    seg_b = jnp.where(anti, cs[:, hb:hb + 1] - cs_t[hb:hb + 1, :], neg)
                    mix = (jnp.exp(seg_f) * dt_t[h:h + 1, :] + jnp.exp(seg_b) * dt_t[hb:hb + 1, :])
                    ys.append(_dot((cb * mix).astype(BF16), x_pair))
                pairs.append(jnp.where(left, ys[0], ys[1]))
            y_diag = jnp.concatenate(pairs, axis=1)
            xg = xbc_ref[:, lo:lo + GROUP_W].astype(F32)
            y = y_diag + y_off + dvec_ref[:, lo:lo + GROUP_W] * xg
            zg = z_ref[:, lo:lo + GROUP_W].astype(F32)
            y = y * _silu(zg)
            ms = jnp.mean(y * y, axis=-1, keepdims=True)
            out_ref[:, lo:lo + GROUP_W] = (y * lax.rsqrt(ms + EPS) * nw_ref[:, lo:lo + GROUP_W]).astype(BF16)


def _ssd(dt_raw, proj, init, dtb, a_neg, dvec, norm_w, e2, *, seq_len, n_seq, tok0, want_final):
    nc = seq_len // CHUNK
    blk0 = tok0 // CHUNK
    has_init = init is not None
    in_specs = [pl.BlockSpec((CHUNK, CONV_DIM), lambda b, p, c: (blk0 + b * nc + c, 0)),
                pl.BlockSpec((CHUNK, LANES), lambda b, p, c: (blk0 + b * nc + c, 0)),
                pl.BlockSpec((CHUNK, D_INNER), lambda b, p, c: (blk0 + b * nc + c * p, COL_Z))]
    args = [proj, dt_raw, proj]
    if has_init:
        in_specs.append(pl.BlockSpec((1, 2, D_INNER, D_STATE), lambda b, p, c: (b, 0, 0, 0)))
        args.append(init)
    const = lambda b, p, c: (0, 0)
    in_specs += [pl.BlockSpec((1, LANES), const), pl.BlockSpec((1, LANES), const),
                 pl.BlockSpec((1, D_INNER), const), pl.BlockSpec((1, D_INNER), const),
                 pl.BlockSpec((LANES, 2 * D_INNER), const)]
    args += [dtb, a_neg, dvec, norm_w, e2]
    out_shape = [jax.ShapeDtypeStruct((n_seq * seq_len, D_INNER), BF16)]
    out_specs = [pl.BlockSpec((CHUNK, D_INNER), lambda b, p, c: (b * nc + c * p, 0))]
    if want_final:
        out_shape.append(jax.ShapeDtypeStruct((n_seq, 2, D_INNER, D_STATE), F32))
        out_specs.append(pl.BlockSpec((1, 2, D_INNER, D_STATE), lambda b, p, c: (b, 0, 0, 0)))
    res = pl.pallas_call(
        functools.partial(_ssd_kernel, n_chunks=nc, has_init=has_init, want_final=want_final),
        out_shape=tuple(out_shape),
        grid=(n_seq, 2, nc),
        in_specs=in_specs,
        out_specs=tuple(out_specs),
        scratch_shapes=[pltpu.VMEM((nc, D_STATE, D_INNER), F32),
                        pltpu.VMEM((nc, D_STATE, D_INNER), F32),
                        pltpu.VMEM((nc, SUBLANES, D_INNER), F32),
                        pltpu.VMEM((nc, SUBLANES, D_INNER), F32)],
        compiler_params=_cparams(("arbitrary", "arbitrary", "arbitrary")),
        name="ssd_final" if want_final else "ssd_init",
    )(*args)
    return res


def _layernorm_silu(v, lw_ref, lb_ref):
    mu = jnp.mean(v, axis=-1, keepdims=True)
    d = v - mu
    var = jnp.mean(d * d, axis=-1, keepdims=True)
    return _silu(d * lax.rsqrt(var + EPS) * lw_ref[...] + lb_ref[...])


def _conf_seq_kernel(cv_ref, cg_ref, w_ref, b_ref, lw_ref, lb_ref, o_ref, pad_scr, acc_scr):
    seq_len = cv_ref.shape[0]
    base = 2 * SUBLANES
    v = cv_ref[...].astype(F32) * jax.nn.sigmoid(cg_ref[...].astype(F32))
    pad_scr[0:base, :] = jnp.zeros((base, D_MODEL), F32)
    pad_scr[base + seq_len:base + seq_len + base, :] = jnp.zeros((base, D_MODEL), F32)
    pad_scr[base:base + seq_len, :] = v
    rt, ct = 128, 256
    for r0 in range(0, seq_len, rt):
        for c0 in range(0, D_MODEL, ct):
            acc = jnp.broadcast_to(b_ref[:, c0:c0 + ct], (rt, ct))
            for k in range(CONF_K):
                off = base + r0 + k - CONF_PAD
                acc = acc + w_ref[k:k + 1, c0:c0 + ct] * pad_scr[off:off + rt, c0:c0 + ct]
            acc_scr[r0:r0 + rt, c0:c0 + ct] = acc
    o_ref[...] = _layernorm_silu(acc_scr[...], lw_ref, lb_ref).astype(BF16)


def _conf_grid_kernel(cv_ref, cg_ref, w_ref, b_ref, lw_ref, lb_ref, o_ref, v_scr, pad_scr, acc_scr):
    half = D_MODEL // 2
    base = 2 * SUBLANES
    stride = GRID_W + 2 * base
    v_scr[...] = cv_ref[...].astype(F32) * jax.nn.sigmoid(cg_ref[...].astype(F32))
    pad_scr[...] = jnp.zeros(pad_scr.shape, F32)
    for r in range(GRID_H):
        pad_scr[r * stride + base:r * stride + base + GRID_W, :] = v_scr[r * GRID_W:(r + 1) * GRID_W, 0:half]
    ct = 256
    for r in range(GRID_H):
        for c0 in range(0, half, ct):
            acc = jnp.broadcast_to(b_ref[:, c0:c0 + ct], (GRID_W, ct))
            for k in range(CONF_K):
                off = r * stride + base + k - CONF_PAD
                acc = acc + w_ref[k:k + 1, c0:c0 + ct] * pad_scr[off:off + GRID_W, c0:c0 + ct]
            acc_scr[r * GRID_W:(r + 1) * GRID_W, c0:c0 + ct] = acc
        for c0 in range(half, D_MODEL, ct):
            acc = jnp.broadcast_to(b_ref[:, c0:c0 + ct], (GRID_W, ct))
            for r2 in range(GRID_H):
                k = r2 - r + CONF_PAD
                acc = acc + w_ref[k:k + 1, c0:c0 + ct] * v_scr[r2 * GRID_W:(r2 + 1) * GRID_W, c0:c0 + ct]
            acc_scr[r * GRID_W:(r + 1) * GRID_W, c0:c0 + ct] = acc
    o_ref[...] = _layernorm_silu(acc_scr[...], lw_ref, lb_ref).astype(BF16)


def _conformer(proj, conv_w, conv_b, ln_w, ln_b, *, seq_len, n_seq, row_block0, grid_mode):
    base = 2 * SUBLANES
    if grid_mode:
        body = _conf_grid_kernel
        scratch = [pltpu.VMEM((seq_len, D_MODEL), F32),
                   pltpu.VMEM((GRID_H * (GRID_W + 2 * base), D_MODEL // 2), F32),
                   pltpu.VMEM((seq_len, D_MODEL), F32)]
    else:
        body = _conf_seq_kernel
        scratch = [pltpu.VMEM((seq_len + 2 * base, D_MODEL), F32),
                   pltpu.VMEM((seq_len, D_MODEL), F32)]
    const = lambda b: (0, 0)
    return pl.pallas_call(
        body,
        out_shape=jax.ShapeDtypeStruct((n_seq * seq_len, D_MODEL), BF16),
        grid=(n_seq,),
        in_specs=[pl.BlockSpec((seq_len, D_MODEL), lambda b: (row_block0 + b, COL_CV)),
                  pl.BlockSpec((seq_len, D_MODEL), lambda b: (row_block0 + b, COL_CG)),
                  pl.BlockSpec((CONF_K, D_MODEL), const),
                  pl.BlockSpec((1, D_MODEL), const),
                  pl.BlockSpec((1, D_MODEL), const),
                  pl.BlockSpec((1, D_MODEL), const)],
        out_specs=pl.BlockSpec((seq_len, D_MODEL), lambda b: (b, 0)),
        scratch_shapes=scratch,
        compiler_params=_cparams(("arbitrary",)),
        name="conformer_grid" if grid_mode else "conformer_seq",
    )(proj, proj, conv_w, conv_b.reshape(1, D_MODEL), ln_w.reshape(1, D_MODEL), ln_b.reshape(1, D_MODEL))


MERGE_TM = 256


def _merge_kernel(yap_ref, yas_ref, ybp_ref, ybs_ref, gl_ref, xp_ref, xs_ref, mod_ref, wa_ref, wb_ref, wo_ref,
                  n2_ref, rw_ref, sgu_ref, sd_ref, pre_ref, v_ref, s_ref):
    npb = T_PROMPT // MERGE_TM
    y_a = _dot(_pick_path(npb, yap_ref, yas_ref), wa_ref[...])
    y_b = _dot(_pick_path(npb, ybp_ref, ybs_ref), wb_ref[...])
    gates = jax.nn.sigmoid(gl_ref[...].astype(F32))
    mix = gates[:, :D_MODEL] * y_a + gates[:, D_MODEL:] * y_b
    out = _dot(mix.astype(BF16), wo_ref[...])
    m = mod_ref[...]
    x1 = _pick_path(npb, xp_ref, xs_ref) + m[2:3] * out
    ms = jnp.mean(x1 * x1, axis=-1, keepdims=True)
    v = x1 * lax.rsqrt(ms + EPS) * n2_ref[...] * (1.0 + m[4:5]) + m[3:4]
    v_ref[...] = v
    s_ref[...] = jax.nn.sigmoid(_dot_f32(v, rw_ref[...]))
    hgu = _dot(v.astype(BF16), sgu_ref[...])
    act = _silu(hgu[:, :D_SHARED]) * hgu[:, D_SHARED:]
    shared = _dot(act.astype(BF16), sd_ref[...])
    pre_ref[...] = x1 + m[5:6] * shared


def _merge(y_ssm_p, y_ssm_s, v_conf_p, v_conf_s, proj, x_p, x_s, mod3, w_out_ssm, w_out_conf, w_o, norm2_w,
           router_w, sh_gu, sh_d):
    tm = MERGE_TM
    npb = T_PROMPT // tm
    row = functools.partial(_mod_row_of_block, blocks_prompt=npb, blocks_per_sample_seq=DEC_SEQ // tm)
    const = lambda i: (0, 0)
    return pl.pallas_call(
        _merge_kernel,
        out_shape=(jax.ShapeDtypeStruct((T_ALL, D_MODEL), F32),
                   jax.ShapeDtypeStruct((T_ALL, D_MODEL), F32),
                   jax.ShapeDtypeStruct((T_ALL, N_EXPERTS), F32)),
        grid=(T_ALL // tm,),
        in_specs=[*_two_path_specs((tm, D_INNER), npb),
                  *_two_path_specs((tm, D_MODEL), npb),
                  pl.BlockSpec((tm, 2 * D_MODEL), lambda i: (i, COL_GL)),
                  *_two_path_specs((tm, D_MODEL), npb),
                  pl.BlockSpec((None, N_MOD, D_MODEL), lambda i: (row(i), 0, 0)),
                  pl.BlockSpec((D_INNER, D_MODEL), const),
                  pl.BlockSpec((D_MODEL, D_MODEL), const),
                  pl.BlockSpec((D_MODEL, D_MODEL), const),
                  pl.BlockSpec((1, D_MODEL), const),
                  pl.BlockSpec((D_MODEL, N_EXPERTS), const),
                  pl.BlockSpec((D_MODEL, 2 * D_SHARED), const),
                  pl.BlockSpec((D_SHARED, D_MODEL), const)],
        out_specs=(pl.BlockSpec((tm, D_MODEL), lambda i: (i, 0)),
                   pl.BlockSpec((tm, D_MODEL), lambda i: (i, 0)),
                   pl.BlockSpec((tm, N_EXPERTS), lambda i: (i, 0))),
        compiler_params=_cparams(("arbitrary",)),
        name="merge",
    )(y_ssm_p, y_ssm_s, v_conf_p, v_conf_s, proj, x_p, x_s, mod3, w_out_ssm, w_out_conf, w_o,
      norm2_w.reshape(1, D_MODEL), router_w, sh_gu, sh_d)


ROUTE_TB = 256


def _first_index_of_max(vals, iota, sentinel):
    m = jnp.max(vals, axis=0, keepdims=True)
    idx = jnp.min(jnp.where(vals == m, iota, jnp.float32(sentinel)), axis=0, keepdims=True)
    return m, idx


def _route_kernel(s_ref, bias_ref, idx_ref, w_ref, rank_ref, cnt_ref, run_scr):
    i = pl.program_id(0)
    tb = ROUTE_TB
    neg = jnp.float32(-jnp.inf)

    @pl.when(i == 0)
    def _():
        run_scr[...] = jnp.zeros(run_scr.shape, F32)

    s_t = s_ref[...].T
    sb_t = s_t + bias_ref[...]
    eiota = lax.broadcasted_iota(I32, (N_EXPERTS, tb), 0).astype(F32)

    liota = lax.broadcasted_iota(I32, (EXPERTS_PER_GROUP, tb), 0).astype(F32)
    gscores = []
    for g in range(N_EXPERT_GROUPS):
        blk = sb_t[g * EXPERTS_PER_GROUP:(g + 1) * EXPERTS_PER_GROUP, :]
        m1, i1 = _first_index_of_max(blk, liota, EXPERTS_PER_GROUP)
        m2 = jnp.max(jnp.where(liota == i1, neg, blk), axis=0, keepdims=True)
        gscores.append(m1 + m2)
    gs = jnp.concatenate(gscores, axis=0)
    giota = lax.broadcasted_iota(I32, (N_EXPERT_GROUPS, tb), 0).astype(F32)
    gsel = jnp.zeros((N_EXPERT_GROUPS, tb), F32)
    for _ in range(TOPK_GROUPS):
        _, gi = _first_index_of_max(gs, giota, N_EXPERT_GROUPS)
        hit = giota == gi
        gsel = jnp.where(hit, 1.0, gsel)
        gs = jnp.where(hit, neg, gs)
    emask = jnp.concatenate(
        [jnp.broadcast_to(gsel[g:g + 1, :], (EXPERTS_PER_GROUP, tb)) for g in range(N_EXPERT_GROUPS)], axis=0)
    masked = jnp.where(emask > 0.5, sb_t, neg)

    onehots, idxs, wts = [], [], []
    for _ in range(TOP_K):
        _, ei = _first_index_of_max(masked, eiota, N_EXPERTS)
        hit = eiota == ei
        onehots.append(hit)
        idxs.append(ei)
        wts.append(jnp.sum(jnp.where(hit, s_t, 0.0), axis=0, keepdims=True))
        masked = jnp.where(hit, neg, masked)
    w = jnp.concatenate(wts, axis=0)
    w_ref[...] = w / jnp.sum(w, axis=0, keepdims=True) * ROUTED_SCALE
    idx_ref[...] = jnp.concatenate(idxs, axis=0).astype(I32)

    assign = jnp.zeros((N_EXPERTS, tb), F32)
    for hit in onehots:
        assign = jnp.where(hit, 1.0, assign)
    assign_b = assign.astype(BF16)
    ti = lax.broadcasted_iota(I32, (tb, tb), 0)
    tj = lax.broadcasted_iota(I32, (tb, tb), 1)
    before = jnp.where(ti < tj, 1.0, 0.0).astype(BF16)
    within = _dot(assign_b, before)
    run = run_scr[...]
    total = within + jnp.concatenate([run] * (tb // LANES), axis=1)
    rank_ref[...] = jnp.concatenate(
        [jnp.sum(jnp.where(hit, total, 0.0), axis=0, keepdims=True) for hit in onehots], axis=0).astype(I32)
    new_run = run + _dot(assign_b, jnp.ones((tb, LANES), BF16))
    run_scr[...] = new_run
    cnt_ref[...] = new_run


def _route(scores, router_bias):
    tb = ROUTE_TB
    return pl.pallas_call(
        _route_kernel,
        out_shape=(jax.ShapeDtypeStruct((TOP_K, T_ALL), I32),
                   jax.ShapeDtypeStruct((TOP_K, T_ALL), F32),
                   jax.ShapeDtypeStruct((TOP_K, T_ALL), I32),
                   jax.ShapeDtypeStruct((N_EXPERTS, LANES), F32)),
        grid=(T_ALL // tb,),
        in_specs=[pl.BlockSpec((tb, N_EXPERTS), lambda i: (i, 0)),
                  pl.BlockSpec((N_EXPERTS, 1), lambda i: (0, 0))],
        out_specs=(pl.BlockSpec((TOP_K, tb), lambda i: (0, i)),
                   pl.BlockSpec((TOP_K, tb), lambda i: (0, i)),
                   pl.BlockSpec((TOP_K, tb), lambda i: (0, i)),
                   pl.BlockSpec((N_EXPERTS, LANES), lambda i: (0, 0))),
        scratch_shapes=[pltpu.VMEM((N_EXPERTS, LANES), F32)],
        compiler_params=_cparams(("arbitrary",)),
        name="route",
    )(scores, router_bias.reshape(N_EXPERTS, 1))


def _dest_kernel(idx_ref, rank_ref, start_ref, dest_ref):
    tb = idx_ref.shape[1]
    eiota = lax.broadcasted_iota(I32, (N_EXPERTS, tb), 0)
    start = jnp.broadcast_to(start_ref[...], (N_EXPERTS, tb))
    idx = idx_ref[...]
    rows = [jnp.sum(jnp.where(eiota == idx[k:k + 1, :], start, 0.0), axis=0, keepdims=True)
            for k in range(TOP_K)]
    dest_ref[...] = jnp.concatenate(rows, axis=0).astype(I32) + rank_ref[...]


def _dest_slots(idx_t, rank_t, group_start):
    tb = 512
    return pl.pallas_call(
        _dest_kernel,
        out_shape=jax.ShapeDtypeStruct((TOP_K, T_ALL), I32),
        grid=(T_ALL // tb,),
        in_specs=[pl.BlockSpec((TOP_K, tb), lambda i: (0, i)),
                  pl.BlockSpec((TOP_K, tb), lambda i: (0, i)),
                  pl.BlockSpec((N_EXPERTS, 1), lambda i: (0, 0))],
        out_specs=pl.BlockSpec((TOP_K, tb), lambda i: (0, i)),
        compiler_params=_cparams(("arbitrary",)),
        name="dest_slots",
    )(idx_t, rank_t, group_start.astype(F32).reshape(N_EXPERTS, 1))


DISPATCH_TB = 256


HI_MASK = -65536


def _pack_pair(hi, lo):
    hb = pltpu.bitcast(hi.astype(BF16).astype(F32), I32)
    lb = pltpu.bitcast(lo.astype(BF16).astype(F32), I32)
    return jnp.bitwise_or(hb, lax.shift_right_logical(lb, 16))


def _unpack_pair(word):
    hi = pltpu.bitcast(jnp.bitwise_and(word, HI_MASK), F32)
    lo = pltpu.bitcast(lax.shift_left(word, 16), F32)
    return hi, lo


def _rows_to_tiles(dst_scr, base, rows):
    n = rows.shape[0]
    for s in range(ROW_TILE):
        dst_scr[pl.ds(base + s, n, stride=ROW_TILE), :] = _pack_pair(
            rows[:, s * LANES:(s + 1) * LANES], rows[:, HALF_D + s * LANES:HALF_D + (s + 1) * LANES])


def _tile_column(src_scr, base, n, s):
    return src_scr[pl.ds(base + s, n, stride=ROW_TILE), :]


def _tiles_to_rows(src_scr, base, n, dtype):
    pairs = [_unpack_pair(_tile_column(src_scr, base, n, s)) for s in range(ROW_TILE)]
    return jnp.concatenate([p[0].astype(dtype) for p in pairs] + [p[1].astype(dtype) for p in pairs], axis=1)


def _row_tile(ref, row):
    return ref.at[pl.ds(pl.multiple_of(row * ROW_TILE, ROW_TILE), ROW_TILE), :]


def _dispatch_kernel(dest_ref, v_ref, xs_hbm, tile_scr, zero_scr, sem, zsem):
    i = pl.program_id(0)
    n = pl.num_programs(0)
    tb = DISPATCH_TB
    slot = i % 2
    base = pl.multiple_of(slot * (tb * ROW_TILE), ROW_TILE)
    _rows_to_tiles(tile_scr, base, v_ref[...])

    def issue(t, carry):
        src = _row_tile(tile_scr, slot * tb + t)
        for k in range(TOP_K):
            pltpu.make_async_copy(src, _row_tile(xs_hbm, dest_ref[k, t]), sem.at[slot]).start(priority=k % 2)
        return carry

    lax.fori_loop(0, tb, issue, 0)

    def wait_block(sl):
        blk = tile_scr.at[pl.ds(pl.multiple_of(sl * (tb * ROW_TILE), ROW_TILE), tb * ROW_TILE), :]
        for _ in range(TOP_K):
            pltpu.make_async_copy(blk, blk, sem.at[sl]).wait()

    @pl.when(i == 0)
    def _():
        zero_scr[...] = jnp.zeros(zero_scr.shape, I32)
        for j in range(GMM_TM // SUBLANES):
            cp = pltpu.make_async_copy(
                zero_scr, xs_hbm.at[pl.ds((N_ASSIGN + j * SUBLANES) * ROW_TILE, SUBLANES * ROW_TILE), :], zsem)
            cp.start()
            cp.wait()

    @pl.when(i > 0)
    def _():
        wait_block(1 - slot)

    @pl.when(i == n - 1)
    def _():
        wait_block(slot)


def _dispatch(dest_t, v_all):
    tb = DISPATCH_TB
    return pl.pallas_call(
        _dispatch_kernel,
        out_shape=jax.ShapeDtypeStruct((N_ROWS * ROW_TILE, LANES), I32),
        grid=(T_ALL // tb,),
        in_specs=[pl.BlockSpec((TOP_K, tb), lambda i: (0, i), memory_space=pltpu.SMEM),
                  pl.BlockSpec((tb, D_MODEL), lambda i: (i, 0))],
        out_specs=pl.BlockSpec(memory_space=pl.ANY),
        scratch_shapes=[pltpu.VMEM((2 * tb * ROW_TILE, LANES), I32),
                        pltpu.VMEM((SUBLANES * ROW_TILE, LANES), I32),
                        pltpu.SemaphoreType.DMA((2,)),
                        pltpu.SemaphoreType.DMA(())],
        compiler_params=_cparams(("arbitrary",)),
        name="dispatch",
    )(dest_t, v_all)


def _flat_rows(ref, row, n_rows):
    return ref.at[pl.ds(pl.multiple_of(row * ROW_TILE, ROW_TILE), n_rows * ROW_TILE), :]


def _tile_writes(y_scr, y_hbm, slot, row0, valid, sem):
    base = slot * GMM_TM
    parts = [(valid == GMM_TM, pltpu.make_async_copy(_flat_rows(y_scr, base, GMM_TM),
                                                     _flat_rows(y_hbm, row0, GMM_TM), sem))]
    off = jnp.int32(0)
    for p in TAIL_SIZES:
        bit = (valid & p) != 0
        take = jnp.logical_and(valid < GMM_TM, bit)
        parts.append((take, pltpu.make_async_copy(_flat_rows(y_scr, base + off, p),
                                                  _flat_rows(y_hbm, row0 + off, p), sem)))
        off = off + jnp.where(bit, p, 0)
    return parts


ST_TILES, ST_ROW, ST_VALID, ST_FETCH_E, ST_FETCH_I, ST_FETCH_G, ST_SIZE = 0, 1, 3, 5, 6, 7, 8
X_SLOTS = 4
X_AHEAD = X_SLOTS - 1


def _gmm_kernel(start_ref, xs_hbm, wg_ref, wu_ref, wd_ref, y_hbm,
                x_scr, y_scr, wgu_scr, wdn_scr, zero_scr, st_ref, xsem, ysem, zsem):
    e = pl.program_id(0)
    row_lo = start_ref[e]
    row_hi = start_ref[e + 1]
    n_rows = row_hi - row_lo
    n_tiles = lax.shift_right_logical(n_rows + (GMM_TM - 1), GMM_TM.bit_length() - 1)

    def x_copy(row, sl):
        return pltpu.make_async_copy(_flat_rows(xs_hbm, row, GMM_TM),
                                     _flat_rows(x_scr, sl * GMM_TM, GMM_TM), xsem.at[sl])

    def tiles_of(ex):
        rows = start_ref[ex + 1] - start_ref[ex]
        return lax.shift_right_logical(rows + (GMM_TM - 1), GMM_TM.bit_length() - 1)

    def skip_empty(ex):
        def empty(q):
            qc = jnp.minimum(q, N_EXPERTS - 1)
            return jnp.logical_and(q < N_EXPERTS, start_ref[qc + 1] == start_ref[qc])
        return lax.while_loop(empty, lambda q: q + 1, ex)

    def fetch_next_tile():
        pe = st_ref[ST_FETCH_E]

        @pl.when(pe < N_EXPERTS)
        def _():
            pi = st_ref[ST_FETCH_I]
            pg = st_ref[ST_FETCH_G]
            pec = jnp.minimum(pe, N_EXPERTS - 1)
            x_copy(start_ref[pec] + pi * GMM_TM, pg % X_SLOTS).start()
            last = pi + 1 >= tiles_of(pec)
            st_ref[ST_FETCH_E] = jnp.where(last, skip_empty(pe + 1), pe)
            st_ref[ST_FETCH_I] = jnp.where(last, 0, pi + 1)
            st_ref[ST_FETCH_G] = pg + 1

    def wait_writes(sl):
        @pl.when(st_ref[ST_VALID + sl] > 0)
        def _():
            for pred, cp in _tile_writes(y_scr, y_hbm, sl, st_ref[ST_ROW + sl], st_ref[ST_VALID + sl],
                                         ysem.at[sl]):
                @pl.when(pred)
                def _():
                    cp.wait()
            st_ref[ST_VALID + sl] = 0

    @pl.when(e == 0)
    def _():
        for j in range(ST_SIZE):
            st_ref[j] = 0
        st_ref[ST_FETCH_E] = skip_empty(jnp.int32(0))
        for _ in range(X_AHEAD):
            fetch_next_tile()

    done = st_ref[ST_TILES]

    @pl.when(n_tiles > 0)
    def _():
        wgu_scr[:, :D_EXPERT] = wg_ref[0].astype(BF16)
        wgu_scr[:, D_EXPERT:] = wu_ref[0].astype(BF16)
        wdn_scr[...] = wd_ref[0].astype(BF16)

        def tile(i, carry):
            g = done + i
            slot = g % 2
            xslot = g % X_SLOTS
            row0 = row_lo + i * GMM_TM
            valid = jnp.minimum(n_rows - i * GMM_TM, GMM_TM)
            x_copy(row0, xslot).wait()
            fetch_next_tile()

            base = pl.multiple_of(slot * (GMM_TM * ROW_TILE), ROW_TILE)
            xbase = pl.multiple_of(xslot * (GMM_TM * ROW_TILE), ROW_TILE)
            x = _tiles_to_rows(x_scr, xbase, GMM_TM, BF16)
            h = _dot(x, wgu_scr[...])
            act = (_silu(h[:, :D_EXPERT]) * h[:, D_EXPERT:]).astype(BF16)
            y = _dot(act, wdn_scr[...])

            wait_writes(slot)
            _rows_to_tiles(y_scr, base, y)
            for pred, cp in _tile_writes(y_scr, y_hbm, slot, row0, valid, ysem.at[slot]):
                @pl.when(pred)
                def _():
                    cp.start()
            st_ref[ST_ROW + slot] = row0
            st_ref[ST_VALID + slot] = valid
            return carry

        lax.fori_loop(0, n_tiles, tile, 0)
        st_ref[ST_TILES] = done + n_tiles

    @pl.when(e == pl.num_programs(0) - 1)
    def _():
        wait_writes(0)
        wait_writes(1)
        zero_scr[...] = jnp.zeros(zero_scr.shape, I32)
        for j in range(GMM_TM // SUBLANES):
            cp = pltpu.make_async_copy(zero_scr, _flat_rows(y_hbm, N_ASSIGN + j * SUBLANES, SUBLANES), zsem)
            cp.start()
            cp.wait()


def _grouped_mlp(group_start, xs, w_gate, w_up, w_down):
    grid_spec = pltpu.PrefetchScalarGridSpec(
        num_scalar_prefetch=1,
        grid=(N_EXPERTS,),
        in_specs=[pl.BlockSpec(memory_space=pl.ANY),
                  pl.BlockSpec((1, D_MODEL, D_EXPERT), lambda e, st: (e, 0, 0)),
                  pl.BlockSpec((1, D_MODEL, D_EXPERT), lambda e, st: (e, 0, 0)),
                  pl.BlockSpec((1, D_EXPERT, D_MODEL), lambda e, st: (e, 0, 0))],
        out_specs=pl.BlockSpec(memory_space=pl.ANY),
        scratch_shapes=[pltpu.VMEM((X_SLOTS * GMM_TM * ROW_TILE, LANES), I32),
                        pltpu.VMEM((2 * GMM_TM * ROW_TILE, LANES), I32),
                        pltpu.VMEM((D_MODEL, 2 * D_EXPERT), BF16),
                        pltpu.VMEM((D_EXPERT, D_MODEL), BF16),
                        pltpu.VMEM((SUBLANES * ROW_TILE, LANES), I32),
                        pltpu.SMEM((ST_SIZE,), I32),
                        pltpu.SemaphoreType.DMA((X_SLOTS,)),
                        pltpu.SemaphoreType.DMA((2,)),
                        pltpu.SemaphoreType.DMA(())],
    )
    return pl.pallas_call(
        _gmm_kernel,
        out_shape=jax.ShapeDtypeStruct((N_ROWS * ROW_TILE, LANES), I32),
        grid_spec=grid_spec,
        compiler_params=_cparams(("arbitrary",)),
        name="grouped_mlp",
    )(group_start, xs, w_gate, w_up, w_down)


COMBINE_TB = 256


def _combine_kernel(dest_ref, dest_next_ref, y_hbm, pre_ref, w_ref, mod_ref, fw_ref, op_ref, os_ref, buf, x2_scr,
                    sem):
    i = pl.program_id(0)
    n = pl.num_programs(0)
    tb = COMBINE_TB
    slot = i % 2
    blk_rows = TOP_K * tb

    def issue_block(d_ref, sl):
        def issue(t, carry):
            for k in range(TOP_K):
                pltpu.make_async_copy(_row_tile(y_hbm, d_ref[k, t]),
                                      _row_tile(buf, sl * blk_rows + k * tb + t),
                                      sem.at[sl]).start(priority=k % 2)
            return carry

        lax.fori_loop(0, tb, issue, 0)

    @pl.when(i == 0)
    def _():
        issue_block(dest_ref, 0)

    @pl.when(i + 1 < n)
    def _():
        issue_block(dest_next_ref, 1 - slot)

    whole = _flat_rows(buf, slot * blk_rows, blk_rows)
    pltpu.make_async_copy(whole, whole, sem.at[slot]).wait()

    base = pl.multiple_of(slot * (blk_rows * ROW_TILE), ROW_TILE)
    w = w_ref[...]
    m = mod_ref[...]
    ssq = jnp.zeros((tb, 1), F32)
    for s in range(ROW_TILE):
        routed = [None, None]
        for k in range(TOP_K):
            halves = _unpack_pair(_tile_column(buf, base + k * tb * ROW_TILE, tb, s))
            for j in range(2):
                term = w[:, k:k + 1] * halves[j]
                routed[j] = term if routed[j] is None else routed[j] + term
        for j in range(2):
            cols = slice(j * HALF_D + s * LANES, j * HALF_D + (s + 1) * LANES)
            x2 = pre_ref[:, cols] + m[5:6, cols] * routed[j]
            x2_scr[:, cols] = x2
            ssq = ssq + jnp.sum(x2 * x2, axis=-1, keepdims=True)
    scale = lax.rsqrt(ssq * (1.0 / D_MODEL) + EPS)

    @pl.when(i < T_PROMPT // tb)
    def _():
        op_ref[...] = x2_scr[...] * scale * fw_ref[...]

    @pl.when(i >= T_PROMPT // tb)
    def _():
        os_ref[...] = x2_scr[...] * scale * fw_ref[...]


def _combine(dest_t, y_rows, pre, w_tok, mod3, final_norm_w):
    tb = COMBINE_TB
    n_blocks = T_ALL // tb
    npb = T_PROMPT // tb
    row = functools.partial(_mod_row_of_block, blocks_prompt=npb, blocks_per_sample_seq=DEC_SEQ // tb)
    return pl.pallas_call(
        _combine_kernel,
        out_shape=(jax.ShapeDtypeStruct((T_PROMPT, D_MODEL), F32),
                   jax.ShapeDtypeStruct((T_SAMPLE, D_MODEL), F32)),
        grid=(n_blocks,),
        in_specs=[pl.BlockSpec((TOP_K, tb), lambda i: (0, i), memory_space=pltpu.SMEM),
                  pl.BlockSpec((TOP_K, tb), lambda i: (0, jnp.minimum(i + 1, n_blocks - 1)),
                               memory_space=pltpu.SMEM),
                  pl.BlockSpec(memory_space=pl.ANY),
                  pl.BlockSpec((tb, D_MODEL), lambda i: (i, 0)),
                  pl.BlockSpec((tb, TOP_K), lambda i: (i, 0)),
                  pl.BlockSpec((None, N_MOD, D_MODEL), lambda i: (row(i), 0, 0)),
                  pl.BlockSpec((1, D_MODEL), lambda i: (0, 0))],
        out_specs=_two_path_specs((tb, D_MODEL), npb),
        scratch_shapes=[pltpu.VMEM((2 * TOP_K * tb * ROW_TILE, LANES), I32),
                        pltpu.VMEM((tb, D_MODEL), F32),
                        pltpu.SemaphoreType.DMA((2,))],
        compiler_params=_cparams(("arbitrary",)),
        name="combine",
    )(dest_t, dest_t, y_rows, pre, w_tok, mod3, final_norm_w.reshape(1, D_MODEL))


def _group_starts(counts):
    return jnp.concatenate([jnp.zeros((1,), I32), jnp.cumsum(counts).astype(I32)])


def _head_expand_matrix():
    r = jnp.arange(LANES)[:, None]
    cidx = jnp.arange(2 * D_INNER)[None, :]
    direction = cidx // D_INNER
    head = (cidx % D_INNER) // HEAD_DIM
    return (r == direction * N_HEADS + head).astype(BF16)


def kernel(x_prompt, x_sample, state_ssm, c, c_ctx, norm1_w, norm2_w, w_mod, b_mod, w_in, ssm_conv_w, ssm_conv_b, ssm_dt_bias, ssm_a_log, ssm_d, ssm_norm_w, w_out_ssm, conf_conv_w, conf_conv_b, conf_ln_w, conf_ln_b, w_out_conf, w_o, router_w, router_bias, exp_w_gate, exp_w_up, exp_w_down, sh_w_gate, sh_w_up, sh_w_down, final_norm_w):
    x_p = x_prompt.reshape(T_PROMPT, D_MODEL)
    x_s = x_sample.reshape(T_SAMPLE, D_MODEL)

    cc = jnp.zeros((MOD_ROWS, D_MODEL), F32).at[:DEC_BATCH].set(c).at[CTX_ROW].set(c_ctx)
    mod3 = _modulation(cc, w_mod[0], b_mod[0]).reshape(MOD_ROWS, N_MOD, D_MODEL)

    w = w_in[0]
    o_xbc, o_dt, o_cv, o_cg, o_gl = D_INNER, D_INNER + CONV_DIM, D_INNER + CONV_DIM + 2 * N_HEADS, \
        D_INNER + CONV_DIM + 2 * N_HEADS + D_MODEL, D_INNER + CONV_DIM + 2 * N_HEADS + 2 * D_MODEL
    w_main = jnp.concatenate([w[:, o_xbc:o_dt], w[:, :o_xbc], w[:, o_gl:], w[:, o_cv:o_cg], w[:, o_cg:o_gl]],
                             axis=1).astype(BF16)
    w_dt = jnp.pad(w[:, o_dt:o_cv], ((0, 0), (0, LANES - 2 * N_HEADS))).astype(BF16)
    proj, dt_raw = _in_projection(x_p, x_s, mod3, norm1_w[0], w_main, w_dt, ssm_conv_w[0], ssm_conv_b[0])

    pad_heads = lambda v: jnp.pad(v.reshape(1, 2 * N_HEADS), ((0, 0), (0, LANES - 2 * N_HEADS)))
    dtb = pad_heads(ssm_dt_bias[0])
    a_neg = pad_heads(-jnp.exp(ssm_a_log[0]))
    dvec = jnp.repeat(ssm_d[0], HEAD_DIM).reshape(1, D_INNER)
    nw = ssm_norm_w[0].reshape(1, D_INNER)
    e2 = _head_expand_matrix()
    y_p, fin = _ssd(dt_raw, proj, None, dtb, a_neg, dvec, nw, e2,
                    seq_len=SEQ, n_seq=BATCH, tok0=0, want_final=True)
    init = state_ssm.reshape(DEC_BATCH, 2, D_INNER, D_STATE)
    (y_s,) = _ssd(dt_raw, proj, init, dtb, a_neg, dvec, nw, e2,
                  seq_len=DEC_SEQ, n_seq=DEC_BATCH, tok0=T_PROMPT, want_final=False)

    v_p = _conformer(proj, conf_conv_w[0], conf_conv_b[0], conf_ln_w[0], conf_ln_b[0],
                     seq_len=SEQ, n_seq=BATCH, row_block0=0, grid_mode=False)
    v_s = _conformer(proj, conf_conv_w[0], conf_conv_b[0], conf_ln_w[0], conf_ln_b[0],
                     seq_len=DEC_SEQ, n_seq=DEC_BATCH, row_block0=T_PROMPT // DEC_SEQ, grid_mode=True)

    sh_gu = jnp.concatenate([sh_w_gate[0], sh_w_up[0]], axis=1).astype(BF16)
    pre, v_all, scores = _merge(y_p, y_s, v_p, v_s, proj, x_p, x_s, mod3, w_out_ssm[0].astype(BF16),
                                w_out_conf[0].astype(BF16), w_o[0].astype(BF16), norm2_w[0], router_w[0],
                                sh_gu, sh_w_down[0].astype(BF16))

    idx_t, w_t, rank_t, cnt = _route(scores, router_bias[0])
    start = _group_starts(cnt[:, 0].astype(I32))
    dest_t = _dest_slots(idx_t, rank_t, start[:N_EXPERTS])

    xs = _dispatch(dest_t, v_all)
    y_rows = _grouped_mlp(start, xs, exp_w_gate[0], exp_w_up[0], exp_w_down[0])
    out_p, out_s = _combine(dest_t, y_rows, pre, w_t.T, mod3, final_norm_w)

    y_prompt = out_p.reshape(BATCH, SEQ, D_MODEL)
    y_sample = out_s.reshape(DEC_BATCH, DEC_SEQ, D_MODEL)
    new_state = fin.reshape(BATCH, 1, 2, N_HEADS, HEAD_DIM, D_STATE)
    return (y_prompt, y_sample, new_state)
```

```python
import functools

import jax
import jax.numpy as jnp
from jax import lax
from jax.experimental import pallas as pl
from jax.experimental.pallas import tpu as pltpu

F32 = jnp.float32
BF16 = jnp.bfloat16
I32 = jnp.int32

D_MODEL = 1024
BATCH = 32
SEQ = 256
DEC_BATCH = 8
DEC_SEQ = 1024
GRID_W = 64
GRID_H = DEC_SEQ // GRID_W
D_INNER = 2048
HEAD_DIM = 64
N_HEADS = 32
D_STATE = 128
N_GROUPS = 8
HEADS_PER_GROUP = N_HEADS // N_GROUPS
GROUP_W = HEADS_PER_GROUP * HEAD_DIM
D_CONV_SSM = 5
CHUNK = 128
CONV_DIM = D_INNER + 2 * N_GROUPS * D_STATE
CONF_K = 31
CONF_PAD = CONF_K // 2
N_EXPERTS = 256
TOP_K = 8
N_EXPERT_GROUPS = 8
EXPERTS_PER_GROUP = N_EXPERTS // N_EXPERT_GROUPS
TOPK_GROUPS = 4
D_EXPERT = 256
D_SHARED = 256
ROUTED_SCALE = 2.5
N_MOD = 6
EPS = 1e-6

T_PROMPT = BATCH * SEQ
T_SAMPLE = DEC_BATCH * DEC_SEQ
T_ALL = T_PROMPT + T_SAMPLE
N_ASSIGN = T_ALL * TOP_K
MOD_ROWS = 16
CTX_ROW = DEC_BATCH

SUBLANES = 8
LANES = 128
VMEM_LIMIT = 56 * 1024 * 1024

PROJ_W = CONV_DIM + D_INNER + 2 * D_MODEL + 2 * D_MODEL
COL_Z = CONV_DIM // D_INNER
COL_GL = COL_Z + 1
COL_CV = (CONV_DIM + 2 * D_INNER) // D_MODEL
COL_CG = COL_CV + 1

HALF_D = D_MODEL // 2
ROW_TILE = HALF_D // LANES
GMM_TM = 256
N_ROWS = N_ASSIGN + GMM_TM
TAIL_SIZES = (128, 64, 32, 16, 8, 4, 2, 1)


def _cparams(sem, vmem=VMEM_LIMIT):
    return pltpu.CompilerParams(dimension_semantics=sem, vmem_limit_bytes=vmem)


def _silu(x):
    return x * jax.nn.sigmoid(x)


def _split2(x):
    hi = x.astype(BF16)
    lo = (x - hi.astype(F32)).astype(BF16)
    return hi, lo


def _split3(x):
    b1 = x.astype(BF16)
    r = x - b1.astype(F32)
    b2 = r.astype(BF16)
    b3 = (r - b2.astype(F32)).astype(BF16)
    return b1, b2, b3


def _dot(a, b):
    return jnp.dot(a, b, preferred_element_type=F32)


def _dot_exact_lhs(a_exact, b):
    b1, b2, b3 = _split3(b)
    return _dot(a_exact, b1) + _dot(a_exact, b2) + _dot(a_exact, b3)


def _dot_f32(a, b):
    a1, a2 = _split2(a)
    b1, b2 = _split2(b)
    return _dot(a1, b1) + _dot(a1, b2) + _dot(a2, b1)


def _mod_row_of_block(i, blocks_prompt, blocks_per_sample_seq):
    return jnp.where(i < blocks_prompt, CTX_ROW, (i - blocks_prompt) // blocks_per_sample_seq)


def _mod_kernel(c_ref, w_ref, b_ref, o_ref):
    c = c_ref[...]
    o_ref[...] = _dot_f32(_silu(c), w_ref[...]) + b_ref[...]


def _modulation(cc, w_mod, b_mod):
    tn = 512
    n = N_MOD * D_MODEL
    return pl.pallas_call(
        _mod_kernel,
        out_shape=jax.ShapeDtypeStruct((MOD_ROWS, n), F32),
        grid=(n // tn,),
        in_specs=[pl.BlockSpec((MOD_ROWS, D_MODEL), lambda j: (0, 0)),
                  pl.BlockSpec((D_MODEL, tn), lambda j: (0, j)),
                  pl.BlockSpec((1, tn), lambda j: (0, j))],
        out_specs=pl.BlockSpec((MOD_ROWS, tn), lambda j: (0, j)),
        compiler_params=_cparams(("arbitrary",)),
        name="modulation",
    )(cc, w_mod, b_mod.reshape(1, n))


INPROJ_TM = 1024
INPROJ_TN = 2048


def _two_path_specs(block, n_prompt_blocks):
    last = n_prompt_blocks - 1
    return (pl.BlockSpec(block, lambda i, *_: (jnp.minimum(i, last), 0)),
            pl.BlockSpec(block, lambda i, *_: (jnp.maximum(i - n_prompt_blocks, 0), 0)))


def _pick_path(n_prompt_blocks, prompt_ref, sample_ref):
    return jnp.where(pl.program_id(0) < n_prompt_blocks, prompt_ref[...], sample_ref[...])


def _inproj_kernel(xp_ref, xs_ref, mod_ref, n1_ref, w_ref, wdt_ref, o_ref, dt_ref, u_scr):
    @pl.when(pl.program_id(1) == 0)
    def _():
        x = _pick_path(T_PROMPT // INPROJ_TM, xp_ref, xs_ref)
        ms = jnp.mean(x * x, axis=-1, keepdims=True)
        y = x * lax.rsqrt(ms + EPS) * n1_ref[...]
        m = mod_ref[...]
        u = (y * (1.0 + m[1:2]) + m[0:1]).astype(BF16)
        u_scr[...] = u
        dt_ref[...] = _dot(u, wdt_ref[...])

    o_ref[...] = _dot(u_scr[...], w_ref[...]).astype(BF16)


def _in_projection(x_p, x_s, mod3, norm1_w, w_main, w_dt):
    tm, tn = INPROJ_TM, INPROJ_TN
    row = functools.partial(_mod_row_of_block, blocks_prompt=T_PROMPT // tm,
                            blocks_per_sample_seq=DEC_SEQ // tm)
    return pl.pallas_call(
        _inproj_kernel,
        out_shape=(jax.ShapeDtypeStruct((T_ALL, PROJ_W), BF16),
                   jax.ShapeDtypeStruct((T_ALL, LANES), F32)),
        grid=(T_ALL // tm, PROJ_W // tn),
        in_specs=[*_two_path_specs((tm, D_MODEL), T_PROMPT // tm),
                  pl.BlockSpec((None, N_MOD, D_MODEL), lambda i, j: (row(i), 0, 0)),
                  pl.BlockSpec((1, D_MODEL), lambda i, j: (0, 0)),
                  pl.BlockSpec((D_MODEL, tn), lambda i, j: (0, j)),
                  pl.BlockSpec((D_MODEL, LANES), lambda i, j: (0, 0))],
        out_specs=(pl.BlockSpec((tm, tn), lambda i, j: (i, j)),
                   pl.BlockSpec((tm, LANES), lambda i, j: (i, 0))),
        scratch_shapes=[pltpu.VMEM((tm, D_MODEL), BF16)],
        compiler_params=_cparams(("arbitrary", "arbitrary")),
        name="in_projection",
    )(x_p, x_s, mod3, norm1_w.reshape(1, D_MODEL), w_main, w_dt)


SSMCONV_TN = 1024
SSMCONV_RT = 64
SSMCONV_HALO = SUBLANES


def _ssmconv_kernel(x_ref, w_ref, b_ref, o_ref, pad_scr, out_scr):
    seq_len = x_ref.shape[0]
    half = D_CONV_SSM // 2
    n_half = SSMCONV_RT // 2
    zeros = jnp.zeros((SSMCONV_HALO, LANES), F32)
    for sl in range(SSMCONV_TN // LANES):
        cs = slice(sl * LANES, (sl + 1) * LANES)
        pad_scr[sl, 0:SSMCONV_HALO, :] = zeros
        pad_scr[sl, SSMCONV_HALO + seq_len:, :] = zeros
        pad_scr[sl, SSMCONV_HALO:SSMCONV_HALO + seq_len, :] = x_ref[:, cs].astype(F32)
        wk = [w_ref[k:k + 1, cs] for k in range(D_CONV_SSM)]
        bias = b_ref[:, cs]
        for r0 in range(0, seq_len, SSMCONV_RT):
            for phase in range(2):
                acc = bias
                for k in range(D_CONV_SSM):
                    acc = acc + wk[k] * pad_scr[sl, pl.ds(SSMCONV_HALO + r0 + phase + k - half, n_half, stride=2), :]
                out_scr[sl, pl.ds(r0 + phase, n_half, stride=2), :] = _silu(acc)
        o_ref[:, cs] = out_scr[sl].astype(BF16)


def _ssm_conv(proj, conv_w, conv_b, *, seq_len, n_seq, row_block0):
    tn = SSMCONV_TN
    return pl.pallas_call(
        _ssmconv_kernel,
        out_shape=jax.ShapeDtypeStruct((n_seq * seq_len, CONV_DIM), BF16),
        grid=(n_seq, CONV_DIM // tn),
        in_specs=[pl.BlockSpec((seq_len, tn), lambda b, j: (row_block0 + b, j)),
                  pl.BlockSpec((D_CONV_SSM, tn), lambda b, j: (0, j)),
                  pl.BlockSpec((1, tn), lambda b, j: (0, j))],
        out_specs=pl.BlockSpec((seq_len, tn), lambda b, j: (b, j)),
        scratch_shapes=[pltpu.VMEM((tn // LANES, seq_len + 2 * SSMCONV_HALO, LANES), F32),
                        pltpu.VMEM((tn // LANES, seq_len, LANES), F32)],
        compiler_params=_cparams(("arbitrary", "arbitrary")),
        name="ssm_conv",
    )(proj, conv_w, conv_b.reshape(1, CONV_DIM))


def _tri_masks():
    ii = lax.broadcasted_iota(I32, (CHUNK, CHUNK), 0)
    jj = lax.broadcasted_iota(I32, (CHUNK, CHUNK), 1)
    return ii, jj


def _chunk_decays(dt_ref, dtb_ref, a_ref):
    ii, jj = _tri_masks()
    pre = dt_ref[...] + dtb_ref[...]
    dt = jnp.maximum(pre, 0.0) + jnp.log(1.0 + jnp.exp(-jnp.abs(pre)))
    la = dt * a_ref[...]
    tri_lo = jnp.where(jj <= ii, 1.0, 0.0).astype(BF16)
    tri_up = jnp.where(jj >= ii, 1.0, 0.0).astype(BF16)
    cs_prefix = _dot_exact_lhs(tri_lo, la)
    cs_suffix = _dot_exact_lhs(tri_up, la)
    fwd_lane = lax.broadcasted_iota(I32, (CHUNK, LANES), 1) < N_HEADS
    cs = jnp.where(fwd_lane, cs_prefix, cs_suffix)
    tot = jnp.where(fwd_lane[0:1], cs_prefix[CHUNK - 1:CHUNK, :], cs_suffix[0:1, :])
    return dt, cs, tot


def _transpose_blocks(src, rows, cols):
    out_rows = []
    for cb in range(cols // LANES):
        pieces = [src[rb * LANES:(rb + 1) * LANES, cb * LANES:(cb + 1) * LANES].T
                  for rb in range(rows // LANES)]
        out_rows.append(jnp.concatenate(pieces, axis=1) if len(pieces) > 1 else pieces[0])
    return jnp.concatenate(out_rows, axis=0) if len(out_rows) > 1 else out_rows[0]


def _ssd_kernel(*refs, n_chunks, has_init, want_final):
    it = iter(refs)
    xbc_ref, dt_ref, z_ref = next(it), next(it), next(it)
    init_ref = next(it) if has_init else None
    dtb_ref, a_ref, dvec_ref, nw_ref, e2_ref = next(it), next(it), next(it), next(it), next(it)
    out_ref = next(it)
    fin_ref = next(it) if want_final else None
    sf_scr, sb_scr, df_scr, db_scr = next(it), next(it), next(it), next(it)

    phase = pl.program_id(1)
    c = pl.program_id(2)

    @pl.when(phase == 0)
    def _chunk_states():
        dt, cs, tot = _chunk_decays(dt_ref, dtb_ref, a_ref)
        w_in = dt * jnp.exp(tot - cs)
        pack = 2 * SUBLANES
        dec_hi, dec_lo = _split2(jnp.exp(jnp.broadcast_to(tot, (pack, LANES))))
        expanded = _dot(jnp.concatenate([w_in.astype(BF16), dec_hi, dec_lo], axis=0), e2_ref[...])
        w_exp = expanded[:CHUNK]
        dec_exp = expanded[CHUNK:CHUNK + SUBLANES] + expanded[CHUNK + pack:CHUNK + pack + SUBLANES]
        df_scr[c] = dec_exp[:, :D_INNER]
        db_scr[c] = dec_exp[:, D_INNER:]
        for g in range(N_GROUPS):
            lo = g * GROUP_W
            xg = xbc_ref[:, lo:lo + GROUP_W].astype(F32)
            xd_f = (xg * w_exp[:, lo:lo + GROUP_W]).astype(BF16)
            xd_b = (xg * w_exp[:, D_INNER + lo:D_INNER + lo + GROUP_W]).astype(BF16)
            bg = xbc_ref[:, D_INNER + g * D_STATE:D_INNER + (g + 1) * D_STATE]
            bg_t = bg.astype(F32).T.astype(BF16)
            sf_scr[c, :, lo:lo + GROUP_W] = _dot(bg_t, xd_f)
            sb_scr[c, :, lo:lo + GROUP_W] = _dot(bg_t, xd_b)

    @pl.when(jnp.logical_and(phase == 1, c == 0))
    def _recurrence():
        for g in range(N_GROUPS):
            lo = g * GROUP_W
            if has_init:
                prev_f = _transpose_blocks(init_ref[0, 0, lo:lo + GROUP_W, :], GROUP_W, D_STATE)
                prev_b = _transpose_blocks(init_ref[0, 1, lo:lo + GROUP_W, :], GROUP_W, D_STATE)
            else:
                prev_f = jnp.zeros((D_STATE, GROUP_W), F32)
                prev_b = jnp.zeros((D_STATE, GROUP_W), F32)
            for cc in range(n_chunks):
                s = sf_scr[cc, :, lo:lo + GROUP_W]
                sf_scr[cc, :, lo:lo + GROUP_W] = prev_f
                prev_f = df_scr[cc, 0:1, lo:lo + GROUP_W] * prev_f + s
            for cc in reversed(range(n_chunks)):
                s = sb_scr[cc, :, lo:lo + GROUP_W]
                sb_scr[cc, :, lo:lo + GROUP_W] = prev_b
                prev_b = db_scr[cc, 0:1, lo:lo + GROUP_W] * prev_b + s
            if want_final:
                fin_ref[0, 0, lo:lo + GROUP_W, :] = _transpose_blocks(prev_f, D_STATE, GROUP_W)
                fin_ref[0, 1, lo:lo + GROUP_W, :] = _transpose_blocks(prev_b, D_STATE, GROUP_W)

    @pl.when(phase == 1)
    def _outputs():
        ii, jj = _tri_masks()
        dt, cs, _ = _chunk_decays(dt_ref, dtb_ref, a_ref)
        out_dec = _dot(jnp.exp(cs).astype(BF16), e2_ref[...])
        cs_t = cs.T
        dt_t = dt.T
        causal = ii >= jj
        anti = jj >= ii
        neg = jnp.float32(-1e30)
        left = lax.broadcasted_iota(I32, (CHUNK, LANES), 1) < HEAD_DIM
        for g in range(N_GROUPS):
            lo = g * GROUP_W
            bg = xbc_ref[:, D_INNER + g * D_STATE:D_INNER + (g + 1) * D_STATE]
            cg = xbc_ref[:, D_INNER + N_GROUPS * D_STATE + g * D_STATE:
                         D_INNER + N_GROUPS * D_STATE + (g + 1) * D_STATE]
            cb = lax.dot_general(cg, bg, (((1,), (1,)), ((), ())), preferred_element_type=F32)
            pf = sf_scr[c, :, lo:lo + GROUP_W].astype(BF16)
            pb = sb_scr[c, :, lo:lo + GROUP_W].astype(BF16)
            y_off = (_dot(cg, pf) * out_dec[:, lo:lo + GROUP_W]
                     + _dot(cg, pb) * out_dec[:, D_INNER + lo:D_INNER + lo + GROUP_W])
            pairs = []
            for m in range(HEADS_PER_GROUP // 2):
                x_pair = xbc_ref[:, lo + m * LANES:lo + (m + 1) * LANES]
                ys = []
                for hh in range(2):
                    h = g * HEADS_PER_GROUP + 2 * m + hh
                    hb = N_HEADS + h
                    seg_f = jnp.where(causal, cs[:, h:h + 1] - cs_t[h:h + 1, :], neg)
                    seg_b = jnp.where(anti, cs[:, hb:hb + 1] - cs_t[hb:hb + 1, :], neg)
                    mix = (jnp.exp(seg_f) * dt_t[h:h + 1, :] + jnp.exp(seg_b) * dt_t[hb:hb + 1, :])
                    ys.append(_dot((cb * mix).astype(BF16), x_pair))
                pairs.append(jnp.where(left, ys[0], ys[1]))
            y_diag = jnp.concatenate(pairs, axis=1)
            xg = xbc_ref[:, lo:lo + GROUP_W].astype(F32)
            y = y_diag + y_off + dvec_ref[:, lo:lo + GROUP_W] * xg
            zg = z_ref[:, lo:lo + GROUP_W].astype(F32)
            y = y * _silu(zg)
            ms = jnp.mean(y * y, axis=-1, keepdims=True)
            out_ref[:, lo:lo + GROUP_W] = (y * lax.rsqrt(ms + EPS) * nw_ref[:, lo:lo + GROUP_W]).astype(BF16)


def _ssd(xbc, dt_raw, proj, init, dtb, a_neg, dvec, norm_w, e2, *, seq_len, n_seq, tok0, want_final):
    nc = seq_len // CHUNK
    blk0 = tok0 // CHUNK
    has_init = init is not None
    in_specs = [pl.BlockSpec((CHUNK, CONV_DIM), lambda b, p, c: (b * nc + c, 0)),
                pl.BlockSpec((CHUNK, LANES), lambda b, p, c: (blk0 + b * nc + c, 0)),
                pl.BlockSpec((CHUNK, D_INNER), lambda b, p, c: (blk0 + b * nc + c * p, COL_Z))]
    args = [xbc, dt_raw, proj]
    if has_init:
        in_specs.append(pl.BlockSpec((1, 2, D_INNER, D_STATE), lambda b, p, c: (b, 0, 0, 0)))
        args.append(init)
    const = lambda b, p, c: (0, 0)
    in_specs += [pl.BlockSpec((1, LANES), const), pl.BlockSpec((1, LANES), const),
                 pl.BlockSpec((1, D_INNER), const), pl.BlockSpec((1, D_INNER), const),
                 pl.BlockSpec((LANES, 2 * D_INNER), const)]
    args += [dtb, a_neg, dvec, norm_w, e2]
    out_shape = [jax.ShapeDtypeStruct((n_seq * seq_len, D_INNER), BF16)]
    out_specs = [pl.BlockSpec((CHUNK, D_INNER), lambda b, p, c: (b * nc + c * p, 0))]
    if want_final:
        out_shape.append(jax.ShapeDtypeStruct((n_seq, 2, D_INNER, D_STATE), F32))
        out_specs.append(pl.BlockSpec((1, 2, D_INNER, D_STATE), lambda b, p, c: (b, 0, 0, 0)))
    res = pl.pallas_call(
        functools.partial(_ssd_kernel, n_chunks=nc, has_init=has_init, want_final=want_final),
        out_shape=tuple(out_shape),
        grid=(n_seq, 2, nc),
        in_specs=in_specs,
        out_specs=tuple(out_specs),
        scratch_shapes=[pltpu.VMEM((nc, D_STATE, D_INNER), F32),
                        pltpu.VMEM((nc, D_STATE, D_INNER), F32),
                        pltpu.VMEM((nc, SUBLANES, D_INNER), F32),
                        pltpu.VMEM((nc, SUBLANES, D_INNER), F32)],
        compiler_params=_cparams(("arbitrary", "arbitrary", "arbitrary")),
        name="ssd_final" if want_final else "ssd_init",
    )(*args)
    return res


CONF_SLABS = D_MODEL // LANES
CONF_HALO = 2 * SUBLANES
CONF_RT = 64


def _glu_slab(cv_ref, cg_ref, sl):
    cs = slice(sl * LANES, (sl + 1) * LANES)
    return cv_ref[:, cs].astype(F32) * jax.nn.sigmoid(cg_ref[:, cs].astype(F32))


def _conv_taps_strided(src_ref, sl, src_row0, dst_ref, dst_row0, w_ref, b_ref):
    cs = slice(sl * LANES, (sl + 1) * LANES)
    n_half = CONF_RT // 2
    for phase in range(2):
        acc = b_ref[:, cs]
        for k in range(CONF_K):
            x = src_ref[sl, pl.ds(src_row0 + phase + k - CONF_PAD, n_half, stride=2), :]
            acc = acc + w_ref[k:k + 1, cs] * x
        dst_ref[sl, pl.ds(dst_row0 + phase, n_half, stride=2), :] = acc


def _layernorm_silu_slabs(acc_scr, lw_ref, lb_ref, o_ref):
    n = acc_scr.shape[1]
    tot = jnp.zeros((n, 1), F32)
    for sl in range(CONF_SLABS):
        tot = tot + jnp.sum(acc_scr[sl], axis=-1, keepdims=True)
    mu = tot * (1.0 / D_MODEL)
    sq = jnp.zeros((n, 1), F32)
    for sl in range(CONF_SLABS):
        d = acc_scr[sl] - mu
        sq = sq + jnp.sum(d * d, axis=-1, keepdims=True)
    rstd = lax.rsqrt(sq * (1.0 / D_MODEL) + EPS)
    for sl in range(CONF_SLABS):
        cs = slice(sl * LANES, (sl + 1) * LANES)
        o_ref[:, cs] = _silu((acc_scr[sl] - mu) * rstd * lw_ref[:, cs] + lb_ref[:, cs]).astype(BF16)


def _conf_seq_kernel(cv_ref, cg_ref, w_ref, b_ref, lw_ref, lb_ref, o_ref, pad_scr, acc_scr):
    seq_len = cv_ref.shape[0]
    zeros = jnp.zeros((CONF_HALO, LANES), F32)
    for sl in range(CONF_SLABS):
        pad_scr[sl, 0:CONF_HALO, :] = zeros
        pad_scr[sl, CONF_HALO + seq_len:, :] = zeros
        pad_scr[sl, CONF_HALO:CONF_HALO + seq_len, :] = _glu_slab(cv_ref, cg_ref, sl)
    for sl in range(CONF_SLABS):
        for r0 in range(0, seq_len, CONF_RT):
            _conv_taps_strided(pad_scr, sl, CONF_HALO + r0, acc_scr, r0, w_ref, b_ref)
    _layernorm_silu_slabs(acc_scr, lw_ref, lb_ref, o_ref)


def _conf_grid_kernel(cv_ref, cg_ref, w_ref, b_ref, lw_ref, lb_ref, o_ref, v_scr, pad_scr, acc_scr):
    half_slabs = CONF_SLABS // 2
    stride = GRID_W + 2 * CONF_HALO
    pad_scr[...] = jnp.zeros(pad_scr.shape, F32)
    for sl in range(half_slabs):
        v = _glu_slab(cv_ref, cg_ref, sl)
        for r in range(GRID_H):
            pad_scr[sl, r * stride + CONF_HALO:r * stride + CONF_HALO + GRID_W, :] = v[r * GRID_W:(r + 1) * GRID_W]
    for sl in range(half_slabs, CONF_SLABS):
        v_scr[sl - half_slabs] = _glu_slab(cv_ref, cg_ref, sl)
    for sl in range(half_slabs):
        for r in range(GRID_H):
            _conv_taps_strided(pad_scr, sl, r * stride + CONF_HALO, acc_scr, r * GRID_W, w_ref, b_ref)
    for sl in range(half_slabs, CONF_SLABS):
        cs = slice(sl * LANES, (sl + 1) * LANES)
        for r in range(GRID_H):
            acc = jnp.broadcast_to(b_ref[:, cs], (GRID_W, LANES))
            for r2 in range(GRID_H):
                k = r2 - r + CONF_PAD
                acc = acc + w_ref[k:k + 1, cs] * v_scr[sl - half_slabs, r2 * GRID_W:(r2 + 1) * GRID_W, :]
            acc_scr[sl, r * GRID_W:(r + 1) * GRID_W, :] = acc
    _layernorm_silu_slabs(acc_scr, lw_ref, lb_ref, o_ref)


def _conformer(proj, conv_w, conv_b, ln_w, ln_b, *, seq_len, n_seq, row_block0, grid_mode):
    if grid_mode:
        body = _conf_grid_kernel
        scratch = [pltpu.VMEM((CONF_SLABS // 2, seq_len, LANES), F32),
                   pltpu.VMEM((CONF_SLABS // 2, GRID_H * (GRID_W + 2 * CONF_HALO), LANES), F32),
                   pltpu.VMEM((CONF_SLABS, seq_len, LANES), F32)]
    else:
        body = _conf_seq_kernel
        scratch = [pltpu.VMEM((CONF_SLABS, seq_len + 2 * CONF_HALO, LANES), F32),
                   pltpu.VMEM((CONF_SLABS, seq_len, LANES), F32)]
    const = lambda b: (0, 0)
    return pl.pallas_call(
        body,
        out_shape=jax.ShapeDtypeStruct((n_seq * seq_len, D_MODEL), BF16),
        grid=(n_seq,),
        in_specs=[pl.BlockSpec((seq_len, D_MODEL), lambda b: (row_block0 + b, COL_CV)),
                  pl.BlockSpec((seq_len, D_MODEL), lambda b: (row_block0 + b, COL_CG)),
                  pl.BlockSpec((CONF_K, D_MODEL), const),
                  pl.BlockSpec((1, D_MODEL), const),
                  pl.BlockSpec((1, D_MODEL), const),
                  pl.BlockSpec((1, D_MODEL), const)],
        out_specs=pl.BlockSpec((seq_len, D_MODEL), lambda b: (b, 0)),
        scratch_shapes=scratch,
        compiler_params=_cparams(("arbitrary",)),
        name="conformer_grid" if grid_mode else "conformer_seq",
    )(proj, proj, conv_w, conv_b.reshape(1, D_MODEL), ln_w.reshape(1, D_MODEL), ln_b.reshape(1, D_MODEL))


MERGE_TM = 256


def _merge_kernel(yap_ref, yas_ref, ybp_ref, ybs_ref, gl_ref, xp_ref, xs_ref, mod_ref, wa_ref, wb_ref, wo_ref,
                  n2_ref, rw_ref, sgu_ref, sd_ref, pre_ref, v_ref, s_ref):
    npb = T_PROMPT // MERGE_TM
    y_a = _dot(_pick_path(npb, yap_ref, yas_ref), wa_ref[...])
    y_b = _dot(_pick_path(npb, ybp_ref, ybs_ref), wb_ref[...])
    gates = jax.nn.sigmoid(gl_ref[...].astype(F32))
    mix = gates[:, :D_MODEL] * y_a + gates[:, D_MODEL:] * y_b
    out = _dot(mix.astype(BF16), wo_ref[...])
    m = mod_ref[...]
    x1 = _pick_path(npb, xp_ref, xs_ref) + m[2:3] * out
    ms = jnp.mean(x1 * x1, axis=-1, keepdims=True)
    v = x1 * lax.rsqrt(ms + EPS) * n2_ref[...] * (1.0 + m[4:5]) + m[3:4]
    v_ref[...] = v
    s_ref[...] = jax.nn.sigmoid(_dot_f32(v, rw_ref[...]))
    hgu = _dot(v.astype(BF16), sgu_ref[...])
    act = _silu(hgu[:, :D_SHARED]) * hgu[:, D_SHARED:]
    shared = _dot(act.astype(BF16), sd_ref[...])
    pre_ref[...] = x1 + m[5:6] * shared


def _merge(y_ssm_p, y_ssm_s, v_conf_p, v_conf_s, proj, x_p, x_s, mod3, w_out_ssm, w_out_conf, w_o, norm2_w,
           router_w, sh_gu, sh_d):
    tm = MERGE_TM
    npb = T_PROMPT // tm
    row = functools.partial(_mod_row_of_block, blocks_prompt=npb, blocks_per_sample_seq=DEC_SEQ // tm)
    const = lambda i: (0, 0)
    return pl.pallas_call(
        _merge_kernel,
        out_shape=(jax.ShapeDtypeStruct((T_ALL, D_MODEL), F32),
                   jax.ShapeDtypeStruct((T_ALL, D_MODEL), F32),
                   jax.ShapeDtypeStruct((T_ALL, N_EXPERTS), F32)),
        grid=(T_ALL // tm,),
        in_specs=[*_two_path_specs((tm, D_INNER), npb),
                  *_two_path_specs((tm, D_MODEL), npb),
                  pl.BlockSpec((tm, 2 * D_MODEL), lambda i: (i, COL_GL)),
                  *_two_path_specs((tm, D_MODEL), npb),
                  pl.BlockSpec((None, N_MOD, D_MODEL), lambda i: (row(i), 0, 0)),
                  pl.BlockSpec((D_INNER, D_MODEL), const),
                  pl.BlockSpec((D_MODEL, D_MODEL), const),
                  pl.BlockSpec((D_MODEL, D_MODEL), const),
                  pl.BlockSpec((1, D_MODEL), const),
                  pl.BlockSpec((D_MODEL, N_EXPERTS), const),
                  pl.BlockSpec((D_MODEL, 2 * D_SHARED), const),
                  pl.BlockSpec((D_SHARED, D_MODEL), const)],
        out_specs=(pl.BlockSpec((tm, D_MODEL), lambda i: (i, 0)),
                   pl.BlockSpec((tm, D_MODEL), lambda i: (i, 0)),
                   pl.BlockSpec((tm, N_EXPERTS), lambda i: (i, 0))),
        compiler_params=_cparams(("arbitrary",)),
        name="merge",
    )(y_ssm_p, y_ssm_s, v_conf_p, v_conf_s, proj, x_p, x_s, mod3, w_out_ssm, w_out_conf, w_o,
      norm2_w.reshape(1, D_MODEL), router_w, sh_gu, sh_d)


ROUTE_TB = 256


def _first_index_of_max(vals, iota, sentinel):
    m = jnp.max(vals, axis=0, keepdims=True)
    idx = jnp.min(jnp.where(vals == m, iota, jnp.float32(sentinel)), axis=0, keepdims=True)
    return m, idx


def _route_kernel(s_ref, bias_ref, idx_ref, w_ref, rank_ref, cnt_ref, run_scr):
    i = pl.program_id(0)
    tb = ROUTE_TB
    neg = jnp.float32(-jnp.inf)

    @pl.when(i == 0)
    def _():
        run_scr[...] = jnp.zeros(run_scr.shape, F32)

    s_t = s_ref[...].T
    sb_t = s_t + bias_ref[...]
    eiota = lax.broadcasted_iota(I32, (N_EXPERTS, tb), 0).astype(F32)

    liota = lax.broadcasted_iota(I32, (EXPERTS_PER_GROUP, tb), 0).astype(F32)
    gscores = []
    for g in range(N_EXPERT_GROUPS):
        blk = sb_t[g * EXPERTS_PER_GROUP:(g + 1) * EXPERTS_PER_GROUP, :]
        m1, i1 = _first_index_of_max(blk, liota, EXPERTS_PER_GROUP)
        m2 = jnp.max(jnp.where(liota == i1, neg, blk), axis=0, keepdims=True)
        gscores.append(m1 + m2)
    gs = jnp.concatenate(gscores, axis=0)
    giota = lax.broadcasted_iota(I32, (N_EXPERT_GROUPS, tb), 0).astype(F32)
    gsel = jnp.zeros((N_EXPERT_GROUPS, tb), F32)
    for _ in range(TOPK_GROUPS):
        _, gi = _first_index_of_max(gs, giota, N_EXPERT_GROUPS)
        hit = giota == gi
        gsel = jnp.where(hit, 1.0, gsel)
        gs = jnp.where(hit, neg, gs)
    emask = jnp.concatenate(
        [jnp.broadcast_to(gsel[g:g + 1, :], (EXPERTS_PER_GROUP, tb)) for g in range(N_EXPERT_GROUPS)], axis=0)
    masked = jnp.where(emask > 0.5, sb_t, neg)

    onehots, idxs, wts = [], [], []
    for _ in range(TOP_K):
        _, ei = _first_index_of_max(masked, eiota, N_EXPERTS)
        hit = eiota == ei
        onehots.append(hit)
        idxs.append(ei)
        wts.append(jnp.sum(jnp.where(hit, s_t, 0.0), axis=0, keepdims=True))
        masked = jnp.where(hit, neg, masked)
    w = jnp.concatenate(wts, axis=0)
    w_ref[...] = w / jnp.sum(w, axis=0, keepdims=True) * ROUTED_SCALE
    idx_ref[...] = jnp.concatenate(idxs, axis=0).astype(I32)

    assign = jnp.zeros((N_EXPERTS, tb), F32)
    for hit in onehots:
        assign = jnp.where(hit, 1.0, assign)
    assign_b = assign.astype(BF16)
    ti = lax.broadcasted_iota(I32, (tb, tb), 0)
    tj = lax.broadcasted_iota(I32, (tb, tb), 1)
    before = jnp.where(ti < tj, 1.0, 0.0).astype(BF16)
    within = _dot(assign_b, before)
    run = run_scr[...]
    total = within + jnp.concatenate([run] * (tb // LANES), axis=1)
    rank_ref[...] = jnp.concatenate(
        [jnp.sum(jnp.where(hit, total, 0.0), axis=0, keepdims=True) for hit in onehots], axis=0).astype(I32)
    new_run = run + _dot(assign_b, jnp.ones((tb, LANES), BF16))
    run_scr[...] = new_run
    cnt_ref[...] = new_run


def _route(scores, router_bias):
    tb = ROUTE_TB
    return pl.pallas_call(
        _route_kernel,
        out_shape=(jax.ShapeDtypeStruct((TOP_K, T_ALL), I32),
                   jax.ShapeDtypeStruct((TOP_K, T_ALL), F32),
                   jax.ShapeDtypeStruct((TOP_K, T_ALL), I32),
                   jax.ShapeDtypeStruct((N_EXPERTS, LANES), F32)),
        grid=(T_ALL // tb,),
        in_specs=[pl.BlockSpec((tb, N_EXPERTS), lambda i: (i, 0)),
                  pl.BlockSpec((N_EXPERTS, 1), lambda i: (0, 0))],
        out_specs=(pl.BlockSpec((TOP_K, tb), lambda i: (0, i)),
                   pl.BlockSpec((TOP_K, tb), lambda i: (0, i)),
                   pl.BlockSpec((TOP_K, tb), lambda i: (0, i)),
                   pl.BlockSpec((N_EXPERTS, LANES), lambda i: (0, 0))),
        scratch_shapes=[pltpu.VMEM((N_EXPERTS, LANES), F32)],
        compiler_params=_cparams(("arbitrary",)),
        name="route",
    )(scores, router_bias.reshape(N_EXPERTS, 1))


def _dest_kernel(idx_ref, rank_ref, start_ref, dest_ref):
    tb = idx_ref.shape[1]
    eiota = lax.broadcasted_iota(I32, (N_EXPERTS, tb), 0)
    start = jnp.broadcast_to(start_ref[...], (N_EXPERTS, tb))
    idx = idx_ref[...]
    rows = [jnp.sum(jnp.where(eiota == idx[k:k + 1, :], start, 0.0), axis=0, keepdims=True)
            for k in range(TOP_K)]
    dest_ref[...] = jnp.concatenate(rows, axis=0).astype(I32) + rank_ref[...]


def _dest_slots(idx_t, rank_t, group_start):
    tb = 512
    return pl.pallas_call(
        _dest_kernel,
        out_shape=jax.ShapeDtypeStruct((TOP_K, T_ALL), I32),
        grid=(T_ALL // tb,),
        in_specs=[pl.BlockSpec((TOP_K, tb), lambda i: (0, i)),
                  pl.BlockSpec((TOP_K, tb), lambda i: (0, i)),
                  pl.BlockSpec((N_EXPERTS, 1), lambda i: (0, 0))],
        out_specs=pl.BlockSpec((TOP_K, tb), lambda i: (0, i)),
        compiler_params=_cparams(("arbitrary",)),
        name="dest_slots",
    )(idx_t, rank_t, group_start.astype(F32).reshape(N_EXPERTS, 1))


DISPATCH_TB = 256


HI_MASK = -65536


def _pack_pair(hi, lo):
    hb = pltpu.bitcast(hi.astype(BF16).astype(F32), I32)
    lb = pltpu.bitcast(lo.astype(BF16).astype(F32), I32)
    return jnp.bitwise_or(hb, lax.shift_right_logical(lb, 16))


def _unpack_pair(word):
    hi = pltpu.bitcast(jnp.bitwise_and(word, HI_MASK), F32)
    lo = pltpu.bitcast(lax.shift_left(word, 16), F32)
    return hi, lo


def _rows_to_tiles(dst_scr, base, rows):
    n = rows.shape[0]
    for s in range(ROW_TILE):
        dst_scr[pl.ds(base + s, n, stride=ROW_TILE), :] = _pack_pair(
            rows[:, s * LANES:(s + 1) * LANES], rows[:, HALF_D + s * LANES:HALF_D + (s + 1) * LANES])


def _tile_column(src_scr, base, n, s):
    return src_scr[pl.ds(base + s, n, stride=ROW_TILE), :]


def _tiles_to_rows(src_scr, base, n, dtype):
    pairs = [_unpack_pair(_tile_column(src_scr, base, n, s)) for s in range(ROW_TILE)]
    return jnp.concatenate([p[0].astype(dtype) for p in pairs] + [p[1].astype(dtype) for p in pairs], axis=1)


def _row_tile(ref, row):
    return ref.at[pl.ds(pl.multiple_of(row * ROW_TILE, ROW_TILE), ROW_TILE), :]


def _dispatch_kernel(dest_ref, v_ref, xs_hbm, tile_scr, zero_scr, sem, zsem):
    i = pl.program_id(0)
    n = pl.num_programs(0)
    tb = DISPATCH_TB
    slot = i % 2
    base = pl.multiple_of(slot * (tb * ROW_TILE), ROW_TILE)
    _rows_to_tiles(tile_scr, base, v_ref[...])

    def issue(t, carry):
        src = _row_tile(tile_scr, slot * tb + t)
        for k in range(TOP_K):
            pltpu.make_async_copy(src, _row_tile(xs_hbm, dest_ref[k, t]), sem.at[slot]).start(priority=k % 2)
        return carry

    lax.fori_loop(0, tb, issue, 0)

    def wait_block(sl):
        blk = tile_scr.at[pl.ds(pl.multiple_of(sl * (tb * ROW_TILE), ROW_TILE), tb * ROW_TILE), :]
        for _ in range(TOP_K):
            pltpu.make_async_copy(blk, blk, sem.at[sl]).wait()

    @pl.when(i == 0)
    def _():
        zero_scr[...] = jnp.zeros(zero_scr.shape, I32)
        for j in range(GMM_TM // SUBLANES):
            cp = pltpu.make_async_copy(
                zero_scr, xs_hbm.at[pl.ds((N_ASSIGN + j * SUBLANES) * ROW_TILE, SUBLANES * ROW_TILE), :], zsem)
            cp.start()
            cp.wait()

    @pl.when(i > 0)
    def _():
        wait_block(1 - slot)

    @pl.when(i == n - 1)
    def _():
        wait_block(slot)


def _dispatch(dest_t, v_all):
    tb = DISPATCH_TB
    return pl.pallas_call(
        _dispatch_kernel,
        out_shape=jax.ShapeDtypeStruct((N_ROWS * ROW_TILE, LANES), I32),
        grid=(T_ALL // tb,),
        in_specs=[pl.BlockSpec((TOP_K, tb), lambda i: (0, i), memory_space=pltpu.SMEM),
                  pl.BlockSpec((tb, D_MODEL), lambda i: (i, 0))],
        out_specs=pl.BlockSpec(memory_space=pl.ANY),
        scratch_shapes=[pltpu.VMEM((2 * tb * ROW_TILE, LANES), I32),
                        pltpu.VMEM((SUBLANES * ROW_TILE, LANES), I32),
                        pltpu.SemaphoreType.DMA((2,)),
                        pltpu.SemaphoreType.DMA(())],
        compiler_params=_cparams(("arbitrary",)),
        name="dispatch",
    )(dest_t, v_all)


def _flat_rows(ref, row, n_rows):
    return ref.at[pl.ds(pl.multiple_of(row * ROW_TILE, ROW_TILE), n_rows * ROW_TILE), :]


def _tile_writes(y_scr, y_hbm, slot, row0, valid, sem):
    base = slot * GMM_TM
    parts = [(valid == GMM_TM, pltpu.make_async_copy(_flat_rows(y_scr, base, GMM_TM),
                                                     _flat_rows(y_hbm, row0, GMM_TM), sem))]
    off = jnp.int32(0)
    for p in TAIL_SIZES:
        bit = (valid & p) != 0
        take = jnp.logical_and(valid < GMM_TM, bit)
        parts.append((take, pltpu.make_async_copy(_flat_rows(y_scr, base + off, p),
                                                  _flat_rows(y_hbm, row0 + off, p), sem)))
        off = off + jnp.where(bit, p, 0)
    return parts


ST_TILES, ST_ROW, ST_VALID, ST_FETCH_E, ST_FETCH_I, ST_FETCH_G, ST_SIZE = 0, 1, 3, 5, 6, 7, 8
X_SLOTS = 4
X_AHEAD = X_SLOTS - 1


def _gmm_kernel(start_ref, xs_hbm, wg_ref, wu_ref, wd_ref, y_hbm,
                x_scr, y_scr, wgu_scr, wdn_scr, zero_scr, st_ref, xsem, ysem, zsem):
    e = pl.program_id(0)
    row_lo = start_ref[e]
    row_hi = start_ref[e + 1]
    n_rows = row_hi - row_lo
    n_tiles = lax.shift_right_logical(n_rows + (GMM_TM - 1), GMM_TM.bit_length() - 1)

    def x_copy(row, sl):
        return pltpu.make_async_copy(_flat_rows(xs_hbm, row, GMM_TM),
                                     _flat_rows(x_scr, sl * GMM_TM, GMM_TM), xsem.at[sl])

    def tiles_of(ex):
        rows = start_ref[ex + 1] - start_ref[ex]
        return lax.shift_right_logical(rows + (GMM_TM - 1), GMM_TM.bit_length() - 1)

    def skip_empty(ex):
        def empty(q):
            qc = jnp.minimum(q, N_EXPERTS - 1)
            return jnp.logical_and(q < N_EXPERTS, start_ref[qc + 1] == start_ref[qc])
        return lax.while_loop(empty, lambda q: q + 1, ex)

    def fetch_next_tile():
        pe = st_ref[ST_FETCH_E]

        @pl.when(pe < N_EXPERTS)
        def _():
            pi = st_ref[ST_FETCH_I]
            pg = st_ref[ST_FETCH_G]
            pec = jnp.minimum(pe, N_EXPERTS - 1)
            x_copy(start_ref[pec] + pi * GMM_TM, pg % X_SLOTS).start()
            last = pi + 1 >= tiles_of(pec)
            st_ref[ST_FETCH_E] = jnp.where(last, skip_empty(pe + 1), pe)
            st_ref[ST_FETCH_I] = jnp.where(last, 0, pi + 1)
            st_ref[ST_FETCH_G] = pg + 1

    def wait_writes(sl):
        @pl.when(st_ref[ST_VALID + sl] > 0)
        def _():
            for pred, cp in _tile_writes(y_scr, y_hbm, sl, st_ref[ST_ROW + sl], st_ref[ST_VALID + sl],
                                         ysem.at[sl]):
                @pl.when(pred)
                def _():
                    cp.wait()
            st_ref[ST_VALID + sl] = 0

    @pl.when(e == 0)
    def _():
        for j in range(ST_SIZE):
            st_ref[j] = 0
        st_ref[ST_FETCH_E] = skip_empty(jnp.int32(0))
        for _ in range(X_AHEAD):
            fetch_next_tile()

    done = st_ref[ST_TILES]

    @pl.when(n_tiles > 0)
    def _():
        wgu_scr[:, :D_EXPERT] = wg_ref[0].astype(BF16)
        wgu_scr[:, D_EXPERT:] = wu_ref[0].astype(BF16)
        wdn_scr[...] = wd_ref[0].astype(BF16)

        def tile(i, carry):
            g = done + i
            slot = g % 2
            xslot = g % X_SLOTS
            row0 = row_lo + i * GMM_TM
            valid = jnp.minimum(n_rows - i * GMM_TM, GMM_TM)
            x_copy(row0, xslot).wait()
            fetch_next_tile()

            base = pl.multiple_of(slot * (GMM_TM * ROW_TILE), ROW_TILE)
            xbase = pl.multiple_of(xslot * (GMM_TM * ROW_TILE), ROW_TILE)
            x = _tiles_to_rows(x_scr, xbase, GMM_TM, BF16)
            h = _dot(x, wgu_scr[...])
            act = (_silu(h[:, :D_EXPERT]) * h[:, D_EXPERT:]).astype(BF16)
            y = _dot(act, wdn_scr[...])

            wait_writes(slot)
            _rows_to_tiles(y_scr, base, y)
            for pred, cp in _tile_writes(y_scr, y_hbm, slot, row0, valid, ysem.at[slot]):
                @pl.when(pred)
                def _():
                    cp.start()
            st_ref[ST_ROW + slot] = row0
            st_ref[ST_VALID + slot] = valid
            return carry

        lax.fori_loop(0, n_tiles, tile, 0)
        st_ref[ST_TILES] = done + n_tiles

    @pl.when(e == pl.num_programs(0) - 1)
    def _():
        wait_writes(0)
        wait_writes(1)
        zero_scr[...] = jnp.zeros(zero_scr.shape, I32)
        for j in range(GMM_TM // SUBLANES):
            cp = pltpu.make_async_copy(zero_scr, _flat_rows(y_hbm, N_ASSIGN + j * SUBLANES, SUBLANES), zsem)
            cp.start()
            cp.wait()


def _grouped_mlp(group_start, xs, w_gate, w_up, w_down):
    grid_spec = pltpu.PrefetchScalarGridSpec(
        num_scalar_prefetch=1,
        grid=(N_EXPERTS,),
        in_specs=[pl.BlockSpec(memory_space=pl.ANY),
                  pl.BlockSpec((1, D_MODEL, D_EXPERT), lambda e, st: (e, 0, 0)),
                  pl.BlockSpec((1, D_MODEL, D_EXPERT), lambda e, st: (e, 0, 0)),
                  pl.BlockSpec((1, D_EXPERT, D_MODEL), lambda e, st: (e, 0, 0))],
        out_specs=pl.BlockSpec(memory_space=pl.ANY),
        scratch_shapes=[pltpu.VMEM((X_SLOTS * GMM_TM * ROW_TILE, LANES), I32),
                        pltpu.VMEM((2 * GMM_TM * ROW_TILE, LANES), I32),
                        pltpu.VMEM((D_MODEL, 2 * D_EXPERT), BF16),
                        pltpu.VMEM((D_EXPERT, D_MODEL), BF16),
                        pltpu.VMEM((SUBLANES * ROW_TILE, LANES), I32),
                        pltpu.SMEM((ST_SIZE,), I32),
                        pltpu.SemaphoreType.DMA((X_SLOTS,)),
                        pltpu.SemaphoreType.DMA((2,)),
                        pltpu.SemaphoreType.DMA(())],
    )
    return pl.pallas_call(
        _gmm_kernel,
        out_shape=jax.ShapeDtypeStruct((N_ROWS * ROW_TILE, LANES), I32),
        grid_spec=grid_spec,
        compiler_params=_cparams(("arbitrary",)),
        name="grouped_mlp",
    )(group_start, xs, w_gate, w_up, w_down)


COMBINE_TB = 256


def _combine_kernel(dest_ref, dest_next_ref, y_hbm, pre_ref, w_ref, mod_ref, fw_ref, op_ref, os_ref, buf, x2_scr,
                    sem):
    i = pl.program_id(0)
    n = pl.num_programs(0)
    tb = COMBINE_TB
    slot = i % 2
    blk_rows = TOP_K * tb

    def issue_block(d_ref, sl):
        def issue(t, carry):
            for k in range(TOP_K):
                pltpu.make_async_copy(_row_tile(y_hbm, d_ref[k, t]),
                                      _row_tile(buf, sl * blk_rows + k * tb + t),
                                      sem.at[sl]).start(priority=k % 2)
            return carry

        lax.fori_loop(0, tb, issue, 0)

    @pl.when(i == 0)
    def _():
        issue_block(dest_ref, 0)

    @pl.when(i + 1 < n)
    def _():
        issue_block(dest_next_ref, 1 - slot)

    whole = _flat_rows(buf, slot * blk_rows, blk_rows)
    pltpu.make_async_copy(whole, whole, sem.at[slot]).wait()

    base = pl.multiple_of(slot * (blk_rows * ROW_TILE), ROW_TILE)
    w = w_ref[...]
    m = mod_ref[...]
    ssq = jnp.zeros((tb, 1), F32)
    for s in range(ROW_TILE):
        routed = [None, None]
        for k in range(TOP_K):
            halves = _unpack_pair(_tile_column(buf, base + k * tb * ROW_TILE, tb, s))
            for j in range(2):
                term = w[:, k:k + 1] * halves[j]
                routed[j] = term if routed[j] is None else routed[j] + term
        for j in range(2):
            cols = slice(j * HALF_D + s * LANES, j * HALF_D + (s + 1) * LANES)
            x2 = pre_ref[:, cols] + m[5:6, cols] * routed[j]
            x2_scr[:, cols] = x2
            ssq = ssq + jnp.sum(x2 * x2, axis=-1, keepdims=True)
    scale = lax.rsqrt(ssq * (1.0 / D_MODEL) + EPS)

    @pl.when(i < T_PROMPT // tb)
    def _():
        op_ref[...] = x2_scr[...] * scale * fw_ref[...]

    @pl.when(i >= T_PROMPT // tb)
    def _():
        os_ref[...] = x2_scr[...] * scale * fw_ref[...]


def _combine(dest_t, y_rows, pre, w_tok, mod3, final_norm_w):
    tb = COMBINE_TB
    n_blocks = T_ALL // tb
    npb = T_PROMPT // tb
    row = functools.partial(_mod_row_of_block, blocks_prompt=npb, blocks_per_sample_seq=DEC_SEQ // tb)
    return pl.pallas_call(
        _combine_kernel,
        out_shape=(jax.ShapeDtypeStruct((T_PROMPT, D_MODEL), F32),
                   jax.ShapeDtypeStruct((T_SAMPLE, D_MODEL), F32)),
        grid=(n_blocks,),
        in_specs=[pl.BlockSpec((TOP_K, tb), lambda i: (0, i), memory_space=pltpu.SMEM),
                  pl.BlockSpec((TOP_K, tb), lambda i: (0, jnp.minimum(i + 1, n_blocks - 1)),
                               memory_space=pltpu.SMEM),
                  pl.BlockSpec(memory_space=pl.ANY),
                  pl.BlockSpec((tb, D_MODEL), lambda i: (i, 0)),
                  pl.BlockSpec((tb, TOP_K), lambda i: (i, 0)),
                  pl.BlockSpec((None, N_MOD, D_MODEL), lambda i: (row(i), 0, 0)),
                  pl.BlockSpec((1, D_MODEL), lambda i: (0, 0))],
        out_specs=_two_path_specs((tb, D_MODEL), npb),
        scratch_shapes=[pltpu.VMEM((2 * TOP_K * tb * ROW_TILE, LANES), I32),
                        pltpu.VMEM((tb, D_MODEL), F32),
                        pltpu.SemaphoreType.DMA((2,))],
        compiler_params=_cparams(("arbitrary",)),
        name="combine",
    )(dest_t, dest_t, y_rows, pre, w_tok, mod3, final_norm_w.reshape(1, D_MODEL))


def _group_starts(counts):
    return jnp.concatenate([jnp.zeros((1,), I32), jnp.cumsum(counts).astype(I32)])


def _head_expand_matrix():
    r = jnp.arange(LANES)[:, None]
    cidx = jnp.arange(2 * D_INNER)[None, :]
    direction = cidx // D_INNER
    head = (cidx % D_INNER) // HEAD_DIM
    return (r == direction * N_HEADS + head).astype(BF16)


def kernel(x_prompt, x_sample, state_ssm, c, c_ctx, norm1_w, norm2_w, w_mod, b_mod, w_in, ssm_conv_w, ssm_conv_b, ssm_dt_bias, ssm_a_log, ssm_d, ssm_norm_w, w_out_ssm, conf_conv_w, conf_conv_b, conf_ln_w, conf_ln_b, w_out_conf, w_o, router_w, router_bias, exp_w_gate, exp_w_up, exp_w_down, sh_w_gate, sh_w_up, sh_w_down, final_norm_w):
    x_p = x_prompt.reshape(T_PROMPT, D_MODEL)
    x_s = x_sample.reshape(T_SAMPLE, D_MODEL)

    cc = jnp.zeros((MOD_ROWS, D_MODEL), F32).at[:DEC_BATCH].set(c).at[CTX_ROW].set(c_ctx)
    mod3 = _modulation(cc, w_mod[0], b_mod[0]).reshape(MOD_ROWS, N_MOD, D_MODEL)

    w = w_in[0]
    o_xbc, o_dt, o_cv, o_cg, o_gl = D_INNER, D_INNER + CONV_DIM, D_INNER + CONV_DIM + 2 * N_HEADS, \
        D_INNER + CONV_DIM + 2 * N_HEADS + D_MODEL, D_INNER + CONV_DIM + 2 * N_HEADS + 2 * D_MODEL
    w_main = jnp.concatenate([w[:, o_xbc:o_dt], w[:, :o_xbc], w[:, o_gl:], w[:, o_cv:o_cg], w[:, o_cg:o_gl]],
                             axis=1).astype(BF16)
    w_dt = jnp.pad(w[:, o_dt:o_cv], ((0, 0), (0, LANES - 2 * N_HEADS))).astype(BF16)
    proj, dt_raw = _in_projection(x_p, x_s, mod3, norm1_w[0], w_main, w_dt)

    pad_heads = lambda v: jnp.pad(v.reshape(1, 2 * N_HEADS), ((0, 0), (0, LANES - 2 * N_HEADS)))
    dtb = pad_heads(ssm_dt_bias[0])
    a_neg = pad_heads(-jnp.exp(ssm_a_log[0]))
    dvec = jnp.repeat(ssm_d[0], HEAD_DIM).reshape(1, D_INNER)
    nw = ssm_norm_w[0].reshape(1, D_INNER)
    e2 = _head_expand_matrix()
    xbc_p = _ssm_conv(proj, ssm_conv_w[0], ssm_conv_b[0], seq_len=SEQ, n_seq=BATCH, row_block0=0)
    xbc_s = _ssm_conv(proj, ssm_conv_w[0], ssm_conv_b[0], seq_len=DEC_SEQ, n_seq=DEC_BATCH,
                      row_block0=T_PROMPT // DEC_SEQ)
    y_p, fin = _ssd(xbc_p, dt_raw, proj, None, dtb, a_neg, dvec, nw, e2,
                    seq_len=SEQ, n_seq=BATCH, tok0=0, want_final=True)
    init = state_ssm.reshape(DEC_BATCH, 2, D_INNER, D_STATE)
    (y_s,) = _ssd(xbc_s, dt_raw, proj, init, dtb, a_neg, dvec, nw, e2,
                  seq_len=DEC_SEQ, n_seq=DEC_BATCH, tok0=T_PROMPT, want_final=False)

    v_p = _conformer(proj, conf_conv_w[0], conf_conv_b[0], conf_ln_w[0], conf_ln_b[0],
                     seq_len=SEQ, n_seq=BATCH, row_block0=0, grid_mode=False)
    v_s = _conformer(proj, conf_conv_w[0], conf_conv_b[0], conf_ln_w[0], conf_ln_b[0],
                     seq_len=DEC_SEQ, n_seq=DEC_BATCH, row_block0=T_PROMPT // DEC_SEQ, grid_mode=True)

    sh_gu = jnp.concatenate([sh_w_gate[0], sh_w_up[0]], axis=1).astype(BF16)
    pre, v_all, scores = _merge(y_p, y_s, v_p, v_s, proj, x_p, x_s, mod3, w_out_ssm[0].astype(BF16),
                                w_out_conf[0].astype(BF16), w_o[0].astype(BF16), norm2_w[0], router_w[0],
                                sh_gu, sh_w_down[0].astype(BF16))

    idx_t, w_t, rank_t, cnt = _route(scores, router_bias[0])
    start = _group_starts(cnt[:, 0].astype(I32))
    dest_t = _dest_slots(idx_t, rank_t, start[:N_EXPERTS])

    xs = _dispatch(dest_t, v_all)
    y_rows = _grouped_mlp(start, xs, exp_w_gate[0], exp_w_up[0], exp_w_down[0])
    out_p, out_s = _combine(dest_t, y_rows, pre, w_t.T, mod3, final_norm_w)

    y_prompt = out_p.reshape(BATCH, SEQ, D_MODEL)
    y_sample = out_s.reshape(DEC_BATCH, DEC_SEQ, D_MODEL)
    new_state = fin.reshape(BATCH, 1, 2, N_HEADS, HEAD_DIM, D_STATE)
    return (y_prompt, y_sample, new_state)
```

```python
import functools

import jax
import jax.numpy as jnp
from jax import lax
from jax.experimental import pallas as pl
from jax.experimental.pallas import tpu as pltpu

F32 = jnp.float32
BF16 = jnp.bfloat16
I32 = jnp.int32

D_MODEL = 1024
BATCH = 32
SEQ = 256
DEC_BATCH = 8
DEC_SEQ = 1024
GRID_W = 64
GRID_H = DEC_SEQ // GRID_W
D_INNER = 2048
HEAD_DIM = 64
N_HEADS = 32
D_STATE = 128
N_GROUPS = 8
HEADS_PER_GROUP = N_HEADS // N_GROUPS
GROUP_W = HEADS_PER_GROUP * HEAD_DIM
D_CONV_SSM = 5
CHUNK = 128
CONV_DIM = D_INNER + 2 * N_GROUPS * D_STATE
CONF_K = 31
CONF_PAD = CONF_K // 2
N_EXPERTS = 256
TOP_K = 8
N_EXPERT_GROUPS = 8
EXPERTS_PER_GROUP = N_EXPERTS // N_EXPERT_GROUPS
TOPK_GROUPS = 4
D_EXPERT = 256
D_SHARED = 256
ROUTED_SCALE = 2.5
N_MOD = 6
EPS = 1e-6

T_PROMPT = BATCH * SEQ
T_SAMPLE = DEC_BATCH * DEC_SEQ
T_ALL = T_PROMPT + T_SAMPLE
N_ASSIGN = T_ALL * TOP_K
MOD_ROWS = 16
CTX_ROW = DEC_BATCH

SUBLANES = 8
LANES = 128
VMEM_LIMIT = 56 * 1024 * 1024

PROJ_W = CONV_DIM + D_INNER + 2 * D_MODEL + 2 * D_MODEL
COL_Z = CONV_DIM // D_INNER
COL_GL = COL_Z + 1
COL_CV = (CONV_DIM + 2 * D_INNER) // D_MODEL
COL_CG = COL_CV + 1

HALF_D = D_MODEL // 2
ROW_TILE = HALF_D // LANES
GMM_TM = 256
N_ROWS = N_ASSIGN + GMM_TM
TAIL_SIZES = (128, 64, 32, 16, 8, 4, 2, 1)


def _cparams(sem, vmem=VMEM_LIMIT):
    return pltpu.CompilerParams(dimension_semantics=sem, vmem_limit_bytes=vmem)


def _silu(x):
    return x * jax.nn.sigmoid(x)


def _split2(x):
    hi = x.astype(BF16)
    lo = (x - hi.astype(F32)).astype(BF16)
    return hi, lo


def _split3(x):
    b1 = x.astype(BF16)
    r = x - b1.astype(F32)
    b2 = r.astype(BF16)
    b3 = (r - b2.astype(F32)).astype(BF16)
    return b1, b2, b3


def _dot(a, b):
    return jnp.dot(a, b, preferred_element_type=F32)


def _dot_exact_lhs(a_exact, b):
    b1, b2, b3 = _split3(b)
    return _dot(a_exact, b1) + _dot(a_exact, b2) + _dot(a_exact, b3)


def _dot_f32(a, b):
    a1, a2 = _split2(a)
    b1, b2 = _split2(b)
    return _dot(a1, b1) + _dot(a1, b2) + _dot(a2, b1)


def _mod_row_of_block(i, blocks_prompt, blocks_per_sample_seq):
    return jnp.where(i < blocks_prompt, CTX_ROW, (i - blocks_prompt) // blocks_per_sample_seq)


def _mod_kernel(c_ref, w_ref, b_ref, o_ref):
    c = c_ref[...]
    o_ref[...] = _dot_f32(_silu(c), w_ref[...]) + b_ref[...]


def _modulation(cc, w_mod, b_mod):
    tn = 512
    n = N_MOD * D_MODEL
    return pl.pallas_call(
        _mod_kernel,
        out_shape=jax.ShapeDtypeStruct((MOD_ROWS, n), F32),
        grid=(n // tn,),
        in_specs=[pl.BlockSpec((MOD_ROWS, D_MODEL), lambda j: (0, 0)),
                  pl.BlockSpec((D_MODEL, tn), lambda j: (0, j)),
                  pl.BlockSpec((1, tn), lambda j: (0, j))],
        out_specs=pl.BlockSpec((MOD_ROWS, tn), lambda j: (0, j)),
        compiler_params=_cparams(("arbitrary",)),
        name="modulation",
    )(cc, w_mod, b_mod.reshape(1, n))


INPROJ_TM = 1024
INPROJ_TN = 2048


def _two_path_specs(block, n_prompt_blocks):
    last = n_prompt_blocks - 1
    return (pl.BlockSpec(block, lambda i, *_: (jnp.minimum(i, last), 0)),
            pl.BlockSpec(block, lambda i, *_: (jnp.maximum(i - n_prompt_blocks, 0), 0)))


def _pick_path(n_prompt_blocks, prompt_ref, sample_ref):
    return jnp.where(pl.program_id(0) < n_prompt_blocks, prompt_ref[...], sample_ref[...])


def _inproj_kernel(xp_ref, xs_ref, mod_ref, n1_ref, w_ref, wdt_ref, o_ref, dt_ref, u_scr):
    @pl.when(pl.program_id(1) == 0)
    def _():
        x = _pick_path(T_PROMPT // INPROJ_TM, xp_ref, xs_ref)
        ms = jnp.mean(x * x, axis=-1, keepdims=True)
        y = x * lax.rsqrt(ms + EPS) * n1_ref[...]
        m = mod_ref[...]
        u = (y * (1.0 + m[1:2]) + m[0:1]).astype(BF16)
        u_scr[...] = u
        dt_ref[...] = _dot(u, wdt_ref[...])

    o_ref[...] = _dot(u_scr[...], w_ref[...]).astype(BF16)


def _in_projection(x_p, x_s, mod3, norm1_w, w_main, w_dt):
    tm, tn = INPROJ_TM, INPROJ_TN
    row = functools.partial(_mod_row_of_block, blocks_prompt=T_PROMPT // tm,
                            blocks_per_sample_seq=DEC_SEQ // tm)
    return pl.pallas_call(
        _inproj_kernel,
        out_shape=(jax.ShapeDtypeStruct((T_ALL, PROJ_W), BF16),
                   jax.ShapeDtypeStruct((T_ALL, LANES), F32)),
        grid=(T_ALL // tm, PROJ_W // tn),
        in_specs=[*_two_path_specs((tm, D_MODEL), T_PROMPT // tm),
                  pl.BlockSpec((None, N_MOD, D_MODEL), lambda i, j: (row(i), 0, 0)),
                  pl.BlockSpec((1, D_MODEL), lambda i, j: (0, 0)),
                  pl.BlockSpec((D_MODEL, tn), lambda i, j: (0, j)),
                  pl.BlockSpec((D_MODEL, LANES), lambda i, j: (0, 0))],
        out_specs=(pl.BlockSpec((tm, tn), lambda i, j: (i, j)),
                   pl.BlockSpec((tm, LANES), lambda i, j: (i, 0))),
        scratch_shapes=[pltpu.VMEM((tm, D_MODEL), BF16)],
        compiler_params=_cparams(("arbitrary", "arbitrary")),
        name="in_projection",
    )(x_p, x_s, mod3, norm1_w.reshape(1, D_MODEL), w_main, w_dt)


SSMCONV_TN = 1024
SSMCONV_RT = 64
SSMCONV_HALO = SUBLANES


def _ssmconv_kernel(x_ref, w_ref, b_ref, o_ref, pad_scr, out_scr):
    seq_len = x_ref.shape[0]
    half = D_CONV_SSM // 2
    n_half = SSMCONV_RT // 2
    zeros = jnp.zeros((SSMCONV_HALO, LANES), F32)
    for sl in range(SSMCONV_TN // LANES):
        cs = slice(sl * LANES, (sl + 1) * LANES)
        pad_scr[sl, 0:SSMCONV_HALO, :] = zeros
        pad_scr[sl, SSMCONV_HALO + seq_len:, :] = zeros
        pad_scr[sl, SSMCONV_HALO:SSMCONV_HALO + seq_len, :] = x_ref[:, cs].astype(F32)
        wk = [w_ref[k:k + 1, cs] for k in range(D_CONV_SSM)]
        bias = b_ref[:, cs]
        for r0 in range(0, seq_len, SSMCONV_RT):
            for phase in range(2):
                acc = bias
                for k in range(D_CONV_SSM):
                    acc = acc + wk[k] * pad_scr[sl, pl.ds(SSMCONV_HALO + r0 + phase + k - half, n_half, stride=2), :]
                out_scr[sl, pl.ds(r0 + phase, n_half, stride=2), :] = _silu(acc)
        o_ref[:, cs] = out_scr[sl].astype(BF16)


def _ssm_conv(proj, conv_w, conv_b, *, seq_len, n_seq, row_block0):
    tn = SSMCONV_TN
    return pl.pallas_call(
        _ssmconv_kernel,
        out_shape=jax.ShapeDtypeStruct((n_seq * seq_len, CONV_DIM), BF16),
        grid=(n_seq, CONV_DIM // tn),
        in_specs=[pl.BlockSpec((seq_len, tn), lambda b, j: (row_block0 + b, j)),
                  pl.BlockSpec((D_CONV_SSM, tn), lambda b, j: (0, j)),
                  pl.BlockSpec((1, tn), lambda b, j: (0, j))],
        out_specs=pl.BlockSpec((seq_len, tn), lambda b, j: (b, j)),
        scratch_shapes=[pltpu.VMEM((tn // LANES, seq_len + 2 * SSMCONV_HALO, LANES), F32),
                        pltpu.VMEM((tn // LANES, seq_len, LANES), F32)],
        compiler_params=_cparams(("arbitrary", "arbitrary")),
        name="ssm_conv",
    )(proj, conv_w, conv_b.reshape(1, CONV_DIM))


def _tri_masks():
    ii = lax.broadcasted_iota(I32, (CHUNK, CHUNK), 0)
    jj = lax.broadcasted_iota(I32, (CHUNK, CHUNK), 1)
    return ii, jj


def _chunk_decays(dt_ref, dtb_ref, a_ref):
    ii, jj = _tri_masks()
    pre = dt_ref[...] + dtb_ref[...]
    dt = jnp.maximum(pre, 0.0) + jnp.log(1.0 + jnp.exp(-jnp.abs(pre)))
    la = dt * a_ref[...]
    tri_lo = jnp.where(jj <= ii, 1.0, 0.0).astype(BF16)
    tri_up = jnp.where(jj >= ii, 1.0, 0.0).astype(BF16)
    cs_prefix = _dot_exact_lhs(tri_lo, la)
    cs_suffix = _dot_exact_lhs(tri_up, la)
    fwd_lane = lax.broadcasted_iota(I32, (CHUNK, LANES), 1) < N_HEADS
    cs = jnp.where(fwd_lane, cs_prefix, cs_suffix)
    tot = jnp.where(fwd_lane[0:1], cs_prefix[CHUNK - 1:CHUNK, :], cs_suffix[0:1, :])
    return dt, cs, tot


def _transpose_blocks(src, rows, cols):
    out_rows = []
    for cb in range(cols // LANES):
        pieces = [src[rb * LANES:(rb + 1) * LANES, cb * LANES:(cb + 1) * LANES].T
                  for rb in range(rows // LANES)]
        out_rows.append(jnp.concatenate(pieces, axis=1) if len(pieces) > 1 else pieces[0])
    return jnp.concatenate(out_rows, axis=0) if len(out_rows) > 1 else out_rows[0]


def _ssd_kernel(*refs, n_chunks, has_init, want_final):
    it = iter(refs)
    xbc_ref, dt_ref, z_ref = next(it), next(it), next(it)
    init_ref = next(it) if has_init else None
    dtb_ref, a_ref, dvec_ref, nw_ref, e2_ref = next(it), next(it), next(it), next(it), next(it)
    out_ref = next(it)
    fin_ref = next(it) if want_final else None
    sf_scr, sb_scr, df_scr, db_scr = next(it), next(it), next(it), next(it)

    phase = pl.program_id(1)
    c = pl.program_id(2)

    @pl.when(phase == 0)
    def _chunk_states():
        dt, cs, tot = _chunk_decays(dt_ref, dtb_ref, a_ref)
        w_in = dt * jnp.exp(tot - cs)
        pack = 2 * SUBLANES
        dec_hi, dec_lo = _split2(jnp.exp(jnp.broadcast_to(tot, (pack, LANES))))
        expanded = _dot(jnp.concatenate([w_in.astype(BF16), dec_hi, dec_lo], axis=0), e2_ref[...])
        w_exp = expanded[:CHUNK]
        dec_exp = expanded[CHUNK:CHUNK + SUBLANES] + expanded[CHUNK + pack:CHUNK + pack + SUBLANES]
        df_scr[c] = dec_exp[:, :D_INNER]
        db_scr[c] = dec_exp[:, D_INNER:]
        for g in range(N_GROUPS):
            lo = g * GROUP_W
            xg = xbc_ref[:, lo:lo + GROUP_W].astype(F32)
            xd_f = (xg * w_exp[:, lo:lo + GROUP_W]).astype(BF16)
            xd_b = (xg * w_exp[:, D_INNER + lo:D_INNER + lo + GROUP_W]).astype(BF16)
            bg = xbc_ref[:, D_INNER + g * D_STATE:D_INNER + (g + 1) * D_STATE]
            bg_t = bg.astype(F32).T.astype(BF16)
            sf_scr[c, :, lo:lo + GROUP_W] = _dot(bg_t, xd_f)
            sb_scr[c, :, lo:lo + GROUP_W] = _dot(bg_t, xd_b)

    @pl.when(jnp.logical_and(phase == 1, c == 0))
    def _recurrence():
        for g in range(N_GROUPS):
            lo = g * GROUP_W
            if has_init:
                prev_f = _transpose_blocks(init_ref[0, 0, lo:lo + GROUP_W, :], GROUP_W, D_STATE)
                prev_b = _transpose_blocks(init_ref[0, 1, lo:lo + GROUP_W, :], GROUP_W, D_STATE)
            else:
                prev_f = jnp.zeros((D_STATE, GROUP_W), F32)
                prev_b = jnp.zeros((D_STATE, GROUP_W), F32)
            for cc in range(n_chunks):
                s = sf_scr[cc, :, lo:lo + GROUP_W]
                sf_scr[cc, :, lo:lo + GROUP_W] = prev_f
                prev_f = df_scr[cc, 0:1, lo:lo + GROUP_W] * prev_f + s
            for cc in reversed(range(n_chunks)):
                s = sb_scr[cc, :, lo:lo + GROUP_W]
                sb_scr[cc, :, lo:lo + GROUP_W] = prev_b
                prev_b = db_scr[cc, 0:1, lo:lo + GROUP_W] * prev_b + s
            if want_final:
                fin_ref[0, 0, lo:lo + GROUP_W, :] = _transpose_blocks(prev_f, D_STATE, GROUP_W)
                fin_ref[0, 1, lo:lo + GROUP_W, :] = _transpose_blocks(prev_b, D_STATE, GROUP_W)

    @pl.when(phase == 1)
    def _outputs():
        ii, jj = _tri_masks()
        dt, cs, _ = _chunk_decays(dt_ref, dtb_ref, a_ref)
        out_dec = _dot(jnp.exp(cs).astype(BF16), e2_ref[...])
        cs_t = cs.T
        dt_t = dt.T
        causal = ii >= jj
        anti = jj >= ii
        neg = jnp.float32(-1e30)
        left = lax.broadcasted_iota(I32, (CHUNK, LANES), 1) < HEAD_DIM
        for g in range(N_GROUPS):
            lo = g * GROUP_W
            bg = xbc_ref[:, D_INNER + g * D_STATE:D_INNER + (g + 1) * D_STATE]
            cg = xbc_ref[:, D_INNER + N_GROUPS * D_STATE + g * D_STATE:
                         D_INNER + N_GROUPS * D_STATE + (g + 1) * D_STATE]
            cb = lax.dot_general(cg, bg, (((1,), (1,)), ((), ())), preferred_element_type=F32)
            pf = sf_scr[c, :, lo:lo + GROUP_W].astype(BF16)
            pb = sb_scr[c, :, lo:lo + GROUP_W].astype(BF16)
            y_off = (_dot(cg, pf) * out_dec[:, lo:lo + GROUP_W]
                     + _dot(cg, pb) * out_dec[:, D_INNER + lo:D_INNER + lo + GROUP_W])
            pairs = []
            for m in range(HEADS_PER_GROUP // 2):
                x_pair = xbc_ref[:, lo + m * LANES:lo + (m + 1) * LANES]
                ys = []
                for hh in range(2):
                    h = g * HEADS_PER_GROUP + 2 * m + hh
                    hb = N_HEADS + h
                    seg_f = jnp.where(causal, cs[:, h:h + 1] - cs_t[h:h + 1, :], neg)
                    seg_b = jnp.where(anti, cs[:, hb:hb + 1] - cs_t[hb:hb + 1, :], neg)
                    mix = (jnp.exp(seg_f) * dt_t[h:h + 1, :] + jnp.exp(seg_b) * dt_t[hb:hb + 1, :])
                    ys.append(_dot((cb * mix).astype(BF16), x_pair))
                pairs.append(jnp.where(left, ys[0], ys[1]))
            y_diag = jnp.concatenate(pairs, axis=1)
            xg = xbc_ref[:, lo:lo + GROUP_W].astype(F32)
            y = y_diag + y_off + dvec_ref[:, lo:lo + GROUP_W] * xg
            zg = z_ref[:, lo:lo + GROUP_W].astype(F32)
            y = y * _silu(zg)
            ms = jnp.mean(y * y, axis=-1, keepdims=True)
            out_ref[:, lo:lo + GROUP_W] = (y * lax.rsqrt(ms + EPS) * nw_ref[:, lo:lo + GROUP_W]).astype(BF16)


def _ssd(xbc, dt_raw, proj, init, dtb, a_neg, dvec, norm_w, e2, *, seq_len, n_seq, tok0, want_final):
    nc = seq_len // CHUNK
    blk0 = tok0 // CHUNK
    has_init = init is not None
    in_specs = [pl.BlockSpec((CHUNK, CONV_DIM), lambda b, p, c: (b * nc + c, 0)),
                pl.BlockSpec((CHUNK, LANES), lambda b, p, c: (blk0 + b * nc + c, 0)),
                pl.BlockSpec((CHUNK, D_INNER), lambda b, p, c: (blk0 + b * nc + c * p, COL_Z))]
    args = [xbc, dt_raw, proj]
    if has_init:
        in_specs.append(pl.BlockSpec((1, 2, D_INNER, D_STATE), lambda b, p, c: (b, 0, 0, 0)))
        args.append(init)
    const = lambda b, p, c: (0, 0)
    in_specs += [pl.BlockSpec((1, LANES), const), pl.BlockSpec((1, LANES), const),
                 pl.BlockSpec((1, D_INNER), const), pl.BlockSpec((1, D_INNER), const),
                 pl.BlockSpec((LANES, 2 * D_INNER), const)]
    args += [dtb, a_neg, dvec, norm_w, e2]
    out_shape = [jax.ShapeDtypeStruct((n_seq * seq_len, D_INNER), BF16)]
    out_specs = [pl.BlockSpec((CHUNK, D_INNER), lambda b, p, c: (b * nc + c * p, 0))]
    if want_final:
        out_shape.append(jax.ShapeDtypeStruct((n_seq, 2, D_INNER, D_STATE), F32))
        out_specs.append(pl.BlockSpec((1, 2, D_INNER, D_STATE), lambda b, p, c: (b, 0, 0, 0)))
    res = pl.pallas_call(
        functools.partial(_ssd_kernel, n_chunks=nc, has_init=has_init, want_final=want_final),
        out_shape=tuple(out_shape),
        grid=(n_seq, 2, nc),
        in_specs=in_specs,
        out_specs=tuple(out_specs),
        scratch_shapes=[pltpu.VMEM((nc, D_STATE, D_INNER), F32),
                        pltpu.VMEM((nc, D_STATE, D_INNER), F32),
                        pltpu.VMEM((nc, SUBLANES, D_INNER), F32),
                        pltpu.VMEM((nc, SUBLANES, D_INNER), F32)],
        compiler_params=_cparams(("arbitrary", "arbitrary", "arbitrary")),
        name="ssd_final" if want_final else "ssd_init",
    )(*args)
    return res


CONF_SLABS = D_MODEL // LANES
CONF_HALO = 2 * SUBLANES
CONF_RT = 64


def _glu_slab(cv_ref, cg_ref, sl):
    cs = slice(sl * LANES, (sl + 1) * LANES)
    return cv_ref[:, cs].astype(F32) * jax.nn.sigmoid(cg_ref[:, cs].astype(F32))


def _conv_taps_strided(src_ref, sl, src_row0, dst_ref, dst_row0, w_ref, b_ref):
    cs = slice(sl * LANES, (sl + 1) * LANES)
    n_half = CONF_RT // 2
    for phase in range(2):
        acc = b_ref[:, cs]
        for k in range(CONF_K):
            x = src_ref[sl, pl.ds(src_row0 + phase + k - CONF_PAD, n_half, stride=2), :]
            acc = acc + w_ref[k:k + 1, cs] * x
        dst_ref[sl, pl.ds(dst_row0 + phase, n_half, stride=2), :] = acc


def _layernorm_silu_slabs(acc_scr, lw_ref, lb_ref, o_ref):
    n = acc_scr.shape[1]
    tot = jnp.zeros((n, 1), F32)
    for sl in range(CONF_SLABS):
        tot = tot + jnp.sum(acc_scr[sl], axis=-1, keepdims=True)
    mu = tot * (1.0 / D_MODEL)
    sq = jnp.zeros((n, 1), F32)
    for sl in range(CONF_SLABS):
        d = acc_scr[sl] - mu
        sq = sq + jnp.sum(d * d, axis=-1, keepdims=True)
    rstd = lax.rsqrt(sq * (1.0 / D_MODEL) + EPS)
    for sl in range(CONF_SLABS):
        cs = slice(sl * LANES, (sl + 1) * LANES)
        o_ref[:, cs] = _silu((acc_scr[sl] - mu) * rstd * lw_ref[:, cs] + lb_ref[:, cs]).astype(BF16)


def _conf_seq_kernel(cv_ref, cg_ref, w_ref, b_ref, lw_ref, lb_ref, o_ref, pad_scr, acc_scr):
    seq_len = cv_ref.shape[0]
    zeros = jnp.zeros((CONF_HALO, LANES), F32)
    for sl in range(CONF_SLABS):
        pad_scr[sl, 0:CONF_HALO, :] = zeros
        pad_scr[sl, CONF_HALO + seq_len:, :] = zeros
        pad_scr[sl, CONF_HALO:CONF_HALO + seq_len, :] = _glu_slab(cv_ref, cg_ref, sl)
    for sl in range(CONF_SLABS):
        for r0 in range(0, seq_len, CONF_RT):
            _conv_taps_strided(pad_scr, sl, CONF_HALO + r0, acc_scr, r0, w_ref, b_ref)
    _layernorm_silu_slabs(acc_scr, lw_ref, lb_ref, o_ref)


def _conf_grid_kernel(cv_ref, cg_ref, w_ref, b_ref, lw_ref, lb_ref, o_ref, v_scr, pad_scr, acc_scr):
    half_slabs = CONF_SLABS // 2
    stride = GRID_W + 2 * CONF_HALO
    pad_scr[...] = jnp.zeros(pad_scr.shape, F32)
    for sl in range(half_slabs):
        v = _glu_slab(cv_ref, cg_ref, sl)
        for r in range(GRID_H):
            pad_scr[sl, r * stride + CONF_HALO:r * stride + CONF_HALO + GRID_W, :] = v[r * GRID_W:(r + 1) * GRID_W]
    for sl in range(half_slabs, CONF_SLABS):
        v_scr[sl - half_slabs] = _glu_slab(cv_ref, cg_ref, sl)
    for sl in range(half_slabs):
        for r in range(GRID_H):
            _conv_taps_strided(pad_scr, sl, r * stride + CONF_HALO, acc_scr, r * GRID_W, w_ref, b_ref)
    for sl in range(half_slabs, CONF_SLABS):
        cs = slice(sl * LANES, (sl + 1) * LANES)
        for r in range(GRID_H):
            acc = jnp.broadcast_to(b_ref[:, cs], (GRID_W, LANES))
            for r2 in range(GRID_H):
                k = r2 - r + CONF_PAD
                acc = acc + w_ref[k:k + 1, cs] * v_scr[sl - half_slabs, r2 * GRID_W:(r2 + 1) * GRID_W, :]
            acc_scr[sl, r * GRID_W:(r + 1) * GRID_W, :] = acc
    _layernorm_silu_slabs(acc_scr, lw_ref, lb_ref, o_ref)


def _conformer(proj, conv_w, conv_b, ln_w, ln_b, *, seq_len, n_seq, row_block0, grid_mode):
    if grid_mode:
        body = _conf_grid_kernel
        scratch = [pltpu.VMEM((CONF_SLABS // 2, seq_len, LANES), F32),
                   pltpu.VMEM((CONF_SLABS // 2, GRID_H * (GRID_W + 2 * CONF_HALO), LANES), F32),
                   pltpu.VMEM((CONF_SLABS, seq_len, LANES), F32)]
    else:
        body = _conf_seq_kernel
        scratch = [pltpu.VMEM((CONF_SLABS, seq_len + 2 * CONF_HALO, LANES), F32),
                   pltpu.VMEM((CONF_SLABS, seq_len, LANES), F32)]
    const = lambda b: (0, 0)
    return pl.pallas_call(
        body,
        out_shape=jax.ShapeDtypeStruct((n_seq * seq_len, D_MODEL), BF16),
        grid=(n_seq,),
        in_specs=[pl.BlockSpec((seq_len, D_MODEL), lambda b: (row_block0 + b, COL_CV)),
                  pl.BlockSpec((seq_len, D_MODEL), lambda b: (row_block0 + b, COL_CG)),
                  pl.BlockSpec((CONF_K, D_MODEL), const),
                  pl.BlockSpec((1, D_MODEL), const),
                  pl.BlockSpec((1, D_MODEL), const),
                  pl.BlockSpec((1, D_MODEL), const)],
        out_specs=pl.BlockSpec((seq_len, D_MODEL), lambda b: (b, 0)),
        scratch_shapes=scratch,
        compiler_params=_cparams(("arbitrary",)),
        name="conformer_grid" if grid_mode else "conformer_seq",
    )(proj, proj, conv_w, conv_b.reshape(1, D_MODEL), ln_w.reshape(1, D_MODEL), ln_b.reshape(1, D_MODEL))


MERGE_TM = 256


def _merge_kernel(yap_ref, yas_ref, ybp_ref, ybs_ref, gl_ref, xp_ref, xs_ref, mod_ref, wa_ref, wb_ref, wo_ref,
                  n2_ref, rw_ref, sgu_ref, sd_ref, pre_ref, v_ref, s_ref):
    npb = T_PROMPT // MERGE_TM
    y_a = _dot(_pick_path(npb, yap_ref, yas_ref), wa_ref[...])
    y_b = _dot(_pick_path(npb, ybp_ref, ybs_ref), wb_ref[...])
    gates = jax.nn.sigmoid(gl_ref[...].astype(F32))
    mix = gates[:, :D_MODEL] * y_a + gates[:, D_MODEL:] * y_b
    out = _dot(mix.astype(BF16), wo_ref[...])
    m = mod_ref[...]
    x1 = _pick_path(npb, xp_ref, xs_ref) + m[2:3] * out
    ms = jnp.mean(x1 * x1, axis=-1, keepdims=True)
    v = x1 * lax.rsqrt(ms + EPS) * n2_ref[...] * (1.0 + m[4:5]) + m[3:4]
    v_ref[...] = v
    s_ref[...] = jax.nn.sigmoid(_dot_f32(v, rw_ref[...]))
    hgu = _dot(v.astype(BF16), sgu_ref[...])
    act = _silu(hgu[:, :D_SHARED]) * hgu[:, D_SHARED:]
    shared = _dot(act.astype(BF16), sd_ref[...])
    pre_ref[...] = x1 + m[5:6] * shared


def _merge(y_ssm_p, y_ssm_s, v_conf_p, v_conf_s, proj, x_p, x_s, mod3, w_out_ssm, w_out_conf, w_o, norm2_w,
           router_w, sh_gu, sh_d):
    tm = MERGE_TM
    npb = T_PROMPT // tm
    row = functools.partial(_mod_row_of_block, blocks_prompt=npb, blocks_per_sample_seq=DEC_SEQ // tm)
    const = lambda i: (0, 0)
    return pl.pallas_call(
        _merge_kernel,
        out_shape=(jax.ShapeDtypeStruct((T_ALL, D_MODEL), F32),
                   jax.ShapeDtypeStruct((T_ALL, D_MODEL), F32),
                   jax.ShapeDtypeStruct((T_ALL, N_EXPERTS), F32)),
        grid=(T_ALL // tm,),
        in_specs=[*_two_path_specs((tm, D_INNER), npb),
                  *_two_path_specs((tm, D_MODEL), npb),
                  pl.BlockSpec((tm, 2 * D_MODEL), lambda i: (i, COL_GL)),
                  *_two_path_specs((tm, D_MODEL), npb),
                  pl.BlockSpec((None, N_MOD, D_MODEL), lambda i: (row(i), 0, 0)),
                  pl.BlockSpec((D_INNER, D_MODEL), const),
                  pl.BlockSpec((D_MODEL, D_MODEL), const),
                  pl.BlockSpec((D_MODEL, D_MODEL), const),
                  pl.BlockSpec((1, D_MODEL), const),
                  pl.BlockSpec((D_MODEL, N_EXPERTS), const),
                  pl.BlockSpec((D_MODEL, 2 * D_SHARED), const),
                  pl.BlockSpec((D_SHARED, D_MODEL), const)],
        out_specs=(pl.BlockSpec((tm, D_MODEL), lambda i: (i, 0)),
                   pl.BlockSpec((tm, D_MODEL), lambda i: (i, 0)),
                   pl.BlockSpec((tm, N_EXPERTS), lambda i: (i, 0))),
        compiler_params=_cparams(("arbitrary",)),
        name="merge",
    )(y_ssm_p, y_ssm_s, v_conf_p, v_conf_s, proj, x_p, x_s, mod3, w_out_ssm, w_out_conf, w_o,
      norm2_w.reshape(1, D_MODEL), router_w, sh_gu, sh_d)


ROUTE_TB = 256


def _first_index_of_max(vals, iota, sentinel):
    m = jnp.max(vals, axis=0, keepdims=True)
    idx = jnp.min(jnp.where(vals == m, iota, jnp.float32(sentinel)), axis=0, keepdims=True)
    return m, idx


def _route_kernel(s_ref, bias_ref, idx_ref, w_ref, rank_ref, cnt_ref, run_scr):
    i = pl.program_id(0)
    tb = ROUTE_TB
    neg = jnp.float32(-jnp.inf)

    @pl.when(i == 0)
    def _():
        run_scr[...] = jnp.zeros(run_scr.shape, F32)

    s_t = s_ref[...].T
    sb_t = s_t + bias_ref[...]
    eiota = lax.broadcasted_iota(I32, (N_EXPERTS, tb), 0).astype(F32)

    liota = lax.broadcasted_iota(I32, (EXPERTS_PER_GROUP, tb), 0).astype(F32)
    gscores = []
    for g in range(N_EXPERT_GROUPS):
        blk = sb_t[g * EXPERTS_PER_GROUP:(g + 1) * EXPERTS_PER_GROUP, :]
        m1, i1 = _first_index_of_max(blk, liota, EXPERTS_PER_GROUP)
        m2 = jnp.max(jnp.where(liota == i1, neg, blk), axis=0, keepdims=True)
        gscores.append(m1 + m2)
    gs = jnp.concatenate(gscores, axis=0)
    giota = lax.broadcasted_iota(I32, (N_EXPERT_GROUPS, tb), 0).astype(F32)
    gsel = jnp.zeros((N_EXPERT_GROUPS, tb), F32)
    for _ in range(TOPK_GROUPS):
        _, gi = _first_index_of_max(gs, giota, N_EXPERT_GROUPS)
        hit = giota == gi
        gsel = jnp.where(hit, 1.0, gsel)
        gs = jnp.where(hit, neg, gs)
    emask = jnp.concatenate(
        [jnp.broadcast_to(gsel[g:g + 1, :], (EXPERTS_PER_GROUP, tb)) for g in range(N_EXPERT_GROUPS)], axis=0)
    masked = jnp.where(emask > 0.5, sb_t, neg)

    onehots, idxs, wts = [], [], []
    for _ in range(TOP_K):
        _, ei = _first_index_of_max(masked, eiota, N_EXPERTS)
        hit = eiota == ei
        onehots.append(hit)
        idxs.append(ei)
        wts.append(jnp.sum(jnp.where(hit, s_t, 0.0), axis=0, keepdims=True))
        masked = jnp.where(hit, neg, masked)
    w = jnp.concatenate(wts, axis=0)
    w_ref[...] = w / jnp.sum(w, axis=0, keepdims=True) * ROUTED_SCALE
    idx_ref[...] = jnp.concatenate(idxs, axis=0).astype(I32)

    assign = jnp.zeros((N_EXPERTS, tb), F32)
    for hit in onehots:
        assign = jnp.where(hit, 1.0, assign)
    assign_b = assign.astype(BF16)
    ti = lax.broadcasted_iota(I32, (tb, tb), 0)
    tj = lax.broadcasted_iota(I32, (tb, tb), 1)
    before = jnp.where(ti < tj, 1.0, 0.0).astype(BF16)
    within = _dot(assign_b, before)
    run = run_scr[...]
    total = within + jnp.concatenate([run] * (tb // LANES), axis=1)
    rank_ref[...] = jnp.concatenate(
        [jnp.sum(jnp.where(hit, total, 0.0), axis=0, keepdims=True) for hit in onehots], axis=0).astype(I32)
    new_run = run + _dot(assign_b, jnp.ones((tb, LANES), BF16))
    run_scr[...] = new_run
    cnt_ref[...] = new_run


def _route(scores, router_bias):
    tb = ROUTE_TB
    return pl.pallas_call(
        _route_kernel,
        out_shape=(jax.ShapeDtypeStruct((TOP_K, T_ALL), I32),
                   jax.ShapeDtypeStruct((TOP_K, T_ALL), F32),
                   jax.ShapeDtypeStruct((TOP_K, T_ALL), I32),
                   jax.ShapeDtypeStruct((N_EXPERTS, LANES), F32)),
        grid=(T_ALL // tb,),
        in_specs=[pl.BlockSpec((tb, N_EXPERTS), lambda i: (i, 0)),
                  pl.BlockSpec((N_EXPERTS, 1), lambda i: (0, 0))],
        out_specs=(pl.BlockSpec((TOP_K, tb), lambda i: (0, i)),
                   pl.BlockSpec((TOP_K, tb), lambda i: (0, i)),
                   pl.BlockSpec((TOP_K, tb), lambda i: (0, i)),
                   pl.BlockSpec((N_EXPERTS, LANES), lambda i: (0, 0))),
        scratch_shapes=[pltpu.VMEM((N_EXPERTS, LANES), F32)],
        compiler_params=_cparams(("arbitrary",)),
        name="route",
    )(scores, router_bias.reshape(N_EXPERTS, 1))


def _dest_kernel(idx_ref, rank_ref, start_ref, dest_ref):
    tb = idx_ref.shape[1]
    eiota = lax.broadcasted_iota(I32, (N_EXPERTS, tb), 0)
    start = jnp.broadcast_to(start_ref[...], (N_EXPERTS, tb))
    idx = idx_ref[...]
    rows = [jnp.sum(jnp.where(eiota == idx[k:k + 1, :], start, 0.0), axis=0, keepdims=True)
            for k in range(TOP_K)]
    dest_ref[...] = jnp.concatenate(rows, axis=0).astype(I32) + rank_ref[...]


def _dest_slots(idx_t, rank_t, group_start):
    tb = 512
    return pl.pallas_call(
        _dest_kernel,
        out_shape=jax.ShapeDtypeStruct((TOP_K, T_ALL), I32),
        grid=(T_ALL // tb,),
        in_specs=[pl.BlockSpec((TOP_K, tb), lambda i: (0, i)),
                  pl.BlockSpec((TOP_K, tb), lambda i: (0, i)),
                  pl.BlockSpec((N_EXPERTS, 1), lambda i: (0, 0))],
        out_specs=pl.BlockSpec((TOP_K, tb), lambda i: (0, i)),
        compiler_params=_cparams(("arbitrary",)),
        name="dest_slots",
    )(idx_t, rank_t, group_start.astype(F32).reshape(N_EXPERTS, 1))


DISPATCH_TB = 256


HI_MASK = -65536


def _pack_pair(hi, lo):
    hb = pltpu.bitcast(hi.astype(BF16).astype(F32), I32)
    lb = pltpu.bitcast(lo.astype(BF16).astype(F32), I32)
    return jnp.bitwise_or(hb, lax.shift_right_logical(lb, 16))


def _unpack_pair(word):
    hi = pltpu.bitcast(jnp.bitwise_and(word, HI_MASK), F32)
    lo = pltpu.bitcast(lax.shift_left(word, 16), F32)
    return hi, lo


def _rows_to_tiles(dst_scr, base, rows):
    n = rows.shape[0]
    for s in range(ROW_TILE):
        dst_scr[pl.ds(base + s, n, stride=ROW_TILE), :] = _pack_pair(
            rows[:, s * LANES:(s + 1) * LANES], rows[:, HALF_D + s * LANES:HALF_D + (s + 1) * LANES])


def _tile_column(src_scr, base, n, s):
    return src_scr[pl.ds(base + s, n, stride=ROW_TILE), :]


def _tiles_to_rows(src_scr, base, n, dtype):
    pairs = [_unpack_pair(_tile_column(src_scr, base, n, s)) for s in range(ROW_TILE)]
    return jnp.concatenate([p[0].astype(dtype) for p in pairs] + [p[1].astype(dtype) for p in pairs], axis=1)


def _row_tile(ref, row):
    return ref.at[pl.ds(pl.multiple_of(row * ROW_TILE, ROW_TILE), ROW_TILE), :]


def _dispatch_kernel(dest_ref, v_ref, xs_hbm, tile_scr, zero_scr, sem, zsem):
    i = pl.program_id(0)
    n = pl.num_programs(0)
    tb = DISPATCH_TB
    slot = i % 2
    base = pl.multiple_of(slot * (tb * ROW_TILE), ROW_TILE)
    _rows_to_tiles(tile_scr, base, v_ref[...])

    def issue(t, carry):
        src = _row_tile(tile_scr, slot * tb + t)
        for k in range(TOP_K):
            pltpu.make_async_copy(src, _row_tile(xs_hbm, dest_ref[k, t]), sem.at[slot]).start(priority=k % 2)
        return carry

    lax.fori_loop(0, tb, issue, 0)

    def wait_block(sl):
        blk = tile_scr.at[pl.ds(pl.multiple_of(sl * (tb * ROW_TILE), ROW_TILE), tb * ROW_TILE), :]
        for _ in range(TOP_K):
            pltpu.make_async_copy(blk, blk, sem.at[sl]).wait()

    @pl.when(i == 0)
    def _():
        zero_scr[...] = jnp.zeros(zero_scr.shape, I32)
        for j in range(GMM_TM // SUBLANES):
            cp = pltpu.make_async_copy(
                zero_scr, xs_hbm.at[pl.ds((N_ASSIGN + j * SUBLANES) * ROW_TILE, SUBLANES * ROW_TILE), :], zsem)
            cp.start()
            cp.wait()

    @pl.when(i > 0)
    def _():
        wait_block(1 - slot)

    @pl.when(i == n - 1)
    def _():
        wait_block(slot)


def _dispatch(dest_t, v_all):
    tb = DISPATCH_TB
    return pl.pallas_call(
        _dispatch_kernel,
        out_shape=jax.ShapeDtypeStruct((N_ROWS * ROW_TILE, LANES), I32),
        grid=(T_ALL // tb,),
        in_specs=[pl.BlockSpec((TOP_K, tb), lambda i: (0, i), memory_space=pltpu.SMEM),
                  pl.BlockSpec((tb, D_MODEL), lambda i: (i, 0))],
        out_specs=pl.BlockSpec(memory_space=pl.ANY),
        scratch_shapes=[pltpu.VMEM((2 * tb * ROW_TILE, LANES), I32),
                        pltpu.VMEM((SUBLANES * ROW_TILE, LANES), I32),
                        pltpu.SemaphoreType.DMA((2,)),
                        pltpu.SemaphoreType.DMA(())],
        compiler_params=_cparams(("arbitrary",)),
        name="dispatch",
    )(dest_t, v_all)


def _flat_rows(ref, row, n_rows):
    return ref.at[pl.ds(pl.multiple_of(row * ROW_TILE, ROW_TILE), n_rows * ROW_TILE), :]


def _tile_writes(y_scr, y_hbm, slot, row0, valid, sem):
    base = slot * GMM_TM
    parts = [(valid == GMM_TM, pltpu.make_async_copy(_flat_rows(y_scr, base, GMM_TM),
                                                     _flat_rows(y_hbm, row0, GMM_TM), sem))]
    off = jnp.int32(0)
    for p in TAIL_SIZES:
        bit = (valid & p) != 0
        take = jnp.logical_and(valid < GMM_TM, bit)
        parts.append((take, pltpu.make_async_copy(_flat_rows(y_scr, base + off, p),
                                                  _flat_rows(y_hbm, row0 + off, p), sem)))
        off = off + jnp.where(bit, p, 0)
    return parts


Y_SLOTS = 4
ST_TILES, ST_ROW = 0, 1
ST_VALID = ST_ROW + Y_SLOTS
ST_FETCH_E = ST_VALID + Y_SLOTS
ST_FETCH_I, ST_FETCH_G, ST_SIZE = ST_FETCH_E + 1, ST_FETCH_E + 2, ST_FETCH_E + 3
X_SLOTS = 6
X_AHEAD = X_SLOTS - 1


def _gmm_kernel(start_ref, xs_hbm, wg_ref, wu_ref, wd_ref, y_hbm,
                x_scr, y_scr, wgu_scr, wdn_scr, zero_scr, st_ref, xsem, ysem, zsem):
    e = pl.program_id(0)
    row_lo = start_ref[e]
    row_hi = start_ref[e + 1]
    n_rows = row_hi - row_lo
    n_tiles = lax.shift_right_logical(n_rows + (GMM_TM - 1), GMM_TM.bit_length() - 1)

    def x_copy(row, sl):
        return pltpu.make_async_copy(_flat_rows(xs_hbm, row, GMM_TM),
                                     _flat_rows(x_scr, sl * GMM_TM, GMM_TM), xsem.at[sl])

    def tiles_of(ex):
        rows = start_ref[ex + 1] - start_ref[ex]
        return lax.shift_right_logical(rows + (GMM_TM - 1), GMM_TM.bit_length() - 1)

    def skip_empty(ex):
        def empty(q):
            qc = jnp.minimum(q, N_EXPERTS - 1)
            return jnp.logical_and(q < N_EXPERTS, start_ref[qc + 1] == start_ref[qc])
        return lax.while_loop(empty, lambda q: q + 1, ex)

    def fetch_next_tile():
        pe = st_ref[ST_FETCH_E]

        @pl.when(pe < N_EXPERTS)
        def _():
            pi = st_ref[ST_FETCH_I]
            pg = st_ref[ST_FETCH_G]
            pec = jnp.minimum(pe, N_EXPERTS - 1)
            x_copy(start_ref[pec] + pi * GMM_TM, pg % X_SLOTS).start()
            last = pi + 1 >= tiles_of(pec)
            st_ref[ST_FETCH_E] = jnp.where(last, skip_empty(pe + 1), pe)
            st_ref[ST_FETCH_I] = jnp.where(last, 0, pi + 1)
            st_ref[ST_FETCH_G] = pg + 1

    def wait_writes(sl):
        @pl.when(st_ref[ST_VALID + sl] > 0)
        def _():
            for pred, cp in _tile_writes(y_scr, y_hbm, sl, st_ref[ST_ROW + sl], st_ref[ST_VALID + sl],
                                         ysem.at[sl]):
                @pl.when(pred)
                def _():
                    cp.wait()
            st_ref[ST_VALID + sl] = 0

    @pl.when(e == 0)
    def _():
        for j in range(ST_SIZE):
            st_ref[j] = 0
        st_ref[ST_FETCH_E] = skip_empty(jnp.int32(0))
        for _ in range(X_AHEAD):
            fetch_next_tile()

    done = st_ref[ST_TILES]

    @pl.when(n_tiles > 0)
    def _():
        wgu_scr[:, :D_EXPERT] = wg_ref[0].astype(BF16)
        wgu_scr[:, D_EXPERT:] = wu_ref[0].astype(BF16)
        wdn_scr[...] = wd_ref[0].astype(BF16)

        def tile(i, carry):
            g = done + i
            slot = g % Y_SLOTS
            xslot = g % X_SLOTS
            row0 = row_lo + i * GMM_TM
            valid = jnp.minimum(n_rows - i * GMM_TM, GMM_TM)
            x_copy(row0, xslot).wait()
            fetch_next_tile()

            base = pl.multiple_of(slot * (GMM_TM * ROW_TILE), ROW_TILE)
            xbase = pl.multiple_of(xslot * (GMM_TM * ROW_TILE), ROW_TILE)
            x = _tiles_to_rows(x_scr, xbase, GMM_TM, BF16)
            h = _dot(x, wgu_scr[...])
            act = (_silu(h[:, :D_EXPERT]) * h[:, D_EXPERT:]).astype(BF16)
            y = _dot(act, wdn_scr[...])

            wait_writes(slot)
            _rows_to_tiles(y_scr, base, y)
            for pred, cp in _tile_writes(y_scr, y_hbm, slot, row0, valid, ysem.at[slot]):
                @pl.when(pred)
                def _():
                    cp.start()
            st_ref[ST_ROW + slot] = row0
            st_ref[ST_VALID + slot] = valid
            return carry

        lax.fori_loop(0, n_tiles, tile, 0)
        st_ref[ST_TILES] = done + n_tiles

    @pl.when(e == pl.num_programs(0) - 1)
    def _():
        for sl in range(Y_SLOTS):
            wait_writes(sl)
        zero_scr[...] = jnp.zeros(zero_scr.shape, I32)
        for j in range(GMM_TM // SUBLANES):
            cp = pltpu.make_async_copy(zero_scr, _flat_rows(y_hbm, N_ASSIGN + j * SUBLANES, SUBLANES), zsem)
            cp.start()
            cp.wait()


def _grouped_mlp(group_start, xs, w_gate, w_up, w_down):
    grid_spec = pltpu.PrefetchScalarGridSpec(
        num_scalar_prefetch=1,
        grid=(N_EXPERTS,),
        in_specs=[pl.BlockSpec(memory_space=pl.ANY),
                  pl.BlockSpec((1, D_MODEL, D_EXPERT), lambda e, st: (e, 0, 0)),
                  pl.BlockSpec((1, D_MODEL, D_EXPERT), lambda e, st: (e, 0, 0)),
                  pl.BlockSpec((1, D_EXPERT, D_MODEL), lambda e, st: (e, 0, 0))],
        out_specs=pl.BlockSpec(memory_space=pl.ANY),
        scratch_shapes=[pltpu.VMEM((X_SLOTS * GMM_TM * ROW_TILE, LANES), I32),
                        pltpu.VMEM((Y_SLOTS * GMM_TM * ROW_TILE, LANES), I32),
                        pltpu.VMEM((D_MODEL, 2 * D_EXPERT), BF16),
                        pltpu.VMEM((D_EXPERT, D_MODEL), BF16),
                        pltpu.VMEM((SUBLANES * ROW_TILE, LANES), I32),
                        pltpu.SMEM((ST_SIZE,), I32),
                        pltpu.SemaphoreType.DMA((X_SLOTS,)),
                        pltpu.SemaphoreType.DMA((Y_SLOTS,)),
                        pltpu.SemaphoreType.DMA(())],
    )
    return pl.pallas_call(
        _gmm_kernel,
        out_shape=jax.ShapeDtypeStruct((N_ROWS * ROW_TILE, LANES), I32),
        grid_spec=grid_spec,
        compiler_params=_cparams(("arbitrary",)),
        name="grouped_mlp",
    )(group_start, xs, w_gate, w_up, w_down)


COMBINE_TB = 256


def _combine_kernel(dest_ref, dest_next_ref, y_hbm, pre_ref, w_ref, mod_ref, fw_ref, op_ref, os_ref, buf, x2_scr,
                    sem):
    i = pl.program_id(0)
    n = pl.num_programs(0)
    tb = COMBINE_TB
    slot = i % 2
    blk_rows = TOP_K * tb

    def issue_block(d_ref, sl):
        def issue(t, carry):
            for k in range(TOP_K):
                pltpu.make_async_copy(_row_tile(y_hbm, d_ref[k, t]),
                                      _row_tile(buf, sl * blk_rows + k * tb + t),
                                      sem.at[sl]).start(priority=k % 2)
            return carry

        lax.fori_loop(0, tb, issue, 0)

    @pl.when(i == 0)
    def _():
        issue_block(dest_ref, 0)

    @pl.when(i + 1 < n)
    def _():
        issue_block(dest_next_ref, 1 - slot)

    whole = _flat_rows(buf, slot * blk_rows, blk_rows)
    pltpu.make_async_copy(whole, whole, sem.at[slot]).wait()

    base = pl.multiple_of(slot * (blk_rows * ROW_TILE), ROW_TILE)
    w = w_ref[...]
    m = mod_ref[...]
    ssq = jnp.zeros((tb, 1), F32)
    for s in range(ROW_TILE):
        routed = [None, None]
        for k in range(TOP_K):
            halves = _unpack_pair(_tile_column(buf, base + k * tb * ROW_TILE, tb, s))
            for j in range(2):
                term = w[:, k:k + 1] * halves[j]
                routed[j] = term if routed[j] is None else routed[j] + term
        for j in range(2):
            cols = slice(j * HALF_D + s * LANES, j * HALF_D + (s + 1) * LANES)
            x2 = pre_ref[:, cols] + m[5:6, cols] * routed[j]
            x2_scr[:, cols] = x2
            ssq = ssq + jnp.sum(x2 * x2, axis=-1, keepdims=True)
    scale = lax.rsqrt(ssq * (1.0 / D_MODEL) + EPS)

    @pl.when(i < T_PROMPT // tb)
    def _():
        op_ref[...] = x2_scr[...] * scale * fw_ref[...]

    @pl.when(i >= T_PROMPT // tb)
    def _():
        os_ref[...] = x2_scr[...] * scale * fw_ref[...]


def _combine(dest_t, y_rows, pre, w_tok, mod3, final_norm_w):
    tb = COMBINE_TB
    n_blocks = T_ALL // tb
    npb = T_PROMPT // tb
    row = functools.partial(_mod_row_of_block, blocks_prompt=npb, blocks_per_sample_seq=DEC_SEQ // tb)
    return pl.pallas_call(
        _combine_kernel,
        out_shape=(jax.ShapeDtypeStruct((T_PROMPT, D_MODEL), F32),
                   jax.ShapeDtypeStruct((T_SAMPLE, D_MODEL), F32)),
        grid=(n_blocks,),
        in_specs=[pl.BlockSpec((TOP_K, tb), lambda i: (0, i), memory_space=pltpu.SMEM),
                  pl.BlockSpec((TOP_K, tb), lambda i: (0, jnp.minimum(i + 1, n_blocks - 1)),
                               memory_space=pltpu.SMEM),
                  pl.BlockSpec(memory_space=pl.ANY),
                  pl.BlockSpec((tb, D_MODEL), lambda i: (i, 0)),
                  pl.BlockSpec((tb, TOP_K), lambda i: (i, 0)),
                  pl.BlockSpec((None, N_MOD, D_MODEL), lambda i: (row(i), 0, 0)),
                  pl.BlockSpec((1, D_MODEL), lambda i: (0, 0))],
        out_specs=_two_path_specs((tb, D_MODEL), npb),
        scratch_shapes=[pltpu.VMEM((2 * TOP_K * tb * ROW_TILE, LANES), I32),
                        pltpu.VMEM((tb, D_MODEL), F32),
                        pltpu.SemaphoreType.DMA((2,))],
        compiler_params=_cparams(("arbitrary",)),
        name="combine",
    )(dest_t, dest_t, y_rows, pre, w_tok, mod3, final_norm_w.reshape(1, D_MODEL))


def _group_starts(counts):
    return jnp.concatenate([jnp.zeros((1,), I32), jnp.cumsum(counts).astype(I32)])


def _head_expand_matrix():
    r = jnp.arange(LANES)[:, None]
    cidx = jnp.arange(2 * D_INNER)[None, :]
    direction = cidx // D_INNER
    head = (cidx % D_INNER) // HEAD_DIM
    return (r == direction * N_HEADS + head).astype(BF16)


def kernel(x_prompt, x_sample, state_ssm, c, c_ctx, norm1_w, norm2_w, w_mod, b_mod, w_in, ssm_conv_w, ssm_conv_b, ssm_dt_bias, ssm_a_log, ssm_d, ssm_norm_w, w_out_ssm, conf_conv_w, conf_conv_b, conf_ln_w, conf_ln_b, w_out_conf, w_o, router_w, router_bias, exp_w_gate, exp_w_up, exp_w_down, sh_w_gate, sh_w_up, sh_w_down, final_norm_w):
    x_p = x_prompt.reshape(T_PROMPT, D_MODEL)
    x_s = x_sample.reshape(T_SAMPLE, D_MODEL)

    cc = jnp.zeros((MOD_ROWS, D_MODEL), F32).at[:DEC_BATCH].set(c).at[CTX_ROW].set(c_ctx)
    mod3 = _modulation(cc, w_mod[0], b_mod[0]).reshape(MOD_ROWS, N_MOD, D_MODEL)

    w = w_in[0]
    o_xbc, o_dt, o_cv, o_cg, o_gl = D_INNER, D_INNER + CONV_DIM, D_INNER + CONV_DIM + 2 * N_HEADS, \
        D_INNER + CONV_DIM + 2 * N_HEADS + D_MODEL, D_INNER + CONV_DIM + 2 * N_HEADS + 2 * D_MODEL
    w_main = jnp.concatenate([w[:, o_xbc:o_dt], w[:, :o_xbc], w[:, o_gl:], w[:, o_cv:o_cg], w[:, o_cg:o_gl]],
                             axis=1).astype(BF16)
    w_dt = jnp.pad(w[:, o_dt:o_cv], ((0, 0), (0, LANES - 2 * N_HEADS))).astype(BF16)
    proj, dt_raw = _in_projection(x_p, x_s, mod3, norm1_w[0], w_main, w_dt)

    pad_heads = lambda v: jnp.pad(v.reshape(1, 2 * N_HEADS), ((0, 0), (0, LANES - 2 * N_HEADS)))
    dtb = pad_heads(ssm_dt_bias[0])
    a_neg = pad_heads(-jnp.exp(ssm_a_log[0]))
    dvec = jnp.repeat(ssm_d[0], HEAD_DIM).reshape(1, D_INNER)
    nw = ssm_norm_w[0].reshape(1, D_INNER)
    e2 = _head_expand_matrix()
    xbc_p = _ssm_conv(proj, ssm_conv_w[0], ssm_conv_b[0], seq_len=SEQ, n_seq=BATCH, row_block0=0)
    xbc_s = _ssm_conv(proj, ssm_conv_w[0], ssm_conv_b[0], seq_len=DEC_SEQ, n_seq=DEC_BATCH,
                      row_block0=T_PROMPT // DEC_SEQ)
    y_p, fin = _ssd(xbc_p, dt_raw, proj, None, dtb, a_neg, dvec, nw, e2,
                    seq_len=SEQ, n_seq=BATCH, tok0=0, want_final=True)
    init = state_ssm.reshape(DEC_BATCH, 2, D_INNER, D_STATE)
    (y_s,) = _ssd(xbc_s, dt_raw, proj, init, dtb, a_neg, dvec, nw, e2,
                  seq_len=DEC_SEQ, n_seq=DEC_BATCH, tok0=T_PROMPT, want_final=False)

    v_p = _conformer(proj, conf_conv_w[0], conf_conv_b[0], conf_ln_w[0], conf_ln_b[0],
                     seq_len=SEQ, n_seq=BATCH, row_block0=0, grid_mode=False)
    v_s = _conformer(proj, conf_conv_w[0], conf_conv_b[0], conf_ln_w[0], conf_ln_b[0],
                     seq_len=DEC_SEQ, n_seq=DEC_BATCH, row_block0=T_PROMPT // DEC_SEQ, grid_mode=True)

    sh_gu = jnp.concatenate([sh_w_gate[0], sh_w_up[0]], axis=1).astype(BF16)
    pre, v_all, scores = _merge(y_p, y_s, v_p, v_s, proj, x_p, x_s, mod3, w_out_ssm[0].astype(BF16),
                                w_out_conf[0].astype(BF16), w_o[0].astype(BF16), norm2_w[0], router_w[0],
                                sh_gu, sh_w_down[0].astype(BF16))

    idx_t, w_t, rank_t, cnt = _route(scores, router_bias[0])
    start = _group_starts(cnt[:, 0].astype(I32))
    dest_t = _dest_slots(idx_t, rank_t, start[:N_EXPERTS])

    xs = _dispatch(dest_t, v_all)
    y_rows = _grouped_mlp(start, xs, exp_w_gate[0], exp_w_up[0], exp_w_down[0])
    out_p, out_s = _combine(dest_t, y_rows, pre, w_t.T, mod3, final_norm_w)

    y_prompt = out_p.reshape(BATCH, SEQ, D_MODEL)
    y_sample = out_s.reshape(DEC_BATCH, DEC_SEQ, D_MODEL)
    new_state = fin.reshape(BATCH, 1, 2, N_HEADS, HEAD_DIM, D_STATE)
    return (y_prompt, y_sample, new_state)
```

```python
import functools

import jax
import jax.numpy as jnp
from jax import lax
from jax.experimental import pallas as pl
from jax.experimental.pallas import tpu as pltpu

F32 = jnp.float32
BF16 = jnp.bfloat16
I32 = jnp.int32

D_MODEL = 1024
BATCH = 32
SEQ = 256
DEC_BATCH = 8
DEC_SEQ = 1024
GRID_W = 64
GRID_H = DEC_SEQ // GRID_W
D_INNER = 2048
HEAD_DIM = 64
N_HEADS = 32
D_STATE = 128
N_GROUPS = 8
HEADS_PER_GROUP = N_HEADS // N_GROUPS
GROUP_W = HEADS_PER_GROUP * HEAD_DIM
D_CONV_SSM = 5
CHUNK = 128
CONV_DIM = D_INNER + 2 * N_GROUPS * D_STATE
CONF_K = 31
CONF_PAD = CONF_K // 2
N_EXPERTS = 256
TOP_K = 8
N_EXPERT_GROUPS = 8
EXPERTS_PER_GROUP = N_EXPERTS // N_EXPERT_GROUPS
TOPK_GROUPS = 4
D_EXPERT = 256
D_SHARED = 256
ROUTED_SCALE = 2.5
N_MOD = 6
EPS = 1e-6

T_PROMPT = BATCH * SEQ
T_SAMPLE = DEC_BATCH * DEC_SEQ
T_ALL = T_PROMPT + T_SAMPLE
N_ASSIGN = T_ALL * TOP_K
MOD_ROWS = 16
CTX_ROW = DEC_BATCH

SUBLANES = 8
LANES = 128
VMEM_LIMIT = 56 * 1024 * 1024

PROJ_W = CONV_DIM + D_INNER + 2 * D_MODEL + 2 * D_MODEL
COL_Z = CONV_DIM // D_INNER
COL_GL = COL_Z + 1
COL_CV = (CONV_DIM + 2 * D_INNER) // D_MODEL
COL_CG = COL_CV + 1

HALF_D = D_MODEL // 2
ROW_TILE = HALF_D // LANES
GMM_TM = 256
N_ROWS = N_ASSIGN + GMM_TM
TAIL_SIZES = (128, 64, 32, 16, 8, 4, 2, 1)


def _cparams(sem, vmem=VMEM_LIMIT):
    return pltpu.CompilerParams(dimension_semantics=sem, vmem_limit_bytes=vmem)


def _silu(x):
    return x * jax.nn.sigmoid(x)


def _split2(x):
    hi = x.astype(BF16)
    lo = (x - hi.astype(F32)).astype(BF16)
    return hi, lo


def _split3(x):
    b1 = x.astype(BF16)
    r = x - b1.astype(F32)
    b2 = r.astype(BF16)
    b3 = (r - b2.astype(F32)).astype(BF16)
    return b1, b2, b3


def _dot(a, b):
    return jnp.dot(a, b, preferred_element_type=F32)


def _dot_exact_lhs(a_exact, b):
    b1, b2, b3 = _split3(b)
    return _dot(a_exact, b1) + _dot(a_exact, b2) + _dot(a_exact, b3)


def _dot_f32(a, b):
    a1, a2 = _split2(a)
    b1, b2 = _split2(b)
    return _dot(a1, b1) + _dot(a1, b2) + _dot(a2, b1)


def _mod_row_of_block(i, blocks_prompt, blocks_per_sample_seq):
    return jnp.where(i < blocks_prompt, CTX_ROW, (i - blocks_prompt) // blocks_per_sample_seq)


def _mod_kernel(c_ref, w_ref, b_ref, o_ref):
    c = c_ref[...]
    o_ref[...] = _dot_f32(_silu(c), w_ref[...]) + b_ref[...]


def _modulation(cc, w_mod, b_mod):
    tn = 512
    n = N_MOD * D_MODEL
    return pl.pallas_call(
        _mod_kernel,
        out_shape=jax.ShapeDtypeStruct((MOD_ROWS, n), F32),
        grid=(n // tn,),
        in_specs=[pl.BlockSpec((MOD_ROWS, D_MODEL), lambda j: (0, 0)),
                  pl.BlockSpec((D_MODEL, tn), lambda j: (0, j)),
                  pl.BlockSpec((1, tn), lambda j: (0, j))],
        out_specs=pl.BlockSpec((MOD_ROWS, tn), lambda j: (0, j)),
        compiler_params=_cparams(("arbitrary",)),
        name="modulation",
    )(cc, w_mod, b_mod.reshape(1, n))


INPROJ_TM = 1024
INPROJ_TN = 2048


def _two_path_specs(block, n_prompt_blocks):
    last = n_prompt_blocks - 1
    return (pl.BlockSpec(block, lambda i, *_: (jnp.minimum(i, last), 0)),
            pl.BlockSpec(block, lambda i, *_: (jnp.maximum(i - n_prompt_blocks, 0), 0)))


def _pick_path(n_prompt_blocks, prompt_ref, sample_ref):
    return jnp.where(pl.program_id(0) < n_prompt_blocks, prompt_ref[...], sample_ref[...])


def _inproj_kernel(xp_ref, xs_ref, mod_ref, n1_ref, w_ref, wdt_ref, o_ref, dt_ref, u_scr):
    @pl.when(pl.program_id(1) == 0)
    def _():
        x = _pick_path(T_PROMPT // INPROJ_TM, xp_ref, xs_ref)
        ms = jnp.mean(x * x, axis=-1, keepdims=True)
        y = x * lax.rsqrt(ms + EPS) * n1_ref[...]
        m = mod_ref[...]
        u = (y * (1.0 + m[1:2]) + m[0:1]).astype(BF16)
        u_scr[...] = u
        dt_ref[...] = _dot(u, wdt_ref[...])

    o_ref[...] = _dot(u_scr[...], w_ref[...]).astype(BF16)


def _in_projection(x_p, x_s, mod3, norm1_w, w_main, w_dt):
    tm, tn = INPROJ_TM, INPROJ_TN
    row = functools.partial(_mod_row_of_block, blocks_prompt=T_PROMPT // tm,
                            blocks_per_sample_seq=DEC_SEQ // tm)
    return pl.pallas_call(
        _inproj_kernel,
        out_shape=(jax.ShapeDtypeStruct((T_ALL, PROJ_W), BF16),
                   jax.ShapeDtypeStruct((T_ALL, LANES), F32)),
        grid=(T_ALL // tm, PROJ_W // tn),
        in_specs=[*_two_path_specs((tm, D_MODEL), T_PROMPT // tm),
                  pl.BlockSpec((None, N_MOD, D_MODEL), lambda i, j: (row(i), 0, 0)),
                  pl.BlockSpec((1, D_MODEL), lambda i, j: (0, 0)),
                  pl.BlockSpec((D_MODEL, tn), lambda i, j: (0, j)),
                  pl.BlockSpec((D_MODEL, LANES), lambda i, j: (0, 0))],
        out_specs=(pl.BlockSpec((tm, tn), lambda i, j: (i, j)),
                   pl.BlockSpec((tm, LANES), lambda i, j: (i, 0))),
        scratch_shapes=[pltpu.VMEM((tm, D_MODEL), BF16)],
        compiler_params=_cparams(("arbitrary", "arbitrary")),
        name="in_projection",
    )(x_p, x_s, mod3, norm1_w.reshape(1, D_MODEL), w_main, w_dt)


SSMCONV_TN = 1024
SSMCONV_RT = 64
SSMCONV_HALO = SUBLANES


def _ssmconv_kernel(x_ref, w_ref, b_ref, o_ref, pad_scr, out_scr):
    seq_len = x_ref.shape[0]
    half = D_CONV_SSM // 2
    n_half = SSMCONV_RT // 2
    zeros = jnp.zeros((SSMCONV_HALO, LANES), F32)
    for sl in range(SSMCONV_TN // LANES):
        cs = slice(sl * LANES, (sl + 1) * LANES)
        pad_scr[sl, 0:SSMCONV_HALO, :] = zeros
        pad_scr[sl, SSMCONV_HALO + seq_len:, :] = zeros
        pad_scr[sl, SSMCONV_HALO:SSMCONV_HALO + seq_len, :] = x_ref[:, cs].astype(F32)
        wk = [w_ref[k:k + 1, cs] for k in range(D_CONV_SSM)]
        bias = b_ref[:, cs]
        for r0 in range(0, seq_len, SSMCONV_RT):
            for phase in range(2):
                acc = bias
                for k in range(D_CONV_SSM):
                    acc = acc + wk[k] * pad_scr[sl, pl.ds(SSMCONV_HALO + r0 + phase + k - half, n_half, stride=2), :]
                out_scr[sl, pl.ds(r0 + phase, n_half, stride=2), :] = _silu(acc)
        o_ref[:, cs] = out_scr[sl].astype(BF16)


def _ssm_conv(proj, conv_w, conv_b, *, seq_len, n_seq, row_block0):
    tn = SSMCONV_TN
    return pl.pallas_call(
        _ssmconv_kernel,
        out_shape=jax.ShapeDtypeStruct((n_seq * seq_len, CONV_DIM), BF16),
        grid=(n_seq, CONV_DIM // tn),
        in_specs=[pl.BlockSpec((seq_len, tn), lambda b, j: (row_block0 + b, j)),
                  pl.BlockSpec((D_CONV_SSM, tn), lambda b, j: (0, j)),
                  pl.BlockSpec((1, tn), lambda b, j: (0, j))],
        out_specs=pl.BlockSpec((seq_len, tn), lambda b, j: (b, j)),
        scratch_shapes=[pltpu.VMEM((tn // LANES, seq_len + 2 * SSMCONV_HALO, LANES), F32),
                        pltpu.VMEM((tn // LANES, seq_len, LANES), F32)],
        compiler_params=_cparams(("arbitrary", "arbitrary")),
        name="ssm_conv",
    )(proj, conv_w, conv_b.reshape(1, CONV_DIM))


def _tri_masks():
    ii = lax.broadcasted_iota(I32, (CHUNK, CHUNK), 0)
    jj = lax.broadcasted_iota(I32, (CHUNK, CHUNK), 1)
    return ii, jj


def _chunk_decays(dt_ref, dtb_ref, a_ref):
    ii, jj = _tri_masks()
    pre = dt_ref[...] + dtb_ref[...]
    dt = jnp.maximum(pre, 0.0) + jnp.log(1.0 + jnp.exp(-jnp.abs(pre)))
    la = dt * a_ref[...]
    tri_lo = jnp.where(jj <= ii, 1.0, 0.0).astype(BF16)
    tri_up = jnp.where(jj >= ii, 1.0, 0.0).astype(BF16)
    cs_prefix = _dot_exact_lhs(tri_lo, la)
    cs_suffix = _dot_exact_lhs(tri_up, la)
    fwd_lane = lax.broadcasted_iota(I32, (CHUNK, LANES), 1) < N_HEADS
    cs = jnp.where(fwd_lane, cs_prefix, cs_suffix)
    tot = jnp.where(fwd_lane[0:1], cs_prefix[CHUNK - 1:CHUNK, :], cs_suffix[0:1, :])
    return dt, cs, tot


def _transpose_blocks(src, rows, cols):
    out_rows = []
    for cb in range(cols // LANES):
        pieces = [src[rb * LANES:(rb + 1) * LANES, cb * LANES:(cb + 1) * LANES].T
                  for rb in range(rows // LANES)]
        out_rows.append(jnp.concatenate(pieces, axis=1) if len(pieces) > 1 else pieces[0])
    return jnp.concatenate(out_rows, axis=0) if len(out_rows) > 1 else out_rows[0]


def _ssd_kernel(*refs, n_chunks, has_init, want_final):
    it = iter(refs)
    xbc_ref, dt_ref, z_ref = next(it), next(it), next(it)
    init_ref = next(it) if has_init else None
    dtb_ref, a_ref, dvec_ref, nw_ref, e2_ref = next(it), next(it), next(it), next(it), next(it)
    out_ref = next(it)
    fin_ref = next(it) if want_final else None
    sf_scr, sb_scr, df_scr, db_scr, dtcs_scr = next(it), next(it), next(it), next(it), next(it)

    phase = pl.program_id(1)
    c = pl.program_id(2)

    @pl.when(phase == 0)
    def _chunk_states():
        dt, cs, tot = _chunk_decays(dt_ref, dtb_ref, a_ref)
        dtcs_scr[c, 0] = dt
        dtcs_scr[c, 1] = cs
        w_in = dt * jnp.exp(tot - cs)
        pack = 2 * SUBLANES
        dec_hi, dec_lo = _split2(jnp.exp(jnp.broadcast_to(tot, (pack, LANES))))
        expanded = _dot(jnp.concatenate([w_in.astype(BF16), dec_hi, dec_lo], axis=0), e2_ref[...])
        w_exp = expanded[:CHUNK]
        dec_exp = expanded[CHUNK:CHUNK + SUBLANES] + expanded[CHUNK + pack:CHUNK + pack + SUBLANES]
        df_scr[c] = dec_exp[:, :D_INNER]
        db_scr[c] = dec_exp[:, D_INNER:]
        for g in range(N_GROUPS):
            lo = g * GROUP_W
            xg = xbc_ref[:, lo:lo + GROUP_W].astype(F32)
            xd_f = (xg * w_exp[:, lo:lo + GROUP_W]).astype(BF16)
            xd_b = (xg * w_exp[:, D_INNER + lo:D_INNER + lo + GROUP_W]).astype(BF16)
            bg = xbc_ref[:, D_INNER + g * D_STATE:D_INNER + (g + 1) * D_STATE]
            bg_t = bg.astype(F32).T.astype(BF16)
            sf_scr[c, :, lo:lo + GROUP_W] = _dot(bg_t, xd_f)
            sb_scr[c, :, lo:lo + GROUP_W] = _dot(bg_t, xd_b)

    @pl.when(jnp.logical_and(phase == 1, c == 0))
    def _recurrence():
        for g in range(N_GROUPS):
            lo = g * GROUP_W
            if has_init:
                prev_f = _transpose_blocks(init_ref[0, 0, lo:lo + GROUP_W, :], GROUP_W, D_STATE)
                prev_b = _transpose_blocks(init_ref[0, 1, lo:lo + GROUP_W, :], GROUP_W, D_STATE)
            else:
                prev_f = jnp.zeros((D_STATE, GROUP_W), F32)
                prev_b = jnp.zeros((D_STATE, GROUP_W), F32)
            for cc in range(n_chunks):
                s = sf_scr[cc, :, lo:lo + GROUP_W]
                sf_scr[cc, :, lo:lo + GROUP_W] = prev_f
                prev_f = df_scr[cc, 0:1, lo:lo + GROUP_W] * prev_f + s
            for cc in reversed(range(n_chunks)):
                s = sb_scr[cc, :, lo:lo + GROUP_W]
                sb_scr[cc, :, lo:lo + GROUP_W] = prev_b
                prev_b = db_scr[cc, 0:1, lo:lo + GROUP_W] * prev_b + s
            if want_final:
                fin_ref[0, 0, lo:lo + GROUP_W, :] = _transpose_blocks(prev_f, D_STATE, GROUP_W)
                fin_ref[0, 1, lo:lo + GROUP_W, :] = _transpose_blocks(prev_b, D_STATE, GROUP_W)

    @pl.when(phase == 1)
    def _outputs():
        ii, jj = _tri_masks()
        dt = dtcs_scr[c, 0]
        cs = dtcs_scr[c, 1]
        out_dec = _dot(jnp.exp(cs).astype(BF16), e2_ref[...])
        cs_t = cs.T
        dt_t = dt.T
        causal = ii >= jj
        below = ii > jj
        above = jj > ii
        left = lax.broadcasted_iota(I32, (CHUNK, LANES), 1) < HEAD_DIM
        for g in range(N_GROUPS):
            lo = g * GROUP_W
            bg = xbc_ref[:, D_INNER + g * D_STATE:D_INNER + (g + 1) * D_STATE]
            cg = xbc_ref[:, D_INNER + N_GROUPS * D_STATE + g * D_STATE:
                         D_INNER + N_GROUPS * D_STATE + (g + 1) * D_STATE]
            cb = lax.dot_general(cg, bg, (((1,), (1,)), ((), ())), preferred_element_type=F32)
            pf = sf_scr[c, :, lo:lo + GROUP_W].astype(BF16)
            pb = sb_scr[c, :, lo:lo + GROUP_W].astype(BF16)
            y_off = (_dot(cg, pf) * out_dec[:, lo:lo + GROUP_W]
                     + _dot(cg, pb) * out_dec[:, D_INNER + lo:D_INNER + lo + GROUP_W])
            pairs = []
            for m in range(HEADS_PER_GROUP // 2):
                x_pair = xbc_ref[:, lo + m * LANES:lo + (m + 1) * LANES]
                ys = []
                for hh in range(2):
                    h = g * HEADS_PER_GROUP + 2 * m + hh
                    hb = N_HEADS + h
                    seg = jnp.where(causal, cs[:, h:h + 1] - cs_t[h:h + 1, :], cs[:, hb:hb + 1] - cs_t[hb:hb + 1, :])
                    dt_f, dt_b = dt_t[h:h + 1, :], dt_t[hb:hb + 1, :]
                    mix = jnp.exp(seg) * jnp.where(below, dt_f, jnp.where(above, dt_b, dt_f + dt_b))
                    ys.append(_dot((cb * mix).astype(BF16), x_pair))
                pairs.append(jnp.where(left, ys[0], ys[1]))
            y_diag = jnp.concatenate(pairs, axis=1)
            xg = xbc_ref[:, lo:lo + GROUP_W].astype(F32)
            y = y_diag + y_off + dvec_ref[:, lo:lo + GROUP_W] * xg
            zg = z_ref[:, lo:lo + GROUP_W].astype(F32)
            y = y * _silu(zg)
            ms = jnp.mean(y * y, axis=-1, keepdims=True)
            out_ref[:, lo:lo + GROUP_W] = (y * lax.rsqrt(ms + EPS) * nw_ref[:, lo:lo + GROUP_W]).astype(BF16)


def _ssd(xbc, dt_raw, proj, init, dtb, a_neg, dvec, norm_w, e2, *, seq_len, n_seq, tok0, want_final):
    nc = seq_len // CHUNK
    blk0 = tok0 // CHUNK
    has_init = init is not None
    in_specs = [pl.BlockSpec((CHUNK, CONV_DIM), lambda b, p, c: (b * nc + c, 0)),
                pl.BlockSpec((CHUNK, LANES), lambda b, p, c: (blk0 + b * nc + c, 0)),
                pl.BlockSpec((CHUNK, D_INNER), lambda b, p, c: (blk0 + b * nc + c * p, COL_Z))]
    args = [xbc, dt_raw, proj]
    if has_init:
        in_specs.append(pl.BlockSpec((1, 2, D_INNER, D_STATE), lambda b, p, c: (b, 0, 0, 0)))
        args.append(init)
    const = lambda b, p, c: (0, 0)
    in_specs += [pl.BlockSpec((1, LANES), const), pl.BlockSpec((1, LANES), const),
                 pl.BlockSpec((1, D_INNER), const), pl.BlockSpec((1, D_INNER), const),
                 pl.BlockSpec((LANES, 2 * D_INNER), const)]
    args += [dtb, a_neg, dvec, norm_w, e2]
    out_shape = [jax.ShapeDtypeStruct((n_seq * seq_len, D_INNER), BF16)]
    out_specs = [pl.BlockSpec((CHUNK, D_INNER), lambda b, p, c: (b * nc + c * p, 0))]
    if want_final:
        out_shape.append(jax.ShapeDtypeStruct((n_seq, 2, D_INNER, D_STATE), F32))
        out_specs.append(pl.BlockSpec((1, 2, D_INNER, D_STATE), lambda b, p, c: (b, 0, 0, 0)))
    res = pl.pallas_call(
        functools.partial(_ssd_kernel, n_chunks=nc, has_init=has_init, want_final=want_final),
        out_shape=tuple(out_shape),
        grid=(n_seq, 2, nc),
        in_specs=in_specs,
        out_specs=tuple(out_specs),
        scratch_shapes=[pltpu.VMEM((nc, D_STATE, D_INNER), F32),
                        pltpu.VMEM((nc, D_STATE, D_INNER), F32),
                        pltpu.VMEM((nc, SUBLANES, D_INNER), F32),
                        pltpu.VMEM((nc, SUBLANES, D_INNER), F32),
                        pltpu.VMEM((nc, 2, CHUNK, LANES), F32)],
        compiler_params=_cparams(("arbitrary", "arbitrary", "arbitrary")),
        name="ssd_final" if want_final else "ssd_init",
    )(*args)
    return res


CONF_SLABS = D_MODEL // LANES
CONF_HALO = 2 * SUBLANES
CONF_RT = 64


def _glu_slab(cv_ref, cg_ref, sl):
    cs = slice(sl * LANES, (sl + 1) * LANES)
    return cv_ref[:, cs].astype(F32) * jax.nn.sigmoid(cg_ref[:, cs].astype(F32))


def _conv_taps_strided(src_ref, sl, src_row0, dst_ref, dst_row0, w_ref, b_ref):
    cs = slice(sl * LANES, (sl + 1) * LANES)
    n_half = CONF_RT // 2
    for phase in range(2):
        acc = b_ref[:, cs]
        for k in range(CONF_K):
            x = src_ref[sl, pl.ds(src_row0 + phase + k - CONF_PAD, n_half, stride=2), :]
            acc = acc + w_ref[k:k + 1, cs] * x
        dst_ref[sl, pl.ds(dst_row0 + phase, n_half, stride=2), :] = acc


def _layernorm_silu_slabs(acc_scr, lw_ref, lb_ref, o_ref):
    n = acc_scr.shape[1]
    tot = jnp.zeros((n, 1), F32)
    for sl in range(CONF_SLABS):
        tot = tot + jnp.sum(acc_scr[sl], axis=-1, keepdims=True)
    mu = tot * (1.0 / D_MODEL)
    sq = jnp.zeros((n, 1), F32)
    for sl in range(CONF_SLABS):
        d = acc_scr[sl] - mu
        sq = sq + jnp.sum(d * d, axis=-1, keepdims=True)
    rstd = lax.rsqrt(sq * (1.0 / D_MODEL) + EPS)
    for sl in range(CONF_SLABS):
        cs = slice(sl * LANES, (sl + 1) * LANES)
        o_ref[:, cs] = _silu((acc_scr[sl] - mu) * rstd * lw_ref[:, cs] + lb_ref[:, cs]).astype(BF16)


def _conf_seq_kernel(cv_ref, cg_ref, w_ref, b_ref, lw_ref, lb_ref, o_ref, pad_scr, acc_scr):
    seq_len = cv_ref.shape[0]
    zeros = jnp.zeros((CONF_HALO, LANES), F32)
    for sl in range(CONF_SLABS):
        pad_scr[sl, 0:CONF_HALO, :] = zeros
        pad_scr[sl, CONF_HALO + seq_len:, :] = zeros
        pad_scr[sl, CONF_HALO:CONF_HALO + seq_len, :] = _glu_slab(cv_ref, cg_ref, sl)
    for sl in range(CONF_SLABS):
        for r0 in range(0, seq_len, CONF_RT):
            _conv_taps_strided(pad_scr, sl, CONF_HALO + r0, acc_scr, r0, w_ref, b_ref)
    _layernorm_silu_slabs(acc_scr, lw_ref, lb_ref, o_ref)


def _conf_grid_kernel(cv_ref, cg_ref, w_ref, b_ref, lw_ref, lb_ref, o_ref, v_scr, pad_scr, acc_scr):
    half_slabs = CONF_SLABS // 2
    stride = GRID_W + 2 * CONF_HALO
    pad_scr[...] = jnp.zeros(pad_scr.shape, F32)
    for sl in range(half_slabs):
        v = _glu_slab(cv_ref, cg_ref, sl)
        for r in range(GRID_H):
            pad_scr[sl, r * stride + CONF_HALO:r * stride + CONF_HALO + GRID_W, :] = v[r * GRID_W:(r + 1) * GRID_W]
    for sl in range(half_slabs, CONF_SLABS):
        v_scr[sl - half_slabs] = _glu_slab(cv_ref, cg_ref, sl)
    for sl in range(half_slabs):
        for r in range(GRID_H):
            _conv_taps_strided(pad_scr, sl, r * stride + CONF_HALO, acc_scr, r * GRID_W, w_ref, b_ref)
    for sl in range(half_slabs, CONF_SLABS):
        cs = slice(sl * LANES, (sl + 1) * LANES)
        for r in range(GRID_H):
            acc = jnp.broadcast_to(b_ref[:, cs], (GRID_W, LANES))
            for r2 in range(GRID_H):
                k = r2 - r + CONF_PAD
                acc = acc + w_ref[k:k + 1, cs] * v_scr[sl - half_slabs, r2 * GRID_W:(r2 + 1) * GRID_W, :]
            acc_scr[sl, r * GRID_W:(r + 1) * GRID_W, :] = acc
    _layernorm_silu_slabs(acc_scr, lw_ref, lb_ref, o_ref)


def _conformer(proj, conv_w, conv_b, ln_w, ln_b, *, seq_len, n_seq, row_block0, grid_mode):
    if grid_mode:
        body = _conf_grid_kernel
        scratch = [pltpu.VMEM((CONF_SLABS // 2, seq_len, LANES), F32),
                   pltpu.VMEM((CONF_SLABS // 2, GRID_H * (GRID_W + 2 * CONF_HALO), LANES), F32),
                   pltpu.VMEM((CONF_SLABS, seq_len, LANES), F32)]
    else:
        body = _conf_seq_kernel
        scratch = [pltpu.VMEM((CONF_SLABS, seq_len + 2 * CONF_HALO, LANES), F32),
                   pltpu.VMEM((CONF_SLABS, seq_len, LANES), F32)]
    const = lambda b: (0, 0)
    return pl.pallas_call(
        body,
        out_shape=jax.ShapeDtypeStruct((n_seq * seq_len, D_MODEL), BF16),
        grid=(n_seq,),
        in_specs=[pl.BlockSpec((seq_len, D_MODEL), lambda b: (row_block0 + b, COL_CV)),
                  pl.BlockSpec((seq_len, D_MODEL), lambda b: (row_block0 + b, COL_CG)),
                  pl.BlockSpec((CONF_K, D_MODEL), const),
                  pl.BlockSpec((1, D_MODEL), const),
                  pl.BlockSpec((1, D_MODEL), const),
                  pl.BlockSpec((1, D_MODEL), const)],
        out_specs=pl.BlockSpec((seq_len, D_MODEL), lambda b: (b, 0)),
        scratch_shapes=scratch,
        compiler_params=_cparams(("arbitrary",)),
        name="conformer_grid" if grid_mode else "conformer_seq",
    )(proj, proj, conv_w, conv_b.reshape(1, D_MODEL), ln_w.reshape(1, D_MODEL), ln_b.reshape(1, D_MODEL))


MERGE_TM = 256


def _merge_kernel(yap_ref, yas_ref, ybp_ref, ybs_ref, gl_ref, xp_ref, xs_ref, mod_ref, wa_ref, wb_ref, wo_ref,
                  n2_ref, rw_ref, sgu_ref, sd_ref, pre_ref, v_ref, s_ref):
    npb = T_PROMPT // MERGE_TM
    y_a = _dot(_pick_path(npb, yap_ref, yas_ref), wa_ref[...])
    y_b = _dot(_pick_path(npb, ybp_ref, ybs_ref), wb_ref[...])
    gates = jax.nn.sigmoid(gl_ref[...].astype(F32))
    mix = gates[:, :D_MODEL] * y_a + gates[:, D_MODEL:] * y_b
    out = _dot(mix.astype(BF16), wo_ref[...])
    m = mod_ref[...]
    x1 = _pick_path(npb, xp_ref, xs_ref) + m[2:3] * out
    ms = jnp.mean(x1 * x1, axis=-1, keepdims=True)
    v = x1 * lax.rsqrt(ms + EPS) * n2_ref[...] * (1.0 + m[4:5]) + m[3:4]
    v_ref[...] = v
    s_ref[...] = jax.nn.sigmoid(_dot_f32(v, rw_ref[...]))
    hgu = _dot(v.astype(BF16), sgu_ref[...])
    act = _silu(hgu[:, :D_SHARED]) * hgu[:, D_SHARED:]
    shared = _dot(act.astype(BF16), sd_ref[...])
    pre_ref[...] = x1 + m[5:6] * shared


def _merge(y_ssm_p, y_ssm_s, v_conf_p, v_conf_s, proj, x_p, x_s, mod3, w_out_ssm, w_out_conf, w_o, norm2_w,
           router_w, sh_gu, sh_d):
    tm = MERGE_TM
    npb = T_PROMPT // tm
    row = functools.partial(_mod_row_of_block, blocks_prompt=npb, blocks_per_sample_seq=DEC_SEQ // tm)
    const = lambda i: (0, 0)
    return pl.pallas_call(
        _merge_kernel,
        out_shape=(jax.ShapeDtypeStruct((T_ALL, D_MODEL), F32),
                   jax.ShapeDtypeStruct((T_ALL, D_MODEL), F32),
                   jax.ShapeDtypeStruct((T_ALL, N_EXPERTS), F32)),
        grid=(T_ALL // tm,),
        in_specs=[*_two_path_specs((tm, D_INNER), npb),
                  *_two_path_specs((tm, D_MODEL), npb),
                  pl.BlockSpec((tm, 2 * D_MODEL), lambda i: (i, COL_GL)),
                  *_two_path_specs((tm, D_MODEL), npb),
                  pl.BlockSpec((None, N_MOD, D_MODEL), lambda i: (row(i), 0, 0)),
                  pl.BlockSpec((D_INNER, D_MODEL), const),
                  pl.BlockSpec((D_MODEL, D_MODEL), const),
                  pl.BlockSpec((D_MODEL, D_MODEL), const),
                  pl.BlockSpec((1, D_MODEL), const),
                  pl.BlockSpec((D_MODEL, N_EXPERTS), const),
                  pl.BlockSpec((D_MODEL, 2 * D_SHARED), const),
                  pl.BlockSpec((D_SHARED, D_MODEL), const)],
        out_specs=(pl.BlockSpec((tm, D_MODEL), lambda i: (i, 0)),
                   pl.BlockSpec((tm, D_MODEL), lambda i: (i, 0)),
                   pl.BlockSpec((tm, N_EXPERTS), lambda i: (i, 0))),
        compiler_params=_cparams(("arbitrary",)),
        name="merge",
    )(y_ssm_p, y_ssm_s, v_conf_p, v_conf_s, proj, x_p, x_s, mod3, w_out_ssm, w_out_conf, w_o,
      norm2_w.reshape(1, D_MODEL), router_w, sh_gu, sh_d)


ROUTE_TB = 256


def _first_index_of_max(vals, iota, sentinel):
    m = jnp.max(vals, axis=0, keepdims=True)
    idx = jnp.min(jnp.where(vals == m, iota, jnp.float32(sentinel)), axis=0, keepdims=True)
    return m, idx


def _route_kernel(s_ref, bias_ref, idx_ref, w_ref, rank_ref, cnt_ref, run_scr):
    i = pl.program_id(0)
    tb = ROUTE_TB
    neg = jnp.float32(-jnp.inf)

    @pl.when(i == 0)
    def _():
        run_scr[...] = jnp.zeros(run_scr.shape, F32)

    s_t = s_ref[...].T
    sb_t = s_t + bias_ref[...]
    eiota = lax.broadcasted_iota(I32, (N_EXPERTS, tb), 0).astype(F32)

    liota = lax.broadcasted_iota(I32, (EXPERTS_PER_GROUP, tb), 0).astype(F32)
    gscores = []
    for g in range(N_EXPERT_GROUPS):
        blk = sb_t[g * EXPERTS_PER_GROUP:(g + 1) * EXPERTS_PER_GROUP, :]
        m1, i1 = _first_index_of_max(blk, liota, EXPERTS_PER_GROUP)
        m2 = jnp.max(jnp.where(liota == i1, neg, blk), axis=0, keepdims=True)
        gscores.append(m1 + m2)
    gs = jnp.concatenate(gscores, axis=0)
    giota = lax.broadcasted_iota(I32, (N_EXPERT_GROUPS, tb), 0).astype(F32)
    gsel = jnp.zeros((N_EXPERT_GROUPS, tb), F32)
    for _ in range(TOPK_GROUPS):
        _, gi = _first_index_of_max(gs, giota, N_EXPERT_GROUPS)
        hit = giota == gi
        gsel = jnp.where(hit, 1.0, gsel)
        gs = jnp.where(hit, neg, gs)
    emask = jnp.concatenate(
        [jnp.broadcast_to(gsel[g:g + 1, :], (EXPERTS_PER_GROUP, tb)) for g in range(N_EXPERT_GROUPS)], axis=0)
    masked = jnp.where(emask > 0.5, sb_t, neg)

    onehots, idxs, wts = [], [], []
    for _ in range(TOP_K):
        _, ei = _first_index_of_max(masked, eiota, N_EXPERTS)
        hit = eiota == ei
        onehots.append(hit)
        idxs.append(ei)
        wts.append(jnp.sum(jnp.where(hit, s_t, 0.0), axis=0, keepdims=True))
        masked = jnp.where(hit, neg, masked)
    w = jnp.concatenate(wts, axis=0)
    w_ref[...] = w / jnp.sum(w, axis=0, keepdims=True) * ROUTED_SCALE
    idx_ref[...] = jnp.concatenate(idxs, axis=0).astype(I32)

    assign = jnp.zeros((N_EXPERTS, tb), F32)
    for hit in onehots:
        assign = jnp.where(hit, 1.0, assign)
    assign_b = assign.astype(BF16)
    ti = lax.broadcasted_iota(I32, (tb, tb), 0)
    tj = lax.broadcasted_iota(I32, (tb, tb), 1)
    before = jnp.where(ti < tj, 1.0, 0.0).astype(BF16)
    within = _dot(assign_b, before)
    run = run_scr[...]
    total = within + jnp.concatenate([run] * (tb // LANES), axis=1)
    rank_ref[...] = jnp.concatenate(
        [jnp.sum(jnp.where(hit, total, 0.0), axis=0, keepdims=True) for hit in onehots], axis=0).astype(I32)
    new_run = run + _dot(assign_b, jnp.ones((tb, LANES), BF16))
    run_scr[...] = new_run
    cnt_ref[...] = new_run


def _route(scores, router_bias):
    tb = ROUTE_TB
    return pl.pallas_call(
        _route_kernel,
        out_shape=(jax.ShapeDtypeStruct((TOP_K, T_ALL), I32),
                   jax.ShapeDtypeStruct((TOP_K, T_ALL), F32),
                   jax.ShapeDtypeStruct((TOP_K, T_ALL), I32),
                   jax.ShapeDtypeStruct((N_EXPERTS, LANES), F32)),
        grid=(T_ALL // tb,),
        in_specs=[pl.BlockSpec((tb, N_EXPERTS), lambda i: (i, 0)),
                  pl.BlockSpec((N_EXPERTS, 1), lambda i: (0, 0))],
        out_specs=(pl.BlockSpec((TOP_K, tb), lambda i: (0, i)),
                   pl.BlockSpec((TOP_K, tb), lambda i: (0, i)),
                   pl.BlockSpec((TOP_K, tb), lambda i: (0, i)),
                   pl.BlockSpec((N_EXPERTS, LANES), lambda i: (0, 0))),
        scratch_shapes=[pltpu.VMEM((N_EXPERTS, LANES), F32)],
        compiler_params=_cparams(("arbitrary",)),
        name="route",
    )(scores, router_bias.reshape(N_EXPERTS, 1))


def _dest_kernel(idx_ref, rank_ref, start_ref, dest_ref):
    tb = idx_ref.shape[1]
    eiota = lax.broadcasted_iota(I32, (N_EXPERTS, tb), 0)
    start = jnp.broadcast_to(start_ref[...], (N_EXPERTS, tb))
    idx = idx_ref[...]
    rows = [jnp.sum(jnp.where(eiota == idx[k:k + 1, :], start, 0.0), axis=0, keepdims=True)
            for k in range(TOP_K)]
    dest_ref[...] = jnp.concatenate(rows, axis=0).astype(I32) + rank_ref[...]


def _dest_slots(idx_t, rank_t, group_start):
    tb = 512
    return pl.pallas_call(
        _dest_kernel,
        out_shape=jax.ShapeDtypeStruct((TOP_K, T_ALL), I32),
        grid=(T_ALL // tb,),
        in_specs=[pl.BlockSpec((TOP_K, tb), lambda i: (0, i)),
                  pl.BlockSpec((TOP_K, tb), lambda i: (0, i)),
                  pl.BlockSpec((N_EXPERTS, 1), lambda i: (0, 0))],
        out_specs=pl.BlockSpec((TOP_K, tb), lambda i: (0, i)),
        compiler_params=_cparams(("arbitrary",)),
        name="dest_slots",
    )(idx_t, rank_t, group_start.astype(F32).reshape(N_EXPERTS, 1))


DISPATCH_TB = 256


HI_MASK = -65536


def _pack_pair(hi, lo):
    hb = pltpu.bitcast(hi.astype(BF16).astype(F32), I32)
    lb = pltpu.bitcast(lo.astype(BF16).astype(F32), I32)
    return jnp.bitwise_or(hb, lax.shift_right_logical(lb, 16))


def _unpack_pair(word):
    hi = pltpu.bitcast(jnp.bitwise_and(word, HI_MASK), F32)
    lo = pltpu.bitcast(lax.shift_left(word, 16), F32)
    return hi, lo


def _rows_to_tiles(dst_scr, base, rows):
    n = rows.shape[0]
    for s in range(ROW_TILE):
        dst_scr[pl.ds(base + s, n, stride=ROW_TILE), :] = _pack_pair(
            rows[:, s * LANES:(s + 1) * LANES], rows[:, HALF_D + s * LANES:HALF_D + (s + 1) * LANES])


def _tile_column(src_scr, base, n, s):
    return src_scr[pl.ds(base + s, n, stride=ROW_TILE), :]


def _tiles_to_rows(src_scr, base, n, dtype):
    pairs = [_unpack_pair(_tile_column(src_scr, base, n, s)) for s in range(ROW_TILE)]
    return jnp.concatenate([p[0].astype(dtype) for p in pairs] + [p[1].astype(dtype) for p in pairs], axis=1)


def _row_tile(ref, row):
    return ref.at[pl.ds(pl.multiple_of(row * ROW_TILE, ROW_TILE), ROW_TILE), :]


def _dispatch_kernel(dest_ref, v_ref, xs_hbm, tile_scr, zero_scr, sem, zsem):
    i = pl.program_id(0)
    n = pl.num_programs(0)
    tb = DISPATCH_TB
    slot = i % 2
    base = pl.multiple_of(slot * (tb * ROW_TILE), ROW_TILE)
    _rows_to_tiles(tile_scr, base, v_ref[...])

    def issue(t, carry):
        src = _row_tile(tile_scr, slot * tb + t)
        for k in range(TOP_K):
            pltpu.make_async_copy(src, _row_tile(xs_hbm, dest_ref[k, t]), sem.at[slot]).start(priority=k % 2)
        return carry

    lax.fori_loop(0, tb, issue, 0)

    def wait_block(sl):
        blk = tile_scr.at[pl.ds(pl.multiple_of(sl * (tb * ROW_TILE), ROW_TILE), tb * ROW_TILE), :]
        for _ in range(TOP_K):
            pltpu.make_async_copy(blk, blk, sem.at[sl]).wait()

    @pl.when(i == 0)
    def _():
        zero_scr[...] = jnp.zeros(zero_scr.shape, I32)
        for j in range(GMM_TM // SUBLANES):
            cp = pltpu.make_async_copy(
                zero_scr, xs_hbm.at[pl.ds((N_ASSIGN + j * SUBLANES) * ROW_TILE, SUBLANES * ROW_TILE), :], zsem)
            cp.start()
            cp.wait()

    @pl.when(i > 0)
    def _():
        wait_block(1 - slot)

    @pl.when(i == n - 1)
    def _():
        wait_block(slot)


def _dispatch(dest_t, v_all):
    tb = DISPATCH_TB
    return pl.pallas_call(
        _dispatch_kernel,
        out_shape=jax.ShapeDtypeStruct((N_ROWS * ROW_TILE, LANES), I32),
        grid=(T_ALL // tb,),
        in_specs=[pl.BlockSpec((TOP_K, tb), lambda i: (0, i), memory_space=pltpu.SMEM),
                  pl.BlockSpec((tb, D_MODEL), lambda i: (i, 0))],
        out_specs=pl.BlockSpec(memory_space=pl.ANY),
        scratch_shapes=[pltpu.VMEM((2 * tb * ROW_TILE, LANES), I32),
                        pltpu.VMEM((SUBLANES * ROW_TILE, LANES), I32),
                        pltpu.SemaphoreType.DMA((2,)),
                        pltpu.SemaphoreType.DMA(())],
        compiler_params=_cparams(("arbitrary",)),
        name="dispatch",
    )(dest_t, v_all)


def _flat_rows(ref, row, n_rows):
    return ref.at[pl.ds(pl.multiple_of(row * ROW_TILE, ROW_TILE), n_rows * ROW_TILE), :]


def _tile_writes(y_scr, y_hbm, slot, row0, valid, sem):
    base = slot * GMM_TM
    parts = [(valid == GMM_TM, pltpu.make_async_copy(_flat_rows(y_scr, base, GMM_TM),
                                                     _flat_rows(y_hbm, row0, GMM_TM), sem))]
    off = jnp.int32(0)
    for p in TAIL_SIZES:
        bit = (valid & p) != 0
        take = jnp.logical_and(valid < GMM_TM, bit)
        parts.append((take, pltpu.make_async_copy(_flat_rows(y_scr, base + off, p),
                                                  _flat_rows(y_hbm, row0 + off, p), sem)))
        off = off + jnp.where(bit, p, 0)
    return parts


ST_TILES, ST_ROW, ST_VALID, ST_FETCH_E, ST_FETCH_I, ST_FETCH_G, ST_SIZE = 0, 1, 3, 5, 6, 7, 8
X_SLOTS = 4
X_AHEAD = X_SLOTS - 1


def _gmm_kernel(start_ref, xs_hbm, wg_ref, wu_ref, wd_ref, y_hbm,
                x_scr, y_scr, wgu_scr, wdn_scr, zero_scr, st_ref, xsem, ysem, zsem):
    e = pl.program_id(0)
    row_lo = start_ref[e]
    row_hi = start_ref[e + 1]
    n_rows = row_hi - row_lo
    n_tiles = lax.shift_right_logical(n_rows + (GMM_TM - 1), GMM_TM.bit_length() - 1)

    def x_copy(row, sl):
        return pltpu.make_async_copy(_flat_rows(xs_hbm, row, GMM_TM),
                                     _flat_rows(x_scr, sl * GMM_TM, GMM_TM), xsem.at[sl])

    def tiles_of(ex):
        rows = start_ref[ex + 1] - start_ref[ex]
        return lax.shift_right_logical(rows + (GMM_TM - 1), GMM_TM.bit_length() - 1)

    def skip_empty(ex):
        def empty(q):
            qc = jnp.minimum(q, N_EXPERTS - 1)
            return jnp.logical_and(q < N_EXPERTS, start_ref[qc + 1] == start_ref[qc])
        return lax.while_loop(empty, lambda q: q + 1, ex)

    def fetch_next_tile():
        pe = st_ref[ST_FETCH_E]

        @pl.when(pe < N_EXPERTS)
        def _():
            pi = st_ref[ST_FETCH_I]
            pg = st_ref[ST_FETCH_G]
            pec = jnp.minimum(pe, N_EXPERTS - 1)
            x_copy(start_ref[pec] + pi * GMM_TM, pg % X_SLOTS).start()
            last = pi + 1 >= tiles_of(pec)
            st_ref[ST_FETCH_E] = jnp.where(last, skip_empty(pe + 1), pe)
            st_ref[ST_FETCH_I] = jnp.where(last, 0, pi + 1)
            st_ref[ST_FETCH_G] = pg + 1

    def wait_writes(sl):
        @pl.when(st_ref[ST_VALID + sl] > 0)
        def _():
            for pred, cp in _tile_writes(y_scr, y_hbm, sl, st_ref[ST_ROW + sl], st_ref[ST_VALID + sl],
                                         ysem.at[sl]):
                @pl.when(pred)
                def _():
                    cp.wait()
            st_ref[ST_VALID + sl] = 0

    @pl.when(e == 0)
    def _():
        for j in range(ST_SIZE):
            st_ref[j] = 0
        st_ref[ST_FETCH_E] = skip_empty(jnp.int32(0))
        for _ in range(X_AHEAD):
            fetch_next_tile()

    done = st_ref[ST_TILES]

    @pl.when(n_tiles > 0)
    def _():
        wgu_scr[:, :D_EXPERT] = wg_ref[0].astype(BF16)
        wgu_scr[:, D_EXPERT:] = wu_ref[0].astype(BF16)
        wdn_scr[...] = wd_ref[0].astype(BF16)

        def tile(i, carry):
            g = done + i
            slot = g % 2
            xslot = g % X_SLOTS
            row0 = row_lo + i * GMM_TM
            valid = jnp.minimum(n_rows - i * GMM_TM, GMM_TM)
            x_copy(row0, xslot).wait()
            fetch_next_tile()

            base = pl.multiple_of(slot * (GMM_TM * ROW_TILE), ROW_TILE)
            xbase = pl.multiple_of(xslot * (GMM_TM * ROW_TILE), ROW_TILE)
            x = _tiles_to_rows(x_scr, xbase, GMM_TM, BF16)
            h = _dot(x, wgu_scr[...])
            act = (_silu(h[:, :D_EXPERT]) * h[:, D_EXPERT:]).astype(BF16)
            y = _dot(act, wdn_scr[...])

            wait_writes(slot)
            _rows_to_tiles(y_scr, base, y)
            for pred, cp in _tile_writes(y_scr, y_hbm, slot, row0, valid, ysem.at[slot]):
                @pl.when(pred)
                def _():
                    cp.start()
            st_ref[ST_ROW + slot] = row0
            st_ref[ST_VALID + slot] = valid
            return carry

        lax.fori_loop(0, n_tiles, tile, 0)
        st_ref[ST_TILES] = done + n_tiles

    @pl.when(e == pl.num_programs(0) - 1)
    def _():
        wait_writes(0)
        wait_writes(1)
        zero_scr[...] = jnp.zeros(zero_scr.shape, I32)
        for j in range(GMM_TM // SUBLANES):
            cp = pltpu.make_async_copy(zero_scr, _flat_rows(y_hbm, N_ASSIGN + j * SUBLANES, SUBLANES), zsem)
            cp.start()
            cp.wait()


def _grouped_mlp(group_start, xs, w_gate, w_up, w_down):
    grid_spec = pltpu.PrefetchScalarGridSpec(
        num_scalar_prefetch=1,
        grid=(N_EXPERTS,),
        in_specs=[pl.BlockSpec(memory_space=pl.ANY),
                  pl.BlockSpec((1, D_MODEL, D_EXPERT), lambda e, st: (e, 0, 0)),
                  pl.BlockSpec((1, D_MODEL, D_EXPERT), lambda e, st: (e, 0, 0)),
                  pl.BlockSpec((1, D_EXPERT, D_MODEL), lambda e, st: (e, 0, 0))],
        out_specs=pl.BlockSpec(memory_space=pl.ANY),
        scratch_shapes=[pltpu.VMEM((X_SLOTS * GMM_TM * ROW_TILE, LANES), I32),
                        pltpu.VMEM((2 * GMM_TM * ROW_TILE, LANES), I32),
                        pltpu.VMEM((D_MODEL, 2 * D_EXPERT), BF16),
                        pltpu.VMEM((D_EXPERT, D_MODEL), BF16),
                        pltpu.VMEM((SUBLANES * ROW_TILE, LANES), I32),
                        pltpu.SMEM((ST_SIZE,), I32),
                        pltpu.SemaphoreType.DMA((X_SLOTS,)),
                        pltpu.SemaphoreType.DMA((2,)),
                        pltpu.SemaphoreType.DMA(())],
    )
    return pl.pallas_call(
        _gmm_kernel,
        out_shape=jax.ShapeDtypeStruct((N_ROWS * ROW_TILE, LANES), I32),
        grid_spec=grid_spec,
        compiler_params=_cparams(("arbitrary",)),
        name="grouped_mlp",
    )(group_start, xs, w_gate, w_up, w_down)


COMBINE_TB = 256


def _combine_kernel(dest_ref, dest_next_ref, y_hbm, pre_ref, w_ref, mod_ref, fw_ref, op_ref, os_ref, buf, x2_scr,
                    sem):
    i = pl.program_id(0)
    n = pl.num_programs(0)
    tb = COMBINE_TB
    slot = i % 2
    blk_rows = TOP_K * tb

    def issue_block(d_ref, sl):
        def issue(t, carry):
            for k in range(TOP_K):
                pltpu.make_async_copy(_row_tile(y_hbm, d_ref[k, t]),
                                      _row_tile(buf, sl * blk_rows + k * tb + t),
                                      sem.at[sl]).start(priority=k % 2)
            return carry

        lax.fori_loop(0, tb, issue, 0)

    @pl.when(i == 0)
    def _():
        issue_block(dest_ref, 0)

    @pl.when(i + 1 < n)
    def _():
        issue_block(dest_next_ref, 1 - slot)

    whole = _flat_rows(buf, slot * blk_rows, blk_rows)
    pltpu.make_async_copy(whole, whole, sem.at[slot]).wait()

    base = pl.multiple_of(slot * (blk_rows * ROW_TILE), ROW_TILE)
    w = w_ref[...]
    m = mod_ref[...]
    ssq = jnp.zeros((tb, 1), F32)
    for s in range(ROW_TILE):
        routed = [None, None]
        for k in range(TOP_K):
            halves = _unpack_pair(_tile_column(buf, base + k * tb * ROW_TILE, tb, s))
            for j in range(2):
                term = w[:, k:k + 1] * halves[j]
                routed[j] = term if routed[j] is None else routed[j] + term
        for j in range(2):
            cols = slice(j * HALF_D + s * LANES, j * HALF_D + (s + 1) * LANES)
            x2 = pre_ref[:, cols] + m[5:6, cols] * routed[j]
            x2_scr[:, cols] = x2
            ssq = ssq + jnp.sum(x2 * x2, axis=-1, keepdims=True)
    scale = lax.rsqrt(ssq * (1.0 / D_MODEL) + EPS)

    @pl.when(i < T_PROMPT // tb)
    def _():
        op_ref[...] = x2_scr[...] * scale * fw_ref[...]

    @pl.when(i >= T_PROMPT // tb)
    def _():
        os_ref[...] = x2_scr[...] * scale * fw_ref[...]


def _combine(dest_t, y_rows, pre, w_tok, mod3, final_norm_w):
    tb = COMBINE_TB
    n_blocks = T_ALL // tb
    npb = T_PROMPT // tb
    row = functools.partial(_mod_row_of_block, blocks_prompt=npb, blocks_per_sample_seq=DEC_SEQ // tb)
    return pl.pallas_call(
        _combine_kernel,
        out_shape=(jax.ShapeDtypeStruct((T_PROMPT, D_MODEL), F32),
                   jax.ShapeDtypeStruct((T_SAMPLE, D_MODEL), F32)),
        grid=(n_blocks,),
        in_specs=[pl.BlockSpec((TOP_K, tb), lambda i: (0, i), memory_space=pltpu.SMEM),
                  pl.BlockSpec((TOP_K, tb), lambda i: (0, jnp.minimum(i + 1, n_blocks - 1)),
                               memory_space=pltpu.SMEM),
                  pl.BlockSpec(memory_space=pl.ANY),
                  pl.BlockSpec((tb, D_MODEL), lambda i: (i, 0)),
                  pl.BlockSpec((tb, TOP_K), lambda i: (i, 0)),
                  pl.BlockSpec((None, N_MOD, D_MODEL), lambda i: (row(i), 0, 0)),
                  pl.BlockSpec((1, D_MODEL), lambda i: (0, 0))],
        out_specs=_two_path_specs((tb, D_MODEL), npb),
        scratch_shapes=[pltpu.VMEM((2 * TOP_K * tb * ROW_TILE, LANES), I32),
                        pltpu.VMEM((tb, D_MODEL), F32),
                        pltpu.SemaphoreType.DMA((2,))],
        compiler_params=_cparams(("arbitrary",)),
        name="combine",
    )(dest_t, dest_t, y_rows, pre, w_tok, mod3, final_norm_w.reshape(1, D_MODEL))


def _group_starts(counts):
    return jnp.concatenate([jnp.zeros((1,), I32), jnp.cumsum(counts).astype(I32)])


def _head_expand_matrix():
    r = jnp.arange(LANES)[:, None]
    cidx = jnp.arange(2 * D_INNER)[None, :]
    direction = cidx // D_INNER
    head = (cidx % D_INNER) // HEAD_DIM
    return (r == direction * N_HEADS + head).astype(BF16)


def kernel(x_prompt, x_sample, state_ssm, c, c_ctx, norm1_w, norm2_w, w_mod, b_mod, w_in, ssm_conv_w, ssm_conv_b, ssm_dt_bias, ssm_a_log, ssm_d, ssm_norm_w, w_out_ssm, conf_conv_w, conf_conv_b, conf_ln_w, conf_ln_b, w_out_conf, w_o, router_w, router_bias, exp_w_gate, exp_w_up, exp_w_down, sh_w_gate, sh_w_up, sh_w_down, final_norm_w):
    x_p = x_prompt.reshape(T_PROMPT, D_MODEL)
    x_s = x_sample.reshape(T_SAMPLE, D_MODEL)

    cc = jnp.zeros((MOD_ROWS, D_MODEL), F32).at[:DEC_BATCH].set(c).at[CTX_ROW].set(c_ctx)
    mod3 = _modulation(cc, w_mod[0], b_mod[0]).reshape(MOD_ROWS, N_MOD, D_MODEL)

    w = w_in[0]
    o_xbc, o_dt, o_cv, o_cg, o_gl = D_INNER, D_INNER + CONV_DIM, D_INNER + CONV_DIM + 2 * N_HEADS, \
        D_INNER + CONV_DIM + 2 * N_HEADS + D_MODEL, D_INNER + CONV_DIM + 2 * N_HEADS + 2 * D_MODEL
    w_main = jnp.concatenate([w[:, o_xbc:o_dt], w[:, :o_xbc], w[:, o_gl:], w[:, o_cv:o_cg], w[:, o_cg:o_gl]],
                             axis=1).astype(BF16)
    w_dt = jnp.pad(w[:, o_dt:o_cv], ((0, 0), (0, LANES - 2 * N_HEADS))).astype(BF16)
    proj, dt_raw = _in_projection(x_p, x_s, mod3, norm1_w[0], w_main, w_dt)

    pad_heads = lambda v: jnp.pad(v.reshape(1, 2 * N_HEADS), ((0, 0), (0, LANES - 2 * N_HEADS)))
    dtb = pad_heads(ssm_dt_bias[0])
    a_neg = pad_heads(-jnp.exp(ssm_a_log[0]))
    dvec = jnp.repeat(ssm_d[0], HEAD_DIM).reshape(1, D_INNER)
    nw = ssm_norm_w[0].reshape(1, D_INNER)
    e2 = _head_expand_matrix()
    xbc_p = _ssm_conv(proj, ssm_conv_w[0], ssm_conv_b[0], seq_len=SEQ, n_seq=BATCH, row_block0=0)
    xbc_s = _ssm_conv(proj, ssm_conv_w[0], ssm_conv_b[0], seq_len=DEC_SEQ, n_seq=DEC_BATCH,
                      row_block0=T_PROMPT // DEC_SEQ)
    y_p, fin = _ssd(xbc_p, dt_raw, proj, None, dtb, a_neg, dvec, nw, e2,
                    seq_len=SEQ, n_seq=BATCH, tok0=0, want_final=True)
    init = state_ssm.reshape(DEC_BATCH, 2, D_INNER, D_STATE)
    (y_s,) = _ssd(xbc_s, dt_raw, proj, init, dtb, a_neg, dvec, nw, e2,
                  seq_len=DEC_SEQ, n_seq=DEC_BATCH, tok0=T_PROMPT, want_final=False)

    v_p = _conformer(proj, conf_conv_w[0], conf_conv_b[0], conf_ln_w[0], conf_ln_b[0],
                     seq_len=SEQ, n_seq=BATCH, row_block0=0, grid_mode=False)
    v_s = _conformer(proj, conf_conv_w[0], conf_conv_b[0], conf_ln_w[0], conf_ln_b[0],
                     seq_len=DEC_SEQ, n_seq=DEC_BATCH, row_block0=T_PROMPT // DEC_SEQ, grid_mode=True)

    sh_gu = jnp.concatenate([sh_w_gate[0], sh_w_up[0]], axis=1).astype(BF16)
    pre, v_all, scores = _merge(y_p, y_s, v_p, v_s, proj, x_p, x_s, mod3, w_out_ssm[0].astype(BF16),
                                w_out_conf[0].astype(BF16), w_o[0].astype(BF16), norm2_w[0], router_w[0],
                                sh_gu, sh_w_down[0].astype(BF16))

    idx_t, w_t, rank_t, cnt = _route(scores, router_bias[0])
    start = _group_starts(cnt[:, 0].astype(I32))
    dest_t = _dest_slots(idx_t, rank_t, start[:N_EXPERTS])

    xs = _dispatch(dest_t, v_all)
    y_rows = _grouped_mlp(start, xs, exp_w_gate[0], exp_w_up[0], exp_w_down[0])
    out_p, out_s = _combine(dest_t, y_rows, pre, w_t.T, mod3, final_norm_w)

    y_prompt = out_p.reshape(BATCH, SEQ, D_MODEL)
    y_sample = out_s.reshape(DEC_BATCH, DEC_SEQ, D_MODEL)
    new_state = fin.reshape(BATCH, 1, 2, N_HEADS, HEAD_DIM, D_STATE)
    return (y_prompt, y_sample, new_state)
```

```python
import functools

import jax
import jax.numpy as jnp
from jax import lax
from jax.experimental import pallas as pl
from jax.experimental.pallas import tpu as pltpu

F32 = jnp.float32
BF16 = jnp.bfloat16
I32 = jnp.int32

D_MODEL = 1024
BATCH = 32
SEQ = 256
DEC_BATCH = 8
DEC_SEQ = 1024
GRID_W = 64
GRID_H = DEC_SEQ // GRID_W
D_INNER = 2048
HEAD_DIM = 64
N_HEADS = 32
D_STATE = 128
N_GROUPS = 8
HEADS_PER_GROUP = N_HEADS // N_GROUPS
GROUP_W = HEADS_PER_GROUP * HEAD_DIM
D_CONV_SSM = 5
CHUNK = 128
CONV_DIM = D_INNER + 2 * N_GROUPS * D_STATE
CONF_K = 31
CONF_PAD = CONF_K // 2
N_EXPERTS = 256
TOP_K = 8
N_EXPERT_GROUPS = 8
EXPERTS_PER_GROUP = N_EXPERTS // N_EXPERT_GROUPS
TOPK_GROUPS = 4
D_EXPERT = 256
D_SHARED = 256
ROUTED_SCALE = 2.5
N_MOD = 6
EPS = 1e-6

T_PROMPT = BATCH * SEQ
T_SAMPLE = DEC_BATCH * DEC_SEQ
T_ALL = T_PROMPT + T_SAMPLE
N_ASSIGN = T_ALL * TOP_K
MOD_ROWS = 16
CTX_ROW = DEC_BATCH

SUBLANES = 8
LANES = 128
VMEM_LIMIT = 56 * 1024 * 1024

PROJ_W = CONV_DIM + D_INNER + 2 * D_MODEL + 2 * D_MODEL
COL_Z = CONV_DIM // D_INNER
COL_GL = COL_Z + 1
COL_CV = (CONV_DIM + 2 * D_INNER) // D_MODEL
COL_CG = COL_CV + 1

HALF_D = D_MODEL // 2
ROW_TILE = HALF_D // LANES
GMM_TM = 256
N_ROWS = N_ASSIGN + GMM_TM
TAIL_SIZES = (128, 64, 32, 16, 8, 4, 2, 1)


def _cparams(sem, vmem=VMEM_LIMIT):
    return pltpu.CompilerParams(dimension_semantics=sem, vmem_limit_bytes=vmem)


def _silu(x):
    return x * jax.nn.sigmoid(x)


def _split2(x):
    hi = x.astype(BF16)
    lo = (x - hi.astype(F32)).astype(BF16)
    return hi, lo


def _split3(x):
    b1 = x.astype(BF16)
    r = x - b1.astype(F32)
    b2 = r.astype(BF16)
    b3 = (r - b2.astype(F32)).astype(BF16)
    return b1, b2, b3


def _dot(a, b):
    return jnp.dot(a, b, preferred_element_type=F32)


def _dot_exact_lhs(a_exact, b):
    b1, b2, b3 = _split3(b)
    return _dot(a_exact, b1) + _dot(a_exact, b2) + _dot(a_exact, b3)


def _dot_f32(a, b):
    a1, a2 = _split2(a)
    b1, b2 = _split2(b)
    return _dot(a1, b1) + _dot(a1, b2) + _dot(a2, b1)


def _mod_row_of_block(i, blocks_prompt, blocks_per_sample_seq):
    return jnp.where(i < blocks_prompt, CTX_ROW, (i - blocks_prompt) // blocks_per_sample_seq)


def _mod_kernel(c_ref, w_ref, b_ref, o_ref):
    c = c_ref[...]
    o_ref[...] = _dot_f32(_silu(c), w_ref[...]) + b_ref[...]


def _modulation(cc, w_mod, b_mod):
    tn = 512
    n = N_MOD * D_MODEL
    return pl.pallas_call(
        _mod_kernel,
        out_shape=jax.ShapeDtypeStruct((MOD_ROWS, n), F32),
        grid=(n // tn,),
        in_specs=[pl.BlockSpec((MOD_ROWS, D_MODEL), lambda j: (0, 0)),
                  pl.BlockSpec((D_MODEL, tn), lambda j: (0, j)),
                  pl.BlockSpec((1, tn), lambda j: (0, j))],
        out_specs=pl.BlockSpec((MOD_ROWS, tn), lambda j: (0, j)),
        compiler_params=_cparams(("arbitrary",)),
        name="modulation",
    )(cc, w_mod, b_mod.reshape(1, n))


INPROJ_TM = 1024
INPROJ_TN = 2048


def _two_path_specs(block, n_prompt_blocks):
    last = n_prompt_blocks - 1
    return (pl.BlockSpec(block, lambda i, *_: (jnp.minimum(i, last), 0)),
            pl.BlockSpec(block, lambda i, *_: (jnp.maximum(i - n_prompt_blocks, 0), 0)))


def _pick_path(n_prompt_blocks, prompt_ref, sample_ref):
    return jnp.where(pl.program_id(0) < n_prompt_blocks, prompt_ref[...], sample_ref[...])


def _inproj_kernel(xp_ref, xs_ref, mod_ref, n1_ref, w_ref, wdt_ref, o_ref, dt_ref, u_scr):
    @pl.when(pl.program_id(1) == 0)
    def _():
        x = _pick_path(T_PROMPT // INPROJ_TM, xp_ref, xs_ref)
        ms = jnp.mean(x * x, axis=-1, keepdims=True)
        y = x * lax.rsqrt(ms + EPS) * n1_ref[...]
        m = mod_ref[...]
        u = (y * (1.0 + m[1:2]) + m[0:1]).astype(BF16)
        u_scr[...] = u
        dt_ref[...] = _dot(u, wdt_ref[...])

    o_ref[...] = _dot(u_scr[...], w_ref[...]).astype(BF16)


def _in_projection(x_p, x_s, mod3, norm1_w, w_main, w_dt):
    tm, tn = INPROJ_TM, INPROJ_TN
    row = functools.partial(_mod_row_of_block, blocks_prompt=T_PROMPT // tm,
                            blocks_per_sample_seq=DEC_SEQ // tm)
    return pl.pallas_call(
        _inproj_kernel,
        out_shape=(jax.ShapeDtypeStruct((T_ALL, PROJ_W), BF16),
                   jax.ShapeDtypeStruct((T_ALL, LANES), F32)),
        grid=(T_ALL // tm, PROJ_W // tn),
        in_specs=[*_two_path_specs((tm, D_MODEL), T_PROMPT // tm),
                  pl.BlockSpec((None, N_MOD, D_MODEL), lambda i, j: (row(i), 0, 0)),
                  pl.BlockSpec((1, D_MODEL), lambda i, j: (0, 0)),
                  pl.BlockSpec((D_MODEL, tn), lambda i, j: (0, j)),
                  pl.BlockSpec((D_MODEL, LANES), lambda i, j: (0, 0))],
        out_specs=(pl.BlockSpec((tm, tn), lambda i, j: (i, j)),
                   pl.BlockSpec((tm, LANES), lambda i, j: (i, 0))),
        scratch_shapes=[pltpu.VMEM((tm, D_MODEL), BF16)],
        compiler_params=_cparams(("arbitrary", "arbitrary")),
        name="in_projection",
    )(x_p, x_s, mod3, norm1_w.reshape(1, D_MODEL), w_main, w_dt)


SSMCONV_TN = 1024
SSMCONV_RT = 64
SSMCONV_HALO = SUBLANES


def _ssmconv_kernel(x_ref, w_ref, b_ref, o_ref, pad_scr, out_scr):
    seq_len = x_ref.shape[0]
    half = D_CONV_SSM // 2
    n_half = SSMCONV_RT // 2
    zeros = jnp.zeros((SSMCONV_HALO, LANES), F32)
    for sl in range(SSMCONV_TN // LANES):
        cs = slice(sl * LANES, (sl + 1) * LANES)
        pad_scr[sl, 0:SSMCONV_HALO, :] = zeros
        pad_scr[sl, SSMCONV_HALO + seq_len:, :] = zeros
        pad_scr[sl, SSMCONV_HALO:SSMCONV_HALO + seq_len, :] = x_ref[:, cs].astype(F32)
        wk = [w_ref[k:k + 1, cs] for k in range(D_CONV_SSM)]
        bias = b_ref[:, cs]
        for r0 in range(0, seq_len, SSMCONV_RT):
            for phase in range(2):
                acc = bias
                for k in range(D_CONV_SSM):
                    acc = acc + wk[k] * pad_scr[sl, pl.ds(SSMCONV_HALO + r0 + phase + k - half, n_half, stride=2), :]
                out_scr[sl, pl.ds(r0 + phase, n_half, stride=2), :] = _silu(acc)
        o_ref[:, cs] = out_scr[sl].astype(BF16)


def _ssm_conv(proj, conv_w, conv_b, *, seq_len, n_seq, row_block0):
    tn = SSMCONV_TN
    return pl.pallas_call(
        _ssmconv_kernel,
        out_shape=jax.ShapeDtypeStruct((n_seq * seq_len, CONV_DIM), BF16),
        grid=(n_seq, CONV_DIM // tn),
        in_specs=[pl.BlockSpec((seq_len, tn), lambda b, j: (row_block0 + b, j)),
                  pl.BlockSpec((D_CONV_SSM, tn), lambda b, j: (0, j)),
                  pl.BlockSpec((1, tn), lambda b, j: (0, j))],
        out_specs=pl.BlockSpec((seq_len, tn), lambda b, j: (b, j)),
        scratch_shapes=[pltpu.VMEM((tn // LANES, seq_len + 2 * SSMCONV_HALO, LANES), F32),
                        pltpu.VMEM((tn // LANES, seq_len, LANES), F32)],
        compiler_params=_cparams(("arbitrary", "arbitrary")),
        name="ssm_conv",
    )(proj, conv_w, conv_b.reshape(1, CONV_DIM))


def _tri_masks():
    ii = lax.broadcasted_iota(I32, (CHUNK, CHUNK), 0)
    jj = lax.broadcasted_iota(I32, (CHUNK, CHUNK), 1)
    return ii, jj


def _chunk_decays(dt_ref, dtb_ref, a_ref):
    ii, jj = _tri_masks()
    pre = dt_ref[...] + dtb_ref[...]
    dt = jnp.maximum(pre, 0.0) + jnp.log(1.0 + jnp.exp(-jnp.abs(pre)))
    la = dt * a_ref[...]
    tri_lo = jnp.where(jj <= ii, 1.0, 0.0).astype(BF16)
    tri_up = jnp.where(jj >= ii, 1.0, 0.0).astype(BF16)
    cs_prefix = _dot_exact_lhs(tri_lo, la)
    cs_suffix = _dot_exact_lhs(tri_up, la)
    fwd_lane = lax.broadcasted_iota(I32, (CHUNK, LANES), 1) < N_HEADS
    cs = jnp.where(fwd_lane, cs_prefix, cs_suffix)
    tot = jnp.where(fwd_lane[0:1], cs_prefix[CHUNK - 1:CHUNK, :], cs_suffix[0:1, :])
    return dt, cs, tot


def _transpose_blocks(src, rows, cols):
    out_rows = []
    for cb in range(cols // LANES):
        pieces = [src[rb * LANES:(rb + 1) * LANES, cb * LANES:(cb + 1) * LANES].T
                  for rb in range(rows // LANES)]
        out_rows.append(jnp.concatenate(pieces, axis=1) if len(pieces) > 1 else pieces[0])
    return jnp.concatenate(out_rows, axis=0) if len(out_rows) > 1 else out_rows[0]


def _ssd_kernel(*refs, n_chunks, has_init, want_final):
    it = iter(refs)
    xbc_ref, dt_ref, z_ref = next(it), next(it), next(it)
    init_ref = next(it) if has_init else None
    dtb_ref, a_ref, dvec_ref, nw_ref, e2_ref = next(it), next(it), next(it), next(it), next(it)
    out_ref = next(it)
    fin_ref = next(it) if want_final else None
    sf_scr, sb_scr, df_scr, db_scr, dtcs_scr = next(it), next(it), next(it), next(it), next(it)

    phase = pl.program_id(1)
    c = pl.program_id(2)

    @pl.when(phase == 0)
    def _chunk_states():
        dt, cs, tot = _chunk_decays(dt_ref, dtb_ref, a_ref)
        dtcs_scr[c, 0] = dt
        dtcs_scr[c, 1] = cs
        w_in = dt * jnp.exp(tot - cs)
        pack = 2 * SUBLANES
        dec_hi, dec_lo = _split2(jnp.exp(jnp.broadcast_to(tot, (pack, LANES))))
        expanded = _dot(jnp.concatenate([w_in.astype(BF16), dec_hi, dec_lo], axis=0), e2_ref[...])
        w_exp = expanded[:CHUNK]
        dec_exp = expanded[CHUNK:CHUNK + SUBLANES] + expanded[CHUNK + pack:CHUNK + pack + SUBLANES]
        df_scr[c] = dec_exp[:, :D_INNER]
        db_scr[c] = dec_exp[:, D_INNER:]
        for g in range(N_GROUPS):
            lo = g * GROUP_W
            xg = xbc_ref[:, lo:lo + GROUP_W].astype(F32)
            xd_f = (xg * w_exp[:, lo:lo + GROUP_W]).astype(BF16)
            xd_b = (xg * w_exp[:, D_INNER + lo:D_INNER + lo + GROUP_W]).astype(BF16)
            bg = xbc_ref[:, D_INNER + g * D_STATE:D_INNER + (g + 1) * D_STATE]
            bg_t = bg.astype(F32).T.astype(BF16)
            sf_scr[c, :, lo:lo + GROUP_W] = _dot(bg_t, xd_f)
            sb_scr[c, :, lo:lo + GROUP_W] = _dot(bg_t, xd_b)

    @pl.when(jnp.logical_and(phase == 1, c == 0))
    def _recurrence():
        for g in range(N_GROUPS):
            lo = g * GROUP_W
            if has_init:
                prev_f = _transpose_blocks(init_ref[0, 0, lo:lo + GROUP_W, :], GROUP_W, D_STATE)
                prev_b = _transpose_blocks(init_ref[0, 1, lo:lo + GROUP_W, :], GROUP_W, D_STATE)
            else:
                prev_f = jnp.zeros((D_STATE, GROUP_W), F32)
                prev_b = jnp.zeros((D_STATE, GROUP_W), F32)
            for cc in range(n_chunks):
                s = sf_scr[cc, :, lo:lo + GROUP_W]
                sf_scr[cc, :, lo:lo + GROUP_W] = prev_f
                prev_f = df_scr[cc, 0:1, lo:lo + GROUP_W] * prev_f + s
            for cc in reversed(range(n_chunks)):
                s = sb_scr[cc, :, lo:lo + GROUP_W]
                sb_scr[cc, :, lo:lo + GROUP_W] = prev_b
                prev_b = db_scr[cc, 0:1, lo:lo + GROUP_W] * prev_b + s
            if want_final:
                fin_ref[0, 0, lo:lo + GROUP_W, :] = _transpose_blocks(prev_f, D_STATE, GROUP_W)
                fin_ref[0, 1, lo:lo + GROUP_W, :] = _transpose_blocks(prev_b, D_STATE, GROUP_W)

    @pl.when(phase == 1)
    def _outputs():
        ii, jj = _tri_masks()
        dt = dtcs_scr[c, 0]
        cs = dtcs_scr[c, 1]
        out_dec = _dot(jnp.exp(cs).astype(BF16), e2_ref[...])
        cs_t = cs.T
        dt_t = dt.T
        causal = ii >= jj
        below = ii > jj
        above = jj > ii
        left = lax.broadcasted_iota(I32, (CHUNK, LANES), 1) < HEAD_DIM
        for g in range(N_GROUPS):
            lo = g * GROUP_W
            bg = xbc_ref[:, D_INNER + g * D_STATE:D_INNER + (g + 1) * D_STATE]
            cg = xbc_ref[:, D_INNER + N_GROUPS * D_STATE + g * D_STATE:
                         D_INNER + N_GROUPS * D_STATE + (g + 1) * D_STATE]
            cb = lax.dot_general(cg, bg, (((1,), (1,)), ((), ())), preferred_element_type=F32)
            pf = sf_scr[c, :, lo:lo + GROUP_W].astype(BF16)
            pb = sb_scr[c, :, lo:lo + GROUP_W].astype(BF16)
            y_off = (_dot(cg, pf) * out_dec[:, lo:lo + GROUP_W]
                     + _dot(cg, pb) * out_dec[:, D_INNER + lo:D_INNER + lo + GROUP_W])
            pairs = []
            for m in range(HEADS_PER_GROUP // 2):
                x_pair = xbc_ref[:, lo + m * LANES:lo + (m + 1) * LANES]
                ys = []
                for hh in range(2):
                    h = g * HEADS_PER_GROUP + 2 * m + hh
                    hb = N_HEADS + h
                    seg = jnp.where(causal, cs[:, h:h + 1] - cs_t[h:h + 1, :], cs[:, hb:hb + 1] - cs_t[hb:hb + 1, :])
                    dt_f, dt_b = dt_t[h:h + 1, :], dt_t[hb:hb + 1, :]
                    mix = jnp.exp(seg) * jnp.where(below, dt_f, jnp.where(above, dt_b, dt_f + dt_b))
                    ys.append(_dot((cb * mix).astype(BF16), x_pair))
                pairs.append(jnp.where(left, ys[0], ys[1]))
            y_diag = jnp.concatenate(pairs, axis=1)
            xg = xbc_ref[:, lo:lo + GROUP_W].astype(F32)
            y = y_diag + y_off + dvec_ref[:, lo:lo + GROUP_W] * xg
            zg = z_ref[:, lo:lo + GROUP_W].astype(F32)
            y = y * _silu(zg)
            ms = jnp.mean(y * y, axis=-1, keepdims=True)
            out_ref[:, lo:lo + GROUP_W] = (y * lax.rsqrt(ms + EPS) * nw_ref[:, lo:lo + GROUP_W]).astype(BF16)


def _ssd(xbc, dt_raw, proj, init, dtb, a_neg, dvec, norm_w, e2, *, seq_len, n_seq, tok0, want_final):
    nc = seq_len // CHUNK
    blk0 = tok0 // CHUNK
    has_init = init is not None
    in_specs = [pl.BlockSpec((CHUNK, CONV_DIM), lambda b, p, c: (b * nc + c, 0)),
                pl.BlockSpec((CHUNK, LANES), lambda b, p, c: (blk0 + b * nc + c, 0)),
                pl.BlockSpec((CHUNK, D_INNER), lambda b, p, c: (blk0 + b * nc + c * p, COL_Z))]
    args = [xbc, dt_raw, proj]
    if has_init:
        in_specs.append(pl.BlockSpec((1, 2, D_INNER, D_STATE), lambda b, p, c: (b, 0, 0, 0)))
        args.append(init)
    const = lambda b, p, c: (0, 0)
    in_specs += [pl.BlockSpec((1, LANES), const), pl.BlockSpec((1, LANES), const),
                 pl.BlockSpec((1, D_INNER), const), pl.BlockSpec((1, D_INNER), const),
                 pl.BlockSpec((LANES, 2 * D_INNER), const)]
    args += [dtb, a_neg, dvec, norm_w, e2]
    out_shape = [jax.ShapeDtypeStruct((n_seq * seq_len, D_INNER), BF16)]
    out_specs = [pl.BlockSpec((CHUNK, D_INNER), lambda b, p, c: (b * nc + c * p, 0))]
    if want_final:
        out_shape.append(jax.ShapeDtypeStruct((n_seq, 2, D_INNER, D_STATE), F32))
        out_specs.append(pl.BlockSpec((1, 2, D_INNER, D_STATE), lambda b, p, c: (b, 0, 0, 0)))
    res = pl.pallas_call(
        functools.partial(_ssd_kernel, n_chunks=nc, has_init=has_init, want_final=want_final),
        out_shape=tuple(out_shape),
        grid=(n_seq, 2, nc),
        in_specs=in_specs,
        out_specs=tuple(out_specs),
        scratch_shapes=[pltpu.VMEM((nc, D_STATE, D_INNER), F32),
                        pltpu.VMEM((nc, D_STATE, D_INNER), F32),
                        pltpu.VMEM((nc, SUBLANES, D_INNER), F32),
                        pltpu.VMEM((nc, SUBLANES, D_INNER), F32),
                        pltpu.VMEM((nc, 2, CHUNK, LANES), F32)],
        compiler_params=_cparams(("arbitrary", "arbitrary", "arbitrary")),
        name="ssd_final" if want_final else "ssd_init",
    )(*args)
    return res


CONF_SLABS = D_MODEL // LANES
CONF_HALO = 2 * SUBLANES
CONF_RT = 64


def _glu_slab(cv_ref, cg_ref, sl):
    cs = slice(sl * LANES, (sl + 1) * LANES)
    return cv_ref[:, cs].astype(F32) * jax.nn.sigmoid(cg_ref[:, cs].astype(F32))


def _conv_taps_strided(src_ref, sl, src_row0, dst_ref, dst_row0, w_ref, b_ref):
    cs = slice(sl * LANES, (sl + 1) * LANES)
    n_half = CONF_RT // 2
    for phase in range(2):
        acc = b_ref[:, cs]
        for k in range(CONF_K):
            x = src_ref[sl, pl.ds(src_row0 + phase + k - CONF_PAD, n_half, stride=2), :]
            acc = acc + w_ref[k:k + 1, cs] * x
        dst_ref[sl, pl.ds(dst_row0 + phase, n_half, stride=2), :] = acc


def _layernorm_silu_slabs(acc_scr, lw_ref, lb_ref, o_ref):
    n = acc_scr.shape[1]
    tot = jnp.zeros((n, 1), F32)
    for sl in range(CONF_SLABS):
        tot = tot + jnp.sum(acc_scr[sl], axis=-1, keepdims=True)
    mu = tot * (1.0 / D_MODEL)
    sq = jnp.zeros((n, 1), F32)
    for sl in range(CONF_SLABS):
        d = acc_scr[sl] - mu
        sq = sq + jnp.sum(d * d, axis=-1, keepdims=True)
    rstd = lax.rsqrt(sq * (1.0 / D_MODEL) + EPS)
    for sl in range(CONF_SLABS):
        cs = slice(sl * LANES, (sl + 1) * LANES)
        o_ref[:, cs] = _silu((acc_scr[sl] - mu) * rstd * lw_ref[:, cs] + lb_ref[:, cs]).astype(BF16)


def _conf_seq_kernel(cv_ref, cg_ref, w_ref, b_ref, lw_ref, lb_ref, o_ref, pad_scr, acc_scr):
    seq_len = cv_ref.shape[0]
    zeros = jnp.zeros((CONF_HALO, LANES), F32)
    for sl in range(CONF_SLABS):
        pad_scr[sl, 0:CONF_HALO, :] = zeros
        pad_scr[sl, CONF_HALO + seq_len:, :] = zeros
        pad_scr[sl, CONF_HALO:CONF_HALO + seq_len, :] = _glu_slab(cv_ref, cg_ref, sl)
    for sl in range(CONF_SLABS):
        for r0 in range(0, seq_len, CONF_RT):
            _conv_taps_strided(pad_scr, sl, CONF_HALO + r0, acc_scr, r0, w_ref, b_ref)
    _layernorm_silu_slabs(acc_scr, lw_ref, lb_ref, o_ref)


def _conf_grid_kernel(cv_ref, cg_ref, w_ref, b_ref, lw_ref, lb_ref, o_ref, v_scr, pad_scr, acc_scr):
    half_slabs = CONF_SLABS // 2
    stride = GRID_W + 2 * CONF_HALO
    pad_scr[...] = jnp.zeros(pad_scr.shape, F32)
    for sl in range(half_slabs):
        v = _glu_slab(cv_ref, cg_ref, sl)
        for r in range(GRID_H):
            pad_scr[sl, r * stride + CONF_HALO:r * stride + CONF_HALO + GRID_W, :] = v[r * GRID_W:(r + 1) * GRID_W]
    for sl in range(half_slabs, CONF_SLABS):
        v_scr[sl - half_slabs] = _glu_slab(cv_ref, cg_ref, sl)
    for sl in range(half_slabs):
        for r in range(GRID_H):
            _conv_taps_strided(pad_scr, sl, r * stride + CONF_HALO, acc_scr, r * GRID_W, w_ref, b_ref)
    for sl in range(half_slabs, CONF_SLABS):
        cs = slice(sl * LANES, (sl + 1) * LANES)
        for r in range(GRID_H):
            acc = jnp.broadcast_to(b_ref[:, cs], (GRID_W, LANES))
            for r2 in range(GRID_H):
                k = r2 - r + CONF_PAD
                acc = acc + w_ref[k:k + 1, cs] * v_scr[sl - half_slabs, r2 * GRID_W:(r2 + 1) * GRID_W, :]
            acc_scr[sl, r * GRID_W:(r + 1) * GRID_W, :] = acc
    _layernorm_silu_slabs(acc_scr, lw_ref, lb_ref, o_ref)


def _conformer(proj, conv_w, conv_b, ln_w, ln_b, *, seq_len, n_seq, row_block0, grid_mode):
    if grid_mode:
        body = _conf_grid_kernel
        scratch = [pltpu.VMEM((CONF_SLABS // 2, seq_len, LANES), F32),
                   pltpu.VMEM((CONF_SLABS // 2, GRID_H * (GRID_W + 2 * CONF_HALO), LANES), F32),
                   pltpu.VMEM((CONF_SLABS, seq_len, LANES), F32)]
    else:
        body = _conf_seq_kernel
        scratch = [pltpu.VMEM((CONF_SLABS, seq_len + 2 * CONF_HALO, LANES), F32),
                   pltpu.VMEM((CONF_SLABS, seq_len, LANES), F32)]
    const = lambda b: (0, 0)
    return pl.pallas_call(
        body,
        out_shape=jax.ShapeDtypeStruct((n_seq * seq_len, D_MODEL), BF16),
        grid=(n_seq,),
        in_specs=[pl.BlockSpec((seq_len, D_MODEL), lambda b: (row_block0 + b, COL_CV)),
                  pl.BlockSpec((seq_len, D_MODEL), lambda b: (row_block0 + b, COL_CG)),
                  pl.BlockSpec((CONF_K, D_MODEL), const),
                  pl.BlockSpec((1, D_MODEL), const),
                  pl.BlockSpec((1, D_MODEL), const),
                  pl.BlockSpec((1, D_MODEL), const)],
        out_specs=pl.BlockSpec((seq_len, D_MODEL), lambda b: (b, 0)),
        scratch_shapes=scratch,
        compiler_params=_cparams(("arbitrary",)),
        name="conformer_grid" if grid_mode else "conformer_seq",
    )(proj, proj, conv_w, conv_b.reshape(1, D_MODEL), ln_w.reshape(1, D_MODEL), ln_b.reshape(1, D_MODEL))


MERGE_TM = 256


def _merge_kernel(yap_ref, yas_ref, ybp_ref, ybs_ref, gl_ref, xp_ref, xs_ref, mod_ref, wa_ref, wb_ref, wo_ref,
                  n2_ref, rw_ref, sgu_ref, sd_ref, pre_ref, v_ref, s_ref):
    npb = T_PROMPT // MERGE_TM
    y_a = _dot(_pick_path(npb, yap_ref, yas_ref), wa_ref[...])
    y_b = _dot(_pick_path(npb, ybp_ref, ybs_ref), wb_ref[...])
    gates = jax.nn.sigmoid(gl_ref[...].astype(F32))
    mix = gates[:, :D_MODEL] * y_a + gates[:, D_MODEL:] * y_b
    out = _dot(mix.astype(BF16), wo_ref[...])
    m = mod_ref[...]
    x1 = _pick_path(npb, xp_ref, xs_ref) + m[2:3] * out
    ms = jnp.mean(x1 * x1, axis=-1, keepdims=True)
    v = x1 * lax.rsqrt(ms + EPS) * n2_ref[...] * (1.0 + m[4:5]) + m[3:4]
    v_ref[...] = v
    s_ref[...] = jax.nn.sigmoid(_dot_f32(v, rw_ref[...]))
    hgu = _dot(v.astype(BF16), sgu_ref[...])
    act = _silu(hgu[:, :D_SHARED]) * hgu[:, D_SHARED:]
    shared = _dot(act.astype(BF16), sd_ref[...])
    pre_ref[...] = x1 + m[5:6] * shared


def _merge(y_ssm_p, y_ssm_s, v_conf_p, v_conf_s, proj, x_p, x_s, mod3, w_out_ssm, w_out_conf, w_o, norm2_w,
           router_w, sh_gu, sh_d):
    tm = MERGE_TM
    npb = T_PROMPT // tm
    row = functools.partial(_mod_row_of_block, blocks_prompt=npb, blocks_per_sample_seq=DEC_SEQ // tm)
    const = lambda i: (0, 0)
    return pl.pallas_call(
        _merge_kernel,
        out_shape=(jax.ShapeDtypeStruct((T_ALL, D_MODEL), F32),
                   jax.ShapeDtypeStruct((T_ALL, D_MODEL), F32),
                   jax.ShapeDtypeStruct((T_ALL, N_EXPERTS), F32)),
        grid=(T_ALL // tm,),
        in_specs=[*_two_path_specs((tm, D_INNER), npb),
                  *_two_path_specs((tm, D_MODEL), npb),
                  pl.BlockSpec((tm, 2 * D_MODEL), lambda i: (i, COL_GL)),
                  *_two_path_specs((tm, D_MODEL), npb),
                  pl.BlockSpec((None, N_MOD, D_MODEL), lambda i: (row(i), 0, 0)),
                  pl.BlockSpec((D_INNER, D_MODEL), const),
                  pl.BlockSpec((D_MODEL, D_MODEL), const),
                  pl.BlockSpec((D_MODEL, D_MODEL), const),
                  pl.BlockSpec((1, D_MODEL), const),
                  pl.BlockSpec((D_MODEL, N_EXPERTS), const),
                  pl.BlockSpec((D_MODEL, 2 * D_SHARED), const),
                  pl.BlockSpec((D_SHARED, D_MODEL), const)],
        out_specs=(pl.BlockSpec((tm, D_MODEL), lambda i: (i, 0)),
                   pl.BlockSpec((tm, D_MODEL), lambda i: (i, 0)),
                   pl.BlockSpec((tm, N_EXPERTS), lambda i: (i, 0))),
        compiler_params=_cparams(("arbitrary",)),
        name="merge",
    )(y_ssm_p, y_ssm_s, v_conf_p, v_conf_s, proj, x_p, x_s, mod3, w_out_ssm, w_out_conf, w_o,
      norm2_w.reshape(1, D_MODEL), router_w, sh_gu, sh_d)


ROUTE_TB = 256


def _first_index_of_max(vals, iota, sentinel):
    m = jnp.max(vals, axis=0, keepdims=True)
    idx = jnp.min(jnp.where(vals == m, iota, jnp.float32(sentinel)), axis=0, keepdims=True)
    return m, idx


def _route_kernel(s_ref, bias_ref, idx_ref, w_ref, rank_ref, cnt_ref, run_scr):
    i = pl.program_id(0)
    tb = ROUTE_TB
    neg = jnp.float32(-jnp.inf)

    @pl.when(i == 0)
    def _():
        run_scr[...] = jnp.zeros(run_scr.shape, F32)

    s_t = s_ref[...].T
    sb_t = s_t + bias_ref[...]
    eiota = lax.broadcasted_iota(I32, (N_EXPERTS, tb), 0).astype(F32)

    liota = lax.broadcasted_iota(I32, (EXPERTS_PER_GROUP, tb), 0).astype(F32)
    gscores = []
    for g in range(N_EXPERT_GROUPS):
        blk = sb_t[g * EXPERTS_PER_GROUP:(g + 1) * EXPERTS_PER_GROUP, :]
        m1, i1 = _first_index_of_max(blk, liota, EXPERTS_PER_GROUP)
        m2 = jnp.max(jnp.where(liota == i1, neg, blk), axis=0, keepdims=True)
        gscores.append(m1 + m2)
    gs = jnp.concatenate(gscores, axis=0)
    giota = lax.broadcasted_iota(I32, (N_EXPERT_GROUPS, tb), 0).astype(F32)
    gsel = jnp.zeros((N_EXPERT_GROUPS, tb), F32)
    for _ in range(TOPK_GROUPS):
        _, gi = _first_index_of_max(gs, giota, N_EXPERT_GROUPS)
        hit = giota == gi
        gsel = jnp.where(hit, 1.0, gsel)
        gs = jnp.where(hit, neg, gs)
    emask = jnp.concatenate(
        [jnp.broadcast_to(gsel[g:g + 1, :], (EXPERTS_PER_GROUP, tb)) for g in range(N_EXPERT_GROUPS)], axis=0)
    masked = jnp.where(emask > 0.5, sb_t, neg)

    onehots, idxs, wts = [], [], []
    for _ in range(TOP_K):
        _, ei = _first_index_of_max(masked, eiota, N_EXPERTS)
        hit = eiota == ei
        onehots.append(hit)
        idxs.append(ei)
        wts.append(jnp.sum(jnp.where(hit, s_t, 0.0), axis=0, keepdims=True))
        masked = jnp.where(hit, neg, masked)
    w = jnp.concatenate(wts, axis=0)
    w_ref[...] = w / jnp.sum(w, axis=0, keepdims=True) * ROUTED_SCALE
    idx_ref[...] = jnp.concatenate(idxs, axis=0).astype(I32)

    assign = jnp.zeros((N_EXPERTS, tb), F32)
    for hit in onehots:
        assign = jnp.where(hit, 1.0, assign)
    assign_b = assign.astype(BF16)
    ti = lax.broadcasted_iota(I32, (tb, tb), 0)
    tj = lax.broadcasted_iota(I32, (tb, tb), 1)
    before = jnp.where(ti < tj, 1.0, 0.0).astype(BF16)
    within = _dot(assign_b, before)
    run = run_scr[...]
    total = within + jnp.concatenate([run] * (tb // LANES), axis=1)
    rank_ref[...] = jnp.concatenate(
        [jnp.sum(jnp.where(hit, total, 0.0), axis=0, keepdims=True) for hit in onehots], axis=0).astype(I32)
    new_run = run + _dot(assign_b, jnp.ones((tb, LANES), BF16))
    run_scr[...] = new_run
    cnt_ref[...] = new_run


def _route(scores, router_bias):
    tb = ROUTE_TB
    return pl.pallas_call(
        _route_kernel,
        out_shape=(jax.ShapeDtypeStruct((TOP_K, T_ALL), I32),
                   jax.ShapeDtypeStruct((TOP_K, T_ALL), F32),
                   jax.ShapeDtypeStruct((TOP_K, T_ALL), I32),
                   jax.ShapeDtypeStruct((N_EXPERTS, LANES), F32)),
        grid=(T_ALL // tb,),
        in_specs=[pl.BlockSpec((tb, N_EXPERTS), lambda i: (i, 0)),
                  pl.BlockSpec((N_EXPERTS, 1), lambda i: (0, 0))],
        out_specs=(pl.BlockSpec((TOP_K, tb), lambda i: (0, i)),
                   pl.BlockSpec((TOP_K, tb), lambda i: (0, i)),
                   pl.BlockSpec((TOP_K, tb), lambda i: (0, i)),
                   pl.BlockSpec((N_EXPERTS, LANES), lambda i: (0, 0))),
        scratch_shapes=[pltpu.VMEM((N_EXPERTS, LANES), F32)],
        compiler_params=_cparams(("arbitrary",)),
        name="route",
    )(scores, router_bias.reshape(N_EXPERTS, 1))


def _dest_kernel(idx_ref, rank_ref, start_ref, dest_ref):
    tb = idx_ref.shape[1]
    eiota = lax.broadcasted_iota(I32, (N_EXPERTS, tb), 0)
    start = jnp.broadcast_to(start_ref[...], (N_EXPERTS, tb))
    idx = idx_ref[...]
    rows = [jnp.sum(jnp.where(eiota == idx[k:k + 1, :], start, 0.0), axis=0, keepdims=True)
            for k in range(TOP_K)]
    dest_ref[...] = jnp.concatenate(rows, axis=0).astype(I32) + rank_ref[...]


def _dest_slots(idx_t, rank_t, group_start):
    tb = 512
    return pl.pallas_call(
        _dest_kernel,
        out_shape=jax.ShapeDtypeStruct((TOP_K, T_ALL), I32),
        grid=(T_ALL // tb,),
        in_specs=[pl.BlockSpec((TOP_K, tb), lambda i: (0, i)),
                  pl.BlockSpec((TOP_K, tb), lambda i: (0, i)),
                  pl.BlockSpec((N_EXPERTS, 1), lambda i: (0, 0))],
        out_specs=pl.BlockSpec((TOP_K, tb), lambda i: (0, i)),
        compiler_params=_cparams(("arbitrary",)),
        name="dest_slots",
    )(idx_t, rank_t, group_start.astype(F32).reshape(N_EXPERTS, 1))


DISPATCH_TB = 256


HI_MASK = -65536


def _pack_pair(hi, lo):
    hb = pltpu.bitcast(hi.astype(BF16).astype(F32), I32)
    lb = pltpu.bitcast(lo.astype(BF16).astype(F32), I32)
    return jnp.bitwise_or(hb, lax.shift_right_logical(lb, 16))


def _unpack_pair(word):
    hi = pltpu.bitcast(jnp.bitwise_and(word, HI_MASK), F32)
    lo = pltpu.bitcast(lax.shift_left(word, 16), F32)
    return hi, lo


def _rows_to_tiles(dst_scr, base, rows):
    n = rows.shape[0]
    for s in range(ROW_TILE):
        dst_scr[pl.ds(base + s, n, stride=ROW_TILE), :] = _pack_pair(
            rows[:, s * LANES:(s + 1) * LANES], rows[:, HALF_D + s * LANES:HALF_D + (s + 1) * LANES])


def _tile_column(src_scr, base, n, s):
    return src_scr[pl.ds(base + s, n, stride=ROW_TILE), :]


def _tiles_to_rows(src_scr, base, n, dtype):
    pairs = [_unpack_pair(_tile_column(src_scr, base, n, s)) for s in range(ROW_TILE)]
    return jnp.concatenate([p[0].astype(dtype) for p in pairs] + [p[1].astype(dtype) for p in pairs], axis=1)


def _row_tile(ref, row):
    return ref.at[pl.ds(pl.multiple_of(row * ROW_TILE, ROW_TILE), ROW_TILE), :]


def _dispatch_kernel(dest_ref, v_ref, xs_hbm, tile_scr, zero_scr, sem, zsem):
    i = pl.program_id(0)
    n = pl.num_programs(0)
    tb = DISPATCH_TB
    slot = i % 2
    base = pl.multiple_of(slot * (tb * ROW_TILE), ROW_TILE)
    _rows_to_tiles(tile_scr, base, v_ref[...])

    def issue(t, carry):
        src = _row_tile(tile_scr, slot * tb + t)
        for k in range(TOP_K):
            pltpu.make_async_copy(src, _row_tile(xs_hbm, dest_ref[k, t]), sem.at[slot]).start(priority=k % 2)
        return carry

    lax.fori_loop(0, tb, issue, 0)

    def wait_block(sl):
        blk = tile_scr.at[pl.ds(pl.multiple_of(sl * (tb * ROW_TILE), ROW_TILE), tb * ROW_TILE), :]
        for _ in range(TOP_K):
            pltpu.make_async_copy(blk, blk, sem.at[sl]).wait()

    @pl.when(i == 0)
    def _():
        zero_scr[...] = jnp.zeros(zero_scr.shape, I32)
        for j in range(GMM_TM // SUBLANES):
            cp = pltpu.make_async_copy(
                zero_scr, xs_hbm.at[pl.ds((N_ASSIGN + j * SUBLANES) * ROW_TILE, SUBLANES * ROW_TILE), :], zsem)
            cp.start()
            cp.wait()

    @pl.when(i > 0)
    def _():
        wait_block(1 - slot)

    @pl.when(i == n - 1)
    def _():
        wait_block(slot)


def _dispatch(dest_t, v_all):
    tb = DISPATCH_TB
    return pl.pallas_call(
        _dispatch_kernel,
        out_shape=jax.ShapeDtypeStruct((N_ROWS * ROW_TILE, LANES), I32),
        grid=(T_ALL // tb,),
        in_specs=[pl.BlockSpec((TOP_K, tb), lambda i: (0, i), memory_space=pltpu.SMEM),
                  pl.BlockSpec((tb, D_MODEL), lambda i: (i, 0))],
        out_specs=pl.BlockSpec(memory_space=pl.ANY),
        scratch_shapes=[pltpu.VMEM((2 * tb * ROW_TILE, LANES), I32),
                        pltpu.VMEM((SUBLANES * ROW_TILE, LANES), I32),
                        pltpu.SemaphoreType.DMA((2,)),
                        pltpu.SemaphoreType.DMA(())],
        compiler_params=_cparams(("arbitrary",)),
        name="dispatch",
    )(dest_t, v_all)


def _flat_rows(ref, row, n_rows):
    return ref.at[pl.ds(pl.multiple_of(row * ROW_TILE, ROW_TILE), n_rows * ROW_TILE), :]


def _tile_writes(y_scr, y_hbm, slot, row0, valid, sem):
    base = slot * GMM_TM
    parts = [(valid == GMM_TM, pltpu.make_async_copy(_flat_rows(y_scr, base, GMM_TM),
                                                     _flat_rows(y_hbm, row0, GMM_TM), sem))]
    off = jnp.int32(0)
    for p in TAIL_SIZES:
        bit = (valid & p) != 0
        take = jnp.logical_and(valid < GMM_TM, bit)
        parts.append((take, pltpu.make_async_copy(_flat_rows(y_scr, base + off, p),
                                                  _flat_rows(y_hbm, row0 + off, p), sem)))
        off = off + jnp.where(bit, p, 0)
    return parts


Y_SLOTS = 4
ST_TILES, ST_ROW = 0, 1
ST_VALID = ST_ROW + Y_SLOTS
ST_FETCH_E = ST_VALID + Y_SLOTS
ST_FETCH_I, ST_FETCH_G, ST_SIZE = ST_FETCH_E + 1, ST_FETCH_E + 2, ST_FETCH_E + 3
X_SLOTS = 4
X_AHEAD = X_SLOTS


def _gmm_kernel(start_ref, xs_hbm, wg_ref, wu_ref, wd_ref, y_hbm,
                x_scr, y_scr, wgu_scr, wdn_scr, zero_scr, st_ref, xsem, ysem, zsem):
    e = pl.program_id(0)
    row_lo = start_ref[e]
    row_hi = start_ref[e + 1]
    n_rows = row_hi - row_lo
    n_tiles = lax.shift_right_logical(n_rows + (GMM_TM - 1), GMM_TM.bit_length() - 1)

    def x_copy(row, sl):
        return pltpu.make_async_copy(_flat_rows(xs_hbm, row, GMM_TM),
                                     _flat_rows(x_scr, sl * GMM_TM, GMM_TM), xsem.at[sl])

    def tiles_of(ex):
        rows = start_ref[ex + 1] - start_ref[ex]
        return lax.shift_right_logical(rows + (GMM_TM - 1), GMM_TM.bit_length() - 1)

    def skip_empty(ex):
        def empty(q):
            qc = jnp.minimum(q, N_EXPERTS - 1)
            return jnp.logical_and(q < N_EXPERTS, start_ref[qc + 1] == start_ref[qc])
        return lax.while_loop(empty, lambda q: q + 1, ex)

    def fetch_next_tile():
        pe = st_ref[ST_FETCH_E]

        @pl.when(pe < N_EXPERTS)
        def _():
            pi = st_ref[ST_FETCH_I]
            pg = st_ref[ST_FETCH_G]
            pec = jnp.minimum(pe, N_EXPERTS - 1)
            x_copy(start_ref[pec] + pi * GMM_TM, pg % X_SLOTS).start()
            last = pi + 1 >= tiles_of(pec)
            st_ref[ST_FETCH_E] = jnp.where(last, skip_empty(pe + 1), pe)
            st_ref[ST_FETCH_I] = jnp.where(last, 0, pi + 1)
            st_ref[ST_FETCH_G] = pg + 1

    def wait_writes(sl):
        @pl.when(st_ref[ST_VALID + sl] > 0)
        def _():
            for pred, cp in _tile_writes(y_scr, y_hbm, sl, st_ref[ST_ROW + sl], st_ref[ST_VALID + sl],
                                         ysem.at[sl]):
                @pl.when(pred)
                def _():
                    cp.wait()
            st_ref[ST_VALID + sl] = 0

    @pl.when(e == 0)
    def _():
        for j in range(ST_SIZE):
            st_ref[j] = 0
        st_ref[ST_FETCH_E] = skip_empty(jnp.int32(0))
        for _ in range(X_AHEAD):
            fetch_next_tile()

    done = st_ref[ST_TILES]

    @pl.when(n_tiles > 0)
    def _():
        wgu_scr[:, :D_EXPERT] = wg_ref[0].astype(BF16)
        wgu_scr[:, D_EXPERT:] = wu_ref[0].astype(BF16)
        wdn_scr[...] = wd_ref[0].astype(BF16)

        def run_tiles(first, count):
            tiles = []
            for j in range(count):
                i = first + j
                g = done + i
                row0 = row_lo + i * GMM_TM
                tiles.append((g % X_SLOTS, g % Y_SLOTS, row0, jnp.minimum(n_rows - i * GMM_TM, GMM_TM)))
            for xslot, yslot, row0, _ in tiles:
                x_copy(row0, xslot).wait()
                wait_writes(yslot)
            for xslot, yslot, _, _ in tiles:
                x = _tiles_to_rows(x_scr, pl.multiple_of(xslot * (GMM_TM * ROW_TILE), ROW_TILE), GMM_TM, BF16)
                h = _dot(x, wgu_scr[...])
                act = (_silu(h[:, :D_EXPERT]) * h[:, D_EXPERT:]).astype(BF16)
                y = _dot(act, wdn_scr[...])
                _rows_to_tiles(y_scr, pl.multiple_of(yslot * (GMM_TM * ROW_TILE), ROW_TILE), y)
            for _, yslot, row0, valid in tiles:
                fetch_next_tile()
                for pred, cp in _tile_writes(y_scr, y_hbm, yslot, row0, valid, ysem.at[yslot]):
                    @pl.when(pred)
                    def _():
                        cp.start()
                st_ref[ST_ROW + yslot] = row0
                st_ref[ST_VALID + yslot] = valid

        def pair(p, carry):
            run_tiles(2 * p, 2)
            return carry

        lax.fori_loop(0, lax.shift_right_logical(n_tiles, 1), pair, 0)

        @pl.when(jnp.bitwise_and(n_tiles, 1) == 1)
        def _():
            run_tiles(n_tiles - 1, 1)

        st_ref[ST_TILES] = done + n_tiles

    @pl.when(e == pl.num_programs(0) - 1)
    def _():
        for sl in range(Y_SLOTS):
            wait_writes(sl)
        zero_scr[...] = jnp.zeros(zero_scr.shape, I32)
        for j in range(GMM_TM // SUBLANES):
            cp = pltpu.make_async_copy(zero_scr, _flat_rows(y_hbm, N_ASSIGN + j * SUBLANES, SUBLANES), zsem)
            cp.start()
            cp.wait()


def _grouped_mlp(group_start, xs, w_gate, w_up, w_down):
    grid_spec = pltpu.PrefetchScalarGridSpec(
        num_scalar_prefetch=1,
        grid=(N_EXPERTS,),
        in_specs=[pl.BlockSpec(memory_space=pl.ANY),
                  pl.BlockSpec((1, D_MODEL, D_EXPERT), lambda e, st: (e, 0, 0)),
                  pl.BlockSpec((1, D_MODEL, D_EXPERT), lambda e, st: (e, 0, 0)),
                  pl.BlockSpec((1, D_EXPERT, D_MODEL), lambda e, st: (e, 0, 0))],
        out_specs=pl.BlockSpec(memory_space=pl.ANY),
        scratch_shapes=[pltpu.VMEM((X_SLOTS * GMM_TM * ROW_TILE, LANES), I32),
                        pltpu.VMEM((Y_SLOTS * GMM_TM * ROW_TILE, LANES), I32),
                        pltpu.VMEM((D_MODEL, 2 * D_EXPERT), BF16),
                        pltpu.VMEM((D_EXPERT, D_MODEL), BF16),
                        pltpu.VMEM((SUBLANES * ROW_TILE, LANES), I32),
                        pltpu.SMEM((ST_SIZE,), I32),
                        pltpu.SemaphoreType.DMA((X_SLOTS,)),
                        pltpu.SemaphoreType.DMA((Y_SLOTS,)),
                        pltpu.SemaphoreType.DMA(())],
    )
    return pl.pallas_call(
        _gmm_kernel,
        out_shape=jax.ShapeDtypeStruct((N_ROWS * ROW_TILE, LANES), I32),
        grid_spec=grid_spec,
        compiler_params=_cparams(("arbitrary",)),
        name="grouped_mlp",
    )(group_start, xs, w_gate, w_up, w_down)


COMBINE_TB = 256


def _combine_kernel(dest_ref, dest_next_ref, y_hbm, pre_ref, w_ref, mod_ref, fw_ref, op_ref, os_ref, buf, x2_scr,
                    sem):
    i = pl.program_id(0)
    n = pl.num_programs(0)
    tb = COMBINE_TB
    slot = i % 2
    blk_rows = TOP_K * tb

    def issue_block(d_ref, sl):
        def issue(t, carry):
            for k in range(TOP_K):
                pltpu.make_async_copy(_row_tile(y_hbm, d_ref[k, t]),
                                      _row_tile(buf, sl * blk_rows + k * tb + t),
                                      sem.at[sl]).start(priority=k % 2)
            return carry

        lax.fori_loop(0, tb, issue, 0)

    @pl.when(i == 0)
    def _():
        issue_block(dest_ref, 0)

    @pl.when(i + 1 < n)
    def _():
        issue_block(dest_next_ref, 1 - slot)

    whole = _flat_rows(buf, slot * blk_rows, blk_rows)
    pltpu.make_async_copy(whole, whole, sem.at[slot]).wait()

    base = pl.multiple_of(slot * (blk_rows * ROW_TILE), ROW_TILE)
    w = w_ref[...]
    m = mod_ref[...]
    ssq = jnp.zeros((tb, 1), F32)
    for s in range(ROW_TILE):
        routed = [None, None]
        for k in range(TOP_K):
            halves = _unpack_pair(_tile_column(buf, base + k * tb * ROW_TILE, tb, s))
            for j in range(2):
                term = w[:, k:k + 1] * halves[j]
                routed[j] = term if routed[j] is None else routed[j] + term
        for j in range(2):
            cols = slice(j * HALF_D + s * LANES, j * HALF_D + (s + 1) * LANES)
            x2 = pre_ref[:, cols] + m[5:6, cols] * routed[j]
            x2_scr[:, cols] = x2
            ssq = ssq + jnp.sum(x2 * x2, axis=-1, keepdims=True)
    scale = lax.rsqrt(ssq * (1.0 / D_MODEL) + EPS)

    @pl.when(i < T_PROMPT // tb)
    def _():
        op_ref[...] = x2_scr[...] * scale * fw_ref[...]

    @pl.when(i >= T_PROMPT // tb)
    def _():
        os_ref[...] = x2_scr[...] * scale * fw_ref[...]


def _combine(dest_t, y_rows, pre, w_tok, mod3, final_norm_w):
    tb = COMBINE_TB
    n_blocks = T_ALL // tb
    npb = T_PROMPT // tb
    row = functools.partial(_mod_row_of_block, blocks_prompt=npb, blocks_per_sample_seq=DEC_SEQ // tb)
    return pl.pallas_call(
        _combine_kernel,
        out_shape=(jax.ShapeDtypeStruct((T_PROMPT, D_MODEL), F32),
                   jax.ShapeDtypeStruct((T_SAMPLE, D_MODEL), F32)),
        grid=(n_blocks,),
        in_specs=[pl.BlockSpec((TOP_K, tb), lambda i: (0, i), memory_space=pltpu.SMEM),
                  pl.BlockSpec((TOP_K, tb), lambda i: (0, jnp.minimum(i + 1, n_blocks - 1)),
                               memory_space=pltpu.SMEM),
                  pl.BlockSpec(memory_space=pl.ANY),
                  pl.BlockSpec((tb, D_MODEL), lambda i: (i, 0)),
                  pl.BlockSpec((tb, TOP_K), lambda i: (i, 0)),
                  pl.BlockSpec((None, N_MOD, D_MODEL), lambda i: (row(i), 0, 0)),
                  pl.BlockSpec((1, D_MODEL), lambda i: (0, 0))],
        out_specs=_two_path_specs((tb, D_MODEL), npb),
        scratch_shapes=[pltpu.VMEM((2 * TOP_K * tb * ROW_TILE, LANES), I32),
                        pltpu.VMEM((tb, D_MODEL), F32),
                        pltpu.SemaphoreType.DMA((2,))],
        compiler_params=_cparams(("arbitrary",)),
        name="combine",
    )(dest_t, dest_t, y_rows, pre, w_tok, mod3, final_norm_w.reshape(1, D_MODEL))


def _group_starts(counts):
    return jnp.concatenate([jnp.zeros((1,), I32), jnp.cumsum(counts).astype(I32)])


def _head_expand_matrix():
    r = jnp.arange(LANES)[:, None]
    cidx = jnp.arange(2 * D_INNER)[None, :]
    direction = cidx // D_INNER
    head = (cidx % D_INNER) // HEAD_DIM
    return (r == direction * N_HEADS + head).astype(BF16)


def kernel(x_prompt, x_sample, state_ssm, c, c_ctx, norm1_w, norm2_w, w_mod, b_mod, w_in, ssm_conv_w, ssm_conv_b, ssm_dt_bias, ssm_a_log, ssm_d, ssm_norm_w, w_out_ssm, conf_conv_w, conf_conv_b, conf_ln_w, conf_ln_b, w_out_conf, w_o, router_w, router_bias, exp_w_gate, exp_w_up, exp_w_down, sh_w_gate, sh_w_up, sh_w_down, final_norm_w):
    x_p = x_prompt.reshape(T_PROMPT, D_MODEL)
    x_s = x_sample.reshape(T_SAMPLE, D_MODEL)

    cc = jnp.zeros((MOD_ROWS, D_MODEL), F32).at[:DEC_BATCH].set(c).at[CTX_ROW].set(c_ctx)
    mod3 = _modulation(cc, w_mod[0], b_mod[0]).reshape(MOD_ROWS, N_MOD, D_MODEL)

    w = w_in[0]
    o_xbc, o_dt, o_cv, o_cg, o_gl = D_INNER, D_INNER + CONV_DIM, D_INNER + CONV_DIM + 2 * N_HEADS, \
        D_INNER + CONV_DIM + 2 * N_HEADS + D_MODEL, D_INNER + CONV_DIM + 2 * N_HEADS + 2 * D_MODEL
    w_main = jnp.concatenate([w[:, o_xbc:o_dt], w[:, :o_xbc], w[:, o_gl:], w[:, o_cv:o_cg], w[:, o_cg:o_gl]],
                             axis=1).astype(BF16)
    w_dt = jnp.pad(w[:, o_dt:o_cv], ((0, 0), (0, LANES - 2 * N_HEADS))).astype(BF16)
    proj, dt_raw = _in_projection(x_p, x_s, mod3, norm1_w[0], w_main, w_dt)

    pad_heads = lambda v: jnp.pad(v.reshape(1, 2 * N_HEADS), ((0, 0), (0, LANES - 2 * N_HEADS)))
    dtb = pad_heads(ssm_dt_bias[0])
    a_neg = pad_heads(-jnp.exp(ssm_a_log[0]))
    dvec = jnp.repeat(ssm_d[0], HEAD_DIM).reshape(1, D_INNER)
    nw = ssm_norm_w[0].reshape(1, D_INNER)
    e2 = _head_expand_matrix()
    xbc_p = _ssm_conv(proj, ssm_conv_w[0], ssm_conv_b[0], seq_len=SEQ, n_seq=BATCH, row_block0=0)
    xbc_s = _ssm_conv(proj, ssm_conv_w[0], ssm_conv_b[0], seq_len=DEC_SEQ, n_seq=DEC_BATCH,
                      row_block0=T_PROMPT // DEC_SEQ)
    y_p, fin = _ssd(xbc_p, dt_raw, proj, None, dtb, a_neg, dvec, nw, e2,
                    seq_len=SEQ, n_seq=BATCH, tok0=0, want_final=True)
    init = state_ssm.reshape(DEC_BATCH, 2, D_INNER, D_STATE)
    (y_s,) = _ssd(xbc_s, dt_raw, proj, init, dtb, a_neg, dvec, nw, e2,
                  seq_len=DEC_SEQ, n_seq=DEC_BATCH, tok0=T_PROMPT, want_final=False)

    v_p = _conformer(proj, conf_conv_w[0], conf_conv_b[0], conf_ln_w[0], conf_ln_b[0],
                     seq_len=SEQ, n_seq=BATCH, row_block0=0, grid_mode=False)
    v_s = _conformer(proj, conf_conv_w[0], conf_conv_b[0], conf_ln_w[0], conf_ln_b[0],
                     seq_len=DEC_SEQ, n_seq=DEC_BATCH, row_block0=T_PROMPT // DEC_SEQ, grid_mode=True)

    sh_gu = jnp.concatenate([sh_w_gate[0], sh_w_up[0]], axis=1).astype(BF16)
    pre, v_all, scores = _merge(y_p, y_s, v_p, v_s, proj, x_p, x_s, mod3, w_out_ssm[0].astype(BF16),
                                w_out_conf[0].astype(BF16), w_o[0].astype(BF16), norm2_w[0], router_w[0],
                                sh_gu, sh_w_down[0].astype(BF16))

    idx_t, w_t, rank_t, cnt = _route(scores, router_bias[0])
    start = _group_starts(cnt[:, 0].astype(I32))
    dest_t = _dest_slots(idx_t, rank_t, start[:N_EXPERTS])

    xs = _dispatch(dest_t, v_all)
    y_rows = _grouped_mlp(start, xs, exp_w_gate[0], exp_w_up[0], exp_w_down[0])
    out_p, out_s = _combine(dest_t, y_rows, pre, w_t.T, mod3, final_norm_w)

    y_prompt = out_p.reshape(BATCH, SEQ, D_MODEL)
    y_sample = out_s.reshape(DEC_BATCH, DEC_SEQ, D_MODEL)
    new_state = fin.reshape(BATCH, 1, 2, N_HEADS, HEAD_DIM, D_STATE)
    return (y_prompt, y_sample, new_state)
```

```python
import functools

import jax
import jax.numpy as jnp
from jax import lax
from jax.experimental import pallas as pl
from jax.experimental.pallas import tpu as pltpu

F32 = jnp.float32
BF16 = jnp.bfloat16
I32 = jnp.int32

D_MODEL = 1024
BATCH = 32
SEQ = 256
DEC_BATCH = 8
DEC_SEQ = 1024
GRID_W = 64
GRID_H = DEC_SEQ // GRID_W
D_INNER = 2048
HEAD_DIM = 64
N_HEADS = 32
D_STATE = 128
N_GROUPS = 8
HEADS_PER_GROUP = N_HEADS // N_GROUPS
GROUP_W = HEADS_PER_GROUP * HEAD_DIM
D_CONV_SSM = 5
CHUNK = 128
CONV_DIM = D_INNER + 2 * N_GROUPS * D_STATE
CONF_K = 31
CONF_PAD = CONF_K // 2
N_EXPERTS = 256
TOP_K = 8
N_EXPERT_GROUPS = 8
EXPERTS_PER_GROUP = N_EXPERTS // N_EXPERT_GROUPS
TOPK_GROUPS = 4
D_EXPERT = 256
D_SHARED = 256
ROUTED_SCALE = 2.5
N_MOD = 6
EPS = 1e-6

T_PROMPT = BATCH * SEQ
T_SAMPLE = DEC_BATCH * DEC_SEQ
T_ALL = T_PROMPT + T_SAMPLE
N_ASSIGN = T_ALL * TOP_K
MOD_ROWS = 16
CTX_ROW = DEC_BATCH

SUBLANES = 8
LANES = 128
VMEM_LIMIT = 56 * 1024 * 1024

PROJ_W = CONV_DIM + D_INNER + 2 * D_MODEL + 2 * D_MODEL
COL_Z = CONV_DIM // D_INNER
COL_GL = COL_Z + 1
COL_CV = (CONV_DIM + 2 * D_INNER) // D_MODEL
COL_CG = COL_CV + 1

HALF_D = D_MODEL // 2
ROW_TILE = HALF_D // LANES
GMM_TM = 256
N_ROWS = N_ASSIGN + GMM_TM
TAIL_SIZES = (128, 64, 32, 16, 8, 4, 2, 1)


def _cparams(sem, vmem=VMEM_LIMIT):
    return pltpu.CompilerParams(dimension_semantics=sem, vmem_limit_bytes=vmem)


def _silu(x):
    return x * jax.nn.sigmoid(x)


def _split2(x):
    hi = x.astype(BF16)
    lo = (x - hi.astype(F32)).astype(BF16)
    return hi, lo


def _split3(x):
    b1 = x.astype(BF16)
    r = x - b1.astype(F32)
    b2 = r.astype(BF16)
    b3 = (r - b2.astype(F32)).astype(BF16)
    return b1, b2, b3


def _dot(a, b):
    return jnp.dot(a, b, preferred_element_type=F32)


def _dot_exact_lhs(a_exact, b):
    b1, b2, b3 = _split3(b)
    return _dot(a_exact, b1) + _dot(a_exact, b2) + _dot(a_exact, b3)


def _dot_f32(a, b):
    a1, a2 = _split2(a)
    b1, b2 = _split2(b)
    return _dot(a1, b1) + _dot(a1, b2) + _dot(a2, b1)


def _mod_row_of_block(i, blocks_prompt, blocks_per_sample_seq):
    return jnp.where(i < blocks_prompt, CTX_ROW, (i - blocks_prompt) // blocks_per_sample_seq)


def _mod_kernel(c_ref, w_ref, b_ref, o_ref):
    c = c_ref[...]
    o_ref[...] = _dot_f32(_silu(c), w_ref[...]) + b_ref[...]


def _modulation(cc, w_mod, b_mod):
    tn = 512
    n = N_MOD * D_MODEL
    return pl.pallas_call(
        _mod_kernel,
        out_shape=jax.ShapeDtypeStruct((MOD_ROWS, n), F32),
        grid=(n // tn,),
        in_specs=[pl.BlockSpec((MOD_ROWS, D_MODEL), lambda j: (0, 0)),
                  pl.BlockSpec((D_MODEL, tn), lambda j: (0, j)),
                  pl.BlockSpec((1, tn), lambda j: (0, j))],
        out_specs=pl.BlockSpec((MOD_ROWS, tn), lambda j: (0, j)),
        compiler_params=_cparams(("arbitrary",)),
        name="modulation",
    )(cc, w_mod, b_mod.reshape(1, n))


INPROJ_TM = 1024
INPROJ_TN = 2048


def _two_path_specs(block, n_prompt_blocks):
    last = n_prompt_blocks - 1
    return (pl.BlockSpec(block, lambda i, *_: (jnp.minimum(i, last), 0)),
            pl.BlockSpec(block, lambda i, *_: (jnp.maximum(i - n_prompt_blocks, 0), 0)))


def _pick_path(n_prompt_blocks, prompt_ref, sample_ref):
    return jnp.where(pl.program_id(0) < n_prompt_blocks, prompt_ref[...], sample_ref[...])


def _inproj_kernel(xp_ref, xs_ref, mod_ref, n1_ref, w_ref, wdt_ref, o_ref, dt_ref, u_scr):
    @pl.when(pl.program_id(1) == 0)
    def _():
        x = _pick_path(T_PROMPT // INPROJ_TM, xp_ref, xs_ref)
        ms = jnp.mean(x * x, axis=-1, keepdims=True)
        y = x * lax.rsqrt(ms + EPS) * n1_ref[...]
        m = mod_ref[...]
        u = (y * (1.0 + m[1:2]) + m[0:1]).astype(BF16)
        u_scr[...] = u
        dt_ref[...] = _dot(u, wdt_ref[...])

    o_ref[...] = _dot(u_scr[...], w_ref[...]).astype(BF16)


def _in_projection(x_p, x_s, mod3, norm1_w, w_main, w_dt):
    tm, tn = INPROJ_TM, INPROJ_TN
    row = functools.partial(_mod_row_of_block, blocks_prompt=T_PROMPT // tm,
                            blocks_per_sample_seq=DEC_SEQ // tm)
    return pl.pallas_call(
        _inproj_kernel,
        out_shape=(jax.ShapeDtypeStruct((T_ALL, PROJ_W), BF16),
                   jax.ShapeDtypeStruct((T_ALL, LANES), F32)),
        grid=(T_ALL // tm, PROJ_W // tn),
        in_specs=[*_two_path_specs((tm, D_MODEL), T_PROMPT // tm),
                  pl.BlockSpec((None, N_MOD, D_MODEL), lambda i, j: (row(i), 0, 0)),
                  pl.BlockSpec((1, D_MODEL), lambda i, j: (0, 0)),
                  pl.BlockSpec((D_MODEL, tn), lambda i, j: (0, j)),
                  pl.BlockSpec((D_MODEL, LANES), lambda i, j: (0, 0))],
        out_specs=(pl.BlockSpec((tm, tn), lambda i, j: (i, j)),
                   pl.BlockSpec((tm, LANES), lambda i, j: (i, 0))),
        scratch_shapes=[pltpu.VMEM((tm, D_MODEL), BF16)],
        compiler_params=_cparams(("arbitrary", "arbitrary")),
        name="in_projection",
    )(x_p, x_s, mod3, norm1_w.reshape(1, D_MODEL), w_main, w_dt)


SSMCONV_TN = 2048
SSMCONV_RT = 64
SSMCONV_HALO = SUBLANES


def _ssmconv_kernel(x_ref, w_ref, b_ref, o_ref, pad_scr, out_scr):
    seq_len = x_ref.shape[0]
    half = D_CONV_SSM // 2
    n_half = SSMCONV_RT // 2
    zeros = jnp.zeros((SSMCONV_HALO, LANES), F32)
    for sl in range(SSMCONV_TN // LANES):
        cs = slice(sl * LANES, (sl + 1) * LANES)
        pad_scr[sl, 0:SSMCONV_HALO, :] = zeros
        pad_scr[sl, SSMCONV_HALO + seq_len:, :] = zeros
        pad_scr[sl, SSMCONV_HALO:SSMCONV_HALO + seq_len, :] = x_ref[:, cs].astype(F32)
        wk = [w_ref[k:k + 1, cs] for k in range(D_CONV_SSM)]
        bias = b_ref[:, cs]
        for r0 in range(0, seq_len, SSMCONV_RT):
            for phase in range(2):
                acc = bias
                for k in range(D_CONV_SSM):
                    acc = acc + wk[k] * pad_scr[sl, pl.ds(SSMCONV_HALO + r0 + phase + k - half, n_half, stride=2), :]
                out_scr[sl, pl.ds(r0 + phase, n_half, stride=2), :] = _silu(acc)
        o_ref[:, cs] = out_scr[sl].astype(BF16)


def _ssm_conv(proj, conv_w, conv_b, *, seq_len, n_seq, row_block0):
    tn = SSMCONV_TN
    return pl.pallas_call(
        _ssmconv_kernel,
        out_shape=jax.ShapeDtypeStruct((n_seq * seq_len, CONV_DIM), BF16),
        grid=(n_seq, CONV_DIM // tn),
        in_specs=[pl.BlockSpec((seq_len, tn), lambda b, j: (row_block0 + b, j)),
                  pl.BlockSpec((D_CONV_SSM, tn), lambda b, j: (0, j)),
                  pl.BlockSpec((1, tn), lambda b, j: (0, j))],
        out_specs=pl.BlockSpec((seq_len, tn), lambda b, j: (b, j)),
        scratch_shapes=[pltpu.VMEM((tn // LANES, seq_len + 2 * SSMCONV_HALO, LANES), F32),
                        pltpu.VMEM((tn // LANES, seq_len, LANES), F32)],
        compiler_params=_cparams(("arbitrary", "arbitrary")),
        name="ssm_conv",
    )(proj, conv_w, conv_b.reshape(1, CONV_DIM))


def _tri_masks():
    ii = lax.broadcasted_iota(I32, (CHUNK, CHUNK), 0)
    jj = lax.broadcasted_iota(I32, (CHUNK, CHUNK), 1)
    return ii, jj


def _chunk_decays(dt_ref, dtb_ref, a_ref):
    ii, jj = _tri_masks()
    pre = dt_ref[...] + dtb_ref[...]
    dt = jnp.maximum(pre, 0.0) + jnp.log(1.0 + jnp.exp(-jnp.abs(pre)))
    la = dt * a_ref[...]
    tri_lo = jnp.where(jj <= ii, 1.0, 0.0).astype(BF16)
    tri_up = jnp.where(jj >= ii, 1.0, 0.0).astype(BF16)
    cs_prefix = _dot_exact_lhs(tri_lo, la)
    cs_suffix = _dot_exact_lhs(tri_up, la)
    fwd_lane = lax.broadcasted_iota(I32, (CHUNK, LANES), 1) < N_HEADS
    cs = jnp.where(fwd_lane, cs_prefix, cs_suffix)
    tot = jnp.where(fwd_lane[0:1], cs_prefix[CHUNK - 1:CHUNK, :], cs_suffix[0:1, :])
    return dt, cs, tot


def _transpose_blocks(src, rows, cols):
    out_rows = []
    for cb in range(cols // LANES):
        pieces = [src[rb * LANES:(rb + 1) * LANES, cb * LANES:(cb + 1) * LANES].T
                  for rb in range(rows // LANES)]
        out_rows.append(jnp.concatenate(pieces, axis=1) if len(pieces) > 1 else pieces[0])
    return jnp.concatenate(out_rows, axis=0) if len(out_rows) > 1 else out_rows[0]


def _ssd_kernel(*refs, n_chunks, has_init, want_final):
    it = iter(refs)
    xbc_ref, dt_ref, z_ref = next(it), next(it), next(it)
    init_ref = next(it) if has_init else None
    dtb_ref, a_ref, dvec_ref, nw_ref, e2_ref = next(it), next(it), next(it), next(it), next(it)
    out_ref = next(it)
    fin_ref = next(it) if want_final else None
    sf_scr, sb_scr, df_scr, db_scr, dtcs_scr = next(it), next(it), next(it), next(it), next(it)

    phase = pl.program_id(1)
    c = pl.program_id(2)

    @pl.when(phase == 0)
    def _chunk_states():
        dt, cs, tot = _chunk_decays(dt_ref, dtb_ref, a_ref)
        dtcs_scr[c, 0] = dt
        dtcs_scr[c, 1] = cs
        w_in = dt * jnp.exp(tot - cs)
        pack = 2 * SUBLANES
        dec_hi, dec_lo = _split2(jnp.exp(jnp.broadcast_to(tot, (pack, LANES))))
        expanded = _dot(jnp.concatenate([w_in.astype(BF16), dec_hi, dec_lo], axis=0), e2_ref[...])
        w_exp = expanded[:CHUNK]
        dec_exp = expanded[CHUNK:CHUNK + SUBLANES] + expanded[CHUNK + pack:CHUNK + pack + SUBLANES]
        df_scr[c] = dec_exp[:, :D_INNER]
        db_scr[c] = dec_exp[:, D_INNER:]
        for g in range(N_GROUPS):
            lo = g * GROUP_W
            xg = xbc_ref[:, lo:lo + GROUP_W].astype(F32)
            xd_f = (xg * w_exp[:, lo:lo + GROUP_W]).astype(BF16)
            xd_b = (xg * w_exp[:, D_INNER + lo:D_INNER + lo + GROUP_W]).astype(BF16)
            bg = xbc_ref[:, D_INNER + g * D_STATE:D_INNER + (g + 1) * D_STATE]
            bg_t = bg.astype(F32).T.astype(BF16)
            sf_scr[c, :, lo:lo + GROUP_W] = _dot(bg_t, xd_f)
            sb_scr[c, :, lo:lo + GROUP_W] = _dot(bg_t, xd_b)

    @pl.when(jnp.logical_and(phase == 1, c == 0))
    def _recurrence():
        for g in range(N_GROUPS):
            lo = g * GROUP_W
            if has_init:
                prev_f = _transpose_blocks(init_ref[0, 0, lo:lo + GROUP_W, :], GROUP_W, D_STATE)
                prev_b = _transpose_blocks(init_ref[0, 1, lo:lo + GROUP_W, :], GROUP_W, D_STATE)
            else:
                prev_f = jnp.zeros((D_STATE, GROUP_W), F32)
                prev_b = jnp.zeros((D_STATE, GROUP_W), F32)
            for cc in range(n_chunks):
                s = sf_scr[cc, :, lo:lo + GROUP_W]
                sf_scr[cc, :, lo:lo + GROUP_W] = prev_f
                prev_f = df_scr[cc, 0:1, lo:lo + GROUP_W] * prev_f + s
            for cc in reversed(range(n_chunks)):
                s = sb_scr[cc, :, lo:lo + GROUP_W]
                sb_scr[cc, :, lo:lo + GROUP_W] = prev_b
                prev_b = db_scr[cc, 0:1, lo:lo + GROUP_W] * prev_b + s
            if want_final:
                fin_ref[0, 0, lo:lo + GROUP_W, :] = _transpose_blocks(prev_f, D_STATE, GROUP_W)
                fin_ref[0, 1, lo:lo + GROUP_W, :] = _transpose_blocks(prev_b, D_STATE, GROUP_W)

    @pl.when(phase == 1)
    def _outputs():
        ii, jj = _tri_masks()
        dt = dtcs_scr[c, 0]
        cs = dtcs_scr[c, 1]
        out_dec = _dot(jnp.exp(cs).astype(BF16), e2_ref[...])
        cs_t = cs.T
        dt_t = dt.T
        causal = ii >= jj
        below = ii > jj
        above = jj > ii
        left = lax.broadcasted_iota(I32, (CHUNK, LANES), 1) < HEAD_DIM
        for g in range(N_GROUPS):
            lo = g * GROUP_W
            bg = xbc_ref[:, D_INNER + g * D_STATE:D_INNER + (g + 1) * D_STATE]
            cg = xbc_ref[:, D_INNER + N_GROUPS * D_STATE + g * D_STATE:
                         D_INNER + N_GROUPS * D_STATE + (g + 1) * D_STATE]
            cb = lax.dot_general(cg, bg, (((1,), (1,)), ((), ())), preferred_element_type=F32)
            pf = sf_scr[c, :, lo:lo + GROUP_W].astype(BF16)
            pb = sb_scr[c, :, lo:lo + GROUP_W].astype(BF16)
            y_off = (_dot(cg, pf) * out_dec[:, lo:lo + GROUP_W]
                     + _dot(cg, pb) * out_dec[:, D_INNER + lo:D_INNER + lo + GROUP_W])
            pairs = []
            for m in range(HEADS_PER_GROUP // 2):
                x_pair = xbc_ref[:, lo + m * LANES:lo + (m + 1) * LANES]
                ys = []
                for hh in range(2):
                    h = g * HEADS_PER_GROUP + 2 * m + hh
                    hb = N_HEADS + h
                    seg = jnp.where(causal, cs[:, h:h + 1] - cs_t[h:h + 1, :], cs[:, hb:hb + 1] - cs_t[hb:hb + 1, :])
                    dt_f, dt_b = dt_t[h:h + 1, :], dt_t[hb:hb + 1, :]
                    mix = jnp.exp(seg) * jnp.where(below, dt_f, jnp.where(above, dt_b, dt_f + dt_b))
                    ys.append(_dot((cb * mix).astype(BF16), x_pair))
                pairs.append(jnp.where(left, ys[0], ys[1]))
            y_diag = jnp.concatenate(pairs, axis=1)
            xg = xbc_ref[:, lo:lo + GROUP_W].astype(F32)
            y = y_diag + y_off + dvec_ref[:, lo:lo + GROUP_W] * xg
            zg = z_ref[:, lo:lo + GROUP_W].astype(F32)
            y = y * _silu(zg)
            ms = jnp.mean(y * y, axis=-1, keepdims=True)
            out_ref[:, lo:lo + GROUP_W] = (y * lax.rsqrt(ms + EPS) * nw_ref[:, lo:lo + GROUP_W]).astype(BF16)


def _ssd(xbc, dt_raw, proj, init, dtb, a_neg, dvec, norm_w, e2, *, seq_len, n_seq, tok0, want_final):
    nc = seq_len // CHUNK
    blk0 = tok0 // CHUNK
    has_init = init is not None
    in_specs = [pl.BlockSpec((CHUNK, CONV_DIM), lambda b, p, c: (b * nc + c, 0)),
                pl.BlockSpec((CHUNK, LANES), lambda b, p, c: (blk0 + b * nc + c, 0)),
                pl.BlockSpec((CHUNK, D_INNER), lambda b, p, c: (blk0 + b * nc + c * p, COL_Z))]
    args = [xbc, dt_raw, proj]
    if has_init:
        in_specs.append(pl.BlockSpec((1, 2, D_INNER, D_STATE), lambda b, p, c: (b, 0, 0, 0)))
        args.append(init)
    const = lambda b, p, c: (0, 0)
    in_specs += [pl.BlockSpec((1, LANES), const), pl.BlockSpec((1, LANES), const),
                 pl.BlockSpec((1, D_INNER), const), pl.BlockSpec((1, D_INNER), const),
                 pl.BlockSpec((LANES, 2 * D_INNER), const)]
    args += [dtb, a_neg, dvec, norm_w, e2]
    out_shape = [jax.ShapeDtypeStruct((n_seq * seq_len, D_INNER), BF16)]
    out_specs = [pl.BlockSpec((CHUNK, D_INNER), lambda b, p, c: (b * nc + c * p, 0))]
    if want_final:
        out_shape.append(jax.ShapeDtypeStruct((n_seq, 2, D_INNER, D_STATE), F32))
        out_specs.append(pl.BlockSpec((1, 2, D_INNER, D_STATE), lambda b, p, c: (b, 0, 0, 0)))
    res = pl.pallas_call(
        functools.partial(_ssd_kernel, n_chunks=nc, has_init=has_init, want_final=want_final),
        out_shape=tuple(out_shape),
        grid=(n_seq, 2, nc),
        in_specs=in_specs,
        out_specs=tuple(out_specs),
        scratch_shapes=[pltpu.VMEM((nc, D_STATE, D_INNER), F32),
                        pltpu.VMEM((nc, D_STATE, D_INNER), F32),
                        pltpu.VMEM((nc, SUBLANES, D_INNER), F32),
                        pltpu.VMEM((nc, SUBLANES, D_INNER), F32),
                        pltpu.VMEM((nc, 2, CHUNK, LANES), F32)],
        compiler_params=_cparams(("arbitrary", "arbitrary", "arbitrary")),
        name="ssd_final" if want_final else "ssd_init",
    )(*args)
    return res


CONF_SLABS = D_MODEL // LANES
CONF_HALO = 2 * SUBLANES
CONF_RT = 64


def _glu_slab(cv_ref, cg_ref, sl):
    cs = slice(sl * LANES, (sl + 1) * LANES)
    return cv_ref[:, cs].astype(F32) * jax.nn.sigmoid(cg_ref[:, cs].astype(F32))


def _conv_taps_strided(src_ref, sl, src_row0, dst_ref, dst_row0, w_ref, b_ref):
    cs = slice(sl * LANES, (sl + 1) * LANES)
    n_half = CONF_RT // 2
    for phase in range(2):
        acc = b_ref[:, cs]
        for k in range(CONF_K):
            x = src_ref[sl, pl.ds(src_row0 + phase + k - CONF_PAD, n_half, stride=2), :]
            acc = acc + w_ref[k:k + 1, cs] * x
        dst_ref[sl, pl.ds(dst_row0 + phase, n_half, stride=2), :] = acc


def _layernorm_silu_slabs(acc_scr, lw_ref, lb_ref, o_ref):
    n = acc_scr.shape[1]
    tot = jnp.zeros((n, 1), F32)
    for sl in range(CONF_SLABS):
        tot = tot + jnp.sum(acc_scr[sl], axis=-1, keepdims=True)
    mu = tot * (1.0 / D_MODEL)
    sq = jnp.zeros((n, 1), F32)
    for sl in range(CONF_SLABS):
        d = acc_scr[sl] - mu
        sq = sq + jnp.sum(d * d, axis=-1, keepdims=True)
    rstd = lax.rsqrt(sq * (1.0 / D_MODEL) + EPS)
    for sl in range(CONF_SLABS):
        cs = slice(sl * LANES, (sl + 1) * LANES)
        o_ref[:, cs] = _silu((acc_scr[sl] - mu) * rstd * lw_ref[:, cs] + lb_ref[:, cs]).astype(BF16)


def _conf_seq_kernel(cv_ref, cg_ref, w_ref, b_ref, lw_ref, lb_ref, o_ref, pad_scr, acc_scr):
    seq_len = cv_ref.shape[0]
    zeros = jnp.zeros((CONF_HALO, LANES), F32)
    for sl in range(CONF_SLABS):
        pad_scr[sl, 0:CONF_HALO, :] = zeros
        pad_scr[sl, CONF_HALO + seq_len:, :] = zeros
        pad_scr[sl, CONF_HALO:CONF_HALO + seq_len, :] = _glu_slab(cv_ref, cg_ref, sl)
    for sl in range(CONF_SLABS):
        for r0 in range(0, seq_len, CONF_RT):
            _conv_taps_strided(pad_scr, sl, CONF_HALO + r0, acc_scr, r0, w_ref, b_ref)
    _layernorm_silu_slabs(acc_scr, lw_ref, lb_ref, o_ref)


def _conf_grid_kernel(cv_ref, cg_ref, w_ref, b_ref, lw_ref, lb_ref, o_ref, v_scr, pad_scr, acc_scr):
    half_slabs = CONF_SLABS // 2
    stride = GRID_W + 2 * CONF_HALO
    pad_scr[...] = jnp.zeros(pad_scr.shape, F32)
    for sl in range(half_slabs):
        v = _glu_slab(cv_ref, cg_ref, sl)
        for r in range(GRID_H):
            pad_scr[sl, r * stride + CONF_HALO:r * stride + CONF_HALO + GRID_W, :] = v[r * GRID_W:(r + 1) * GRID_W]
    for sl in range(half_slabs, CONF_SLABS):
        v_scr[sl - half_slabs] = _glu_slab(cv_ref, cg_ref, sl)
    for sl in range(half_slabs):
        for r in range(GRID_H):
            _conv_taps_strided(pad_scr, sl, r * stride + CONF_HALO, acc_scr, r * GRID_W, w_ref, b_ref)
    for sl in range(half_slabs, CONF_SLABS):
        cs = slice(sl * LANES, (sl + 1) * LANES)
        for r in range(GRID_H):
            acc = jnp.broadcast_to(b_ref[:, cs], (GRID_W, LANES))
            for r2 in range(GRID_H):
                k = r2 - r + CONF_PAD
                acc = acc + w_ref[k:k + 1, cs] * v_scr[sl - half_slabs, r2 * GRID_W:(r2 + 1) * GRID_W, :]
            acc_scr[sl, r * GRID_W:(r + 1) * GRID_W, :] = acc
    _layernorm_silu_slabs(acc_scr, lw_ref, lb_ref, o_ref)


def _conformer(proj, conv_w, conv_b, ln_w, ln_b, *, seq_len, n_seq, row_block0, grid_mode):
    if grid_mode:
        body = _conf_grid_kernel
        scratch = [pltpu.VMEM((CONF_SLABS // 2, seq_len, LANES), F32),
                   pltpu.VMEM((CONF_SLABS // 2, GRID_H * (GRID_W + 2 * CONF_HALO), LANES), F32),
                   pltpu.VMEM((CONF_SLABS, seq_len, LANES), F32)]
    else:
        body = _conf_seq_kernel
        scratch = [pltpu.VMEM((CONF_SLABS, seq_len + 2 * CONF_HALO, LANES), F32),
                   pltpu.VMEM((CONF_SLABS, seq_len, LANES), F32)]
    const = lambda b: (0, 0)
    return pl.pallas_call(
        body,
        out_shape=jax.ShapeDtypeStruct((n_seq * seq_len, D_MODEL), BF16),
        grid=(n_seq,),
        in_specs=[pl.BlockSpec((seq_len, D_MODEL), lambda b: (row_block0 + b, COL_CV)),
                  pl.BlockSpec((seq_len, D_MODEL), lambda b: (row_block0 + b, COL_CG)),
                  pl.BlockSpec((CONF_K, D_MODEL), const),
                  pl.BlockSpec((1, D_MODEL), const),
                  pl.BlockSpec((1, D_MODEL), const),
                  pl.BlockSpec((1, D_MODEL), const)],
        out_specs=pl.BlockSpec((seq_len, D_MODEL), lambda b: (b, 0)),
        scratch_shapes=scratch,
        compiler_params=_cparams(("arbitrary",)),
        name="conformer_grid" if grid_mode else "conformer_seq",
    )(proj, proj, conv_w, conv_b.reshape(1, D_MODEL), ln_w.reshape(1, D_MODEL), ln_b.reshape(1, D_MODEL))


MERGE_TM = 256


def _merge_kernel(yap_ref, yas_ref, ybp_ref, ybs_ref, gl_ref, xp_ref, xs_ref, mod_ref, wa_ref, wb_ref, wo_ref,
                  n2_ref, rw_ref, sgu_ref, sd_ref, pre_ref, v_ref, s_ref):
    npb = T_PROMPT // MERGE_TM
    y_a = _dot(_pick_path(npb, yap_ref, yas_ref), wa_ref[...])
    y_b = _dot(_pick_path(npb, ybp_ref, ybs_ref), wb_ref[...])
    gates = jax.nn.sigmoid(gl_ref[...].astype(F32))
    mix = gates[:, :D_MODEL] * y_a + gates[:, D_MODEL:] * y_b
    out = _dot(mix.astype(BF16), wo_ref[...])
    m = mod_ref[...]
    x1 = _pick_path(npb, xp_ref, xs_ref) + m[2:3] * out
    ms = jnp.mean(x1 * x1, axis=-1, keepdims=True)
    v = x1 * lax.rsqrt(ms + EPS) * n2_ref[...] * (1.0 + m[4:5]) + m[3:4]
    v_ref[...] = v
    s_ref[...] = jax.nn.sigmoid(_dot_f32(v, rw_ref[...]))
    hgu = _dot(v.astype(BF16), sgu_ref[...])
    act = _silu(hgu[:, :D_SHARED]) * hgu[:, D_SHARED:]
    shared = _dot(act.astype(BF16), sd_ref[...])
    pre_ref[...] = x1 + m[5:6] * shared


def _merge(y_ssm_p, y_ssm_s, v_conf_p, v_conf_s, proj, x_p, x_s, mod3, w_out_ssm, w_out_conf, w_o, norm2_w,
           router_w, sh_gu, sh_d):
    tm = MERGE_TM
    npb = T_PROMPT // tm
    row = functools.partial(_mod_row_of_block, blocks_prompt=npb, blocks_per_sample_seq=DEC_SEQ // tm)
    const = lambda i: (0, 0)
    return pl.pallas_call(
        _merge_kernel,
        out_shape=(jax.ShapeDtypeStruct((T_ALL, D_MODEL), F32),
                   jax.ShapeDtypeStruct((T_ALL, D_MODEL), F32),
                   jax.ShapeDtypeStruct((T_ALL, N_EXPERTS), F32)),
        grid=(T_ALL // tm,),
        in_specs=[*_two_path_specs((tm, D_INNER), npb),
                  *_two_path_specs((tm, D_MODEL), npb),
                  pl.BlockSpec((tm, 2 * D_MODEL), lambda i: (i, COL_GL)),
                  *_two_path_specs((tm, D_MODEL), npb),
                  pl.BlockSpec((None, N_MOD, D_MODEL), lambda i: (row(i), 0, 0)),
                  pl.BlockSpec((D_INNER, D_MODEL), const),
                  pl.BlockSpec((D_MODEL, D_MODEL), const),
                  pl.BlockSpec((D_MODEL, D_MODEL), const),
                  pl.BlockSpec((1, D_MODEL), const),
                  pl.BlockSpec((D_MODEL, N_EXPERTS), const),
                  pl.BlockSpec((D_MODEL, 2 * D_SHARED), const),
                  pl.BlockSpec((D_SHARED, D_MODEL), const)],
        out_specs=(pl.BlockSpec((tm, D_MODEL), lambda i: (i, 0)),
                   pl.BlockSpec((tm, D_MODEL), lambda i: (i, 0)),
                   pl.BlockSpec((tm, N_EXPERTS), lambda i: (i, 0))),
        compiler_params=_cparams(("arbitrary",)),
        name="merge",
    )(y_ssm_p, y_ssm_s, v_conf_p, v_conf_s, proj, x_p, x_s, mod3, w_out_ssm, w_out_conf, w_o,
      norm2_w.reshape(1, D_MODEL), router_w, sh_gu, sh_d)


ROUTE_TB = 256


def _first_index_of_max(vals, iota, sentinel):
    m = jnp.max(vals, axis=0, keepdims=True)
    idx = jnp.min(jnp.where(vals == m, iota, jnp.float32(sentinel)), axis=0, keepdims=True)
    return m, idx


def _route_kernel(s_ref, bias_ref, idx_ref, w_ref, rank_ref, cnt_ref, run_scr):
    i = pl.program_id(0)
    tb = ROUTE_TB
    neg = jnp.float32(-jnp.inf)

    @pl.when(i == 0)
    def _():
        run_scr[...] = jnp.zeros(run_scr.shape, F32)

    s_t = s_ref[...].T
    sb_t = s_t + bias_ref[...]
    eiota = lax.broadcasted_iota(I32, (N_EXPERTS, tb), 0).astype(F32)

    liota = lax.broadcasted_iota(I32, (EXPERTS_PER_GROUP, tb), 0).astype(F32)
    gscores = []
    for g in range(N_EXPERT_GROUPS):
        blk = sb_t[g * EXPERTS_PER_GROUP:(g + 1) * EXPERTS_PER_GROUP, :]
        m1, i1 = _first_index_of_max(blk, liota, EXPERTS_PER_GROUP)
        m2 = jnp.max(jnp.where(liota == i1, neg, blk), axis=0, keepdims=True)
        gscores.append(m1 + m2)
    gs = jnp.concatenate(gscores, axis=0)
    giota = lax.broadcasted_iota(I32, (N_EXPERT_GROUPS, tb), 0).astype(F32)
    gsel = jnp.zeros((N_EXPERT_GROUPS, tb), F32)
    for _ in range(TOPK_GROUPS):
        _, gi = _first_index_of_max(gs, giota, N_EXPERT_GROUPS)
        hit = giota == gi
        gsel = jnp.where(hit, 1.0, gsel)
        gs = jnp.where(hit, neg, gs)
    emask = jnp.concatenate(
        [jnp.broadcast_to(gsel[g:g + 1, :], (EXPERTS_PER_GROUP, tb)) for g in range(N_EXPERT_GROUPS)], axis=0)
    masked = jnp.where(emask > 0.5, sb_t, neg)

    onehots, idxs, wts = [], [], []
    for _ in range(TOP_K):
        _, ei = _first_index_of_max(masked, eiota, N_EXPERTS)
        hit = eiota == ei
        onehots.append(hit)
        idxs.append(ei)
        wts.append(jnp.sum(jnp.where(hit, s_t, 0.0), axis=0, keepdims=True))
        masked = jnp.where(hit, neg, masked)
    w = jnp.concatenate(wts, axis=0)
    w_ref[...] = w / jnp.sum(w, axis=0, keepdims=True) * ROUTED_SCALE
    idx_ref[...] = jnp.concatenate(idxs, axis=0).astype(I32)

    assign = jnp.zeros((N_EXPERTS, tb), F32)
    for hit in onehots:
        assign = jnp.where(hit, 1.0, assign)
    assign_b = assign.astype(BF16)
    ti = lax.broadcasted_iota(I32, (tb, tb), 0)
    tj = lax.broadcasted_iota(I32, (tb, tb), 1)
    before = jnp.where(ti < tj, 1.0, 0.0).astype(BF16)
    within = _dot(assign_b, before)
    run = run_scr[...]
    total = within + jnp.concatenate([run] * (tb // LANES), axis=1)
    rank_ref[...] = jnp.concatenate(
        [jnp.sum(jnp.where(hit, total, 0.0), axis=0, keepdims=True) for hit in onehots], axis=0).astype(I32)
    new_run = run + _dot(assign_b, jnp.ones((tb, LANES), BF16))
    run_scr[...] = new_run
    cnt_ref[...] = new_run


def _route(scores, router_bias):
    tb = ROUTE_TB
    return pl.pallas_call(
        _route_kernel,
        out_shape=(jax.ShapeDtypeStruct((TOP_K, T_ALL), I32),
                   jax.ShapeDtypeStruct((TOP_K, T_ALL), F32),
                   jax.ShapeDtypeStruct((TOP_K, T_ALL), I32),
                   jax.ShapeDtypeStruct((N_EXPERTS, LANES), F32)),
        grid=(T_ALL // tb,),
        in_specs=[pl.BlockSpec((tb, N_EXPERTS), lambda i: (i, 0)),
                  pl.BlockSpec((N_EXPERTS, 1), lambda i: (0, 0))],
        out_specs=(pl.BlockSpec((TOP_K, tb), lambda i: (0, i)),
                   pl.BlockSpec((TOP_K, tb), lambda i: (0, i)),
                   pl.BlockSpec((TOP_K, tb), lambda i: (0, i)),
                   pl.BlockSpec((N_EXPERTS, LANES), lambda i: (0, 0))),
        scratch_shapes=[pltpu.VMEM((N_EXPERTS, LANES), F32)],
        compiler_params=_cparams(("arbitrary",)),
        name="route",
    )(scores, router_bias.reshape(N_EXPERTS, 1))


def _dest_kernel(idx_ref, rank_ref, start_ref, dest_ref):
    tb = idx_ref.shape[1]
    eiota = lax.broadcasted_iota(I32, (N_EXPERTS, tb), 0)
    start = jnp.broadcast_to(start_ref[...], (N_EXPERTS, tb))
    idx = idx_ref[...]
    rows = [jnp.sum(jnp.where(eiota == idx[k:k + 1, :], start, 0.0), axis=0, keepdims=True)
            for k in range(TOP_K)]
    dest_ref[...] = jnp.concatenate(rows, axis=0).astype(I32) + rank_ref[...]


def _dest_slots(idx_t, rank_t, group_start):
    tb = 512
    return pl.pallas_call(
        _dest_kernel,
        out_shape=jax.ShapeDtypeStruct((TOP_K, T_ALL), I32),
        grid=(T_ALL // tb,),
        in_specs=[pl.BlockSpec((TOP_K, tb), lambda i: (0, i)),
                  pl.BlockSpec((TOP_K, tb), lambda i: (0, i)),
                  pl.BlockSpec((N_EXPERTS, 1), lambda i: (0, 0))],
        out_specs=pl.BlockSpec((TOP_K, tb), lambda i: (0, i)),
        compiler_params=_cparams(("arbitrary",)),
        name="dest_slots",
    )(idx_t, rank_t, group_start.astype(F32).reshape(N_EXPERTS, 1))


DISPATCH_TB = 256


HI_MASK = -65536


def _pack_pair(hi, lo):
    hb = pltpu.bitcast(hi.astype(BF16).astype(F32), I32)
    lb = pltpu.bitcast(lo.astype(BF16).astype(F32), I32)
    return jnp.bitwise_or(hb, lax.shift_right_logical(lb, 16))


def _unpack_pair(word):
    hi = pltpu.bitcast(jnp.bitwise_and(word, HI_MASK), F32)
    lo = pltpu.bitcast(lax.shift_left(word, 16), F32)
    return hi, lo


def _rows_to_tiles(dst_scr, base, rows):
    n = rows.shape[0]
    for s in range(ROW_TILE):
        dst_scr[pl.ds(base + s, n, stride=ROW_TILE), :] = _pack_pair(
            rows[:, s * LANES:(s + 1) * LANES], rows[:, HALF_D + s * LANES:HALF_D + (s + 1) * LANES])


def _tile_column(src_scr, base, n, s):
    return src_scr[pl.ds(base + s, n, stride=ROW_TILE), :]


def _tiles_to_rows(src_scr, base, n, dtype):
    pairs = [_unpack_pair(_tile_column(src_scr, base, n, s)) for s in range(ROW_TILE)]
    return jnp.concatenate([p[0].astype(dtype) for p in pairs] + [p[1].astype(dtype) for p in pairs], axis=1)


def _row_tile(ref, row):
    return ref.at[pl.ds(pl.multiple_of(row * ROW_TILE, ROW_TILE), ROW_TILE), :]


def _dispatch_kernel(dest_ref, v_ref, xs_hbm, tile_scr, zero_scr, sem, zsem):
    i = pl.program_id(0)
    n = pl.num_programs(0)
    tb = DISPATCH_TB
    slot = i % 2
    base = pl.multiple_of(slot * (tb * ROW_TILE), ROW_TILE)
    _rows_to_tiles(tile_scr, base, v_ref[...])

    def issue(t, carry):
        src = _row_tile(tile_scr, slot * tb + t)
        for k in range(TOP_K):
            pltpu.make_async_copy(src, _row_tile(xs_hbm, dest_ref[k, t]), sem.at[slot]).start(priority=k % 2)
        return carry

    lax.fori_loop(0, tb, issue, 0)

    def wait_block(sl):
        blk = tile_scr.at[pl.ds(pl.multiple_of(sl * (tb * ROW_TILE), ROW_TILE), tb * ROW_TILE), :]
        for _ in range(TOP_K):
            pltpu.make_async_copy(blk, blk, sem.at[sl]).wait()

    @pl.when(i == 0)
    def _():
        zero_scr[...] = jnp.zeros(zero_scr.shape, I32)
        for j in range(GMM_TM // SUBLANES):
            cp = pltpu.make_async_copy(
                zero_scr, xs_hbm.at[pl.ds((N_ASSIGN + j * SUBLANES) * ROW_TILE, SUBLANES * ROW_TILE), :], zsem)
            cp.start()
            cp.wait()

    @pl.when(i > 0)
    def _():
        wait_block(1 - slot)

    @pl.when(i == n - 1)
    def _():
        wait_block(slot)


def _dispatch(dest_t, v_all):
    tb = DISPATCH_TB
    return pl.pallas_call(
        _dispatch_kernel,
        out_shape=jax.ShapeDtypeStruct((N_ROWS * ROW_TILE, LANES), I32),
        grid=(T_ALL // tb,),
        in_specs=[pl.BlockSpec((TOP_K, tb), lambda i: (0, i), memory_space=pltpu.SMEM),
                  pl.BlockSpec((tb, D_MODEL), lambda i: (i, 0))],
        out_specs=pl.BlockSpec(memory_space=pl.ANY),
        scratch_shapes=[pltpu.VMEM((2 * tb * ROW_TILE, LANES), I32),
                        pltpu.VMEM((SUBLANES * ROW_TILE, LANES), I32),
                        pltpu.SemaphoreType.DMA((2,)),
                        pltpu.SemaphoreType.DMA(())],
        compiler_params=_cparams(("arbitrary",)),
        name="dispatch",
    )(dest_t, v_all)


def _flat_rows(ref, row, n_rows):
    return ref.at[pl.ds(pl.multiple_of(row * ROW_TILE, ROW_TILE), n_rows * ROW_TILE), :]


def _tile_writes(y_scr, y_hbm, slot, row0, valid, sem):
    base = slot * GMM_TM
    parts = [(valid == GMM_TM, pltpu.make_async_copy(_flat_rows(y_scr, base, GMM_TM),
                                                     _flat_rows(y_hbm, row0, GMM_TM), sem))]
    off = jnp.int32(0)
    for p in TAIL_SIZES:
        bit = (valid & p) != 0
        take = jnp.logical_and(valid < GMM_TM, bit)
        parts.append((take, pltpu.make_async_copy(_flat_rows(y_scr, base + off, p),
                                                  _flat_rows(y_hbm, row0 + off, p), sem)))
        off = off + jnp.where(bit, p, 0)
    return parts


Y_SLOTS = 4
ST_TILES, ST_ROW = 0, 1
ST_VALID = ST_ROW + Y_SLOTS
ST_FETCH_E = ST_VALID + Y_SLOTS
ST_FETCH_I, ST_FETCH_G, ST_SIZE = ST_FETCH_E + 1, ST_FETCH_E + 2, ST_FETCH_E + 3
X_SLOTS = 4
X_AHEAD = X_SLOTS


def _gmm_kernel(start_ref, xs_hbm, wg_ref, wu_ref, wd_ref, y_hbm,
                x_scr, y_scr, wgu_scr, wdn_scr, zero_scr, st_ref, xsem, ysem, zsem):
    e = pl.program_id(0)
    row_lo = start_ref[e]
    row_hi = start_ref[e + 1]
    n_rows = row_hi - row_lo
    n_tiles = lax.shift_right_logical(n_rows + (GMM_TM - 1), GMM_TM.bit_length() - 1)

    def x_copy(row, sl):
        return pltpu.make_async_copy(_flat_rows(xs_hbm, row, GMM_TM),
                                     _flat_rows(x_scr, sl * GMM_TM, GMM_TM), xsem.at[sl])

    def tiles_of(ex):
        rows = start_ref[ex + 1] - start_ref[ex]
        return lax.shift_right_logical(rows + (GMM_TM - 1), GMM_TM.bit_length() - 1)

    def skip_empty(ex):
        def empty(q):
            qc = jnp.minimum(q, N_EXPERTS - 1)
            return jnp.logical_and(q < N_EXPERTS, start_ref[qc + 1] == start_ref[qc])
        return lax.while_loop(empty, lambda q: q + 1, ex)

    def fetch_next_tile():
        pe = st_ref[ST_FETCH_E]

        @pl.when(pe < N_EXPERTS)
        def _():
            pi = st_ref[ST_FETCH_I]
            pg = st_ref[ST_FETCH_G]
            pec = jnp.minimum(pe, N_EXPERTS - 1)
            x_copy(start_ref[pec] + pi * GMM_TM, pg % X_SLOTS).start()
            last = pi + 1 >= tiles_of(pec)
            st_ref[ST_FETCH_E] = jnp.where(last, skip_empty(pe + 1), pe)
            st_ref[ST_FETCH_I] = jnp.where(last, 0, pi + 1)
            st_ref[ST_FETCH_G] = pg + 1

    def wait_writes(sl):
        @pl.when(st_ref[ST_VALID + sl] > 0)
        def _():
            for pred, cp in _tile_writes(y_scr, y_hbm, sl, st_ref[ST_ROW + sl], st_ref[ST_VALID + sl],
                                         ysem.at[sl]):
                @pl.when(pred)
                def _():
                    cp.wait()
            st_ref[ST_VALID + sl] = 0

    @pl.when(e == 0)
    def _():
        for j in range(ST_SIZE):
            st_ref[j] = 0
        st_ref[ST_FETCH_E] = skip_empty(jnp.int32(0))
        for _ in range(X_AHEAD):
            fetch_next_tile()

    done = st_ref[ST_TILES]

    @pl.when(n_tiles > 0)
    def _():
        wgu_scr[:, :D_EXPERT] = wg_ref[0].astype(BF16)
        wgu_scr[:, D_EXPERT:] = wu_ref[0].astype(BF16)
        wdn_scr[...] = wd_ref[0].astype(BF16)

        def run_tiles(first, count):
            tiles = []
            for j in range(count):
                i = first + j
                g = done + i
                row0 = row_lo + i * GMM_TM
                tiles.append((g % X_SLOTS, g % Y_SLOTS, row0, jnp.minimum(n_rows - i * GMM_TM, GMM_TM)))
            for xslot, yslot, row0, _ in tiles:
                x_copy(row0, xslot).wait()
                wait_writes(yslot)
            for xslot, yslot, _, _ in tiles:
                x = _tiles_to_rows(x_scr, pl.multiple_of(xslot * (GMM_TM * ROW_TILE), ROW_TILE), GMM_TM, BF16)
                h = _dot(x, wgu_scr[...])
                act = (_silu(h[:, :D_EXPERT]) * h[:, D_EXPERT:]).astype(BF16)
                y = _dot(act, wdn_scr[...])
                _rows_to_tiles(y_scr, pl.multiple_of(yslot * (GMM_TM * ROW_TILE), ROW_TILE), y)
            for _, yslot, row0, valid in tiles:
                fetch_next_tile()
                for pred, cp in _tile_writes(y_scr, y_hbm, yslot, row0, valid, ysem.at[yslot]):
                    @pl.when(pred)
                    def _():
                        cp.start()
                st_ref[ST_ROW + yslot] = row0
                st_ref[ST_VALID + yslot] = valid

        def pair(p, carry):
            run_tiles(2 * p, 2)
            return carry

        lax.fori_loop(0, lax.shift_right_logical(n_tiles, 1), pair, 0)

        @pl.when(jnp.bitwise_and(n_tiles, 1) == 1)
        def _():
            run_tiles(n_tiles - 1, 1)

        st_ref[ST_TILES] = done + n_tiles

    @pl.when(e == pl.num_programs(0) - 1)
    def _():
        for sl in range(Y_SLOTS):
            wait_writes(sl)
        zero_scr[...] = jnp.zeros(zero_scr.shape, I32)
        for j in range(GMM_TM // SUBLANES):
            cp = pltpu.make_async_copy(zero_scr, _flat_rows(y_hbm, N_ASSIGN + j * SUBLANES, SUBLANES), zsem)
            cp.start()
            cp.wait()


def _grouped_mlp(group_start, xs, w_gate, w_up, w_down):
    grid_spec = pltpu.PrefetchScalarGridSpec(
        num_scalar_prefetch=1,
        grid=(N_EXPERTS,),
        in_specs=[pl.BlockSpec(memory_space=pl.ANY),
                  pl.BlockSpec((1, D_MODEL, D_EXPERT), lambda e, st: (e, 0, 0)),
                  pl.BlockSpec((1, D_MODEL, D_EXPERT), lambda e, st: (e, 0, 0)),
                  pl.BlockSpec((1, D_EXPERT, D_MODEL), lambda e, st: (e, 0, 0))],
        out_specs=pl.BlockSpec(memory_space=pl.ANY),
        scratch_shapes=[pltpu.VMEM((X_SLOTS * GMM_TM * ROW_TILE, LANES), I32),
                        pltpu.VMEM((Y_SLOTS * GMM_TM * ROW_TILE, LANES), I32),
                        pltpu.VMEM((D_MODEL, 2 * D_EXPERT), BF16),
                        pltpu.VMEM((D_EXPERT, D_MODEL), BF16),
                        pltpu.VMEM((SUBLANES * ROW_TILE, LANES), I32),
                        pltpu.SMEM((ST_SIZE,), I32),
                        pltpu.SemaphoreType.DMA((X_SLOTS,)),
                        pltpu.SemaphoreType.DMA((Y_SLOTS,)),
                        pltpu.SemaphoreType.DMA(())],
    )
    return pl.pallas_call(
        _gmm_kernel,
        out_shape=jax.ShapeDtypeStruct((N_ROWS * ROW_TILE, LANES), I32),
        grid_spec=grid_spec,
        compiler_params=_cparams(("arbitrary",)),
        name="grouped_mlp",
    )(group_start, xs, w_gate, w_up, w_down)


COMBINE_TB = 256


def _combine_kernel(dest_ref, dest_next_ref, y_hbm, pre_ref, w_ref, mod_ref, fw_ref, op_ref, os_ref, buf, x2_scr,
                    sem):
    i = pl.program_id(0)
    n = pl.num_programs(0)
    tb = COMBINE_TB
    slot = i % 2
    blk_rows = TOP_K * tb

    def issue_block(d_ref, sl):
        def issue(t, carry):
            for k in range(TOP_K):
                pltpu.make_async_copy(_row_tile(y_hbm, d_ref[k, t]),
                                      _row_tile(buf, sl * blk_rows + k * tb + t),
                                      sem.at[sl]).start(priority=k % 2)
            return carry

        lax.fori_loop(0, tb, issue, 0)

    @pl.when(i == 0)
    def _():
        issue_block(dest_ref, 0)

    @pl.when(i + 1 < n)
    def _():
        issue_block(dest_next_ref, 1 - slot)

    whole = _flat_rows(buf, slot * blk_rows, blk_rows)
    pltpu.make_async_copy(whole, whole, sem.at[slot]).wait()

    base = pl.multiple_of(slot * (blk_rows * ROW_TILE), ROW_TILE)
    w = w_ref[...]
    m = mod_ref[...]
    ssq = jnp.zeros((tb, 1), F32)
    for s in range(ROW_TILE):
        routed = [None, None]
        for k in range(TOP_K):
            halves = _unpack_pair(_tile_column(buf, base + k * tb * ROW_TILE, tb, s))
            for j in range(2):
                term = w[:, k:k + 1] * halves[j]
                routed[j] = term if routed[j] is None else routed[j] + term
        for j in range(2):
            cols = slice(j * HALF_D + s * LANES, j * HALF_D + (s + 1) * LANES)
            x2 = pre_ref[:, cols] + m[5:6, cols] * routed[j]
            x2_scr[:, cols] = x2
            ssq = ssq + jnp.sum(x2 * x2, axis=-1, keepdims=True)
    scale = lax.rsqrt(ssq * (1.0 / D_MODEL) + EPS)

    @pl.when(i < T_PROMPT // tb)
    def _():
        op_ref[...] = x2_scr[...] * scale * fw_ref[...]

    @pl.when(i >= T_PROMPT // tb)
    def _():
        os_ref[...] = x2_scr[...] * scale * fw_ref[...]


def _combine(dest_t, y_rows, pre, w_tok, mod3, final_norm_w):
    tb = COMBINE_TB
    n_blocks = T_ALL // tb
    npb = T_PROMPT // tb
    row = functools.partial(_mod_row_of_block, blocks_prompt=npb, blocks_per_sample_seq=DEC_SEQ // tb)
    return pl.pallas_call(
        _combine_kernel,
        out_shape=(jax.ShapeDtypeStruct((T_PROMPT, D_MODEL), F32),
                   jax.ShapeDtypeStruct((T_SAMPLE, D_MODEL), F32)),
        grid=(n_blocks,),
        in_specs=[pl.BlockSpec((TOP_K, tb), lambda i: (0, i), memory_space=pltpu.SMEM),
                  pl.BlockSpec((TOP_K, tb), lambda i: (0, jnp.minimum(i + 1, n_blocks - 1)),
                               memory_space=pltpu.SMEM),
                  pl.BlockSpec(memory_space=pl.ANY),
                  pl.BlockSpec((tb, D_MODEL), lambda i: (i, 0)),
                  pl.BlockSpec((tb, TOP_K), lambda i: (i, 0)),
                  pl.BlockSpec((None, N_MOD, D_MODEL), lambda i: (row(i), 0, 0)),
                  pl.BlockSpec((1, D_MODEL), lambda i: (0, 0))],
        out_specs=_two_path_specs((tb, D_MODEL), npb),
        scratch_shapes=[pltpu.VMEM((2 * TOP_K * tb * ROW_TILE, LANES), I32),
                        pltpu.VMEM((tb, D_MODEL), F32),
                        pltpu.SemaphoreType.DMA((2,))],
        compiler_params=_cparams(("arbitrary",)),
        name="combine",
    )(dest_t, dest_t, y_rows, pre, w_tok, mod3, final_norm_w.reshape(1, D_MODEL))


def _group_starts(counts):
    return jnp.concatenate([jnp.zeros((1,), I32), jnp.cumsum(counts).astype(I32)])


def _head_expand_matrix():
    r = jnp.arange(LANES)[:, None]
    cidx = jnp.arange(2 * D_INNER)[None, :]
    direction = cidx // D_INNER
    head = (cidx % D_INNER) // HEAD_DIM
    return (r == direction * N_HEADS + head).astype(BF16)


def kernel(x_prompt, x_sample, state_ssm, c, c_ctx, norm1_w, norm2_w, w_mod, b_mod, w_in, ssm_conv_w, ssm_conv_b, ssm_dt_bias, ssm_a_log, ssm_d, ssm_norm_w, w_out_ssm, conf_conv_w, conf_conv_b, conf_ln_w, conf_ln_b, w_out_conf, w_o, router_w, router_bias, exp_w_gate, exp_w_up, exp_w_down, sh_w_gate, sh_w_up, sh_w_down, final_norm_w):
    x_p = x_prompt.reshape(T_PROMPT, D_MODEL)
    x_s = x_sample.reshape(T_SAMPLE, D_MODEL)

    cc = jnp.zeros((MOD_ROWS, D_MODEL), F32).at[:DEC_BATCH].set(c).at[CTX_ROW].set(c_ctx)
    mod3 = _modulation(cc, w_mod[0], b_mod[0]).reshape(MOD_ROWS, N_MOD, D_MODEL)

    w = w_in[0]
    o_xbc, o_dt, o_cv, o_cg, o_gl = D_INNER, D_INNER + CONV_DIM, D_INNER + CONV_DIM + 2 * N_HEADS, \
        D_INNER + CONV_DIM + 2 * N_HEADS + D_MODEL, D_INNER + CONV_DIM + 2 * N_HEADS + 2 * D_MODEL
    w_main = jnp.concatenate([w[:, o_xbc:o_dt], w[:, :o_xbc], w[:, o_gl:], w[:, o_cv:o_cg], w[:, o_cg:o_gl]],
                             axis=1).astype(BF16)
    w_dt = jnp.pad(w[:, o_dt:o_cv], ((0, 0), (0, LANES - 2 * N_HEADS))).astype(BF16)
    proj, dt_raw = _in_projection(x_p, x_s, mod3, norm1_w[0], w_main, w_dt)

    pad_heads = lambda v: jnp.pad(v.reshape(1, 2 * N_HEADS), ((0, 0), (0, LANES - 2 * N_HEADS)))
    dtb = pad_heads(ssm_dt_bias[0])
    a_neg = pad_heads(-jnp.exp(ssm_a_log[0]))
    dvec = jnp.repeat(ssm_d[0], HEAD_DIM).reshape(1, D_INNER)
    nw = ssm_norm_w[0].reshape(1, D_INNER)
    e2 = _head_expand_matrix()
    xbc_p = _ssm_conv(proj, ssm_conv_w[0], ssm_conv_b[0], seq_len=SEQ, n_seq=BATCH, row_block0=0)
    xbc_s = _ssm_conv(proj, ssm_conv_w[0], ssm_conv_b[0], seq_len=DEC_SEQ, n_seq=DEC_BATCH,
                      row_block0=T_PROMPT // DEC_SEQ)
    y_p, fin = _ssd(xbc_p, dt_raw, proj, None, dtb, a_neg, dvec, nw, e2,
                    seq_len=SEQ, n_seq=BATCH, tok0=0, want_final=True)
    init = state_ssm.reshape(DEC_BATCH, 2, D_INNER, D_STATE)
    (y_s,) = _ssd(xbc_s, dt_raw, proj, init, dtb, a_neg, dvec, nw, e2,
                  seq_len=DEC_SEQ, n_seq=DEC_BATCH, tok0=T_PROMPT, want_final=False)

    v_p = _conformer(proj, conf_conv_w[0], conf_conv_b[0], conf_ln_w[0], conf_ln_b[0],
                     seq_len=SEQ, n_seq=BATCH, row_block0=0, grid_mode=False)
    v_s = _conformer(proj, conf_conv_w[0], conf_conv_b[0], conf_ln_w[0], conf_ln_b[0],
                     seq_len=DEC_SEQ, n_seq=DEC_BATCH, row_block0=T_PROMPT // DEC_SEQ, grid_mode=True)

    sh_gu = jnp.concatenate([sh_w_gate[0], sh_w_up[0]], axis=1).astype(BF16)
    pre, v_all, scores = _merge(y_p, y_s, v_p, v_s, proj, x_p, x_s, mod3, w_out_ssm[0].astype(BF16),
                                w_out_conf[0].astype(BF16), w_o[0].astype(BF16), norm2_w[0], router_w[0],
                                sh_gu, sh_w_down[0].astype(BF16))

    idx_t, w_t, rank_t, cnt = _route(scores, router_bias[0])
    start = _group_starts(cnt[:, 0].astype(I32))
    dest_t = _dest_slots(idx_t, rank_t, start[:N_EXPERTS])

    xs = _dispatch(dest_t, v_all)
    y_rows = _grouped_mlp(start, xs, exp_w_gate[0], exp_w_up[0], exp_w_down[0])
    out_p, out_s = _combine(dest_t, y_rows, pre, w_t.T, mod3, final_norm_w)

    y_prompt = out_p.reshape(BATCH, SEQ, D_MODEL)
    y_sample = out_s.reshape(DEC_BATCH, DEC_SEQ, D_MODEL)
    new_state = fin.reshape(BATCH, 1, 2, N_HEADS, HEAD_DIM, D_STATE)
    return (y_prompt, y_sample, new_state)
```
